```python
import math
import jax, jax.numpy as jnp
from jax import lax
import numpy as np

D_MODEL = 1024
BATCH = 2
SEQ = 8192
DEPTH = 1

HEAD_DIM = 64
ATTN_PATTERNS = ((128, 1), (512, 4), (2048, 16))
N_ATTN_GROUPS = len(ATTN_PATTERNS)
HEADS_PER_GROUP = 4
N_ATTN_HEADS = N_ATTN_GROUPS * HEADS_PER_GROUP
ATTN_WIDTH = N_ATTN_HEADS * HEAD_DIM
ATTN_OUT_WIDTH = HEADS_PER_GROUP * HEAD_DIM
ROT_DIM = HEAD_DIM // 4
ROPE_THETA = 500000.0
BLK = 128
NEG_INF = -1e30
SSM_GROUP = 16
SSM_WIDTH = 512
SSM_GROUPS = SSM_WIDTH // SSM_GROUP
SSM_STATE = 64
DT_MIN = 0.001
DT_MAX = 0.1
D_FF = 2816
IN_WIDTH = 3 * ATTN_WIDTH + SSM_WIDTH + 2 * D_MODEL
SPLITS = [ATTN_WIDTH, 2 * ATTN_WIDTH, 3 * ATTN_WIDTH, 3 * ATTN_WIDTH + SSM_WIDTH,
          3 * ATTN_WIDTH + SSM_WIDTH + D_MODEL]
DEEPNORM_ALPHA = (2 * DEPTH) ** 0.25
DEEPNORM_BETA = (8 * DEPTH) ** -0.25
LN_EPS = 1e-5

kernel_name = 'hybrid_dilated_attn_s5_macaron_deepnorm'


def layer_norm(x, g, b):
    xf = x.astype(jnp.float32)
    mu = jnp.mean(xf, axis=-1, keepdims=True)
    var = jnp.mean(jnp.square(xf - mu), axis=-1, keepdims=True)
    y = (xf - mu) * lax.rsqrt(var + LN_EPS)
    return (y * g.astype(jnp.float32) + b.astype(jnp.float32)).astype(x.dtype)


def swiglu(x, wg, wu, wd):
    return (jax.nn.silu(x @ wg) * (x @ wu)) @ wd


def partial_rotary(t, positions):
    half = ROT_DIM // 2
    inv_freq = ROPE_THETA ** (-jnp.arange(half, dtype=jnp.float32) * 2.0 / ROT_DIM)
    ang = positions.astype(jnp.float32)[..., None] * inv_freq
    cos = jnp.cos(ang)[:, :, None, :]
    sin = jnp.sin(ang)[:, :, None, :]
    t1 = t[..., :half].astype(jnp.float32)
    t2 = t[..., half:ROT_DIM].astype(jnp.float32)
    rot = jnp.concatenate([t1 * cos - t2 * sin, t2 * cos + t1 * sin], axis=-1).astype(t.dtype)
    return jnp.concatenate([rot, t[..., ROT_DIM:]], axis=-1)


def banded_causal_attention(q, k, v, window):
    nb, n, h, hd = q.shape
    n_blk = -(-n // BLK)
    pad = n_blk * BLK - n
    q = jnp.pad(q, ((0, 0), (0, pad), (0, 0), (0, 0)))
    k = jnp.pad(k, ((0, 0), (BLK, pad), (0, 0), (0, 0)))
    v = jnp.pad(v, ((0, 0), (BLK, pad), (0, 0), (0, 0)))
    qb = q.reshape(nb, n_blk, BLK, h, hd)
    kb = k.reshape(nb, n_blk + 1, BLK, h, hd)
    vb = v.reshape(nb, n_blk + 1, BLK, h, hd)
    kw = jnp.concatenate([kb[:, :-1], kb[:, 1:]], axis=2)
    vw = jnp.concatenate([vb[:, :-1], vb[:, 1:]], axis=2)
    s = jnp.einsum('bnqhd,bnkhd->bnhqk', qb, kw).astype(jnp.float32) * (hd ** -0.5)
    dist = jnp.arange(BLK)[:, None] + BLK - jnp.arange(2 * BLK)[None, :]
    key_pos = (jnp.arange(n_blk) * BLK)[:, None] + jnp.arange(2 * BLK)[None, :] - BLK
    valid = ((dist >= 0) & (dist <= window))[None] & (key_pos >= 0)[:, None, :]
    s = jnp.where(valid[None, :, None], s, NEG_INF)
    m = jnp.max(s, axis=-1)
    p = jnp.exp(s - m[..., None])
    den = jnp.sum(p, axis=-1)
    o = jnp.einsum('bnhqk,bnkhd->bnqhd', p, vw.astype(jnp.float32))
    den_t = jnp.swapaxes(den, 2, 3)
    o = (o / den_t[..., None]).reshape(nb, n_blk * BLK, h, hd)[:, :n]
    m = jnp.swapaxes(m, 2, 3).reshape(nb, n_blk * BLK, h)[:, :n]
    den = den_t.reshape(nb, n_blk * BLK, h)[:, :n]
    return o, m, den


def dilated_attention(q, k, v, window, dilation):
    b, l, h, hd = q.shape
    n = l // dilation

    def split(t):
        return t.reshape(b, n, dilation, h, hd).swapaxes(1, 2).reshape(b * dilation, n, h, hd)

    o, m, den = banded_causal_attention(split(q), split(k), split(v), window // dilation)
    o = o.reshape(b, dilation, n, h, hd).swapaxes(1, 2).reshape(b, l, h, hd)
    m = m.reshape(b, dilation, n, h).swapaxes(1, 2).reshape(b, l, h)
    den = den.reshape(b, dilation, n, h).swapaxes(1, 2).reshape(b, l, h)
    return o, m, den


def dilated_mixture_attention(q, k, v):
    outs, maxes, dens = [], [], []
    for g, (window, dilation) in enumerate(ATTN_PATTERNS):
        o, m, den = dilated_attention(q[:, :, g], k[:, :, g], v[:, :, g], window, dilation)
        outs.append(o)
        maxes.append(m)
        dens.append(den)
    o = jnp.stack(outs)
    m = jnp.stack(maxes)
    den = jnp.stack(dens)
    wgt = den * jnp.exp(m - jnp.max(m, axis=0, keepdims=True))
    wgt = wgt / jnp.sum(wgt, axis=0, keepdims=True)
    return jnp.sum(wgt[..., None] * o, axis=0)


def _complex_affine_combine(e1, e2):
    a1r, a1i, b1r, b1i = e1
    a2r, a2i, b2r, b2i = e2
    ar = a2r * a1r - a2i * a1i
    ai = a2r * a1i + a2i * a1r
    br = a2r * b1r - a2i * b1i + b2r
    bi = a2r * b1i + a2i * b1r + b2i
    return (ar, ai, br, bi)


def s5_branch(u, a_re, a_im, log_dt, b_re, b_im, c_re, c_im, d_skip, w_glu, b_glu):
    f32 = jnp.float32
    bsz, l, _ = u.shape
    uf = u.astype(f32)
    ug = uf.reshape(bsz, l, SSM_GROUPS, SSM_GROUP)
    lam_re = jnp.minimum(a_re.astype(f32), -1e-4)
    lam_im = a_im.astype(f32)
    dt = jnp.exp(log_dt.astype(f32))[:, None]
    mag = jnp.exp(lam_re * dt)
    lb_re = mag * jnp.cos(lam_im * dt)
    lb_im = mag * jnp.sin(lam_im * dt)
    inv_abs2 = 1.0 / (lam_re * lam_re + lam_im * lam_im)
    coef_re = ((lb_re - 1.0) * lam_re + lb_im * lam_im) * inv_abs2
    coef_im = (lb_im * lam_re - (lb_re - 1.0) * lam_im) * inv_abs2
    br = b_re.astype(f32)
    bi = b_im.astype(f32)
    bbar_re = coef_re[..., None] * br - coef_im[..., None] * bi
    bbar_im = coef_re[..., None] * bi + coef_im[..., None] * br
    bu_re = jnp.einsum('blgc,gpc->blgp', ug, bbar_re)
    bu_im = jnp.einsum('blgc,gpc->blgp', ug, bbar_im)
    a_full_re = jnp.broadcast_to(lb_re, bu_re.shape)
    a_full_im = jnp.broadcast_to(lb_im, bu_im.shape)
    _, _, h_re, h_im = lax.associative_scan(
        _complex_affine_combine, (a_full_re, a_full_im, bu_re, bu_im), axis=1)
    y = (jnp.einsum('blgp,gcp->blgc', h_re, c_re.astype(f32))
         - jnp.einsum('blgp,gcp->blgc', h_im, c_im.astype(f32)))
    y = y.reshape(bsz, l, SSM_WIDTH) + d_skip.astype(f32) * uf
    y = jax.nn.gelu(y).astype(u.dtype)
    return y * jax.nn.sigmoid(y @ w_glu + b_glu)


def hybrid_mixer(x, positions, w_in, w_attn_out, a_re, a_im, log_dt, b_re, b_im, c_re, c_im,
                 d_skip, w_glu, b_glu, w_ssm_out, w_o):
    b, l, _ = x.shape
    z = x @ w_in
    q, k, v, u, g_attn, g_ssm = jnp.split(z, SPLITS, axis=-1)
    q = partial_rotary(q.reshape(b, l, N_ATTN_HEADS, HEAD_DIM), positions)
    k = partial_rotary(k.reshape(b, l, N_ATTN_HEADS, HEAD_DIM), positions)
    shape5 = (b, l, N_ATTN_GROUPS, HEADS_PER_GROUP, HEAD_DIM)
    att = dilated_mixture_attention(q.reshape(shape5), k.reshape(shape5), v.reshape(shape5))
    y_attn = att.reshape(b, l, ATTN_OUT_WIDTH).astype(x.dtype) @ w_attn_out
    y_ssm = s5_branch(u, a_re, a_im, log_dt, b_re, b_im, c_re, c_im, d_skip, w_glu, b_glu) @ w_ssm_out
    merged = jax.nn.sigmoid(g_attn) * y_attn + jax.nn.sigmoid(g_ssm) * y_ssm
    return merged @ w_o


def setup_inputs(seed: int = 0) -> dict:
    key = jax.random.key(seed)
    ks = jax.random.split(key, 32)
    f32 = jnp.float32

    def nrm(k, shape, scale):
        return jax.random.normal(k, shape, f32) * scale

    x = nrm(ks[0], (BATCH, SEQ, D_MODEL), 1.0)
    offs = jax.random.randint(ks[1], (BATCH, 1), 0, 4096, dtype=jnp.int32)
    positions = jnp.arange(SEQ, dtype=jnp.int32)[None, :] + offs
    w_in = nrm(ks[2], (DEPTH, D_MODEL, IN_WIDTH), D_MODEL ** -0.5)
    w_attn_out = nrm(ks[3], (DEPTH, ATTN_OUT_WIDTH, D_MODEL), ATTN_OUT_WIDTH ** -0.5 * DEEPNORM_BETA)
    n_idx = jnp.arange(SSM_STATE, dtype=f32)
    a_re = -0.5 + nrm(ks[4], (DEPTH, SSM_GROUPS, SSM_STATE), 0.01)
    a_im = math.pi * n_idx + nrm(ks[5], (DEPTH, SSM_GROUPS, SSM_STATE), 0.01)
    log_dt = jax.random.uniform(ks[6], (DEPTH, SSM_GROUPS), f32, math.log(DT_MIN), math.log(DT_MAX))
    b_re = nrm(ks[7], (DEPTH, SSM_GROUPS, SSM_STATE, SSM_GROUP), (2 * SSM_GROUP) ** -0.5)
    b_im = nrm(ks[8], (DEPTH, SSM_GROUPS, SSM_STATE, SSM_GROUP), (2 * SSM_GROUP) ** -0.5)
    c_re = nrm(ks[9], (DEPTH, SSM_GROUPS, SSM_GROUP, SSM_STATE), (2 * SSM_STATE) ** -0.5)
    c_im = nrm(ks[10], (DEPTH, SSM_GROUPS, SSM_GROUP, SSM_STATE), (2 * SSM_STATE) ** -0.5)
    d_skip = nrm(ks[11], (DEPTH, SSM_WIDTH), 1.0)
    w_glu = nrm(ks[12], (DEPTH, SSM_WIDTH, SSM_WIDTH), SSM_WIDTH ** -0.5)
    b_glu = nrm(ks[13], (DEPTH, SSM_WIDTH), 0.01)
    w_ssm_out = nrm(ks[14], (DEPTH, SSM_WIDTH, D_MODEL), SSM_WIDTH ** -0.5 * DEEPNORM_BETA)
    w_o = nrm(ks[15], (DEPTH, D_MODEL, D_MODEL), D_MODEL ** -0.5 * DEEPNORM_BETA)
    ffn1_wg = nrm(ks[16], (DEPTH, D_MODEL, D_FF), D_MODEL ** -0.5)
    ffn1_wu = nrm(ks[17], (DEPTH, D_MODEL, D_FF), D_MODEL ** -0.5)
    ffn1_wd = nrm(ks[18], (DEPTH, D_FF, D_MODEL), D_FF ** -0.5 * DEEPNORM_BETA)
    ffn2_wg = nrm(ks[19], (DEPTH, D_MODEL, D_FF), D_MODEL ** -0.5)
    ffn2_wu = nrm(ks[20], (DEPTH, D_MODEL, D_FF), D_MODEL ** -0.5)
    ffn2_wd = nrm(ks[21], (DEPTH, D_FF, D_MODEL), D_FF ** -0.5 * DEEPNORM_BETA)
    ln1_g = 1.0 + nrm(ks[22], (DEPTH, D_MODEL), 0.01)
    ln1_b = nrm(ks[23], (DEPTH, D_MODEL), 0.01)
    ln2_g = 1.0 + nrm(ks[24], (DEPTH, D_MODEL), 0.01)
    ln2_b = nrm(ks[25], (DEPTH, D_MODEL), 0.01)
    ln3_g = 1.0 + nrm(ks[26], (DEPTH, D_MODEL), 0.01)
    ln3_b = nrm(ks[27], (DEPTH, D_MODEL), 0.01)
    return {'x': x, 'positions': positions, 'w_in': w_in, 'w_attn_out': w_attn_out,
            'a_re': a_re, 'a_im': a_im, 'log_dt': log_dt, 'b_re': b_re, 'b_im': b_im,
            'c_re': c_re, 'c_im': c_im, 'd_skip': d_skip, 'w_glu': w_glu, 'b_glu': b_glu,
            'w_ssm_out': w_ssm_out, 'w_o': w_o,
            'ffn1_wg': ffn1_wg, 'ffn1_wu': ffn1_wu, 'ffn1_wd': ffn1_wd,
            'ffn2_wg': ffn2_wg, 'ffn2_wu': ffn2_wu, 'ffn2_wd': ffn2_wd,
            'ln1_g': ln1_g, 'ln1_b': ln1_b, 'ln2_g': ln2_g, 'ln2_b': ln2_b,
            'ln3_g': ln3_g, 'ln3_b': ln3_b}


def reference(x, positions, w_in, w_attn_out, a_re, a_im, log_dt, b_re, b_im, c_re, c_im,
              d_skip, w_glu, b_glu, w_ssm_out, w_o, ffn1_wg, ffn1_wu, ffn1_wd,
              ffn2_wg, ffn2_wu, ffn2_wd, ln1_g, ln1_b, ln2_g, ln2_b, ln3_g, ln3_b):
    for i in range(DEPTH):
        x = layer_norm(DEEPNORM_ALPHA * x + 0.5 * swiglu(x, ffn1_wg[i], ffn1_wu[i], ffn1_wd[i]),
                       ln1_g[i], ln1_b[i])
        mix = hybrid_mixer(x, positions, w_in[i], w_attn_out[i], a_re[i], a_im[i], log_dt[i],
                           b_re[i], b_im[i], c_re[i], c_im[i], d_skip[i], w_glu[i], b_glu[i],
                           w_ssm_out[i], w_o[i])
        x = layer_norm(DEEPNORM_ALPHA * x + mix, ln2_g[i], ln2_b[i])
        x = layer_norm(DEEPNORM_ALPHA * x + 0.5 * swiglu(x, ffn2_wg[i], ffn2_wu[i], ffn2_wd[i]),
                       ln3_g[i], ln3_b[i])
    return x
```

```python
import functools

import jax
import jax.numpy as jnp
from jax import lax
from jax.experimental import pallas as pl
from jax.experimental.pallas import tpu as pltpu

F32 = jnp.float32
BF16 = jnp.bfloat16

HEAD_DIM = 64
HEADS_PER_GROUP = 4
GROUP_WIDTH = HEADS_PER_GROUP * HEAD_DIM
ATTN_PATTERNS = ((128, 1), (512, 4), (2048, 16))
N_ATTN_GROUPS = len(ATTN_PATTERNS)
ATTN_WIDTH = N_ATTN_GROUPS * GROUP_WIDTH
ROT_DIM = HEAD_DIM // 4
ROPE_THETA = 500000.0
ATTN_BLK = 128
NEG_INF = -1e30
SSM_GROUP = 16
SSM_STATE = 64
SSM_CHUNK = 16
SSM_ROW = SSM_CHUNK * SSM_GROUP
DEEPNORM_ALPHA = 2.0 ** 0.25
LN_EPS = 1e-5

LANES = 128
VMEM_LIMIT_BYTES = 56 * 1024 * 1024


def _params(*semantics):
    return pltpu.CompilerParams(dimension_semantics=semantics, vmem_limit_bytes=VMEM_LIMIT_BYTES)


def _resident(shape):
    zeros = (0,) * len(shape)
    return pl.BlockSpec(shape, lambda *_: zeros, pipeline_mode=pl.Buffered(1))


def _layer_norm(y, g, b):
    mu = jnp.mean(y, axis=-1, keepdims=True)
    yc = y - mu
    var = jnp.mean(yc * yc, axis=-1, keepdims=True)
    return yc * lax.rsqrt(var + LN_EPS) * g + b


def _sigmoid(x):
    return 1.0 / (1.0 + jnp.exp(-x))


def _gelu_tanh(x):
    c = 0.7978845608028654
    return 0.5 * x * (1.0 + jnp.tanh(c * (x + 0.044715 * (x * x * x))))


def _ffn_ln_kernel(x_ref, wg_ref, wu_ref, wd_ref, g_ref, b_ref, o_ref, *, chunks):
    x = x_ref[...]
    xb = x.astype(BF16)
    acc = None
    for c0, c1 in chunks:
        gate = jnp.dot(xb, wg_ref[:, c0:c1], preferred_element_type=F32)
        up = jnp.dot(xb, wu_ref[:, c0:c1], preferred_element_type=F32)
        h = (gate * _sigmoid(gate) * up).astype(BF16)
        part = jnp.dot(h, wd_ref[c0:c1, :], preferred_element_type=F32)
        acc = part if acc is None else acc + part
    y = DEEPNORM_ALPHA * x + 0.5 * acc
    o_ref[...] = _layer_norm(y, g_ref[...], b_ref[...])


def _ffn_chunks(d_ff, width):
    edges = list(range(0, d_ff, width)) + [d_ff]
    return tuple(zip(edges[:-1], edges[1:]))


def _ffn_ln(x, wg, wu, wd, g, b, *, tm=512, chunk=1024):
    t, d = x.shape
    d_ff = wg.shape[1]
    kern = functools.partial(_ffn_ln_kernel, chunks=_ffn_chunks(d_ff, chunk))
    return pl.pallas_call(
        kern,
        grid=(t // tm,),
        in_specs=[
            pl.BlockSpec((tm, d), lambda i: (i, 0)),
            _resident((d, d_ff)), _resident((d, d_ff)), _resident((d_ff, d)),
            _resident((1, d)), _resident((1, d)),
        ],
        out_specs=pl.BlockSpec((tm, d), lambda i: (i, 0)),
        out_shape=jax.ShapeDtypeStruct((t, d), F32),
        compiler_params=_params("parallel"),
        name="ffn_ln",
    )(x, wg.astype(BF16), wu.astype(BF16), wd.astype(BF16), g.reshape(1, d), b.reshape(1, d))


def _in_proj_kernel(x_ref, pos_ref, invf_ref, w_ref,
                    q_ref, k_ref, v_ref, u_ref, ga_ref, gs_ref, *, splits):
    xb = x_ref[...].astype(BF16)
    tm = xb.shape[0]

    ang = pos_ref[...].astype(F32) * invf_ref[...]
    cos = jnp.cos(ang)
    sin = jnp.sin(ang)
    lane = lax.broadcasted_iota(jnp.int32, (tm, LANES), 1) % HEAD_DIM
    half = ROT_DIM // 2
    sin_lo = jnp.where(lane < half, -sin, 0.0)
    sin_hi = jnp.where((lane >= half) & (lane < ROT_DIM), sin, 0.0)

    def rotate(z):
        cols = []
        for c in range(z.shape[1] // LANES):
            zc = z[:, c * LANES:(c + 1) * LANES]
            up = pltpu.roll(zc, LANES - half, axis=1)
            dn = pltpu.roll(zc, half, axis=1)
            cols.append(zc * cos + up * sin_lo + dn * sin_hi)
        return jnp.concatenate(cols, axis=1)

    def proj(lo, hi):
        return jnp.dot(xb, w_ref[:, lo:hi], preferred_element_type=F32)

    s0, s1, s2, s3, s4, s5 = splits
    q_ref[...] = (rotate(proj(0, s0)) * (HEAD_DIM ** -0.5)).astype(BF16)
    k_ref[...] = rotate(proj(s0, s1)).astype(BF16)
    v_ref[...] = proj(s1, s2).astype(BF16)
    u_ref[...] = proj(s2, s3).astype(BF16)
    ga_ref[...] = _sigmoid(proj(s3, s4)).astype(BF16)
    gs_ref[...] = _sigmoid(proj(s4, s5)).astype(BF16)


def _in_proj(x1, positions, w_in, *, ssm_width, tm=512):
    t, d = x1.shape
    splits = (ATTN_WIDTH, 2 * ATTN_WIDTH, 3 * ATTN_WIDTH, 3 * ATTN_WIDTH + ssm_width,
              3 * ATTN_WIDTH + ssm_width + d, 3 * ATTN_WIDTH + ssm_width + 2 * d)
    assert splits[-1] == w_in.shape[1]
    half = ROT_DIM // 2
    inv_freq = ROPE_THETA ** (-jnp.arange(half, dtype=F32) * 2.0 / ROT_DIM)
    lane = jnp.arange(LANES) % HEAD_DIM
    invf = jnp.where(lane < ROT_DIM, inv_freq[lane % half], 0.0).reshape(1, LANES).astype(F32)
    widths = (ATTN_WIDTH, ATTN_WIDTH, ATTN_WIDTH, ssm_width, d, d)
    kern = functools.partial(_in_proj_kernel, splits=splits)
    return pl.pallas_call(
        kern,
        grid=(t // tm,),
        in_specs=[
            pl.BlockSpec((tm, d), lambda i: (i, 0)),
            pl.BlockSpec((tm, 1), lambda i: (i, 0)),
            _resident((1, LANES)),
            _resident(w_in.shape),
        ],
        out_specs=[pl.BlockSpec((tm, w), lambda i: (i, 0)) for w in widths],
        out_shape=[jax.ShapeDtypeStruct((t, w), BF16) for w in widths],
        compiler_params=_params("parallel"),
        name="in_proj",
    )(x1, positions.reshape(t, 1), invf, w_in.astype(BF16))


def _attn_kernel(q_ref, kc_ref, kp_ref, vc_ref, vp_ref, o_ref, st_ref, *, n_sub):
    first = pl.program_id(2) == 0
    row = lax.broadcasted_iota(jnp.int32, (ATTN_BLK, 2 * ATTN_BLK), 0)
    col = lax.broadcasted_iota(jnp.int32, (ATTN_BLK, 2 * ATTN_BLK), 1)
    band = (col >= row) & (col <= row + ATTN_BLK)
    lane = lax.broadcasted_iota(jnp.int32, (ATTN_BLK, LANES), 1)
    lo_half = lane < HEAD_DIM

    for j in range(n_sub):
        r0 = j * ATTN_BLK
        if j == 0:
            valid = band & ((col >= ATTN_BLK) | jnp.logical_not(first))
        else:
            valid = band
        stats = jnp.zeros((ATTN_BLK, LANES), F32)
        for hp in range(GROUP_WIDTH // LANES):
            c0 = hp * LANES
            q2 = q_ref[r0:r0 + ATTN_BLK, c0:c0 + LANES]
            if j == 0:
                kk = jnp.concatenate([kp_ref[:, c0:c0 + LANES], kc_ref[0:ATTN_BLK, c0:c0 + LANES]], axis=0)
                vv = jnp.concatenate([vp_ref[:, c0:c0 + LANES], vc_ref[0:ATTN_BLK, c0:c0 + LANES]], axis=0)
            else:
                kk = kc_ref[r0 - ATTN_BLK:r0 + ATTN_BLK, c0:c0 + LANES]
                vv = vc_ref[r0 - ATTN_BLK:r0 + ATTN_BLK, c0:c0 + LANES]
            outs = []
            for hh in range(2):
                qm = jnp.where(lo_half if hh == 0 else jnp.logical_not(lo_half), q2, jnp.zeros_like(q2))
                s = lax.dot_general(qm, kk, (((1,), (1,)), ((), ())), preferred_element_type=F32)
                s = jnp.where(valid, s, NEG_INF)
                m = jnp.max(s, axis=1, keepdims=True)
                p = jnp.exp(s - m)
                den = jnp.sum(p, axis=1, keepdims=True)
                o = jnp.dot(p.astype(BF16), vv, preferred_element_type=F32) / den
                outs.append(o)
                h = 2 * hp + hh
                stats = jnp.where(lane == h, m, stats)
                stats = jnp.where(lane == HEADS_PER_GROUP + h, den, stats)
            o_ref[r0:r0 + ATTN_BLK, c0:c0 + LANES] = jnp.where(lo_half, outs[0], outs[1]).astype(BF16)
        st_ref[r0:r0 + ATTN_BLK, :] = stats


def _dilated_attention(q, k, v, group, dilation, *, batch, seq):
    n = seq // dilation
    qb = min(512, n)
    n_sub = qb // ATTN_BLK
    view = lambda a: a.reshape(batch, n, dilation * ATTN_WIDTH)
    cur = pl.BlockSpec((None, qb, GROUP_WIDTH), lambda b, r, i: (b, i, r * N_ATTN_GROUPS + group))
    prev = pl.BlockSpec((None, ATTN_BLK, GROUP_WIDTH),
                        lambda b, r, i: (b, jnp.maximum(i * n_sub - 1, 0), r * N_ATTN_GROUPS + group))
    o, st = pl.pallas_call(
        functools.partial(_attn_kernel, n_sub=n_sub),
        grid=(batch, dilation, n // qb),
        in_specs=[cur, cur, prev, cur, prev],
        out_specs=[pl.BlockSpec((None, qb, GROUP_WIDTH), lambda b, r, i: (b, i, r)),
                   pl.BlockSpec((None, qb, LANES), lambda b, r, i: (b, i, r))],
        out_shape=[jax.ShapeDtypeStruct((batch, n, dilation * GROUP_WIDTH), BF16),
                   jax.ShapeDtypeStruct((batch, n, dilation * LANES), F32)],
        compiler_params=_params("parallel", "parallel", "parallel"),
        name=f"attn_d{dilation}",
    )(view(q), view(k), view(k), view(v), view(v))
    return o.reshape(batch * seq, GROUP_WIDTH), st.reshape(batch * seq, LANES)


def _hdot(a, b):
    return jnp.dot(a, b, preferred_element_type=F32, precision=lax.Precision.HIGHEST)


def _ssm_prep_kernel(are_l, aim_l, are_s, aim_s, ldt_ref, bre_ref, bim_ref, cre_ref, cim_ref,
                     t_ref, wir_ref, wii_ref, wor_ref, woi_ref, ar_ref, ai_ref):
    dt = jnp.exp(ldt_ref[...])

    lr = jnp.minimum(are_l[...], -1e-4) * dt
    li = aim_l[...] * dt
    lam_r = jnp.minimum(are_l[...], -1e-4)
    lam_i = aim_l[...]
    mag = jnp.exp(lr)
    lbr = mag * jnp.cos(li)
    lbi = mag * jnp.sin(li)
    inv = 1.0 / (lam_r * lam_r + lam_i * lam_i)
    coef_r = ((lbr - 1.0) * lam_r + lbi * lam_i) * inv
    coef_i = (lbi * lam_r - (lbr - 1.0) * lam_i) * inv
    bre = bre_ref[...]
    bim = bim_ref[...]
    bbar_r = coef_r * bre - coef_i * bim
    bbar_i = coef_r * bim + coef_i * bre
    s_idx = lax.broadcasted_iota(jnp.int32, bre.shape, 0) // SSM_GROUP
    rem = (SSM_CHUNK - 1 - s_idx).astype(F32)
    pm = jnp.exp(rem * lr)
    pw_r = pm * jnp.cos(rem * li)
    pw_i = pm * jnp.sin(rem * li)
    wir_ref[...] = (pw_r * bbar_r - pw_i * bbar_i).astype(BF16)
    wii_ref[...] = (pw_r * bbar_i + pw_i * bbar_r).astype(BF16)
    am = jnp.exp(SSM_CHUNK * lr)
    ar_ref[...] = am * jnp.cos(SSM_CHUNK * li)
    ai_ref[...] = am * jnp.sin(SSM_CHUNK * li)

    dts = dt
    lrs = jnp.minimum(are_s[...], -1e-4) * dts
    lis = aim_s[...] * dts
    cre = cre_ref[...]
    cim = cim_ref[...]
    tau = (lax.broadcasted_iota(jnp.int32, cre.shape, 1) // SSM_GROUP).astype(F32)
    em = jnp.exp(tau * lrs)
    e_r = em * jnp.cos(tau * lis)
    e_i = em * jnp.sin(tau * lis)
    q_r = e_r * cre - e_i * cim
    q_i = e_r * cim + e_i * cre
    mags = jnp.exp(lrs)
    lbr_s = mags * jnp.cos(lis)
    lbi_s = mags * jnp.sin(lis)
    wor_ref[...] = (q_r * lbr_s - q_i * lbi_s).astype(BF16)
    woi_ref[...] = (-(q_r * lbi_s + q_i * lbr_s)).astype(BF16)

    g = _hdot(bbar_r[0:SSM_GROUP, :], q_r) - _hdot(bbar_i[0:SSM_GROUP, :], q_i)
    lane = lax.broadcasted_iota(jnp.int32, g.shape, 1)
    blocks = [g]
    for s in range(1, SSM_CHUNK):
        sh = s * SSM_GROUP
        blocks.append(jnp.where(lane >= sh, pltpu.roll(g, sh, axis=1), 0.0))
    t_ref[...] = jnp.concatenate(blocks, axis=0).astype(BF16)


def _ssm_prep(a_re, a_im, log_dt, b_re, b_im, c_re, c_im):
    g, p = a_re.shape
    row = SSM_ROW
    per_group = lambda shape: pl.BlockSpec((None,) + shape, lambda i: (i, 0, 0))
    b_t = lambda b: jnp.tile(jnp.swapaxes(b, 1, 2), (1, SSM_CHUNK, 1))
    c_t = lambda c: jnp.tile(jnp.swapaxes(c, 1, 2), (1, 1, SSM_CHUNK))
    out_shapes = [
        jax.ShapeDtypeStruct((g, row, row), BF16),
        jax.ShapeDtypeStruct((g, row, p), BF16), jax.ShapeDtypeStruct((g, row, p), BF16),
        jax.ShapeDtypeStruct((g, p, row), BF16), jax.ShapeDtypeStruct((g, p, row), BF16),
        jax.ShapeDtypeStruct((g, 1, p), F32), jax.ShapeDtypeStruct((g, 1, p), F32),
    ]
    return pl.pallas_call(
        _ssm_prep_kernel,
        grid=(g,),
        in_specs=[per_group((1, p)), per_group((1, p)), per_group((p, 1)), per_group((p, 1)),
                  per_group((1, 1)),
                  per_group((row, p)), per_group((row, p)), per_group((p, row)), per_group((p, row))],
        out_specs=[per_group(s.shape[1:]) for s in out_shapes],
        out_shape=out_shapes,
        compiler_params=_params("parallel"),
        name="ssm_prep",
    )(a_re.reshape(g, 1, p), a_im.reshape(g, 1, p), a_re.reshape(g, p, 1), a_im.reshape(g, p, 1),
      log_dt.reshape(g, 1, 1), b_t(b_re), b_t(b_im), c_t(c_re), c_t(c_im))


def _ssm_kernel(u_ref, t_ref, wir_ref, wii_ref, wor_ref, woi_ref, ar_ref, ai_ref, dsk_ref,
                y_ref, slr, sli, spr, spi, *, gb, batch, n_chunk):
    rows = batch * n_chunk
    for g in range(gb):
        ug = u_ref[g]
        slr[g * rows:(g + 1) * rows, :] = jnp.dot(ug, wir_ref[g], preferred_element_type=F32)
        sli[g * rows:(g + 1) * rows, :] = jnp.dot(ug, wii_ref[g], preferred_element_type=F32)

    nchain = gb * batch
    p = ar_ref.shape[-1]
    a_r = jnp.concatenate([jnp.broadcast_to(ar_ref[g], (batch, p)) for g in range(gb)], axis=0)
    a_i = jnp.concatenate([jnp.broadcast_to(ai_ref[g], (batch, p)) for g in range(gb)], axis=0)

    def step(k, carry):
        s_r, s_i = carry
        idx = pl.ds(k, nchain, stride=n_chunk)
        spr[idx, :] = s_r
        spi[idx, :] = s_i
        n_r = a_r * s_r - a_i * s_i + slr[idx, :]
        n_i = a_r * s_i + a_i * s_r + sli[idx, :]
        return n_r, n_i

    zero = jnp.zeros((nchain, p), F32)
    lax.fori_loop(0, n_chunk, step, (zero, zero))

    for g in range(gb):
        ug = u_ref[g]
        y = jnp.dot(ug, t_ref[g], preferred_element_type=F32)
        y += jnp.dot(spr[g * rows:(g + 1) * rows, :].astype(BF16), wor_ref[g], preferred_element_type=F32)
        y += jnp.dot(spi[g * rows:(g + 1) * rows, :].astype(BF16), woi_ref[g], preferred_element_type=F32)
        y += dsk_ref[g] * ug.astype(F32)
        y_ref[g] = _gelu_tanh(y).astype(BF16)


def _ssm(u_g, prep, dsk, *, batch, gb=4):
    t_mat, wir, wii, wor, woi, a_r, a_i = prep
    g, rows, row = u_g.shape
    n_chunk = rows // batch
    p = a_r.shape[-1]
    blk = lambda shape: pl.BlockSpec((gb,) + shape, lambda i: (i, 0, 0))
    kern = functools.partial(_ssm_kernel, gb=gb, batch=batch, n_chunk=n_chunk)
    return pl.pallas_call(
        kern,
        grid=(g // gb,),
        in_specs=[blk((rows, row)), blk((row, row)), blk((row, p)), blk((row, p)),
                  blk((p, row)), blk((p, row)), blk((1, p)), blk((1, p)), blk((1, row))],
        out_specs=blk((rows, row)),
        out_shape=jax.ShapeDtypeStruct((g, rows, row), BF16),
        scratch_shapes=[pltpu.VMEM((gb * rows, p), F32) for _ in range(4)],
        compiler_params=_params("parallel"),
        name="ssm",
    )(u_g, t_mat, wir, wii, wor, woi, a_r, a_i, dsk)


def _mix_out_kernel(x_ref, o0, o1, o2, st0, st1, st2, y_ref, ga_ref, gs_ref,
                    wao_ref, wglu_ref, bglu_ref, wso_ref, wo_ref, g_ref, b_ref, out_ref):
    tm = x_ref.shape[0]
    sts = [st0[...], st1[...], st2[...]]
    outs = [o0, o1, o2]
    lane = lax.broadcasted_iota(jnp.int32, (tm, LANES), 1)
    lo_half = lane < HEAD_DIM

    def head_cols(vals):
        cols = []
        for hp in range(GROUP_WIDTH // LANES):
            cols.append(jnp.where(lo_half, vals[2 * hp], vals[2 * hp + 1]))
        return jnp.concatenate(cols, axis=1)

    ms = [[st[:, h:h + 1] for h in range(HEADS_PER_GROUP)] for st in sts]
    dens = [[st[:, HEADS_PER_GROUP + h:HEADS_PER_GROUP + h + 1] for h in range(HEADS_PER_GROUP)] for st in sts]
    wts = [[None] * HEADS_PER_GROUP for _ in sts]
    for h in range(HEADS_PER_GROUP):
        mx = jnp.maximum(jnp.maximum(ms[0][h], ms[1][h]), ms[2][h])
        w = [dens[g][h] * jnp.exp(ms[g][h] - mx) for g in range(N_ATTN_GROUPS)]
        tot = w[0] + w[1] + w[2]
        for g in range(N_ATTN_GROUPS):
            wts[g][h] = w[g] / tot
    att = None
    for g in range(N_ATTN_GROUPS):
        term = head_cols(wts[g]) * outs[g][...].astype(F32)
        att = term if att is None else att + term
    y_attn = jnp.dot(att.astype(BF16), wao_ref[...], preferred_element_type=F32)

    y = y_ref[...]
    gate = jnp.dot(y, wglu_ref[...], preferred_element_type=F32) + bglu_ref[...]
    glu = (y.astype(F32) * _sigmoid(gate)).astype(BF16)
    y_ssm = jnp.dot(glu, wso_ref[...], preferred_element_type=F32)

    merged = ga_ref[...].astype(F32) * y_attn + gs_ref[...].astype(F32) * y_ssm
    mix = jnp.dot(merged.astype(BF16), wo_ref[...], preferred_element_type=F32)
    out_ref[...] = _layer_norm(DEEPNORM_ALPHA * x_ref[...] + mix, g_ref[...], b_ref[...])


def _mix_out(x1, os_, sts, y, ga, gs, w_attn_out, w_glu, b_glu, w_ssm_out, w_o, g, b, *, tm=512):
    t, d = x1.shape
    sw = y.shape[1]
    tile = lambda w: pl.BlockSpec((tm, w), lambda i: (i, 0))
    return pl.pallas_call(
        _mix_out_kernel,
        grid=(t // tm,),
        in_specs=[tile(d)] + [tile(GROUP_WIDTH)] * 3 + [tile(LANES)] * 3 + [tile(sw), tile(d), tile(d),
                  _resident(w_attn_out.shape), _resident(w_glu.shape), _resident((1, sw)),
                  _resident(w_ssm_out.shape), _resident(w_o.shape), _resident((1, d)), _resident((1, d))],
        out_specs=tile(d),
        out_shape=jax.ShapeDtypeStruct((t, d), F32),
        compiler_params=_params("parallel"),
        name="mix_out",
    )(x1, *os_, *sts, y, ga, gs, w_attn_out.astype(BF16), w_glu.astype(BF16), b_glu.reshape(1, sw),
      w_ssm_out.astype(BF16), w_o.astype(BF16), g.reshape(1, d), b.reshape(1, d))


def _layer(x, positions, w_in, w_attn_out, a_re, a_im, log_dt, b_re, b_im, c_re, c_im, d_skip,
           w_glu, b_glu, w_ssm_out, w_o, ffn1, ffn2, ln1, ln2, ln3):
    batch, seq, d = x.shape
    t = batch * seq
    n_grp = a_re.shape[0]
    ssm_width = n_grp * SSM_GROUP
    n_chunk = seq // SSM_CHUNK

    x1 = _ffn_ln(x.reshape(t, d), *ffn1, *ln1)
    q, k, v, u, ga, gs = _in_proj(x1, positions, w_in, ssm_width=ssm_width)

    os_, sts = [], []
    for gi, (window, dilation) in enumerate(ATTN_PATTERNS):
        assert window == ATTN_BLK * dilation
        o, st = _dilated_attention(q, k, v, gi, dilation, batch=batch, seq=seq)
        os_.append(o)
        sts.append(st)

    prep = _ssm_prep(a_re, a_im, log_dt, b_re, b_im, c_re, c_im)
    u_g = (u.reshape(batch, n_chunk, SSM_CHUNK, n_grp, SSM_GROUP)
           .transpose(3, 0, 1, 2, 4).reshape(n_grp, batch * n_chunk, SSM_ROW))
    dsk = jnp.tile(d_skip.reshape(n_grp, 1, SSM_GROUP), (1, 1, SSM_CHUNK))
    y_g = _ssm(u_g, prep, dsk, batch=batch)
    y = (y_g.reshape(n_grp, batch, n_chunk, SSM_CHUNK, SSM_GROUP)
         .transpose(1, 2, 3, 0, 4).reshape(t, ssm_width))

    x2 = _mix_out(x1, os_, sts, y, ga, gs, w_attn_out, w_glu, b_glu, w_ssm_out, w_o, *ln2)
    x3 = _ffn_ln(x2, *ffn2, *ln3)
    return x3.reshape(batch, seq, d)


def kernel(x, positions, w_in, w_attn_out, a_re, a_im, log_dt, b_re, b_im, c_re, c_im, d_skip, w_glu, b_glu, w_ssm_out, w_o, ffn1_wg, ffn1_wu, ffn1_wd, ffn2_wg, ffn2_wu, ffn2_wd, ln1_g, ln1_b, ln2_g, ln2_b, ln3_g, ln3_b):
    depth = w_in.shape[0]
    for i in range(depth):
        x = _layer(x, positions, w_in[i], w_attn_out[i], a_re[i], a_im[i], log_dt[i], b_re[i], b_im[i],
                   c_re[i], c_im[i], d_skip[i], w_glu[i], b_glu[i], w_ssm_out[i], w_o[i],
                   (ffn1_wg[i], ffn1_wu[i], ffn1_wd[i]), (ffn2_wg[i], ffn2_wu[i], ffn2_wd[i]),
                   (ln1_g[i], ln1_b[i]), (ln2_g[i], ln2_b[i]), (ln3_g[i], ln3_b[i]))
    return x
```

```python
import functools

import jax
import jax.numpy as jnp
from jax import lax
from jax.experimental import pallas as pl
from jax.experimental.pallas import tpu as pltpu

F32 = jnp.float32
BF16 = jnp.bfloat16

HEAD_DIM = 64
HEADS_PER_GROUP = 4
GROUP_WIDTH = HEADS_PER_GROUP * HEAD_DIM
ATTN_PATTERNS = ((128, 1), (512, 4), (2048, 16))
N_ATTN_GROUPS = len(ATTN_PATTERNS)
ATTN_WIDTH = N_ATTN_GROUPS * GROUP_WIDTH
ROT_DIM = HEAD_DIM // 4
ROPE_THETA = 500000.0
ATTN_BLK = 128
NEG_INF = -1e30
SSM_GROUP = 16
SSM_STATE = 64
SSM_CHUNK = 16
SSM_ROW = SSM_CHUNK * SSM_GROUP
DEEPNORM_ALPHA = 2.0 ** 0.25
LN_EPS = 1e-5

LANES = 128
VMEM_LIMIT_BYTES = 56 * 1024 * 1024


def _params(*semantics):
    return pltpu.CompilerParams(dimension_semantics=semantics, vmem_limit_bytes=VMEM_LIMIT_BYTES)


def _resident(shape):
    zeros = (0,) * len(shape)
    return pl.BlockSpec(shape, lambda *_: zeros, pipeline_mode=pl.Buffered(1))


def _layer_norm(y, g, b):
    mu = jnp.mean(y, axis=-1, keepdims=True)
    yc = y - mu
    var = jnp.mean(yc * yc, axis=-1, keepdims=True)
    return yc * lax.rsqrt(var + LN_EPS) * g + b


def _sigmoid(x):
    return 1.0 / (1.0 + jnp.exp(-x))


def _gelu_tanh(x):
    c = 0.7978845608028654
    return 0.5 * x * (1.0 + jnp.tanh(c * (x + 0.044715 * (x * x * x))))


def _ffn_ln_kernel(x_ref, wg_ref, wu_ref, wd_ref, g_ref, b_ref, o_ref, *, chunks):
    x = x_ref[...]
    xb = x.astype(BF16)
    acc = None
    for c0, c1 in chunks:
        gate = jnp.dot(xb, wg_ref[:, c0:c1], preferred_element_type=F32)
        up = jnp.dot(xb, wu_ref[:, c0:c1], preferred_element_type=F32)
        h = (gate * _sigmoid(gate) * up).astype(BF16)
        part = jnp.dot(h, wd_ref[c0:c1, :], preferred_element_type=F32)
        acc = part if acc is None else acc + part
    y = DEEPNORM_ALPHA * x + 0.5 * acc
    o_ref[...] = _layer_norm(y, g_ref[...], b_ref[...])


def _ffn_chunks(d_ff, width):
    edges = list(range(0, d_ff, width)) + [d_ff]
    return tuple(zip(edges[:-1], edges[1:]))


def _ffn_ln(x, wg, wu, wd, g, b, *, tm=512, chunk=1024):
    t, d = x.shape
    d_ff = wg.shape[1]
    kern = functools.partial(_ffn_ln_kernel, chunks=_ffn_chunks(d_ff, chunk))
    return pl.pallas_call(
        kern,
        grid=(t // tm,),
        in_specs=[
            pl.BlockSpec((tm, d), lambda i: (i, 0)),
            _resident((d, d_ff)), _resident((d, d_ff)), _resident((d_ff, d)),
            _resident((1, d)), _resident((1, d)),
        ],
        out_specs=pl.BlockSpec((tm, d), lambda i: (i, 0)),
        out_shape=jax.ShapeDtypeStruct((t, d), F32),
        compiler_params=_params("parallel"),
        name="ffn_ln",
    )(x, wg.astype(BF16), wu.astype(BF16), wd.astype(BF16), g.reshape(1, d), b.reshape(1, d))


def _in_proj_kernel(x_ref, pos_ref, invf_ref, w_ref,
                    qkv0_ref, qkv1_ref, qkv2_ref, u_ref, ga_ref, gs_ref, stage_ref, *, splits):
    xb = x_ref[...].astype(BF16)
    tm = xb.shape[0]

    ang = pos_ref[...].astype(F32) * invf_ref[...]
    cos = jnp.cos(ang)
    sin = jnp.sin(ang)
    lane = lax.broadcasted_iota(jnp.int32, (tm, LANES), 1) % HEAD_DIM
    half = ROT_DIM // 2
    sin_lo = jnp.where(lane < half, -sin, 0.0)
    sin_hi = jnp.where((lane >= half) & (lane < ROT_DIM), sin, 0.0)

    def rotate(z):
        cols = []
        for c in range(z.shape[1] // LANES):
            zc = z[:, c * LANES:(c + 1) * LANES]
            up = pltpu.roll(zc, LANES - half, axis=1)
            dn = pltpu.roll(zc, half, axis=1)
            cols.append(zc * cos + up * sin_lo + dn * sin_hi)
        return jnp.concatenate(cols, axis=1)

    def proj(lo, hi):
        return jnp.dot(xb, w_ref[:, lo:hi], preferred_element_type=F32)

    s0, s1, s2, s3, s4, s5 = splits
    q = rotate(proj(0, s0)) * (HEAD_DIM ** -0.5)
    k = rotate(proj(s0, s1))
    v = proj(s1, s2)
    for gi, out_ref in enumerate((qkv0_ref, qkv1_ref, qkv2_ref)):
        dil = ATTN_PATTERNS[gi][1]
        c0 = gi * GROUP_WIDTH
        qkv = jnp.concatenate([a[:, c0:c0 + GROUP_WIDTH] for a in (q, k, v)], axis=1)
        if dil == 1:
            out_ref[0] = qkv.astype(BF16)
        else:
            n_col = qkv.shape[1] // LANES
            for c in range(n_col):
                stage_ref[c] = qkv[:, c * LANES:(c + 1) * LANES]
            for r in range(dil):
                rows = pl.ds(r, tm // dil, stride=dil)
                out_ref[r] = jnp.concatenate([stage_ref[c, rows, :] for c in range(n_col)], axis=1).astype(BF16)
    u_ref[...] = proj(s2, s3).astype(BF16)
    ga_ref[...] = _sigmoid(proj(s3, s4)).astype(BF16)
    gs_ref[...] = _sigmoid(proj(s4, s5)).astype(BF16)


def _in_proj(x1, positions, w_in, *, batch, seq, ssm_width, tm=512):
    t, d = x1.shape
    splits = (ATTN_WIDTH, 2 * ATTN_WIDTH, 3 * ATTN_WIDTH, 3 * ATTN_WIDTH + ssm_width,
              3 * ATTN_WIDTH + ssm_width + d, 3 * ATTN_WIDTH + ssm_width + 2 * d)
    assert splits[-1] == w_in.shape[1] and seq % tm == 0
    half = ROT_DIM // 2
    inv_freq = ROPE_THETA ** (-jnp.arange(half, dtype=F32) * 2.0 / ROT_DIM)
    lane = jnp.arange(LANES) % HEAD_DIM
    invf = jnp.where(lane < ROT_DIM, inv_freq[lane % half], 0.0).reshape(1, LANES).astype(F32)
    tiles = seq // tm
    qkv_w = 3 * GROUP_WIDTH
    qkv_specs = [pl.BlockSpec((None, dil, tm // dil, qkv_w), lambda i: (i // tiles, 0, i % tiles, 0))
                 for _, dil in ATTN_PATTERNS]
    qkv_shapes = [jax.ShapeDtypeStruct((batch, dil, seq // dil, qkv_w), BF16) for _, dil in ATTN_PATTERNS]
    widths = (ssm_width, d, d)
    kern = functools.partial(_in_proj_kernel, splits=splits)
    return pl.pallas_call(
        kern,
        grid=(t // tm,),
        in_specs=[
            pl.BlockSpec((tm, d), lambda i: (i, 0)),
            pl.BlockSpec((tm, 1), lambda i: (i, 0)),
            _resident((1, LANES)),
            _resident(w_in.shape),
        ],
        out_specs=qkv_specs + [pl.BlockSpec((tm, w), lambda i: (i, 0)) for w in widths],
        out_shape=qkv_shapes + [jax.ShapeDtypeStruct((t, w), BF16) for w in widths],
        scratch_shapes=[pltpu.VMEM((qkv_w // LANES, tm, LANES), F32)],
        compiler_params=_params("parallel"),
        name="in_proj",
    )(x1, positions.reshape(t, 1), invf, w_in.astype(BF16))


def _attn_kernel(q_ref, kc_ref, kp_ref, vc_ref, vp_ref, o_ref, st_ref, *, n_sub):
    first = pl.program_id(2) == 0
    row = lax.broadcasted_iota(jnp.int32, (ATTN_BLK, 2 * ATTN_BLK), 0)
    col = lax.broadcasted_iota(jnp.int32, (ATTN_BLK, 2 * ATTN_BLK), 1)
    band = (col >= row) & (col <= row + ATTN_BLK)
    lane = lax.broadcasted_iota(jnp.int32, (ATTN_BLK, LANES), 1)
    lo_half = lane < HEAD_DIM

    for j in range(n_sub):
        r0 = j * ATTN_BLK
        if j == 0:
            valid = band & ((col >= ATTN_BLK) | jnp.logical_not(first))
        else:
            valid = band
        stats = jnp.zeros((ATTN_BLK, LANES), F32)
        for hp in range(GROUP_WIDTH // LANES):
            c0 = hp * LANES
            q2 = q_ref[r0:r0 + ATTN_BLK, c0:c0 + LANES]
            if j == 0:
                kk = jnp.concatenate([kp_ref[:, c0:c0 + LANES], kc_ref[0:ATTN_BLK, c0:c0 + LANES]], axis=0)
                vv = jnp.concatenate([vp_ref[:, c0:c0 + LANES], vc_ref[0:ATTN_BLK, c0:c0 + LANES]], axis=0)
            else:
                kk = kc_ref[r0 - ATTN_BLK:r0 + ATTN_BLK, c0:c0 + LANES]
                vv = vc_ref[r0 - ATTN_BLK:r0 + ATTN_BLK, c0:c0 + LANES]
            outs = []
            for hh in range(2):
                qm = jnp.where(lo_half if hh == 0 else jnp.logical_not(lo_half), q2, jnp.zeros_like(q2))
                s = lax.dot_general(qm, kk, (((1,), (1,)), ((), ())), preferred_element_type=F32)
                s = jnp.where(valid, s, NEG_INF)
                m = jnp.max(s, axis=1, keepdims=True)
                p = jnp.exp(s - m)
                den = jnp.sum(p, axis=1, keepdims=True)
                o = jnp.dot(p.astype(BF16), vv, preferred_element_type=F32) / den
                outs.append(o)
                h = 2 * hp + hh
                stats = jnp.where(lane == h, m, stats)
                stats = jnp.where(lane == HEADS_PER_GROUP + h, den, stats)
            o_ref[r0:r0 + ATTN_BLK, c0:c0 + LANES] = jnp.where(lo_half, outs[0], outs[1]).astype(BF16)
        st_ref[r0:r0 + ATTN_BLK, :] = stats


def _dilated_attention(qkv):
    batch, dilation, n, _ = qkv.shape
    qb = min(512, n)
    n_sub = qb // ATTN_BLK

    def cur(part):
        return pl.BlockSpec((None, None, qb, GROUP_WIDTH), lambda b, r, i: (b, r, i, part))

    def prev(part):
        return pl.BlockSpec((None, None, ATTN_BLK, GROUP_WIDTH),
                            lambda b, r, i: (b, r, jnp.maximum(i * n_sub - 1, 0), part))

    return pl.pallas_call(
        functools.partial(_attn_kernel, n_sub=n_sub),
        grid=(batch, dilation, n // qb),
        in_specs=[cur(0), cur(1), prev(1), cur(2), prev(2)],
        out_specs=[pl.BlockSpec((None, None, qb, GROUP_WIDTH), lambda b, r, i: (b, r, i, 0)),
                   pl.BlockSpec((None, None, qb, LANES), lambda b, r, i: (b, r, i, 0))],
        out_shape=[jax.ShapeDtypeStruct((batch, dilation, n, GROUP_WIDTH), BF16),
                   jax.ShapeDtypeStruct((batch, dilation, n, LANES), F32)],
        compiler_params=_params("parallel", "parallel", "parallel"),
        name=f"attn_d{dilation}",
    )(qkv, qkv, qkv, qkv, qkv)


def _hdot(a, b):
    return jnp.dot(a, b, preferred_element_type=F32, precision=lax.Precision.HIGHEST)


def _ssm_prep_kernel(are_l, aim_l, are_s, aim_s, ldt_ref, bre_ref, bim_ref, cre_ref, cim_ref,
                     t_ref, wir_ref, wii_ref, wor_ref, woi_ref, ar_ref, ai_ref):
    dt = jnp.exp(ldt_ref[...])

    lr = jnp.minimum(are_l[...], -1e-4) * dt
    li = aim_l[...] * dt
    lam_r = jnp.minimum(are_l[...], -1e-4)
    lam_i = aim_l[...]
    mag = jnp.exp(lr)
    lbr = mag * jnp.cos(li)
    lbi = mag * jnp.sin(li)
    inv = 1.0 / (lam_r * lam_r + lam_i * lam_i)
    coef_r = ((lbr - 1.0) * lam_r + lbi * lam_i) * inv
    coef_i = (lbi * lam_r - (lbr - 1.0) * lam_i) * inv
    bre = bre_ref[...]
    bim = bim_ref[...]
    bbar_r = coef_r * bre - coef_i * bim
    bbar_i = coef_r * bim + coef_i * bre
    s_idx = lax.broadcasted_iota(jnp.int32, bre.shape, 0) // SSM_GROUP
    rem = (SSM_CHUNK - 1 - s_idx).astype(F32)
    pm = jnp.exp(rem * lr)
    pw_r = pm * jnp.cos(rem * li)
    pw_i = pm * jnp.sin(rem * li)
    wir_ref[...] = (pw_r * bbar_r - pw_i * bbar_i).astype(BF16)
    wii_ref[...] = (pw_r * bbar_i + pw_i * bbar_r).astype(BF16)
    am = jnp.exp(SSM_CHUNK * lr)
    ar_ref[...] = am * jnp.cos(SSM_CHUNK * li)
    ai_ref[...] = am * jnp.sin(SSM_CHUNK * li)

    dts = dt
    lrs = jnp.minimum(are_s[...], -1e-4) * dts
    lis = aim_s[...] * dts
    cre = cre_ref[...]
    cim = cim_ref[...]
    tau = (lax.broadcasted_iota(jnp.int32, cre.shape, 1) // SSM_GROUP).astype(F32)
    em = jnp.exp(tau * lrs)
    e_r = em * jnp.cos(tau * lis)
    e_i = em * jnp.sin(tau * lis)
    q_r = e_r * cre - e_i * cim
    q_i = e_r * cim + e_i * cre
    mags = jnp.exp(lrs)
    lbr_s = mags * jnp.cos(lis)
    lbi_s = mags * jnp.sin(lis)
    wor_ref[...] = (q_r * lbr_s - q_i * lbi_s).astype(BF16)
    woi_ref[...] = (-(q_r * lbi_s + q_i * lbr_s)).astype(BF16)

    g = _hdot(bbar_r[0:SSM_GROUP, :], q_r) - _hdot(bbar_i[0:SSM_GROUP, :], q_i)
    lane = lax.broadcasted_iota(jnp.int32, g.shape, 1)
    blocks = [g]
    for s in range(1, SSM_CHUNK):
        sh = s * SSM_GROUP
        blocks.append(jnp.where(lane >= sh, pltpu.roll(g, sh, axis=1), 0.0))
    t_ref[...] = jnp.concatenate(blocks, axis=0).astype(BF16)


def _ssm_prep(a_re, a_im, log_dt, b_re, b_im, c_re, c_im):
    g, p = a_re.shape
    row = SSM_ROW
    per_group = lambda shape: pl.BlockSpec((None,) + shape, lambda i: (i, 0, 0))
    b_t = lambda b: jnp.tile(jnp.swapaxes(b, 1, 2), (1, SSM_CHUNK, 1))
    c_t = lambda c: jnp.tile(jnp.swapaxes(c, 1, 2), (1, 1, SSM_CHUNK))
    out_shapes = [
        jax.ShapeDtypeStruct((g, row, row), BF16),
        jax.ShapeDtypeStruct((g, row, p), BF16), jax.ShapeDtypeStruct((g, row, p), BF16),
        jax.ShapeDtypeStruct((g, p, row), BF16), jax.ShapeDtypeStruct((g, p, row), BF16),
        jax.ShapeDtypeStruct((g, 1, p), F32), jax.ShapeDtypeStruct((g, 1, p), F32),
    ]
    return pl.pallas_call(
        _ssm_prep_kernel,
        grid=(g,),
        in_specs=[per_group((1, p)), per_group((1, p)), per_group((p, 1)), per_group((p, 1)),
                  per_group((1, 1)),
                  per_group((row, p)), per_group((row, p)), per_group((p, row)), per_group((p, row))],
        out_specs=[per_group(s.shape[1:]) for s in out_shapes],
        out_shape=out_shapes,
        compiler_params=_params("parallel"),
        name="ssm_prep",
    )(a_re.reshape(g, 1, p), a_im.reshape(g, 1, p), a_re.reshape(g, p, 1), a_im.reshape(g, p, 1),
      log_dt.reshape(g, 1, 1), b_t(b_re), b_t(b_im), c_t(c_re), c_t(c_im))


def _ssm_kernel(u_ref, t_ref, wir_ref, wii_ref, wor_ref, woi_ref, ar_ref, ai_ref, dsk_ref,
                y_ref, slr, sli, spr, spi, *, gb, batch, n_chunk):
    rows = batch * n_chunk
    for g in range(gb):
        ug = u_ref[g]
        slr[g * rows:(g + 1) * rows, :] = jnp.dot(ug, wir_ref[g], preferred_element_type=F32)
        sli[g * rows:(g + 1) * rows, :] = jnp.dot(ug, wii_ref[g], preferred_element_type=F32)

    nchain = gb * batch
    p = ar_ref.shape[-1]
    a_r = jnp.concatenate([jnp.broadcast_to(ar_ref[g], (batch, p)) for g in range(gb)], axis=0)
    a_i = jnp.concatenate([jnp.broadcast_to(ai_ref[g], (batch, p)) for g in range(gb)], axis=0)

    def step(k, carry):
        s_r, s_i = carry
        idx = pl.ds(k, nchain, stride=n_chunk)
        spr[idx, :] = s_r
        spi[idx, :] = s_i
        n_r = a_r * s_r - a_i * s_i + slr[idx, :]
        n_i = a_r * s_i + a_i * s_r + sli[idx, :]
        return n_r, n_i

    zero = jnp.zeros((nchain, p), F32)
    lax.fori_loop(0, n_chunk, step, (zero, zero))

    for g in range(gb):
        ug = u_ref[g]
        y = jnp.dot(ug, t_ref[g], preferred_element_type=F32)
        y += jnp.dot(spr[g * rows:(g + 1) * rows, :].astype(BF16), wor_ref[g], preferred_element_type=F32)
        y += jnp.dot(spi[g * rows:(g + 1) * rows, :].astype(BF16), woi_ref[g], preferred_element_type=F32)
        y += dsk_ref[g] * ug.astype(F32)
        y_ref[g] = _gelu_tanh(y).astype(BF16)


def _ssm(u_g, prep, dsk, *, batch, gb=4):
    t_mat, wir, wii, wor, woi, a_r, a_i = prep
    g, rows, row = u_g.shape
    n_chunk = rows // batch
    p = a_r.shape[-1]
    blk = lambda shape: pl.BlockSpec((gb,) + shape, lambda i: (i, 0, 0))
    kern = functools.partial(_ssm_kernel, gb=gb, batch=batch, n_chunk=n_chunk)
    return pl.pallas_call(
        kern,
        grid=(g // gb,),
        in_specs=[blk((rows, row)), blk((row, row)), blk((row, p)), blk((row, p)),
                  blk((p, row)), blk((p, row)), blk((1, p)), blk((1, p)), blk((1, row))],
        out_specs=blk((rows, row)),
        out_shape=jax.ShapeDtypeStruct((g, rows, row), BF16),
        scratch_shapes=[pltpu.VMEM((gb * rows, p), F32) for _ in range(4)],
        compiler_params=_params("parallel"),
        name="ssm",
    )(u_g, t_mat, wir, wii, wor, woi, a_r, a_i, dsk)


def _mix_out_kernel(x_ref, o0, o1, o2, st0, st1, st2, y_ref, ga_ref, gs_ref,
                    wao_ref, wglu_ref, bglu_ref, wso_ref, wo_ref, g_ref, b_ref, out_ref,
                    o_stage, st_stage):
    tm = x_ref.shape[0]

    def token_order(ref, stage):
        dil = ref.shape[0]
        if dil == 1:
            return ref[0].astype(F32)
        n_col = ref.shape[2] // LANES
        for r in range(dil):
            blk = ref[r].astype(F32)
            for c in range(n_col):
                stage[c, pl.ds(r, tm // dil, stride=dil), :] = blk[:, c * LANES:(c + 1) * LANES]
        return jnp.concatenate([stage[c] for c in range(n_col)], axis=1)

    sts = [token_order(st, st_stage) for st in (st0, st1, st2)]
    outs = [token_order(o, o_stage) for o in (o0, o1, o2)]
    lane = lax.broadcasted_iota(jnp.int32, (tm, LANES), 1)
    lo_half = lane < HEAD_DIM

    def head_cols(vals):
        cols = []
        for hp in range(GROUP_WIDTH // LANES):
            cols.append(jnp.where(lo_half, vals[2 * hp], vals[2 * hp + 1]))
        return jnp.concatenate(cols, axis=1)

    ms = [[st[:, h:h + 1] for h in range(HEADS_PER_GROUP)] for st in sts]
    dens = [[st[:, HEADS_PER_GROUP + h:HEADS_PER_GROUP + h + 1] for h in range(HEADS_PER_GROUP)] for st in sts]
    wts = [[None] * HEADS_PER_GROUP for _ in sts]
    for h in range(HEADS_PER_GROUP):
        mx = jnp.maximum(jnp.maximum(ms[0][h], ms[1][h]), ms[2][h])
        w = [dens[g][h] * jnp.exp(ms[g][h] - mx) for g in range(N_ATTN_GROUPS)]
        tot = w[0] + w[1] + w[2]
        for g in range(N_ATTN_GROUPS):
            wts[g][h] = w[g] / tot
    att = None
    for g in range(N_ATTN_GROUPS):
        term = head_cols(wts[g]) * outs[g]
        att = term if att is None else att + term
    y_attn = jnp.dot(att.astype(BF16), wao_ref[...], preferred_element_type=F32)

    y = y_ref[...]
    gate = jnp.dot(y, wglu_ref[...], preferred_element_type=F32) + bglu_ref[...]
    glu = (y.astype(F32) * _sigmoid(gate)).astype(BF16)
    y_ssm = jnp.dot(glu, wso_ref[...], preferred_element_type=F32)

    merged = ga_ref[...].astype(F32) * y_attn + gs_ref[...].astype(F32) * y_ssm
    mix = jnp.dot(merged.astype(BF16), wo_ref[...], preferred_element_type=F32)
    out_ref[...] = _layer_norm(DEEPNORM_ALPHA * x_ref[...] + mix, g_ref[...], b_ref[...])


def _mix_out(x1, os_, sts, y, ga, gs, w_attn_out, w_glu, b_glu, w_ssm_out, w_o, g, b, *, seq, tm=512):
    t, d = x1.shape
    sw = y.shape[1]
    tiles = seq // tm
    tile = lambda w: pl.BlockSpec((tm, w), lambda i: (i, 0))

    def by_class(a):
        _, dil, n, w = a.shape
        return pl.BlockSpec((None, dil, tm // dil, w), lambda i: (i // tiles, 0, i % tiles, 0))

    return pl.pallas_call(
        _mix_out_kernel,
        grid=(t // tm,),
        in_specs=[tile(d)] + [by_class(a) for a in os_] + [by_class(a) for a in sts] + [tile(sw), tile(d), tile(d),
                  _resident(w_attn_out.shape), _resident(w_glu.shape), _resident((1, sw)),
                  _resident(w_ssm_out.shape), _resident(w_o.shape), _resident((1, d)), _resident((1, d))],
        out_specs=tile(d),
        out_shape=jax.ShapeDtypeStruct((t, d), F32),
        scratch_shapes=[pltpu.VMEM((GROUP_WIDTH // LANES, tm, LANES), F32), pltpu.VMEM((1, tm, LANES), F32)],
        compiler_params=_params("parallel"),
        name="mix_out",
    )(x1, *os_, *sts, y, ga, gs, w_attn_out.astype(BF16), w_glu.astype(BF16), b_glu.reshape(1, sw),
      w_ssm_out.astype(BF16), w_o.astype(BF16), g.reshape(1, d), b.reshape(1, d))


def _layer(x, positions, w_in, w_attn_out, a_re, a_im, log_dt, b_re, b_im, c_re, c_im, d_skip,
           w_glu, b_glu, w_ssm_out, w_o, ffn1, ffn2, ln1, ln2, ln3):
    batch, seq, d = x.shape
    t = batch * seq
    n_grp = a_re.shape[0]
    ssm_width = n_grp * SSM_GROUP
    n_chunk = seq // SSM_CHUNK

    x1 = _ffn_ln(x.reshape(t, d), *ffn1, *ln1)
    *qkvs, u, ga, gs = _in_proj(x1, positions, w_in, batch=batch, seq=seq, ssm_width=ssm_width)

    os_, sts = [], []
    for qkv, (window, dilation) in zip(qkvs, ATTN_PATTERNS):
        assert window == ATTN_BLK * dilation
        o, st = _dilated_attention(qkv)
        os_.append(o)
        sts.append(st)

    prep = _ssm_prep(a_re, a_im, log_dt, b_re, b_im, c_re, c_im)
    u_g = (u.reshape(batch, n_chunk, SSM_CHUNK, n_grp, SSM_GROUP)
           .transpose(3, 0, 1, 2, 4).reshape(n_grp, batch * n_chunk, SSM_ROW))
    dsk = jnp.tile(d_skip.reshape(n_grp, 1, SSM_GROUP), (1, 1, SSM_CHUNK))
    y_g = _ssm(u_g, prep, dsk, batch=batch)
    y = (y_g.reshape(n_grp, batch, n_chunk, SSM_CHUNK, SSM_GROUP)
         .transpose(1, 2, 3, 0, 4).reshape(t, ssm_width))

    x2 = _mix_out(x1, os_, sts, y, ga, gs, w_attn_out, w_glu, b_glu, w_ssm_out, w_o, *ln2, seq=seq)
    x3 = _ffn_ln(x2, *ffn2, *ln3)
    return x3.reshape(batch, seq, d)


def kernel(x, positions, w_in, w_attn_out, a_re, a_im, log_dt, b_re, b_im, c_re, c_im, d_skip, w_glu, b_glu, w_ssm_out, w_o, ffn1_wg, ffn1_wu, ffn1_wd, ffn2_wg, ffn2_wu, ffn2_wd, ln1_g, ln1_b, ln2_g, ln2_b, ln3_g, ln3_b):
    depth = w_in.shape[0]
    for i in range(depth):
        x = _layer(x, positions, w_in[i], w_attn_out[i], a_re[i], a_im[i], log_dt[i], b_re[i], b_im[i],
                   c_re[i], c_im[i], d_skip[i], w_glu[i], b_glu[i], w_ssm_out[i], w_o[i],
                   (ffn1_wg[i], ffn1_wu[i], ffn1_wd[i]), (ffn2_wg[i], ffn2_wu[i], ffn2_wd[i]),
                   (ln1_g[i], ln1_b[i]), (ln2_g[i], ln2_b[i]), (ln3_g[i], ln3_b[i]))
    return x
```

```python
import functools

import jax
import jax.numpy as jnp
from jax import lax
from jax.experimental import pallas as pl
from jax.experimental.pallas import tpu as pltpu

F32 = jnp.float32
BF16 = jnp.bfloat16

HEAD_DIM = 64
HEADS_PER_GROUP = 4
GROUP_WIDTH = HEADS_PER_GROUP * HEAD_DIM
ATTN_PATTERNS = ((128, 1), (512, 4), (2048, 16))
N_ATTN_GROUPS = len(ATTN_PATTERNS)
ATTN_WIDTH = N_ATTN_GROUPS * GROUP_WIDTH
ROT_DIM = HEAD_DIM // 4
ROPE_THETA = 500000.0
ATTN_BLK = 128
NEG_INF = -1e30
SSM_GROUP = 16
SSM_STATE = 64
SSM_CHUNK = 16
SSM_ROW = SSM_CHUNK * SSM_GROUP
SSM_POWERS = 16
DEEPNORM_ALPHA = 2.0 ** 0.25
LN_EPS = 1e-5

LANES = 128
SUBLANES = 8
VMEM_LIMIT_BYTES = 56 * 1024 * 1024


def _params(*semantics):
    return pltpu.CompilerParams(dimension_semantics=semantics, vmem_limit_bytes=VMEM_LIMIT_BYTES)


def _resident(shape):
    zeros = (0,) * len(shape)
    return pl.BlockSpec(shape, lambda *_: zeros, pipeline_mode=pl.Buffered(1))


def _layer_norm(y, g, b):
    mu = jnp.mean(y, axis=-1, keepdims=True)
    yc = y - mu
    var = jnp.mean(yc * yc, axis=-1, keepdims=True)
    return yc * lax.rsqrt(var + LN_EPS) * g + b


def _sigmoid(x):
    return 1.0 / (1.0 + jnp.exp(-x))


def _gelu_tanh(x):
    c = 0.7978845608028654
    return 0.5 * x * (1.0 + jnp.tanh(c * (x + 0.044715 * (x * x * x))))


def _ffn_ln_kernel(x_ref, wg_ref, wu_ref, wd_ref, g_ref, b_ref, o_ref, *, chunks):
    x = x_ref[...]
    xb = x.astype(BF16)
    acc = None
    for c0, c1 in chunks:
        gate = jnp.dot(xb, wg_ref[:, c0:c1], preferred_element_type=F32)
        up = jnp.dot(xb, wu_ref[:, c0:c1], preferred_element_type=F32)
        h = (gate * _sigmoid(gate) * up).astype(BF16)
        part = jnp.dot(h, wd_ref[c0:c1, :], preferred_element_type=F32)
        acc = part if acc is None else acc + part
    y = DEEPNORM_ALPHA * x + 0.5 * acc
    o_ref[...] = _layer_norm(y, g_ref[...], b_ref[...])


def _ffn_chunks(d_ff, width):
    edges = list(range(0, d_ff, width)) + [d_ff]
    return tuple(zip(edges[:-1], edges[1:]))


def _ffn_ln(x, wg, wu, wd, g, b, *, tm=512, chunk=1024):
    t, d = x.shape
    d_ff = wg.shape[1]
    kern = functools.partial(_ffn_ln_kernel, chunks=_ffn_chunks(d_ff, chunk))
    return pl.pallas_call(
        kern,
        grid=(t // tm,),
        in_specs=[
            pl.BlockSpec((tm, d), lambda i: (i, 0)),
            _resident((d, d_ff)), _resident((d, d_ff)), _resident((d_ff, d)),
            _resident((1, d)), _resident((1, d)),
        ],
        out_specs=pl.BlockSpec((tm, d), lambda i: (i, 0)),
        out_shape=jax.ShapeDtypeStruct((t, d), F32),
        compiler_params=_params("parallel"),
        name="ffn_ln",
    )(x, wg.astype(BF16), wu.astype(BF16), wd.astype(BF16), g.reshape(1, d), b.reshape(1, d))


def _in_proj_kernel(x_ref, pos_ref, invf_ref, w_ref,
                    qkv0_ref, qkv1_ref, qkv2_ref, ga_ref, gs_ref, stage_ref, *, splits):
    xb = x_ref[...].astype(BF16)
    tm = xb.shape[0]

    ang = pos_ref[...].astype(F32) * invf_ref[...]
    cos = jnp.cos(ang)
    sin = jnp.sin(ang)
    lane = lax.broadcasted_iota(jnp.int32, (tm, LANES), 1) % HEAD_DIM
    half = ROT_DIM // 2
    sin_lo = jnp.where(lane < half, -sin, 0.0)
    sin_hi = jnp.where((lane >= half) & (lane < ROT_DIM), sin, 0.0)

    def rotate(z):
        cols = []
        for c in range(z.shape[1] // LANES):
            zc = z[:, c * LANES:(c + 1) * LANES]
            up = pltpu.roll(zc, LANES - half, axis=1)
            dn = pltpu.roll(zc, half, axis=1)
            cols.append(zc * cos + up * sin_lo + dn * sin_hi)
        return jnp.concatenate(cols, axis=1)

    def proj(lo, hi):
        return jnp.dot(xb, w_ref[:, lo:hi], preferred_element_type=F32)

    s0, s1, s2, s3, s4, s5 = splits
    q = rotate(proj(0, s0)) * (HEAD_DIM ** -0.5)
    k = rotate(proj(s0, s1))
    v = proj(s1, s2)
    for gi, out_ref in enumerate((qkv0_ref, qkv1_ref, qkv2_ref)):
        dil = ATTN_PATTERNS[gi][1]
        c0 = gi * GROUP_WIDTH
        qkv = jnp.concatenate([a[:, c0:c0 + GROUP_WIDTH] for a in (q, k, v)], axis=1)
        if dil == 1:
            out_ref[0] = qkv.astype(BF16)
        else:
            n_col = qkv.shape[1] // LANES
            for c in range(n_col):
                stage_ref[c] = qkv[:, c * LANES:(c + 1) * LANES]
            for r in range(dil):
                rows = pl.ds(r, tm // dil, stride=dil)
                out_ref[r] = jnp.concatenate([stage_ref[c, rows, :] for c in range(n_col)], axis=1).astype(BF16)
    ga_ref[...] = _sigmoid(proj(s3, s4)).astype(BF16)
    gs_ref[...] = _sigmoid(proj(s4, s5)).astype(BF16)


def _in_proj(x1, positions, w_in, *, batch, seq, ssm_width, tm=512):
    t, d = x1.shape
    splits = (ATTN_WIDTH, 2 * ATTN_WIDTH, 3 * ATTN_WIDTH, 3 * ATTN_WIDTH + ssm_width,
              3 * ATTN_WIDTH + ssm_width + d, 3 * ATTN_WIDTH + ssm_width + 2 * d)
    assert splits[-1] == w_in.shape[1] and seq % tm == 0
    half = ROT_DIM // 2
    inv_freq = ROPE_THETA ** (-jnp.arange(half, dtype=F32) * 2.0 / ROT_DIM)
    lane = jnp.arange(LANES) % HEAD_DIM
    invf = jnp.where(lane < ROT_DIM, inv_freq[lane % half], 0.0).reshape(1, LANES).astype(F32)
    tiles = seq // tm
    qkv_w = 3 * GROUP_WIDTH
    qkv_specs = [pl.BlockSpec((None, dil, tm // dil, qkv_w), lambda i: (i // tiles, 0, i % tiles, 0))
                 for _, dil in ATTN_PATTERNS]
    qkv_shapes = [jax.ShapeDtypeStruct((batch, dil, seq // dil, qkv_w), BF16) for _, dil in ATTN_PATTERNS]
    widths = (d, d)
    kern = functools.partial(_in_proj_kernel, splits=splits)
    return pl.pallas_call(
        kern,
        grid=(t // tm,),
        in_specs=[
            pl.BlockSpec((tm, d), lambda i: (i, 0)),
            pl.BlockSpec((tm, 1), lambda i: (i, 0)),
            _resident((1, LANES)),
            _resident(w_in.shape),
        ],
        out_specs=qkv_specs + [pl.BlockSpec((tm, w), lambda i: (i, 0)) for w in widths],
        out_shape=qkv_shapes + [jax.ShapeDtypeStruct((t, w), BF16) for w in widths],
        scratch_shapes=[pltpu.VMEM((qkv_w // LANES, tm, LANES), F32)],
        compiler_params=_params("parallel"),
        name="in_proj",
    )(x1, positions.reshape(t, 1), invf, w_in.astype(BF16))


def _attn_kernel(q_ref, kc_ref, kp_ref, vc_ref, vp_ref, o_ref, st_ref, *, n_sub):
    first = pl.program_id(2) == 0
    row = lax.broadcasted_iota(jnp.int32, (ATTN_BLK, 2 * ATTN_BLK), 0)
    col = lax.broadcasted_iota(jnp.int32, (ATTN_BLK, 2 * ATTN_BLK), 1)
    band = (col >= row) & (col <= row + ATTN_BLK)
    lane = lax.broadcasted_iota(jnp.int32, (ATTN_BLK, LANES), 1)
    lo_half = lane < HEAD_DIM

    for j in range(n_sub):
        r0 = j * ATTN_BLK
        if j == 0:
            valid = band & ((col >= ATTN_BLK) | jnp.logical_not(first))
        else:
            valid = band
        stats = jnp.zeros((ATTN_BLK, LANES), F32)
        for hp in range(GROUP_WIDTH // LANES):
            c0 = hp * LANES
            q2 = q_ref[r0:r0 + ATTN_BLK, c0:c0 + LANES]
            if j == 0:
                kk = jnp.concatenate([kp_ref[:, c0:c0 + LANES], kc_ref[0:ATTN_BLK, c0:c0 + LANES]], axis=0)
                vv = jnp.concatenate([vp_ref[:, c0:c0 + LANES], vc_ref[0:ATTN_BLK, c0:c0 + LANES]], axis=0)
            else:
                kk = kc_ref[r0 - ATTN_BLK:r0 + ATTN_BLK, c0:c0 + LANES]
                vv = vc_ref[r0 - ATTN_BLK:r0 + ATTN_BLK, c0:c0 + LANES]
            outs = []
            for hh in range(2):
                qm = jnp.where(lo_half if hh == 0 else jnp.logical_not(lo_half), q2, jnp.zeros_like(q2))
                s = lax.dot_general(qm, kk, (((1,), (1,)), ((), ())), preferred_element_type=F32)
                s = jnp.where(valid, s, NEG_INF)
                m = jnp.max(s, axis=1, keepdims=True)
                p = jnp.exp(s - m)
                den = jnp.sum(p, axis=1, keepdims=True)
                o = jnp.dot(p.astype(BF16), vv, preferred_element_type=F32) / den
                outs.append(o)
                h = 2 * hp + hh
                stats = jnp.where(lane == h, m, stats)
                stats = jnp.where(lane == HEADS_PER_GROUP + h, den, stats)
            o_ref[r0:r0 + ATTN_BLK, c0:c0 + LANES] = jnp.where(lo_half, outs[0], outs[1]).astype(BF16)
        st_ref[r0:r0 + ATTN_BLK, :] = stats


def _dilated_attention(qkv):
    batch, dilation, n, _ = qkv.shape
    qb = min(512, n)
    n_sub = qb // ATTN_BLK

    def cur(part):
        return pl.BlockSpec((None, None, qb, GROUP_WIDTH), lambda b, r, i: (b, r, i, part))

    def prev(part):
        return pl.BlockSpec((None, None, ATTN_BLK, GROUP_WIDTH),
                            lambda b, r, i: (b, r, jnp.maximum(i * n_sub - 1, 0), part))

    return pl.pallas_call(
        functools.partial(_attn_kernel, n_sub=n_sub),
        grid=(batch, dilation, n // qb),
        in_specs=[cur(0), cur(1), prev(1), cur(2), prev(2)],
        out_specs=[pl.BlockSpec((None, None, qb, GROUP_WIDTH), lambda b, r, i: (b, r, i, 0)),
                   pl.BlockSpec((None, None, qb, LANES), lambda b, r, i: (b, r, i, 0))],
        out_shape=[jax.ShapeDtypeStruct((batch, dilation, n, GROUP_WIDTH), BF16),
                   jax.ShapeDtypeStruct((batch, dilation, n, LANES), F32)],
        compiler_params=_params("parallel", "parallel", "parallel"),
        name=f"attn_d{dilation}",
    )(qkv, qkv, qkv, qkv, qkv)


def _ssm_in_kernel(x_ref, w_ref, ut_ref, *, n_grp):
    for s in range(x_ref.shape[1]):
        xs = x_ref[:, s, :].astype(BF16)
        ut = lax.dot_general(w_ref[...], xs, (((1,), (1,)), ((), ())), preferred_element_type=F32)
        ut_ref[:, s * SSM_GROUP:(s + 1) * SSM_GROUP, :] = (
            ut.reshape(n_grp, SSM_GROUP, ut.shape[1]).astype(BF16))


def _ssm_in(x1, w_u, *, ch=LANES):
    t, d = x1.shape
    width = w_u.shape[1]
    n_grp = width // SSM_GROUP
    n_chunk_all = t // SSM_CHUNK
    return pl.pallas_call(
        functools.partial(_ssm_in_kernel, n_grp=n_grp),
        grid=(n_chunk_all // ch, SSM_CHUNK // SUBLANES),
        in_specs=[pl.BlockSpec((ch, SUBLANES, d), lambda i, j: (i, j, 0)), _resident((width, d))],
        out_specs=pl.BlockSpec((n_grp, SUBLANES * SSM_GROUP, ch), lambda i, j: (0, j, i)),
        out_shape=jax.ShapeDtypeStruct((n_grp, SSM_ROW, n_chunk_all), BF16),
        compiler_params=_params("parallel", "parallel"),
        name="ssm_in",
    )(x1.reshape(n_chunk_all, SSM_CHUNK, d), w_u.T.astype(BF16))


def _hdot(a, b):
    return jnp.dot(a, b, preferred_element_type=F32, precision=lax.Precision.HIGHEST)


def _ssm_prep_kernel(are_s, aim_s, are_l, aim_l, ldt_ref, bre_ref, bim_ref, cre_ref, cim_ref,
                     tt_ref, win_ref, wout_ref, apr_ref, api_ref):
    p = are_s.shape[0]
    dt = jnp.exp(ldt_ref[...])

    lam_r = jnp.minimum(are_s[...], -1e-4)
    lam_i = aim_s[...]
    lr = lam_r * dt
    li = lam_i * dt
    mag = jnp.exp(lr)
    lbr = mag * jnp.cos(li)
    lbi = mag * jnp.sin(li)
    inv = 1.0 / (lam_r * lam_r + lam_i * lam_i)
    coef_r = ((lbr - 1.0) * lam_r + lbi * lam_i) * inv
    coef_i = (lbi * lam_r - (lbr - 1.0) * lam_i) * inv
    bre = bre_ref[...]
    bim = bim_ref[...]
    bbar_r = coef_r * bre - coef_i * bim
    bbar_i = coef_r * bim + coef_i * bre
    rem = (SSM_CHUNK - 1 - lax.broadcasted_iota(jnp.int32, (p, SSM_CHUNK), 1)).astype(F32)
    pm = jnp.exp(rem * lr)
    pw_r = pm * jnp.cos(rem * li)
    pw_i = pm * jnp.sin(rem * li)
    expand = (lax.broadcasted_iota(jnp.int32, (SSM_CHUNK, SSM_ROW), 1) // SSM_GROUP
              == lax.broadcasted_iota(jnp.int32, (SSM_CHUNK, SSM_ROW), 0)).astype(F32)
    pwx_r = _hdot(pw_r, expand)
    pwx_i = _hdot(pw_i, expand)
    win_r = pwx_r * bbar_r - pwx_i * bbar_i
    win_i = pwx_r * bbar_i + pwx_i * bbar_r
    win_ref[...] = jnp.concatenate([win_r, win_i], axis=0).astype(BF16)

    a_r = pw_r[:, 0:1] * lbr - pw_i[:, 0:1] * lbi
    a_i = pw_r[:, 0:1] * lbi + pw_i[:, 0:1] * lbr
    col = lax.broadcasted_iota(jnp.int32, (p, SSM_POWERS), 1)
    apr = jnp.zeros((p, SSM_POWERS), F32)
    api = jnp.zeros((p, SSM_POWERS), F32)
    for j in range(SSM_POWERS):
        apr = jnp.where(col == j, a_r, apr)
        api = jnp.where(col == j, a_i, api)
        a_r, a_i = a_r * a_r - a_i * a_i, 2.0 * a_r * a_i
    apr_ref[...] = apr
    api_ref[...] = api

    lr_l = jnp.minimum(are_l[...], -1e-4) * dt
    li_l = aim_l[...] * dt
    cre = cre_ref[...]
    cim = cim_ref[...]
    tp1 = (lax.broadcasted_iota(jnp.int32, (SSM_CHUNK, p), 0) + 1).astype(F32)
    em = jnp.exp(tp1 * lr_l)
    e_r = em * jnp.cos(tp1 * li_l)
    e_i = em * jnp.sin(tp1 * li_l)
    expand_t = (lax.broadcasted_iota(jnp.int32, (SSM_ROW, SSM_CHUNK), 0) // SSM_GROUP
                == lax.broadcasted_iota(jnp.int32, (SSM_ROW, SSM_CHUNK), 1)).astype(F32)
    ex_r = _hdot(expand_t, e_r)
    ex_i = _hdot(expand_t, e_i)
    wout_ref[...] = jnp.concatenate([ex_r * cre - ex_i * cim,
                                     -(ex_r * cim + ex_i * cre)],
                                    axis=1).astype(BF16)

    hrev = _hdot(cre[0:SSM_GROUP, :], win_r) - _hdot(cim[0:SSM_GROUP, :], win_i)
    lane = lax.broadcasted_iota(jnp.int32, hrev.shape, 1)
    blocks = []
    for t in range(SSM_CHUNK):
        hi = (t + 1) * SSM_GROUP
        rolled = hrev if hi == SSM_ROW else pltpu.roll(hrev, hi, axis=1)
        blocks.append(jnp.where(lane < hi, rolled, 0.0))
    tt_ref[...] = jnp.concatenate(blocks, axis=0).astype(BF16)


def _ssm_prep(a_re, a_im, log_dt, b_re, b_im, c_re, c_im):
    g, p = a_re.shape
    row = SSM_ROW
    per_group = lambda shape: pl.BlockSpec((None,) + shape, lambda i: (i, 0, 0))
    b_t = lambda b: jnp.tile(b, (1, 1, SSM_CHUNK))
    c_t = lambda c: jnp.tile(c, (1, SSM_CHUNK, 1))
    out_shapes = [
        jax.ShapeDtypeStruct((g, row, row), BF16),
        jax.ShapeDtypeStruct((g, 2 * p, row), BF16),
        jax.ShapeDtypeStruct((g, row, 2 * p), BF16),
        jax.ShapeDtypeStruct((g, p, SSM_POWERS), F32), jax.ShapeDtypeStruct((g, p, SSM_POWERS), F32),
    ]
    return pl.pallas_call(
        _ssm_prep_kernel,
        grid=(g,),
        in_specs=[per_group((p, 1)), per_group((p, 1)), per_group((1, p)), per_group((1, p)),
                  per_group((1, 1)),
                  per_group((p, row)), per_group((p, row)), per_group((row, p)), per_group((row, p))],
        out_specs=[per_group(s.shape[1:]) for s in out_shapes],
        out_shape=out_shapes,
        compiler_params=_params("parallel"),
        name="ssm_prep",
    )(a_re.reshape(g, p, 1), a_im.reshape(g, p, 1), a_re.reshape(g, 1, p), a_im.reshape(g, 1, p),
      log_dt.reshape(g, 1, 1), b_t(b_re), b_t(b_im), c_t(c_re), c_t(c_im))


def _ssm_kernel(ut_ref, tt_ref, win_ref, wout_ref, apr_ref, api_ref, dsk_ref, y_ref, *, gb, n_chunk):
    nk = ut_ref.shape[-1]
    p = apr_ref.shape[1]
    pos = lax.broadcasted_iota(jnp.int32, (p, nk), 1) % n_chunk
    n_steps = (n_chunk - 1).bit_length()
    assert n_steps <= SSM_POWERS
    for g in range(gb):
        ut = ut_ref[g]
        sl = jnp.dot(win_ref[g], ut, preferred_element_type=F32)
        s_r, s_i = sl[:p], sl[p:]
        apr = apr_ref[g]
        api = api_ref[g]
        for j in range(n_steps):
            sh = 1 << j
            a_r = apr[:, j:j + 1]
            a_i = api[:, j:j + 1]
            keep = pos >= sh
            p_r = jnp.where(keep, pltpu.roll(s_r, sh, axis=1), 0.0)
            p_i = jnp.where(keep, pltpu.roll(s_i, sh, axis=1), 0.0)
            s_r, s_i = s_r + a_r * p_r - a_i * p_i, s_i + a_r * p_i + a_i * p_r
        keep = pos >= 1
        prev = jnp.concatenate([jnp.where(keep, pltpu.roll(s_r, 1, axis=1), 0.0),
                                jnp.where(keep, pltpu.roll(s_i, 1, axis=1), 0.0)], axis=0).astype(BF16)
        y = jnp.dot(tt_ref[g], ut, preferred_element_type=F32)
        y += jnp.dot(wout_ref[g], prev, preferred_element_type=F32)
        y += dsk_ref[g] * ut.astype(F32)
        y_ref[g] = _gelu_tanh(y).astype(BF16)


def _ssm(ut, prep, dsk, *, n_chunk, gb=4):
    tt, win, wout, apr, api = prep
    g, row, nk = ut.shape
    p = apr.shape[1]
    blk = lambda shape: pl.BlockSpec((gb,) + shape, lambda i: (i, 0, 0))
    kern = functools.partial(_ssm_kernel, gb=gb, n_chunk=n_chunk)
    return pl.pallas_call(
        kern,
        grid=(g // gb,),
        in_specs=[blk((row, nk)), blk((row, row)), blk((2 * p, row)), blk((row, 2 * p)),
                  blk((p, SSM_POWERS)), blk((p, SSM_POWERS)), blk((row, 1))],
        out_specs=blk((row, nk)),
        out_shape=jax.ShapeDtypeStruct((g, row, nk), BF16),
        compiler_params=_params("parallel"),
        name="ssm",
    )(ut, tt, win, wout, apr, api, dsk)


def _ssm_out_kernel(yt_ref, wglu_ref, bglu_ref, wso_ref, o_ref, stage_ref, *, t_per_pass):
    n_grp, _, ch = yt_ref.shape
    width = n_grp * SSM_GROUP
    n_col = o_ref.shape[1] // LANES
    for t0 in range(0, SSM_CHUNK, t_per_pass):
        y = jnp.concatenate(
            [yt_ref[:, t * SSM_GROUP:(t + 1) * SSM_GROUP, :].reshape(width, ch)
             for t in range(t0, t0 + t_per_pass)], axis=1)
        gate = jnp.dot(wglu_ref[...], y, preferred_element_type=F32) + bglu_ref[...]
        glu = (y.astype(F32) * _sigmoid(gate)).astype(BF16)
        yo = lax.dot_general(glu, wso_ref[...], (((0,), (0,)), ((), ())), preferred_element_type=F32)
        for i in range(t_per_pass):
            rows = pl.ds(t0 + i, ch, stride=SSM_CHUNK)
            for c in range(n_col):
                stage_ref[c, rows, :] = yo[i * ch:(i + 1) * ch, c * LANES:(c + 1) * LANES]
    o_ref[...] = jnp.concatenate([stage_ref[c] for c in range(n_col)], axis=1).astype(BF16)


def _ssm_out(yt, w_glu, b_glu, w_ssm_out, *, ch=LANES, t_per_pass=4):
    n_grp, row, n_chunk_all = yt.shape
    width, d = w_ssm_out.shape
    tm = ch * SSM_CHUNK
    return pl.pallas_call(
        functools.partial(_ssm_out_kernel, t_per_pass=t_per_pass),
        grid=(n_chunk_all // ch,),
        in_specs=[pl.BlockSpec((n_grp, row, ch), lambda i: (0, 0, i)),
                  _resident((width, width)), _resident((width, 1)), _resident((width, d))],
        out_specs=pl.BlockSpec((tm, d), lambda i: (i, 0)),
        out_shape=jax.ShapeDtypeStruct((n_chunk_all * SSM_CHUNK, d), BF16),
        scratch_shapes=[pltpu.VMEM((d // LANES, tm, LANES), F32)],
        compiler_params=_params("parallel"),
        name="ssm_out",
    )(yt, w_glu.T.astype(BF16), b_glu.reshape(width, 1), w_ssm_out.astype(BF16))


def _mix_out_kernel(x_ref, o0, o1, o2, st0, st1, st2, ys_ref, ga_ref, gs_ref,
                    wao_ref, wo_ref, g_ref, b_ref, out_ref, o_stage, st_stage):
    tm = x_ref.shape[0]

    def token_order(ref, stage):
        dil = ref.shape[0]
        if dil == 1:
            return ref[0].astype(F32)
        n_col = ref.shape[2] // LANES
        for r in range(dil):
            blk = ref[r].astype(F32)
            for c in range(n_col):
                stage[c, pl.ds(r, tm // dil, stride=dil), :] = blk[:, c * LANES:(c + 1) * LANES]
        return jnp.concatenate([stage[c] for c in range(n_col)], axis=1)

    sts = [token_order(st, st_stage) for st in (st0, st1, st2)]
    outs = [token_order(o, o_stage) for o in (o0, o1, o2)]
    lane = lax.broadcasted_iota(jnp.int32, (tm, LANES), 1)
    lo_half = lane < HEAD_DIM

    def head_cols(vals):
        cols = []
        for hp in range(GROUP_WIDTH // LANES):
            cols.append(jnp.where(lo_half, vals[2 * hp], vals[2 * hp + 1]))
        return jnp.concatenate(cols, axis=1)

    ms = [[st[:, h:h + 1] for h in range(HEADS_PER_GROUP)] for st in sts]
    dens = [[st[:, HEADS_PER_GROUP + h:HEADS_PER_GROUP + h + 1] for h in range(HEADS_PER_GROUP)] for st in sts]
    wts = [[None] * HEADS_PER_GROUP for _ in sts]
    for h in range(HEADS_PER_GROUP):
        mx = jnp.maximum(jnp.maximum(ms[0][h], ms[1][h]), ms[2][h])
        w = [dens[g][h] * jnp.exp(ms[g][h] - mx) for g in range(N_ATTN_GROUPS)]
        tot = w[0] + w[1] + w[2]
        for g in range(N_ATTN_GROUPS):
            wts[g][h] = w[g] / tot
    att = None
    for g in range(N_ATTN_GROUPS):
        term = head_cols(wts[g]) * outs[g]
        att = term if att is None else att + term
    y_attn = jnp.dot(att.astype(BF16), wao_ref[...], preferred_element_type=F32)

    merged = ga_ref[...].astype(F32) * y_attn + gs_ref[...].astype(F32) * ys_ref[...].astype(F32)
    mix = jnp.dot(merged.astype(BF16), wo_ref[...], preferred_element_type=F32)
    out_ref[...] = _layer_norm(DEEPNORM_ALPHA * x_ref[...] + mix, g_ref[...], b_ref[...])


def _mix_out(x1, os_, sts, y_ssm, ga, gs, w_attn_out, w_o, g, b, *, seq, tm=512):
    t, d = x1.shape
    tiles = seq // tm
    tile = lambda w: pl.BlockSpec((tm, w), lambda i: (i, 0))

    def by_class(a):
        _, dil, n, w = a.shape
        return pl.BlockSpec((None, dil, tm // dil, w), lambda i: (i // tiles, 0, i % tiles, 0))

    return pl.pallas_call(
        _mix_out_kernel,
        grid=(t // tm,),
        in_specs=[tile(d)] + [by_class(a) for a in os_] + [by_class(a) for a in sts] + [tile(d), tile(d), tile(d),
                  _resident(w_attn_out.shape), _resident(w_o.shape), _resident((1, d)), _resident((1, d))],
        out_specs=tile(d),
        out_shape=jax.ShapeDtypeStruct((t, d), F32),
        scratch_shapes=[pltpu.VMEM((GROUP_WIDTH // LANES, tm, LANES), F32), pltpu.VMEM((1, tm, LANES), F32)],
        compiler_params=_params("parallel"),
        name="mix_out",
    )(x1, *os_, *sts, y_ssm, ga, gs, w_attn_out.astype(BF16), w_o.astype(BF16),
      g.reshape(1, d), b.reshape(1, d))


def _layer(x, positions, w_in, w_attn_out, a_re, a_im, log_dt, b_re, b_im, c_re, c_im, d_skip,
           w_glu, b_glu, w_ssm_out, w_o, ffn1, ffn2, ln1, ln2, ln3):
    batch, seq, d = x.shape
    t = batch * seq
    n_grp = a_re.shape[0]
    ssm_width = n_grp * SSM_GROUP
    u_lo = 3 * ATTN_WIDTH

    x1 = _ffn_ln(x.reshape(t, d), *ffn1, *ln1)
    *qkvs, ga, gs = _in_proj(x1, positions, w_in, batch=batch, seq=seq, ssm_width=ssm_width)

    os_, sts = [], []
    for qkv, (window, dilation) in zip(qkvs, ATTN_PATTERNS):
        assert window == ATTN_BLK * dilation
        o, st = _dilated_attention(qkv)
        os_.append(o)
        sts.append(st)

    ut = _ssm_in(x1, w_in[:, u_lo:u_lo + ssm_width])
    prep = _ssm_prep(a_re, a_im, log_dt, b_re, b_im, c_re, c_im)
    dsk = jnp.tile(d_skip.reshape(n_grp, 1, SSM_GROUP), (1, SSM_CHUNK, 1)).reshape(n_grp, SSM_ROW, 1)
    yt = _ssm(ut, prep, dsk, n_chunk=seq // SSM_CHUNK)
    y_ssm = _ssm_out(yt, w_glu, b_glu, w_ssm_out)

    x2 = _mix_out(x1, os_, sts, y_ssm, ga, gs, w_attn_out, w_o, *ln2, seq=seq)
    x3 = _ffn_ln(x2, *ffn2, *ln3)
    return x3.reshape(batch, seq, d)


def kernel(x, positions, w_in, w_attn_out, a_re, a_im, log_dt, b_re, b_im, c_re, c_im, d_skip, w_glu, b_glu, w_ssm_out, w_o, ffn1_wg, ffn1_wu, ffn1_wd, ffn2_wg, ffn2_wu, ffn2_wd, ln1_g, ln1_b, ln2_g, ln2_b, ln3_g, ln3_b):
    depth = w_in.shape[0]
    for i in range(depth):
        x = _layer(x, positions, w_in[i], w_attn_out[i], a_re[i], a_im[i], log_dt[i], b_re[i], b_im[i],
                   c_re[i], c_im[i], d_skip[i], w_glu[i], b_glu[i], w_ssm_out[i], w_o[i],
                   (ffn1_wg[i], ffn1_wu[i], ffn1_wd[i]), (ffn2_wg[i], ffn2_wu[i], ffn2_wd[i]),
                   (ln1_g[i], ln1_b[i]), (ln2_g[i], ln2_b[i]), (ln3_g[i], ln3_b[i]))
    return x
```

```python
import functools

import jax
import jax.numpy as jnp
from jax import lax
from jax.experimental import pallas as pl
from jax.experimental.pallas import tpu as pltpu

F32 = jnp.float32
BF16 = jnp.bfloat16

HEAD_DIM = 64
HEADS_PER_GROUP = 4
GROUP_WIDTH = HEADS_PER_GROUP * HEAD_DIM
ATTN_PATTERNS = ((128, 1), (512, 4), (2048, 16))
N_ATTN_GROUPS = len(ATTN_PATTERNS)
ATTN_WIDTH = N_ATTN_GROUPS * GROUP_WIDTH
ROT_DIM = HEAD_DIM // 4
ROPE_THETA = 500000.0
ATTN_BLK = 128
NEG_INF = -1e30
SSM_GROUP = 16
SSM_STATE = 64
SSM_CHUNK = 16
SSM_ROW = SSM_CHUNK * SSM_GROUP
SSM_POWERS = 16
DEEPNORM_ALPHA = 2.0 ** 0.25
LN_EPS = 1e-5

LANES = 128
SUBLANES = 8
VMEM_LIMIT_BYTES = 56 * 1024 * 1024


def _params(*semantics):
    return pltpu.CompilerParams(dimension_semantics=semantics, vmem_limit_bytes=VMEM_LIMIT_BYTES)


def _resident(shape):
    zeros = (0,) * len(shape)
    return pl.BlockSpec(shape, lambda *_: zeros, pipeline_mode=pl.Buffered(1))


def _layer_norm(y, g, b):
    mu = jnp.mean(y, axis=-1, keepdims=True)
    yc = y - mu
    var = jnp.mean(yc * yc, axis=-1, keepdims=True)
    return yc * lax.rsqrt(var + LN_EPS) * g + b


def _sigmoid(x):
    return 1.0 / (1.0 + jnp.exp(-x))


def _gelu_tanh(x):
    c = 0.7978845608028654
    return 0.5 * x * (1.0 + jnp.tanh(c * (x + 0.044715 * (x * x * x))))


def _ffn_ln_kernel(x_ref, wg_ref, wu_ref, wd_ref, g_ref, b_ref, o_ref, *, chunks):
    x = x_ref[...]
    xb = x.astype(BF16)
    acc = None
    for c0, c1 in chunks:
        gate = jnp.dot(xb, wg_ref[:, c0:c1], preferred_element_type=F32)
        up = jnp.dot(xb, wu_ref[:, c0:c1], preferred_element_type=F32)
        h = (gate * _sigmoid(gate) * up).astype(BF16)
        part = jnp.dot(h, wd_ref[c0:c1, :], preferred_element_type=F32)
        acc = part if acc is None else acc + part
    y = DEEPNORM_ALPHA * x + 0.5 * acc
    o_ref[...] = _layer_norm(y, g_ref[...], b_ref[...])


def _ffn_chunks(d_ff, width):
    edges = list(range(0, d_ff, width)) + [d_ff]
    return tuple(zip(edges[:-1], edges[1:]))


def _ffn_ln(x, wg, wu, wd, g, b, *, tm=512, chunk=1024):
    t, d = x.shape
    d_ff = wg.shape[1]
    kern = functools.partial(_ffn_ln_kernel, chunks=_ffn_chunks(d_ff, chunk))
    return pl.pallas_call(
        kern,
        grid=(t // tm,),
        in_specs=[
            pl.BlockSpec((tm, d), lambda i: (i, 0)),
            _resident((d, d_ff)), _resident((d, d_ff)), _resident((d_ff, d)),
            _resident((1, d)), _resident((1, d)),
        ],
        out_specs=pl.BlockSpec((tm, d), lambda i: (i, 0)),
        out_shape=jax.ShapeDtypeStruct((t, d), F32),
        compiler_params=_params("parallel"),
        name="ffn_ln",
    )(x, wg.astype(BF16), wu.astype(BF16), wd.astype(BF16), g.reshape(1, d), b.reshape(1, d))


def _in_proj_kernel(x_ref, pos_ref, invf_ref, w_ref,
                    qkv0_ref, qkv1_ref, qkv2_ref, ga_ref, gs_ref, stage_ref, *, splits):
    xb = x_ref[...].astype(BF16)
    tm = xb.shape[0]

    ang = pos_ref[...].astype(F32) * invf_ref[...]
    cos = jnp.cos(ang)
    sin = jnp.sin(ang)
    lane = lax.broadcasted_iota(jnp.int32, (tm, LANES), 1) % HEAD_DIM
    half = ROT_DIM // 2
    sin_lo = jnp.where(lane < half, -sin, 0.0)
    sin_hi = jnp.where((lane >= half) & (lane < ROT_DIM), sin, 0.0)

    def rotate(z):
        cols = []
        for c in range(z.shape[1] // LANES):
            zc = z[:, c * LANES:(c + 1) * LANES]
            up = pltpu.roll(zc, LANES - half, axis=1)
            dn = pltpu.roll(zc, half, axis=1)
            cols.append(zc * cos + up * sin_lo + dn * sin_hi)
        return jnp.concatenate(cols, axis=1)

    def proj(lo, hi):
        return jnp.dot(xb, w_ref[:, lo:hi], preferred_element_type=F32)

    s0, s1, s2, s3, s4, s5 = splits
    q = rotate(proj(0, s0)) * (HEAD_DIM ** -0.5)
    k = rotate(proj(s0, s1))
    v = proj(s1, s2)
    for gi, out_ref in enumerate((qkv0_ref, qkv1_ref, qkv2_ref)):
        dil = ATTN_PATTERNS[gi][1]
        c0 = gi * GROUP_WIDTH
        qkv = jnp.concatenate([a[:, c0:c0 + GROUP_WIDTH] for a in (q, k, v)], axis=1)
        if dil == 1:
            out_ref[0] = qkv.astype(BF16)
        else:
            n_col = qkv.shape[1] // LANES
            for c in range(n_col):
                stage_ref[c] = qkv[:, c * LANES:(c + 1) * LANES]
            for r in range(dil):
                rows = pl.ds(r, tm // dil, stride=dil)
                out_ref[r] = jnp.concatenate([stage_ref[c, rows, :] for c in range(n_col)], axis=1).astype(BF16)
    ga_ref[...] = _sigmoid(proj(s3, s4)).astype(BF16)
    gs_ref[...] = _sigmoid(proj(s4, s5)).astype(BF16)


def _in_proj(x1, positions, w_in, *, batch, seq, ssm_width, tm=512):
    t, d = x1.shape
    splits = (ATTN_WIDTH, 2 * ATTN_WIDTH, 3 * ATTN_WIDTH, 3 * ATTN_WIDTH + ssm_width,
              3 * ATTN_WIDTH + ssm_width + d, 3 * ATTN_WIDTH + ssm_width + 2 * d)
    assert splits[-1] == w_in.shape[1] and seq % tm == 0
    half = ROT_DIM // 2
    inv_freq = ROPE_THETA ** (-jnp.arange(half, dtype=F32) * 2.0 / ROT_DIM)
    lane = jnp.arange(LANES) % HEAD_DIM
    invf = jnp.where(lane < ROT_DIM, inv_freq[lane % half], 0.0).reshape(1, LANES).astype(F32)
    tiles = seq // tm
    qkv_w = 3 * GROUP_WIDTH
    qkv_specs = [pl.BlockSpec((None, dil, tm // dil, qkv_w), lambda i: (i // tiles, 0, i % tiles, 0))
                 for _, dil in ATTN_PATTERNS]
    qkv_shapes = [jax.ShapeDtypeStruct((batch, dil, seq // dil, qkv_w), BF16) for _, dil in ATTN_PATTERNS]
    widths = (d, d)
    kern = functools.partial(_in_proj_kernel, splits=splits)
    return pl.pallas_call(
        kern,
        grid=(t // tm,),
        in_specs=[
            pl.BlockSpec((tm, d), lambda i: (i, 0)),
            pl.BlockSpec((tm, 1), lambda i: (i, 0)),
            _resident((1, LANES)),
            _resident(w_in.shape),
        ],
        out_specs=qkv_specs + [pl.BlockSpec((tm, w), lambda i: (i, 0)) for w in widths],
        out_shape=qkv_shapes + [jax.ShapeDtypeStruct((t, w), BF16) for w in widths],
        scratch_shapes=[pltpu.VMEM((qkv_w // LANES, tm, LANES), F32)],
        compiler_params=_params("parallel"),
        name="in_proj",
    )(x1, positions.reshape(t, 1), invf, w_in.astype(BF16))


def _attn_kernel(q_ref, kc_ref, kp_ref, vc_ref, vp_ref, o_ref, st_ref, *, n_sub):
    first = pl.program_id(2) == 0
    row = lax.broadcasted_iota(jnp.int32, (ATTN_BLK, 2 * ATTN_BLK), 0)
    col = lax.broadcasted_iota(jnp.int32, (ATTN_BLK, 2 * ATTN_BLK), 1)
    band = (col >= row) & (col <= row + ATTN_BLK)
    lane = lax.broadcasted_iota(jnp.int32, (ATTN_BLK, LANES), 1)
    lo_half = lane < HEAD_DIM

    for j in range(n_sub):
        r0 = j * ATTN_BLK
        if j == 0:
            valid = band & ((col >= ATTN_BLK) | jnp.logical_not(first))
        else:
            valid = band
        stats = jnp.zeros((ATTN_BLK, LANES), F32)
        for hp in range(GROUP_WIDTH // LANES):
            c0 = hp * LANES
            q2 = q_ref[r0:r0 + ATTN_BLK, c0:c0 + LANES]
            if j == 0:
                kk = jnp.concatenate([kp_ref[:, c0:c0 + LANES], kc_ref[0:ATTN_BLK, c0:c0 + LANES]], axis=0)
                vv = jnp.concatenate([vp_ref[:, c0:c0 + LANES], vc_ref[0:ATTN_BLK, c0:c0 + LANES]], axis=0)
            else:
                kk = kc_ref[r0 - ATTN_BLK:r0 + ATTN_BLK, c0:c0 + LANES]
                vv = vc_ref[r0 - ATTN_BLK:r0 + ATTN_BLK, c0:c0 + LANES]
            outs = []
            for hh in range(2):
                qm = jnp.where(lo_half if hh == 0 else jnp.logical_not(lo_half), q2, jnp.zeros_like(q2))
                s = lax.dot_general(qm, kk, (((1,), (1,)), ((), ())), preferred_element_type=F32)
                s = jnp.where(valid, s, NEG_INF)
                m = jnp.max(s, axis=1, keepdims=True)
                p = jnp.exp(s - m)
                den = jnp.sum(p, axis=1, keepdims=True)
                o = jnp.dot(p.astype(BF16), vv, preferred_element_type=F32) / den
                outs.append(o)
                h = 2 * hp + hh
                stats = jnp.where(lane == h, m, stats)
                stats = jnp.where(lane == HEADS_PER_GROUP + h, den, stats)
            o_ref[r0:r0 + ATTN_BLK, c0:c0 + LANES] = jnp.where(lo_half, outs[0], outs[1]).astype(BF16)
        st_ref[r0:r0 + ATTN_BLK, :] = stats


def _dilated_attention(qkv):
    batch, dilation, n, _ = qkv.shape
    qb = min(512, n)
    n_sub = qb // ATTN_BLK

    def cur(part):
        return pl.BlockSpec((None, None, qb, GROUP_WIDTH), lambda b, r, i: (b, r, i, part))

    def prev(part):
        return pl.BlockSpec((None, None, ATTN_BLK, GROUP_WIDTH),
                            lambda b, r, i: (b, r, jnp.maximum(i * n_sub - 1, 0), part))

    return pl.pallas_call(
        functools.partial(_attn_kernel, n_sub=n_sub),
        grid=(batch, dilation, n // qb),
        in_specs=[cur(0), cur(1), prev(1), cur(2), prev(2)],
        out_specs=[pl.BlockSpec((None, None, qb, GROUP_WIDTH), lambda b, r, i: (b, r, i, 0)),
                   pl.BlockSpec((None, None, qb, LANES), lambda b, r, i: (b, r, i, 0))],
        out_shape=[jax.ShapeDtypeStruct((batch, dilation, n, GROUP_WIDTH), BF16),
                   jax.ShapeDtypeStruct((batch, dilation, n, LANES), F32)],
        compiler_params=_params("parallel", "parallel", "parallel"),
        name=f"attn_d{dilation}",
    )(qkv, qkv, qkv, qkv, qkv)


def _ssm_in_kernel(*refs, n_grp, ch):
    *x_refs, w_ref, ut_ref = refs
    for s in range(0, SSM_CHUNK, 2):
        xs = jnp.concatenate(
            [jnp.concatenate([x_ref[pl.ds(s + i, ch, stride=SSM_CHUNK), :] for x_ref in x_refs], axis=1)
             for i in range(2)], axis=0).astype(BF16)
        ut = lax.dot_general(w_ref[...], xs, (((1,), (1,)), ((), ())), preferred_element_type=F32)
        for i in range(2):
            ut_ref[:, (s + i) * SSM_GROUP:(s + i + 1) * SSM_GROUP, :] = (
                ut[:, i * ch:(i + 1) * ch].reshape(n_grp, SSM_GROUP, ch).astype(BF16))


def _ssm_in(x1, w_u_t, *, ch=LANES):
    t, d = x1.shape
    width = w_u_t.shape[0]
    n_grp = width // SSM_GROUP
    n_chunk_all = t // SSM_CHUNK
    tm = ch * SSM_CHUNK
    slabs = [pl.BlockSpec((tm, LANES), lambda i, c=c: (i, c)) for c in range(d // LANES)]
    return pl.pallas_call(
        functools.partial(_ssm_in_kernel, n_grp=n_grp, ch=ch),
        grid=(n_chunk_all // ch,),
        in_specs=slabs + [_resident((width, d))],
        out_specs=pl.BlockSpec((n_grp, SSM_ROW, ch), lambda i: (0, 0, i)),
        out_shape=jax.ShapeDtypeStruct((n_grp, SSM_ROW, n_chunk_all), BF16),
        compiler_params=_params("parallel"),
        name="ssm_in",
    )(*([x1] * (d // LANES)), w_u_t)


def _hdot(a, b):
    return jnp.dot(a, b, preferred_element_type=F32, precision=lax.Precision.HIGHEST)


def _cmul(ar, ai, br, bi):
    return ar * br - ai * bi, ar * bi + ai * br


def _cpowers(lbr, lbi, n, bits):
    out_r = jnp.ones(n.shape, F32)
    out_i = jnp.zeros(n.shape, F32)
    for b in range(bits):
        sel = (n & (1 << b)) != 0
        out_r, out_i = _cmul(out_r, out_i, jnp.where(sel, lbr, 1.0), jnp.where(sel, lbi, 0.0))
        lbr, lbi = _cmul(lbr, lbi, lbr, lbi)
    return out_r, out_i


def _ssm_disc_kernel(are_ref, aim_ref, ldt_ref, lbr_ref, lbi_ref, cfr_ref, cfi_ref):
    dt = jnp.exp(ldt_ref[...])
    lam_r = jnp.minimum(are_ref[...], -1e-4)
    lam_i = aim_ref[...]
    mag = jnp.exp(lam_r * dt)
    lbr = mag * jnp.cos(lam_i * dt)
    lbi = mag * jnp.sin(lam_i * dt)
    inv = 1.0 / (lam_r * lam_r + lam_i * lam_i)
    lbr_ref[...] = lbr
    lbi_ref[...] = lbi
    cfr_ref[...] = ((lbr - 1.0) * lam_r + lbi * lam_i) * inv
    cfi_ref[...] = (lbi * lam_r - (lbr - 1.0) * lam_i) * inv


def _ssm_prep_kernel(lbr_s, lbi_s, cfr_s, cfi_s, are_l, aim_l, ldt_ref, bre_ref, bim_ref, cre_ref, cim_ref,
                     tt_ref, win_ref, wout_ref, apr_ref, api_ref, *, gb):
    p = lbr_s.shape[1]
    expand = (lax.broadcasted_iota(jnp.int32, (SSM_CHUNK, SSM_ROW), 1) // SSM_GROUP
              == lax.broadcasted_iota(jnp.int32, (SSM_CHUNK, SSM_ROW), 0)).astype(F32)
    expand_t = (lax.broadcasted_iota(jnp.int32, (SSM_ROW, SSM_CHUNK), 0) // SSM_GROUP
                == lax.broadcasted_iota(jnp.int32, (SSM_ROW, SSM_CHUNK), 1)).astype(F32)
    rem = SSM_CHUNK - 1 - lax.broadcasted_iota(jnp.int32, (p, SSM_CHUNK), 1)
    tp1 = lax.broadcasted_iota(jnp.int32, (SSM_CHUNK, p), 0) + 1
    col = lax.broadcasted_iota(jnp.int32, (p, SSM_POWERS), 1)
    lane = lax.broadcasted_iota(jnp.int32, (SSM_GROUP, SSM_ROW), 1)
    chunk_bits = SSM_CHUNK.bit_length()

    for g in range(gb):
        lbr = lbr_s[g]
        lbi = lbi_s[g]
        bre = bre_ref[g]
        bim = bim_ref[g]
        bbar_r, bbar_i = _cmul(cfr_s[g], cfi_s[g], bre, bim)
        pw_r, pw_i = _cpowers(lbr, lbi, rem, chunk_bits - 1)
        win_r, win_i = _cmul(_hdot(pw_r, expand), _hdot(pw_i, expand), bbar_r, bbar_i)
        win_ref[g] = jnp.concatenate([win_r, win_i], axis=0).astype(BF16)

        a_r, a_i = _cmul(pw_r[:, 0:1], pw_i[:, 0:1], lbr, lbi)
        apr = jnp.zeros((p, SSM_POWERS), F32)
        api = jnp.zeros((p, SSM_POWERS), F32)
        for j in range(SSM_POWERS):
            apr = jnp.where(col == j, a_r, apr)
            api = jnp.where(col == j, a_i, api)
            a_r, a_i = _cmul(a_r, a_i, a_r, a_i)
        apr_ref[g] = apr
        api_ref[g] = api

        dt = jnp.exp(ldt_ref[g])
        mag_l = jnp.exp(jnp.minimum(are_l[g], -1e-4) * dt)
        e_r, e_i = _cpowers(mag_l * jnp.cos(aim_l[g] * dt), mag_l * jnp.sin(aim_l[g] * dt), tp1, chunk_bits)
        cre = cre_ref[g]
        cim = cim_ref[g]
        wo_r, wo_i = _cmul(_hdot(expand_t, e_r), _hdot(expand_t, e_i), cre, cim)
        wout_ref[g] = jnp.concatenate([wo_r, -wo_i], axis=1).astype(BF16)

        hrev = _hdot(cre[0:SSM_GROUP, :], win_r) - _hdot(cim[0:SSM_GROUP, :], win_i)
        blocks = []
        for t in range(SSM_CHUNK):
            hi = (t + 1) * SSM_GROUP
            rolled = hrev if hi == SSM_ROW else pltpu.roll(hrev, hi, axis=1)
            blocks.append(jnp.where(lane < hi, rolled, 0.0))
        tt_ref[g] = jnp.concatenate(blocks, axis=0).astype(BF16)


def _ssm_prep(a_re, a_im, log_dt, b_re, b_im, c_re, c_im, *, gb=4):
    g, p = a_re.shape
    row = SSM_ROW
    whole = lambda shape: pl.BlockSpec(shape, lambda: (0,) * len(shape))
    disc = pl.pallas_call(
        _ssm_disc_kernel,
        in_specs=[whole((p, g)), whole((p, g)), whole((1, g))],
        out_specs=[whole((p, g))] * 4,
        out_shape=[jax.ShapeDtypeStruct((p, g), F32)] * 4,
        name="ssm_disc",
    )(a_re.T, a_im.T, log_dt.reshape(1, g))
    lbr_s, lbi_s, cfr_s, cfi_s = [a.T.reshape(g, p, 1) for a in disc]

    blk = lambda shape: pl.BlockSpec((gb,) + shape, lambda i: (i, 0, 0))
    b_t = lambda b: jnp.tile(b, (1, 1, SSM_CHUNK))
    c_t = lambda c: jnp.tile(c, (1, SSM_CHUNK, 1))
    out_shapes = [
        jax.ShapeDtypeStruct((g, row, row), BF16),
        jax.ShapeDtypeStruct((g, 2 * p, row), BF16),
        jax.ShapeDtypeStruct((g, row, 2 * p), BF16),
        jax.ShapeDtypeStruct((g, p, SSM_POWERS), F32), jax.ShapeDtypeStruct((g, p, SSM_POWERS), F32),
    ]
    return pl.pallas_call(
        functools.partial(_ssm_prep_kernel, gb=gb),
        grid=(g // gb,),
        in_specs=[blk((p, 1))] * 4 + [blk((1, p)), blk((1, p)), blk((1, 1)),
                  blk((p, row)), blk((p, row)), blk((row, p)), blk((row, p))],
        out_specs=[blk(s.shape[1:]) for s in out_shapes],
        out_shape=out_shapes,
        compiler_params=_params("parallel"),
        name="ssm_prep",
    )(lbr_s, lbi_s, cfr_s, cfi_s, a_re.reshape(g, 1, p), a_im.reshape(g, 1, p),
      log_dt.reshape(g, 1, 1), b_t(b_re), b_t(b_im), c_t(c_re), c_t(c_im))


def _ssm_kernel(ut_ref, tt_ref, win_ref, wout_ref, apr_ref, api_ref, dsk_ref, y_ref, *, gb, n_chunk):
    nk = ut_ref.shape[-1]
    p = apr_ref.shape[1]
    pos = lax.broadcasted_iota(jnp.int32, (p, nk), 1) % n_chunk
    n_steps = (n_chunk - 1).bit_length()
    assert n_steps <= SSM_POWERS
    for g in range(gb):
        ut = ut_ref[g]
        sl = jnp.dot(win_ref[g], ut, preferred_element_type=F32)
        s_r, s_i = sl[:p], sl[p:]
        apr = apr_ref[g]
        api = api_ref[g]
        for j in range(n_steps):
            sh = 1 << j
            a_r = apr[:, j:j + 1]
            a_i = api[:, j:j + 1]
            keep = pos >= sh
            p_r = jnp.where(keep, pltpu.roll(s_r, sh, axis=1), 0.0)
            p_i = jnp.where(keep, pltpu.roll(s_i, sh, axis=1), 0.0)
            s_r, s_i = s_r + a_r * p_r - a_i * p_i, s_i + a_r * p_i + a_i * p_r
        keep = pos >= 1
        prev = jnp.concatenate([jnp.where(keep, pltpu.roll(s_r, 1, axis=1), 0.0),
                                jnp.where(keep, pltpu.roll(s_i, 1, axis=1), 0.0)], axis=0).astype(BF16)
        y = jnp.dot(tt_ref[g], ut, preferred_element_type=F32)
        y += jnp.dot(wout_ref[g], prev, preferred_element_type=F32)
        y += dsk_ref[g] * ut.astype(F32)
        y_ref[g] = _gelu_tanh(y).astype(BF16)


def _ssm(ut, prep, dsk, *, n_chunk, gb=4):
    tt, win, wout, apr, api = prep
    g, row, nk = ut.shape
    p = apr.shape[1]
    blk = lambda shape: pl.BlockSpec((gb,) + shape, lambda i: (i, 0, 0))
    kern = functools.partial(_ssm_kernel, gb=gb, n_chunk=n_chunk)
    return pl.pallas_call(
        kern,
        grid=(g // gb,),
        in_specs=[blk((row, nk)), blk((row, row)), blk((2 * p, row)), blk((row, 2 * p)),
                  blk((p, SSM_POWERS)), blk((p, SSM_POWERS)), blk((row, 1))],
        out_specs=blk((row, nk)),
        out_shape=jax.ShapeDtypeStruct((g, row, nk), BF16),
        compiler_params=_params("parallel"),
        name="ssm",
    )(ut, tt, win, wout, apr, api, dsk)


def _ssm_out_kernel(yt_ref, wglu_ref, bglu_ref, wso_ref, o_ref, stage_ref, *, t_per_pass):
    n_grp, _, ch = yt_ref.shape
    width = n_grp * SSM_GROUP
    n_col = o_ref.shape[1] // LANES
    for t0 in range(0, SSM_CHUNK, t_per_pass):
        y = jnp.concatenate(
            [yt_ref[:, t * SSM_GROUP:(t + 1) * SSM_GROUP, :].reshape(width, ch)
             for t in range(t0, t0 + t_per_pass)], axis=1)
        gate = jnp.dot(wglu_ref[...], y, preferred_element_type=F32) + bglu_ref[...]
        glu = (y.astype(F32) * _sigmoid(gate)).astype(BF16)
        yo = lax.dot_general(glu, wso_ref[...], (((0,), (0,)), ((), ())), preferred_element_type=F32)
        for i in range(t_per_pass):
            rows = pl.ds(t0 + i, ch, stride=SSM_CHUNK)
            for c in range(n_col):
                stage_ref[c, rows, :] = yo[i * ch:(i + 1) * ch, c * LANES:(c + 1) * LANES]
    o_ref[...] = jnp.concatenate([stage_ref[c] for c in range(n_col)], axis=1).astype(BF16)


def _ssm_out(yt, w_glu, b_glu, w_ssm_out, *, ch=LANES, t_per_pass=4):
    n_grp, row, n_chunk_all = yt.shape
    width, d = w_ssm_out.shape
    tm = ch * SSM_CHUNK
    return pl.pallas_call(
        functools.partial(_ssm_out_kernel, t_per_pass=t_per_pass),
        grid=(n_chunk_all // ch,),
        in_specs=[pl.BlockSpec((n_grp, row, ch), lambda i: (0, 0, i)),
                  _resident((width, width)), _resident((width, 1)), _resident((width, d))],
        out_specs=pl.BlockSpec((tm, d), lambda i: (i, 0)),
        out_shape=jax.ShapeDtypeStruct((n_chunk_all * SSM_CHUNK, d), BF16),
        scratch_shapes=[pltpu.VMEM((d // LANES, tm, LANES), F32)],
        compiler_params=_params("parallel"),
        name="ssm_out",
    )(yt, w_glu.T.astype(BF16), b_glu.reshape(width, 1), w_ssm_out.astype(BF16))


def _mix_out_kernel(x_ref, o0, o1, o2, st0, st1, st2, ys_ref, ga_ref, gs_ref,
                    wao_ref, wo_ref, g_ref, b_ref, out_ref, o_stage, st_stage):
    tm = x_ref.shape[0]

    def token_order(ref, stage):
        dil = ref.shape[0]
        if dil == 1:
            return ref[0].astype(F32)
        n_col = ref.shape[2] // LANES
        for r in range(dil):
            blk = ref[r].astype(F32)
            for c in range(n_col):
                stage[c, pl.ds(r, tm // dil, stride=dil), :] = blk[:, c * LANES:(c + 1) * LANES]
        return jnp.concatenate([stage[c] for c in range(n_col)], axis=1)

    sts = [token_order(st, st_stage) for st in (st0, st1, st2)]
    outs = [token_order(o, o_stage) for o in (o0, o1, o2)]
    lane = lax.broadcasted_iota(jnp.int32, (tm, LANES), 1)
    lo_half = lane < HEAD_DIM

    def head_cols(vals):
        cols = []
        for hp in range(GROUP_WIDTH // LANES):
            cols.append(jnp.where(lo_half, vals[2 * hp], vals[2 * hp + 1]))
        return jnp.concatenate(cols, axis=1)

    dens = [pltpu.roll(st, LANES - HEADS_PER_GROUP, axis=1) for st in sts]
    mx = jnp.maximum(jnp.maximum(sts[0], sts[1]), sts[2])
    w = [den * jnp.exp(st - mx) for den, st in zip(dens, sts)]
    tot = jnp.where(lane < HEADS_PER_GROUP, w[0] + w[1] + w[2], 1.0)
    att = None
    for g in range(N_ATTN_GROUPS):
        wt = w[g] / tot
        term = head_cols([wt[:, h:h + 1] for h in range(HEADS_PER_GROUP)]) * outs[g]
        att = term if att is None else att + term
    y_attn = jnp.dot(att.astype(BF16), wao_ref[...], preferred_element_type=F32)

    merged = ga_ref[...].astype(F32) * y_attn + gs_ref[...].astype(F32) * ys_ref[...].astype(F32)
    mix = jnp.dot(merged.astype(BF16), wo_ref[...], preferred_element_type=F32)
    out_ref[...] = _layer_norm(DEEPNORM_ALPHA * x_ref[...] + mix, g_ref[...], b_ref[...])


def _mix_out(x1, os_, sts, y_ssm, ga, gs, w_attn_out, w_o, g, b, *, seq, tm=512):
    t, d = x1.shape
    tiles = seq // tm
    tile = lambda w: pl.BlockSpec((tm, w), lambda i: (i, 0))

    def by_class(a):
        _, dil, n, w = a.shape
        return pl.BlockSpec((None, dil, tm // dil, w), lambda i: (i // tiles, 0, i % tiles, 0))

    return pl.pallas_call(
        _mix_out_kernel,
        grid=(t // tm,),
        in_specs=[tile(d)] + [by_class(a) for a in os_] + [by_class(a) for a in sts] + [tile(d), tile(d), tile(d),
                  _resident(w_attn_out.shape), _resident(w_o.shape), _resident((1, d)), _resident((1, d))],
        out_specs=tile(d),
        out_shape=jax.ShapeDtypeStruct((t, d), F32),
        scratch_shapes=[pltpu.VMEM((GROUP_WIDTH // LANES, tm, LANES), F32), pltpu.VMEM((1, tm, LANES), F32)],
        compiler_params=_params("parallel"),
        name="mix_out",
    )(x1, *os_, *sts, y_ssm, ga, gs, w_attn_out.astype(BF16), w_o.astype(BF16),
      g.reshape(1, d), b.reshape(1, d))


def _layer(x, positions, w_in, w_attn_out, a_re, a_im, log_dt, b_re, b_im, c_re, c_im, d_skip,
           w_glu, b_glu, w_ssm_out, w_o, ffn1, ffn2, ln1, ln2, ln3):
    batch, seq, d = x.shape
    t = batch * seq
    n_grp = a_re.shape[0]
    ssm_width = n_grp * SSM_GROUP
    u_lo = 3 * ATTN_WIDTH

    x1 = _ffn_ln(x.reshape(t, d), *ffn1, *ln1)
    *qkvs, ga, gs = _in_proj(x1, positions, w_in, batch=batch, seq=seq, ssm_width=ssm_width)

    os_, sts = [], []
    for qkv, (window, dilation) in zip(qkvs, ATTN_PATTERNS):
        assert window == ATTN_BLK * dilation
        o, st = _dilated_attention(qkv)
        os_.append(o)
        sts.append(st)

    ut = _ssm_in(x1, jnp.transpose(w_in[:, u_lo:u_lo + ssm_width]).astype(BF16))
    prep = _ssm_prep(a_re, a_im, log_dt, b_re, b_im, c_re, c_im)
    dsk = jnp.tile(d_skip.reshape(n_grp, 1, SSM_GROUP), (1, SSM_CHUNK, 1)).reshape(n_grp, SSM_ROW, 1)
    yt = _ssm(ut, prep, dsk, n_chunk=seq // SSM_CHUNK)
    y_ssm = _ssm_out(yt, w_glu, b_glu, w_ssm_out)

    x2 = _mix_out(x1, os_, sts, y_ssm, ga, gs, w_attn_out, w_o, *ln2, seq=seq)
    x3 = _ffn_ln(x2, *ffn2, *ln3)
    return x3.reshape(batch, seq, d)


def kernel(x, positions, w_in, w_attn_out, a_re, a_im, log_dt, b_re, b_im, c_re, c_im, d_skip, w_glu, b_glu, w_ssm_out, w_o, ffn1_wg, ffn1_wu, ffn1_wd, ffn2_wg, ffn2_wu, ffn2_wd, ln1_g, ln1_b, ln2_g, ln2_b, ln3_g, ln3_b):
    depth = w_in.shape[0]
    for i in range(depth):
        x = _layer(x, positions, w_in[i], w_attn_out[i], a_re[i], a_im[i], log_dt[i], b_re[i], b_im[i],
                   c_re[i], c_im[i], d_skip[i], w_glu[i], b_glu[i], w_ssm_out[i], w_o[i],
                   (ffn1_wg[i], ffn1_wu[i], ffn1_wd[i]), (ffn2_wg[i], ffn2_wu[i], ffn2_wd[i]),
                   (ln1_g[i], ln1_b[i]), (ln2_g[i], ln2_b[i]), (ln3_g[i], ln3_b[i]))
    return x
```

```python
import functools

import jax
import jax.numpy as jnp
from jax import lax
from jax.experimental import pallas as pl
from jax.experimental.pallas import tpu as pltpu

F32 = jnp.float32
BF16 = jnp.bfloat16

HEAD_DIM = 64
HEADS_PER_GROUP = 4
GROUP_WIDTH = HEADS_PER_GROUP * HEAD_DIM
ATTN_PATTERNS = ((128, 1), (512, 4), (2048, 16))
N_ATTN_GROUPS = len(ATTN_PATTERNS)
ATTN_WIDTH = N_ATTN_GROUPS * GROUP_WIDTH
ROT_DIM = HEAD_DIM // 4
ROPE_THETA = 500000.0
ROPE_ROWS = 32
ATTN_BLK = 128
NEG_INF = -1e30
SSM_GROUP = 16
SSM_STATE = 64
SSM_CHUNK = 16
SSM_ROW = SSM_CHUNK * SSM_GROUP
SSM_POWERS = 16
DEEPNORM_ALPHA = 2.0 ** 0.25
LN_EPS = 1e-5

LANES = 128
SUBLANES = 8
VMEM_LIMIT_BYTES = 56 * 1024 * 1024


def _params(*semantics):
    return pltpu.CompilerParams(dimension_semantics=semantics, vmem_limit_bytes=VMEM_LIMIT_BYTES)


def _resident(shape):
    zeros = (0,) * len(shape)
    return pl.BlockSpec(shape, lambda *_: zeros, pipeline_mode=pl.Buffered(1))


def _layer_norm(y, g, b):
    mu = jnp.mean(y, axis=-1, keepdims=True)
    yc = y - mu
    var = jnp.mean(yc * yc, axis=-1, keepdims=True)
    return yc * lax.rsqrt(var + LN_EPS) * g + b


def _sigmoid(x):
    return 1.0 / (1.0 + jnp.exp(-x))


def _gelu_tanh(x):
    c = 0.7978845608028654
    return 0.5 * x * (1.0 + jnp.tanh(c * (x + 0.044715 * (x * x * x))))


def _ffn_ln_kernel(x_ref, wg_ref, wu_ref, wd_ref, g_ref, b_ref, o_ref, *, chunks, sub):
    for r0 in range(0, x_ref.shape[0], sub):
        x = x_ref[r0:r0 + sub, :]
        xb = x.astype(BF16)
        acc = None
        for c0, c1 in chunks:
            gate = jnp.dot(xb, wg_ref[:, c0:c1], preferred_element_type=F32)
            up = jnp.dot(xb, wu_ref[:, c0:c1], preferred_element_type=F32)
            h = (gate * _sigmoid(gate) * up).astype(BF16)
            part = jnp.dot(h, wd_ref[c0:c1, :], preferred_element_type=F32)
            acc = part if acc is None else acc + part
        y = DEEPNORM_ALPHA * x + 0.5 * acc
        o_ref[r0:r0 + sub, :] = _layer_norm(y, g_ref[...], b_ref[...])


def _ffn_chunks(d_ff, width):
    edges = list(range(0, d_ff, width)) + [d_ff]
    return tuple(zip(edges[:-1], edges[1:]))


def _ffn_ln(x, wg, wu, wd, g, b, *, tm=1024, sub=256, chunk=1024):
    t, d = x.shape
    d_ff = wg.shape[1]
    kern = functools.partial(_ffn_ln_kernel, chunks=_ffn_chunks(d_ff, chunk), sub=sub)
    return pl.pallas_call(
        kern,
        grid=(t // tm,),
        in_specs=[
            pl.BlockSpec((tm, d), lambda i: (i, 0)),
            _resident((d, d_ff)), _resident((d, d_ff)), _resident((d_ff, d)),
            _resident((1, d)), _resident((1, d)),
        ],
        out_specs=pl.BlockSpec((tm, d), lambda i: (i, 0)),
        out_shape=jax.ShapeDtypeStruct((t, d), F32),
        compiler_params=_params("parallel"),
        name="ffn_ln",
    )(x, wg.astype(BF16), wu.astype(BF16), wd.astype(BF16), g.reshape(1, d), b.reshape(1, d))


def _rope_spread():
    half = ROT_DIM // 2
    rows = lax.broadcasted_iota(jnp.int32, (ROPE_ROWS, 2 * LANES), 0)
    cols = lax.broadcasted_iota(jnp.int32, (ROPE_ROWS, 2 * LANES), 1)
    in_head = cols % HEAD_DIM
    freq = in_head % half
    is_cos = cols < LANES
    rot = in_head < ROT_DIM
    cos_part = is_cos & rot & (rows == freq)
    one_part = is_cos & jnp.logical_not(rot) & (rows == 2 * half)
    sin_part = jnp.logical_not(is_cos) & rot & (rows == half + freq)
    sign = jnp.where(in_head < half, 1.0, -1.0)
    return jnp.where(cos_part | one_part, 1.0, jnp.where(sin_part, sign, 0.0)).astype(BF16)


def _in_proj_kernel(x_ref, pos_ref, invf_ref, w_ref,
                    qkv0_ref, qkv1_ref, qkv2_ref, ga_ref, gs_ref, stage_ref, *, splits, sub):
    half = ROT_DIM // 2
    spread = _rope_spread()
    first = lax.broadcasted_iota(jnp.int32, (sub, LANES), 1) % HEAD_DIM < half
    s0, s1, s2, s3, s4, s5 = splits
    tn = (((0,), (0,)), ((), ()))

    for r0 in range(0, x_ref.shape[0], sub):
        xb = x_ref[r0:r0 + sub, :].astype(BF16)

        ang = invf_ref[...] * pos_ref[:, r0:r0 + sub].astype(F32)
        tab = jnp.concatenate([jnp.cos(ang), jnp.sin(ang), jnp.ones((half, sub), F32),
                               jnp.zeros((ROPE_ROWS - 3 * half, sub), F32)], axis=0)
        tab_hi = tab.astype(BF16)
        tab_lo = (tab - tab_hi.astype(F32)).astype(BF16)
        cs = (lax.dot_general(tab_hi, spread, tn, preferred_element_type=F32)
              + lax.dot_general(tab_lo, spread, tn, preferred_element_type=F32))
        cos = cs[:, :LANES]
        sin = cs[:, LANES:]

        def rotate(z):
            cols = []
            for c in range(z.shape[1] // LANES):
                zc = z[:, c * LANES:(c + 1) * LANES]
                zs = zc * sin
                up = pltpu.roll(jnp.where(first, zs, 0.0), half, axis=1)
                dn = pltpu.roll(jnp.where(first, 0.0, zs), LANES - half, axis=1)
                cols.append(zc * cos + up + dn)
            return jnp.concatenate(cols, axis=1)

        def proj(lo, hi):
            return jnp.dot(xb, w_ref[:, lo:hi], preferred_element_type=F32)

        q = rotate(proj(0, s0)) * (HEAD_DIM ** -0.5)
        k = rotate(proj(s0, s1))
        v = proj(s1, s2)
        for gi, out_ref in enumerate((qkv0_ref, qkv1_ref, qkv2_ref)):
            dil = ATTN_PATTERNS[gi][1]
            c0 = gi * GROUP_WIDTH
            qkv = jnp.concatenate([a[:, c0:c0 + GROUP_WIDTH] for a in (q, k, v)], axis=1)
            n = sub // dil
            i0 = r0 // dil
            if dil == 1:
                out_ref[0, i0:i0 + n, :] = qkv.astype(BF16)
            else:
                n_col = qkv.shape[1] // LANES
                stage = stage_ref.at[r0 // sub, gi - 1]
                for c in range(n_col):
                    stage[c] = qkv[:, c * LANES:(c + 1) * LANES]
                for r in range(dil):
                    rows = pl.ds(r, n, stride=dil)
                    out_ref[r, i0:i0 + n, :] = jnp.concatenate(
                        [stage[c, rows, :] for c in range(n_col)], axis=1).astype(BF16)
        ga_ref[r0:r0 + sub, :] = _sigmoid(proj(s3, s4)).astype(BF16)
        gs_ref[r0:r0 + sub, :] = _sigmoid(proj(s4, s5)).astype(BF16)


def _in_proj(x1, positions, w_in, *, batch, seq, ssm_width, tm=1024, sub=256):
    t, d = x1.shape
    splits = (ATTN_WIDTH, 2 * ATTN_WIDTH, 3 * ATTN_WIDTH, 3 * ATTN_WIDTH + ssm_width,
              3 * ATTN_WIDTH + ssm_width + d, 3 * ATTN_WIDTH + ssm_width + 2 * d)
    assert splits[-1] == w_in.shape[1] and seq % tm == 0
    half = ROT_DIM // 2
    invf = (ROPE_THETA ** (-jnp.arange(half, dtype=F32) * 2.0 / ROT_DIM)).reshape(half, 1)
    tiles = seq // tm
    qkv_w = 3 * GROUP_WIDTH
    qkv_specs = [pl.BlockSpec((None, dil, tm // dil, qkv_w), lambda i: (i // tiles, 0, i % tiles, 0))
                 for _, dil in ATTN_PATTERNS]
    qkv_shapes = [jax.ShapeDtypeStruct((batch, dil, seq // dil, qkv_w), BF16) for _, dil in ATTN_PATTERNS]
    widths = (d, d)
    kern = functools.partial(_in_proj_kernel, splits=splits, sub=sub)
    n_strided = sum(dil > 1 for _, dil in ATTN_PATTERNS)
    return pl.pallas_call(
        kern,
        grid=(t // tm,),
        in_specs=[
            pl.BlockSpec((tm, d), lambda i: (i, 0)),
            pl.BlockSpec((1, tm), lambda i: (0, i)),
            _resident((half, 1)),
            _resident(w_in.shape),
        ],
        out_specs=qkv_specs + [pl.BlockSpec((tm, w), lambda i: (i, 0)) for w in widths],
        out_shape=qkv_shapes + [jax.ShapeDtypeStruct((t, w), BF16) for w in widths],
        scratch_shapes=[pltpu.VMEM((tm // sub, n_strided, qkv_w // LANES, sub, LANES), F32)],
        compiler_params=_params("parallel"),
        name="in_proj",
    )(x1, positions.reshape(1, t), invf, w_in.astype(BF16))


def _attn_kernel(q_ref, kc_ref, kp_ref, vc_ref, vp_ref, o_ref, st_ref, *, n_sub):
    first = pl.program_id(2) == 0
    row = lax.broadcasted_iota(jnp.int32, (ATTN_BLK, 2 * ATTN_BLK), 0)
    col = lax.broadcasted_iota(jnp.int32, (ATTN_BLK, 2 * ATTN_BLK), 1)
    band = (col >= row) & (col <= row + ATTN_BLK)
    lane = lax.broadcasted_iota(jnp.int32, (ATTN_BLK, LANES), 1)
    lo_half = lane < HEAD_DIM

    for j in range(n_sub):
        r0 = j * ATTN_BLK
        if j == 0:
            valid = band & ((col >= ATTN_BLK) | jnp.logical_not(first))
        else:
            valid = band
        stats = jnp.zeros((ATTN_BLK, LANES), F32)
        for hp in range(GROUP_WIDTH // LANES):
            c0 = hp * LANES
            q2 = q_ref[r0:r0 + ATTN_BLK, c0:c0 + LANES]
            if j == 0:
                kk = jnp.concatenate([kp_ref[:, c0:c0 + LANES], kc_ref[0:ATTN_BLK, c0:c0 + LANES]], axis=0)
                vv = jnp.concatenate([vp_ref[:, c0:c0 + LANES], vc_ref[0:ATTN_BLK, c0:c0 + LANES]], axis=0)
            else:
                kk = kc_ref[r0 - ATTN_BLK:r0 + ATTN_BLK, c0:c0 + LANES]
                vv = vc_ref[r0 - ATTN_BLK:r0 + ATTN_BLK, c0:c0 + LANES]
            outs = []
            for hh in range(2):
                qm = jnp.where(lo_half if hh == 0 else jnp.logical_not(lo_half), q2, jnp.zeros_like(q2))
                s = lax.dot_general(qm, kk, (((1,), (1,)), ((), ())), preferred_element_type=F32)
                s = jnp.where(valid, s, NEG_INF)
                m = jnp.max(s, axis=1, keepdims=True)
                p = jnp.exp(s - m)
                den = jnp.sum(p, axis=1, keepdims=True)
                o = jnp.dot(p.astype(BF16), vv, preferred_element_type=F32) / den
                outs.append(o)
                h = 2 * hp + hh
                stats = jnp.where(lane == h, m, stats)
                stats = jnp.where(lane == HEADS_PER_GROUP + h, den, stats)
            o_ref[r0:r0 + ATTN_BLK, c0:c0 + LANES] = jnp.where(lo_half, outs[0], outs[1]).astype(BF16)
        st_ref[r0:r0 + ATTN_BLK, :] = stats


def _dilated_attention(qkv):
    batch, dilation, n, _ = qkv.shape
    qb = min(512, n)
    n_sub = qb // ATTN_BLK

    def cur(part):
        return pl.BlockSpec((None, None, qb, GROUP_WIDTH), lambda b, r, i: (b, r, i, part))

    def prev(part):
        return pl.BlockSpec((None, None, ATTN_BLK, GROUP_WIDTH),
                            lambda b, r, i: (b, r, jnp.maximum(i * n_sub - 1, 0), part))

    return pl.pallas_call(
        functools.partial(_attn_kernel, n_sub=n_sub),
        grid=(batch, dilation, n // qb),
        in_specs=[cur(0), cur(1), prev(1), cur(2), prev(2)],
        out_specs=[pl.BlockSpec((None, None, qb, GROUP_WIDTH), lambda b, r, i: (b, r, i, 0)),
                   pl.BlockSpec((None, None, qb, LANES), lambda b, r, i: (b, r, i, 0))],
        out_shape=[jax.ShapeDtypeStruct((batch, dilation, n, GROUP_WIDTH), BF16),
                   jax.ShapeDtypeStruct((batch, dilation, n, LANES), F32)],
        compiler_params=_params("parallel", "parallel", "parallel"),
        name=f"attn_d{dilation}",
    )(qkv, qkv, qkv, qkv, qkv)


def _ssm_in_kernel(*refs, n_grp, ch):
    *x_refs, w_ref, ut_ref = refs
    for s in range(0, SSM_CHUNK, 2):
        xs = jnp.concatenate(
            [jnp.concatenate([x_ref[pl.ds(s + i, ch, stride=SSM_CHUNK), :] for x_ref in x_refs], axis=1)
             for i in range(2)], axis=0).astype(BF16)
        ut = lax.dot_general(w_ref[...], xs, (((1,), (1,)), ((), ())), preferred_element_type=F32)
        for i in range(2):
            ut_ref[:, (s + i) * SSM_GROUP:(s + i + 1) * SSM_GROUP, :] = (
                ut[:, i * ch:(i + 1) * ch].reshape(n_grp, SSM_GROUP, ch).astype(BF16))


def _ssm_in(x1, w_u_t, *, ch=LANES):
    t, d = x1.shape
    width = w_u_t.shape[0]
    n_grp = width // SSM_GROUP
    n_chunk_all = t // SSM_CHUNK
    tm = ch * SSM_CHUNK
    slabs = [pl.BlockSpec((tm, LANES), lambda i, c=c: (i, c)) for c in range(d // LANES)]
    return pl.pallas_call(
        functools.partial(_ssm_in_kernel, n_grp=n_grp, ch=ch),
        grid=(n_chunk_all // ch,),
        in_specs=slabs + [_resident((width, d))],
        out_specs=pl.BlockSpec((n_grp, SSM_ROW, ch), lambda i: (0, 0, i)),
        out_shape=jax.ShapeDtypeStruct((n_grp, SSM_ROW, n_chunk_all), BF16),
        compiler_params=_params("parallel"),
        name="ssm_in",
    )(*([x1] * (d // LANES)), w_u_t)


def _hdot(a, b):
    return jnp.dot(a, b, preferred_element_type=F32, precision=lax.Precision.HIGHEST)


def _cmul(ar, ai, br, bi):
    return ar * br - ai * bi, ar * bi + ai * br


def _cpowers(lbr, lbi, n, bits):
    out_r = jnp.ones(n.shape, F32)
    out_i = jnp.zeros(n.shape, F32)
    for b in range(bits):
        sel = (n & (1 << b)) != 0
        out_r, out_i = _cmul(out_r, out_i, jnp.where(sel, lbr, 1.0), jnp.where(sel, lbi, 0.0))
        lbr, lbi = _cmul(lbr, lbi, lbr, lbi)
    return out_r, out_i


def _ssm_disc_kernel(are_ref, aim_ref, ldt_ref, lbr_ref, lbi_ref, cfr_ref, cfi_ref):
    dt = jnp.exp(ldt_ref[...])
    lam_r = jnp.minimum(are_ref[...], -1e-4)
    lam_i = aim_ref[...]
    mag = jnp.exp(lam_r * dt)
    lbr = mag * jnp.cos(lam_i * dt)
    lbi = mag * jnp.sin(lam_i * dt)
    inv = 1.0 / (lam_r * lam_r + lam_i * lam_i)
    lbr_ref[...] = lbr
    lbi_ref[...] = lbi
    cfr_ref[...] = ((lbr - 1.0) * lam_r + lbi * lam_i) * inv
    cfi_ref[...] = (lbi * lam_r - (lbr - 1.0) * lam_i) * inv


def _ssm_prep_kernel(lbr_s, lbi_s, cfr_s, cfi_s, are_l, aim_l, ldt_ref, bre_ref, bim_ref, cre_ref, cim_ref,
                     tt_ref, win_ref, wout_ref, apr_ref, api_ref, *, gb):
    p = lbr_s.shape[1]
    expand = (lax.broadcasted_iota(jnp.int32, (SSM_CHUNK, SSM_ROW), 1) // SSM_GROUP
              == lax.broadcasted_iota(jnp.int32, (SSM_CHUNK, SSM_ROW), 0)).astype(F32)
    expand_t = (lax.broadcasted_iota(jnp.int32, (SSM_ROW, SSM_CHUNK), 0) // SSM_GROUP
                == lax.broadcasted_iota(jnp.int32, (SSM_ROW, SSM_CHUNK), 1)).astype(F32)
    rem = SSM_CHUNK - 1 - lax.broadcasted_iota(jnp.int32, (p, SSM_CHUNK), 1)
    tp1 = lax.broadcasted_iota(jnp.int32, (SSM_CHUNK, p), 0) + 1
    col = lax.broadcasted_iota(jnp.int32, (p, SSM_POWERS), 1)
    lane = lax.broadcasted_iota(jnp.int32, (SSM_GROUP, SSM_ROW), 1)
    chunk_bits = SSM_CHUNK.bit_length()

    for g in range(gb):
        lbr = lbr_s[g]
        lbi = lbi_s[g]
        bre = bre_ref[g]
        bim = bim_ref[g]
        bbar_r, bbar_i = _cmul(cfr_s[g], cfi_s[g], bre, bim)
        pw_r, pw_i = _cpowers(lbr, lbi, rem, chunk_bits - 1)
        win_r, win_i = _cmul(_hdot(pw_r, expand), _hdot(pw_i, expand), bbar_r, bbar_i)
        win_ref[g] = jnp.concatenate([win_r, win_i], axis=0).astype(BF16)

        a_r, a_i = _cmul(pw_r[:, 0:1], pw_i[:, 0:1], lbr, lbi)
        apr = jnp.zeros((p, SSM_POWERS), F32)
        api = jnp.zeros((p, SSM_POWERS), F32)
        for j in range(SSM_POWERS):
            apr = jnp.where(col == j, a_r, apr)
            api = jnp.where(col == j, a_i, api)
            a_r, a_i = _cmul(a_r, a_i, a_r, a_i)
        apr_ref[g] = apr
        api_ref[g] = api

        dt = jnp.exp(ldt_ref[g])
        mag_l = jnp.exp(jnp.minimum(are_l[g], -1e-4) * dt)
        e_r, e_i = _cpowers(mag_l * jnp.cos(aim_l[g] * dt), mag_l * jnp.sin(aim_l[g] * dt), tp1, chunk_bits)
        cre = cre_ref[g]
        cim = cim_ref[g]
        wo_r, wo_i = _cmul(_hdot(expand_t, e_r), _hdot(expand_t, e_i), cre, cim)
        wout_ref[g] = jnp.concatenate([wo_r, -wo_i], axis=1).astype(BF16)

        hrev = _hdot(cre[0:SSM_GROUP, :], win_r) - _hdot(cim[0:SSM_GROUP, :], win_i)
        blocks = []
        for t in range(SSM_CHUNK):
            hi = (t + 1) * SSM_GROUP
            rolled = hrev if hi == SSM_ROW else pltpu.roll(hrev, hi, axis=1)
            blocks.append(jnp.where(lane < hi, rolled, 0.0))
        tt_ref[g] = jnp.concatenate(blocks, axis=0).astype(BF16)


def _ssm_prep(a_re, a_im, log_dt, b_re, b_im, c_re, c_im, *, gb=4):
    g, p = a_re.shape
    row = SSM_ROW
    whole = lambda shape: pl.BlockSpec(shape, lambda: (0,) * len(shape))
    disc = pl.pallas_call(
        _ssm_disc_kernel,
        in_specs=[whole((p, g)), whole((p, g)), whole((1, g))],
        out_specs=[whole((p, g))] * 4,
        out_shape=[jax.ShapeDtypeStruct((p, g), F32)] * 4,
        name="ssm_disc",
    )(a_re.T, a_im.T, log_dt.reshape(1, g))
    lbr_s, lbi_s, cfr_s, cfi_s = [a.T.reshape(g, p, 1) for a in disc]

    blk = lambda shape: pl.BlockSpec((gb,) + shape, lambda i: (i, 0, 0))
    b_t = lambda b: jnp.tile(b, (1, 1, SSM_CHUNK))
    c_t = lambda c: jnp.tile(c, (1, SSM_CHUNK, 1))
    out_shapes = [
        jax.ShapeDtypeStruct((g, row, row), BF16),
        jax.ShapeDtypeStruct((g, 2 * p, row), BF16),
        jax.ShapeDtypeStruct((g, row, 2 * p), BF16),
        jax.ShapeDtypeStruct((g, p, SSM_POWERS), F32), jax.ShapeDtypeStruct((g, p, SSM_POWERS), F32),
    ]
    return pl.pallas_call(
        functools.partial(_ssm_prep_kernel, gb=gb),
        grid=(g // gb,),
        in_specs=[blk((p, 1))] * 4 + [blk((1, p)), blk((1, p)), blk((1, 1)),
                  blk((p, row)), blk((p, row)), blk((row, p)), blk((row, p))],
        out_specs=[blk(s.shape[1:]) for s in out_shapes],
        out_shape=out_shapes,
        compiler_params=_params("parallel"),
        name="ssm_prep",
    )(lbr_s, lbi_s, cfr_s, cfi_s, a_re.reshape(g, 1, p), a_im.reshape(g, 1, p),
      log_dt.reshape(g, 1, 1), b_t(b_re), b_t(b_im), c_t(c_re), c_t(c_im))


def _ssm_kernel(ut_ref, tt_ref, win_ref, wout_ref, apr_ref, api_ref, dsk_ref, y_ref, *, gb, n_chunk):
    nk = ut_ref.shape[-1]
    p = apr_ref.shape[1]
    pos = lax.broadcasted_iota(jnp.int32, (p, nk), 1) % n_chunk
    n_steps = (n_chunk - 1).bit_length()
    assert n_steps <= SSM_POWERS
    for g in range(gb):
        ut = ut_ref[g]
        sl = jnp.dot(win_ref[g], ut, preferred_element_type=F32)
        s_r, s_i = sl[:p], sl[p:]
        apr = apr_ref[g]
        api = api_ref[g]
        for j in range(n_steps):
            sh = 1 << j
            a_r = apr[:, j:j + 1]
            a_i = api[:, j:j + 1]
            keep = pos >= sh
            p_r = jnp.where(keep, pltpu.roll(s_r, sh, axis=1), 0.0)
            p_i = jnp.where(keep, pltpu.roll(s_i, sh, axis=1), 0.0)
            s_r, s_i = s_r + a_r * p_r - a_i * p_i, s_i + a_r * p_i + a_i * p_r
        keep = pos >= 1
        prev = jnp.concatenate([jnp.where(keep, pltpu.roll(s_r, 1, axis=1), 0.0),
                                jnp.where(keep, pltpu.roll(s_i, 1, axis=1), 0.0)], axis=0).astype(BF16)
        y = jnp.dot(tt_ref[g], ut, preferred_element_type=F32)
        y += jnp.dot(wout_ref[g], prev, preferred_element_type=F32)
        y += dsk_ref[g] * ut.astype(F32)
        y_ref[g] = _gelu_tanh(y).astype(BF16)


def _ssm(ut, prep, dsk, *, n_chunk, gb=4):
    tt, win, wout, apr, api = prep
    g, row, nk = ut.shape
    p = apr.shape[1]
    blk = lambda shape: pl.BlockSpec((gb,) + shape, lambda i: (i, 0, 0))
    kern = functools.partial(_ssm_kernel, gb=gb, n_chunk=n_chunk)
    return pl.pallas_call(
        kern,
        grid=(g // gb,),
        in_specs=[blk((row, nk)), blk((row, row)), blk((2 * p, row)), blk((row, 2 * p)),
                  blk((p, SSM_POWERS)), blk((p, SSM_POWERS)), blk((row, 1))],
        out_specs=blk((row, nk)),
        out_shape=jax.ShapeDtypeStruct((g, row, nk), BF16),
        compiler_params=_params("parallel"),
        name="ssm",
    )(ut, tt, win, wout, apr, api, dsk)


def _ssm_out_kernel(yt_ref, wglu_ref, bglu_ref, wso_ref, o_ref, stage_ref, *, t_per_pass):
    n_grp, _, ch = yt_ref.shape
    width = n_grp * SSM_GROUP
    n_col = o_ref.shape[1] // LANES
    for t0 in range(0, SSM_CHUNK, t_per_pass):
        y = jnp.concatenate(
            [yt_ref[:, t * SSM_GROUP:(t + 1) * SSM_GROUP, :].reshape(width, ch)
             for t in range(t0, t0 + t_per_pass)], axis=1)
        gate = jnp.dot(wglu_ref[...], y, preferred_element_type=F32) + bglu_ref[...]
        glu = (y.astype(F32) * _sigmoid(gate)).astype(BF16)
        yo = lax.dot_general(glu, wso_ref[...], (((0,), (0,)), ((), ())), preferred_element_type=F32)
        for i in range(t_per_pass):
            rows = pl.ds(t0 + i, ch, stride=SSM_CHUNK)
            for c in range(n_col):
                stage_ref[c, rows, :] = yo[i * ch:(i + 1) * ch, c * LANES:(c + 1) * LANES]
    o_ref[...] = jnp.concatenate([stage_ref[c] for c in range(n_col)], axis=1).astype(BF16)


def _ssm_out(yt, w_glu, b_glu, w_ssm_out, *, ch=LANES, t_per_pass=4):
    n_grp, row, n_chunk_all = yt.shape
    width, d = w_ssm_out.shape
    tm = ch * SSM_CHUNK
    return pl.pallas_call(
        functools.partial(_ssm_out_kernel, t_per_pass=t_per_pass),
        grid=(n_chunk_all // ch,),
        in_specs=[pl.BlockSpec((n_grp, row, ch), lambda i: (0, 0, i)),
                  _resident((width, width)), _resident((width, 1)), _resident((width, d))],
        out_specs=pl.BlockSpec((tm, d), lambda i: (i, 0)),
        out_shape=jax.ShapeDtypeStruct((n_chunk_all * SSM_CHUNK, d), BF16),
        scratch_shapes=[pltpu.VMEM((d // LANES, tm, LANES), F32)],
        compiler_params=_params("parallel"),
        name="ssm_out",
    )(yt, w_glu.T.astype(BF16), b_glu.reshape(width, 1), w_ssm_out.astype(BF16))


def _mix_out_kernel(x_ref, o0, o1, o2, st0, st1, st2, ys_ref, ga_ref, gs_ref,
                    wao_ref, wo_ref, g_ref, b_ref, out_ref, o_stage, st_stage, *, sub):
    lane = lax.broadcasted_iota(jnp.int32, (sub, LANES), 1)
    lo_half = lane < HEAD_DIM

    def token_order(ref, stage, r0):
        dil = ref.shape[0]
        n = sub // dil
        i0 = r0 // dil
        if dil == 1:
            return ref[0, i0:i0 + n, :].astype(F32)
        n_col = ref.shape[2] // LANES
        for r in range(dil):
            blk = ref[r, i0:i0 + n, :].astype(F32)
            for c in range(n_col):
                stage[c, pl.ds(r, n, stride=dil), :] = blk[:, c * LANES:(c + 1) * LANES]
        return jnp.concatenate([stage[c] for c in range(n_col)], axis=1)

    def head_cols(vals):
        cols = []
        for hp in range(GROUP_WIDTH // LANES):
            cols.append(jnp.where(lo_half, vals[2 * hp], vals[2 * hp + 1]))
        return jnp.concatenate(cols, axis=1)

    for r0 in range(0, x_ref.shape[0], sub):
        rows = slice(r0, r0 + sub)
        k = r0 // sub
        sts = [token_order(st, st_stage.at[k, i], r0) for i, st in enumerate((st0, st1, st2))]
        outs = [token_order(o, o_stage.at[k, i], r0) for i, o in enumerate((o0, o1, o2))]

        dens = [pltpu.roll(st, LANES - HEADS_PER_GROUP, axis=1) for st in sts]
        mx = jnp.maximum(jnp.maximum(sts[0], sts[1]), sts[2])
        w = [den * jnp.exp(st - mx) for den, st in zip(dens, sts)]
        tot = jnp.where(lane < HEADS_PER_GROUP, w[0] + w[1] + w[2], 1.0)
        att = None
        for g in range(N_ATTN_GROUPS):
            wt = w[g] / tot
            term = head_cols([wt[:, h:h + 1] for h in range(HEADS_PER_GROUP)]) * outs[g]
            att = term if att is None else att + term
        y_attn = jnp.dot(att.astype(BF16), wao_ref[...], preferred_element_type=F32)

        merged = (ga_ref[rows, :].astype(F32) * y_attn
                  + gs_ref[rows, :].astype(F32) * ys_ref[rows, :].astype(F32))
        mix = jnp.dot(merged.astype(BF16), wo_ref[...], preferred_element_type=F32)
        out_ref[rows, :] = _layer_norm(DEEPNORM_ALPHA * x_ref[rows, :] + mix, g_ref[...], b_ref[...])


def _mix_out(x1, os_, sts, y_ssm, ga, gs, w_attn_out, w_o, g, b, *, seq, tm=1024, sub=256):
    t, d = x1.shape
    tiles = seq // tm
    tile = lambda w: pl.BlockSpec((tm, w), lambda i: (i, 0))

    def by_class(a):
        _, dil, n, w = a.shape
        return pl.BlockSpec((None, dil, tm // dil, w), lambda i: (i // tiles, 0, i % tiles, 0))

    n_grp = len(os_)
    return pl.pallas_call(
        functools.partial(_mix_out_kernel, sub=sub),
        grid=(t // tm,),
        in_specs=[tile(d)] + [by_class(a) for a in os_] + [by_class(a) for a in sts] + [tile(d), tile(d), tile(d),
                  _resident(w_attn_out.shape), _resident(w_o.shape), _resident((1, d)), _resident((1, d))],
        out_specs=tile(d),
        out_shape=jax.ShapeDtypeStruct((t, d), F32),
        scratch_shapes=[pltpu.VMEM((tm // sub, n_grp, GROUP_WIDTH // LANES, sub, LANES), F32),
                        pltpu.VMEM((tm // sub, n_grp, 1, sub, LANES), F32)],
        compiler_params=_params("parallel"),
        name="mix_out",
    )(x1, *os_, *sts, y_ssm, ga, gs, w_attn_out.astype(BF16), w_o.astype(BF16),
      g.reshape(1, d), b.reshape(1, d))


def _layer(x, positions, w_in, w_attn_out, a_re, a_im, log_dt, b_re, b_im, c_re, c_im, d_skip,
           w_glu, b_glu, w_ssm_out, w_o, ffn1, ffn2, ln1, ln2, ln3):
    batch, seq, d = x.shape
    t = batch * seq
    n_grp = a_re.shape[0]
    ssm_width = n_grp * SSM_GROUP
    u_lo = 3 * ATTN_WIDTH

    x1 = _ffn_ln(x.reshape(t, d), *ffn1, *ln1)
    *qkvs, ga, gs = _in_proj(x1, positions, w_in, batch=batch, seq=seq, ssm_width=ssm_width)

    os_, sts = [], []
    for qkv, (window, dilation) in zip(qkvs, ATTN_PATTERNS):
        assert window == ATTN_BLK * dilation
        o, st = _dilated_attention(qkv)
        os_.append(o)
        sts.append(st)

    w_u = lax.optimization_barrier(w_in[:, u_lo:u_lo + ssm_width])
    ut = _ssm_in(x1, jnp.transpose(w_u).astype(BF16))
    prep = _ssm_prep(a_re, a_im, log_dt, b_re, b_im, c_re, c_im)
    dsk = jnp.tile(d_skip.reshape(n_grp, 1, SSM_GROUP), (1, SSM_CHUNK, 1)).reshape(n_grp, SSM_ROW, 1)
    yt = _ssm(ut, prep, dsk, n_chunk=seq // SSM_CHUNK)
    y_ssm = _ssm_out(yt, w_glu, b_glu, w_ssm_out)

    x2 = _mix_out(x1, os_, sts, y_ssm, ga, gs, w_attn_out, w_o, *ln2, seq=seq)
    x3 = _ffn_ln(x2, *ffn2, *ln3)
    return x3.reshape(batch, seq, d)


def kernel(x, positions, w_in, w_attn_out, a_re, a_im, log_dt, b_re, b_im, c_re, c_im, d_skip, w_glu, b_glu, w_ssm_out, w_o, ffn1_wg, ffn1_wu, ffn1_wd, ffn2_wg, ffn2_wu, ffn2_wd, ln1_g, ln1_b, ln2_g, ln2_b, ln3_g, ln3_b):
    depth = w_in.shape[0]
    for i in range(depth):
        x = _layer(x, positions, w_in[i], w_attn_out[i], a_re[i], a_im[i], log_dt[i], b_re[i], b_im[i],
                   c_re[i], c_im[i], d_skip[i], w_glu[i], b_glu[i], w_ssm_out[i], w_o[i],
                   (ffn1_wg[i], ffn1_wu[i], ffn1_wd[i]), (ffn2_wg[i], ffn2_wu[i], ffn2_wd[i]),
                   (ln1_g[i], ln1_b[i]), (ln2_g[i], ln2_b[i]), (ln3_g[i], ln3_b[i]))
    return x
```

```python
import functools

import jax
import jax.numpy as jnp
from jax import lax
from jax.experimental import pallas as pl
from jax.experimental.pallas import tpu as pltpu

F32 = jnp.float32
BF16 = jnp.bfloat16

HEAD_DIM = 64
HEADS_PER_GROUP = 4
GROUP_WIDTH = HEADS_PER_GROUP * HEAD_DIM
ATTN_PATTERNS = ((128, 1), (512, 4), (2048, 16))
N_ATTN_GROUPS = len(ATTN_PATTERNS)
ATTN_WIDTH = N_ATTN_GROUPS * GROUP_WIDTH
ROT_DIM = HEAD_DIM // 4
ROPE_THETA = 500000.0
ROPE_ROWS = 32
ATTN_BLK = 128
NEG_INF = -1e30
SSM_GROUP = 16
SSM_STATE = 64
SSM_CHUNK = 16
SSM_ROW = SSM_CHUNK * SSM_GROUP
SSM_POWERS = 16
CHUNK_PITCH = 24


def _scatter_pitch(stride):
    return CHUNK_PITCH if stride == SSM_CHUNK else stride
DEEPNORM_ALPHA = 2.0 ** 0.25
LN_EPS = 1e-5

LANES = 128
SUBLANES = 8
VMEM_LIMIT_BYTES = 56 * 1024 * 1024


def _params(*semantics):
    return pltpu.CompilerParams(dimension_semantics=semantics, vmem_limit_bytes=VMEM_LIMIT_BYTES)


def _resident(shape):
    zeros = (0,) * len(shape)
    return pl.BlockSpec(shape, lambda *_: zeros, pipeline_mode=pl.Buffered(1))


def _layer_norm(y, g, b):
    mu = jnp.mean(y, axis=-1, keepdims=True)
    yc = y - mu
    var = jnp.mean(yc * yc, axis=-1, keepdims=True)
    return yc * lax.rsqrt(var + LN_EPS) * g + b


def _sigmoid(x):
    return 1.0 / (1.0 + jnp.exp(-x))


def _gelu_tanh(x):
    c = 0.7978845608028654
    return 0.5 * x * (1.0 + jnp.tanh(c * (x + 0.044715 * (x * x * x))))


def _ffn_ln_kernel(x_ref, wg_ref, wu_ref, wd_ref, g_ref, b_ref, o_ref, *, chunks, sub):
    for r0 in range(0, x_ref.shape[0], sub):
        x = x_ref[r0:r0 + sub, :]
        xb = x.astype(BF16)
        acc = None
        for c0, c1 in chunks:
            gate = jnp.dot(xb, wg_ref[:, c0:c1], preferred_element_type=F32)
            up = jnp.dot(xb, wu_ref[:, c0:c1], preferred_element_type=F32)
            h = (gate * _sigmoid(gate) * up).astype(BF16)
            part = jnp.dot(h, wd_ref[c0:c1, :], preferred_element_type=F32)
            acc = part if acc is None else acc + part
        y = DEEPNORM_ALPHA * x + 0.5 * acc
        o_ref[r0:r0 + sub, :] = _layer_norm(y, g_ref[...], b_ref[...])


def _ffn_chunks(d_ff, width):
    edges = list(range(0, d_ff, width)) + [d_ff]
    return tuple(zip(edges[:-1], edges[1:]))


def _ffn_ln(x, wg, wu, wd, g, b, *, tm=1024, sub=256, chunk=1024):
    t, d = x.shape
    d_ff = wg.shape[1]
    kern = functools.partial(_ffn_ln_kernel, chunks=_ffn_chunks(d_ff, chunk), sub=sub)
    return pl.pallas_call(
        kern,
        grid=(t // tm,),
        in_specs=[
            pl.BlockSpec((tm, d), lambda i: (i, 0)),
            _resident((d, d_ff)), _resident((d, d_ff)), _resident((d_ff, d)),
            _resident((1, d)), _resident((1, d)),
        ],
        out_specs=pl.BlockSpec((tm, d), lambda i: (i, 0)),
        out_shape=jax.ShapeDtypeStruct((t, d), F32),
        compiler_params=_params("parallel"),
        name="ffn_ln",
    )(x, wg.astype(BF16), wu.astype(BF16), wd.astype(BF16), g.reshape(1, d), b.reshape(1, d))


def _rope_spread():
    half = ROT_DIM // 2
    rows = lax.broadcasted_iota(jnp.int32, (ROPE_ROWS, 2 * LANES), 0)
    cols = lax.broadcasted_iota(jnp.int32, (ROPE_ROWS, 2 * LANES), 1)
    in_head = cols % HEAD_DIM
    freq = in_head % half
    is_cos = cols < LANES
    rot = in_head < ROT_DIM
    cos_part = is_cos & rot & (rows == freq)
    one_part = is_cos & jnp.logical_not(rot) & (rows == 2 * half)
    sin_part = jnp.logical_not(is_cos) & rot & (rows == half + freq)
    sign = jnp.where(in_head < half, 1.0, -1.0)
    return jnp.where(cos_part | one_part, 1.0, jnp.where(sin_part, sign, 0.0)).astype(BF16)


def _in_proj_kernel(x_ref, pos_ref, invf_ref, w_ref,
                    qkv0_ref, qkv1_ref, qkv2_ref, ga_ref, gs_ref, stage_ref, *, splits, sub):
    half = ROT_DIM // 2
    spread = _rope_spread()
    first = lax.broadcasted_iota(jnp.int32, (sub, LANES), 1) % HEAD_DIM < half
    s0, s1, s2, s3, s4, s5 = splits
    tn = (((0,), (0,)), ((), ()))

    for r0 in range(0, x_ref.shape[0], sub):
        xb = x_ref[r0:r0 + sub, :].astype(BF16)

        ang = invf_ref[...] * pos_ref[:, r0:r0 + sub].astype(F32)
        tab = jnp.concatenate([jnp.cos(ang), jnp.sin(ang), jnp.ones((half, sub), F32),
                               jnp.zeros((ROPE_ROWS - 3 * half, sub), F32)], axis=0)
        tab_hi = tab.astype(BF16)
        tab_lo = (tab - tab_hi.astype(F32)).astype(BF16)
        cs = (lax.dot_general(tab_hi, spread, tn, preferred_element_type=F32)
              + lax.dot_general(tab_lo, spread, tn, preferred_element_type=F32))
        cos = cs[:, :LANES]
        sin = cs[:, LANES:]

        def rotate(z):
            cols = []
            for c in range(z.shape[1] // LANES):
                zc = z[:, c * LANES:(c + 1) * LANES]
                zs = zc * sin
                up = pltpu.roll(jnp.where(first, zs, 0.0), half, axis=1)
                dn = pltpu.roll(jnp.where(first, 0.0, zs), LANES - half, axis=1)
                cols.append(zc * cos + up + dn)
            return jnp.concatenate(cols, axis=1)

        def proj(lo, hi):
            return jnp.dot(xb, w_ref[:, lo:hi], preferred_element_type=F32)

        q = rotate(proj(0, s0)) * (HEAD_DIM ** -0.5)
        k = rotate(proj(s0, s1))
        v = proj(s1, s2)
        for gi, out_ref in enumerate((qkv0_ref, qkv1_ref, qkv2_ref)):
            dil = ATTN_PATTERNS[gi][1]
            c0 = gi * GROUP_WIDTH
            qkv = jnp.concatenate([a[:, c0:c0 + GROUP_WIDTH] for a in (q, k, v)], axis=1)
            n = sub // dil
            i0 = r0 // dil
            if dil == 1:
                out_ref[0, i0:i0 + n, :] = qkv.astype(BF16)
            else:
                n_col = qkv.shape[1] // LANES
                pitch = _scatter_pitch(dil)
                stage = stage_ref.at[r0 // sub, gi - 1]
                for c in range(n_col):
                    col = qkv[:, c * LANES:(c + 1) * LANES]
                    if pitch != dil:
                        col = jnp.concatenate([col.reshape(n, dil, LANES),
                                               jnp.zeros((n, pitch - dil, LANES), F32)], axis=1)
                        col = col.reshape(n * pitch, LANES)
                    stage[c, 0:n * pitch, :] = col
                for r in range(dil):
                    rows = pl.ds(r, n, stride=pitch)
                    out_ref[r, i0:i0 + n, :] = jnp.concatenate(
                        [stage[c, rows, :] for c in range(n_col)], axis=1).astype(BF16)
        ga_ref[r0:r0 + sub, :] = _sigmoid(proj(s3, s4)).astype(BF16)
        gs_ref[r0:r0 + sub, :] = _sigmoid(proj(s4, s5)).astype(BF16)


def _in_proj(x1, positions, w_in, *, batch, seq, ssm_width, tm=1024, sub=256):
    t, d = x1.shape
    splits = (ATTN_WIDTH, 2 * ATTN_WIDTH, 3 * ATTN_WIDTH, 3 * ATTN_WIDTH + ssm_width,
              3 * ATTN_WIDTH + ssm_width + d, 3 * ATTN_WIDTH + ssm_width + 2 * d)
    assert splits[-1] == w_in.shape[1] and seq % tm == 0
    half = ROT_DIM // 2
    invf = (ROPE_THETA ** (-jnp.arange(half, dtype=F32) * 2.0 / ROT_DIM)).reshape(half, 1)
    tiles = seq // tm
    qkv_w = 3 * GROUP_WIDTH
    qkv_specs = [pl.BlockSpec((None, dil, tm // dil, qkv_w), lambda i: (i // tiles, 0, i % tiles, 0))
                 for _, dil in ATTN_PATTERNS]
    qkv_shapes = [jax.ShapeDtypeStruct((batch, dil, seq // dil, qkv_w), BF16) for _, dil in ATTN_PATTERNS]
    widths = (d, d)
    kern = functools.partial(_in_proj_kernel, splits=splits, sub=sub)
    n_strided = sum(dil > 1 for _, dil in ATTN_PATTERNS)
    stage_rows = max(sub // dil * _scatter_pitch(dil) for _, dil in ATTN_PATTERNS)
    return pl.pallas_call(
        kern,
        grid=(t // tm,),
        in_specs=[
            pl.BlockSpec((tm, d), lambda i: (i, 0)),
            pl.BlockSpec((1, tm), lambda i: (0, i)),
            _resident((half, 1)),
            _resident(w_in.shape),
        ],
        out_specs=qkv_specs + [pl.BlockSpec((tm, w), lambda i: (i, 0)) for w in widths],
        out_shape=qkv_shapes + [jax.ShapeDtypeStruct((t, w), BF16) for w in widths],
        scratch_shapes=[pltpu.VMEM((tm // sub, n_strided, qkv_w // LANES, stage_rows, LANES), F32)],
        compiler_params=_params("parallel"),
        name="in_proj",
    )(x1, positions.reshape(1, t), invf, w_in)


def _attn_kernel(q_ref, kc_ref, kp_ref, vc_ref, vp_ref, o_ref, st_ref, *, n_sub):
    first = pl.program_id(2) == 0
    row = lax.broadcasted_iota(jnp.int32, (ATTN_BLK, 2 * ATTN_BLK), 0)
    col = lax.broadcasted_iota(jnp.int32, (ATTN_BLK, 2 * ATTN_BLK), 1)
    band = (col >= row) & (col <= row + ATTN_BLK)
    lane = lax.broadcasted_iota(jnp.int32, (ATTN_BLK, LANES), 1)
    lo_half = lane < HEAD_DIM

    n_pair = GROUP_WIDTH // LANES
    tiles = [(j, hp, hh) for j in range(n_sub) for hp in range(n_pair) for hh in range(2)]

    def window(ref_prev, ref_cur, j, c0):
        r0 = j * ATTN_BLK
        if j == 0:
            return jnp.concatenate([ref_prev[:, c0:c0 + LANES], ref_cur[0:ATTN_BLK, c0:c0 + LANES]], axis=0)
        return ref_cur[r0 - ATTN_BLK:r0 + ATTN_BLK, c0:c0 + LANES]

    def scores(tile):
        j, hp, hh = tile
        q2 = q_ref[j * ATTN_BLK:(j + 1) * ATTN_BLK, hp * LANES:(hp + 1) * LANES]
        qm = jnp.where(lo_half if hh == 0 else jnp.logical_not(lo_half), q2, jnp.zeros_like(q2))
        return lax.dot_general(qm, window(kp_ref, kc_ref, j, hp * LANES), (((1,), (1,)), ((), ())),
                               preferred_element_type=F32)

    def softmax(tile, s):
        valid = band & ((col >= ATTN_BLK) | jnp.logical_not(first)) if tile[0] == 0 else band
        s = jnp.where(valid, s, NEG_INF)
        m = jnp.max(s, axis=1, keepdims=True)
        p = jnp.exp(s - m)
        return p.astype(BF16), m, jnp.sum(p, axis=1, keepdims=True)

    def weighted_values(tile, p, den):
        j, hp, _ = tile
        return jnp.dot(p, window(vp_ref, vc_ref, j, hp * LANES), preferred_element_type=F32) / den

    s_q, p_q, outs, stats = {}, {}, {}, {}
    for step in range(len(tiles) + 2):
        if step < len(tiles):
            s_q[step] = scores(tiles[step])
        if 0 <= step - 1 < len(tiles):
            p_q[step - 1] = softmax(tiles[step - 1], s_q.pop(step - 1))
        if 0 <= step - 2 < len(tiles):
            t = step - 2
            j, hp, hh = tiles[t]
            p, m, den = p_q.pop(t)
            outs[hh] = weighted_values(tiles[t], p, den)
            h = 2 * hp + hh
            st = stats.get(j, jnp.zeros((ATTN_BLK, LANES), F32))
            st = jnp.where(lane == h, m, st)
            stats[j] = jnp.where(lane == HEADS_PER_GROUP + h, den, st)
            rows = slice(j * ATTN_BLK, (j + 1) * ATTN_BLK)
            if hh == 1:
                o_ref[rows, hp * LANES:(hp + 1) * LANES] = jnp.where(lo_half, outs[0], outs[1]).astype(BF16)
                if hp == n_pair - 1:
                    st_ref[rows, :] = stats.pop(j)


def _dilated_attention(qkv):
    batch, dilation, n, _ = qkv.shape
    qb = min(1024, n)
    n_sub = qb // ATTN_BLK

    def cur(part):
        return pl.BlockSpec((None, None, qb, GROUP_WIDTH), lambda b, r, i: (b, r, i, part))

    def prev(part):
        return pl.BlockSpec((None, None, ATTN_BLK, GROUP_WIDTH),
                            lambda b, r, i: (b, r, jnp.maximum(i * n_sub - 1, 0), part))

    return pl.pallas_call(
        functools.partial(_attn_kernel, n_sub=n_sub),
        grid=(batch, dilation, n // qb),
        in_specs=[cur(0), cur(1), prev(1), cur(2), prev(2)],
        out_specs=[pl.BlockSpec((None, None, qb, GROUP_WIDTH), lambda b, r, i: (b, r, i, 0)),
                   pl.BlockSpec((None, None, qb, LANES), lambda b, r, i: (b, r, i, 0))],
        out_shape=[jax.ShapeDtypeStruct((batch, dilation, n, GROUP_WIDTH), BF16),
                   jax.ShapeDtypeStruct((batch, dilation, n, LANES), F32)],
        compiler_params=_params("parallel", "parallel", "parallel"),
        name=f"attn_d{dilation}",
    )(qkv, qkv, qkv, qkv, qkv)


def _ssm_in_kernel(*refs, n_grp, ch, n_w):
    x_refs, w_refs, ut_ref = refs[:-n_w - 1], refs[-n_w - 1:-1], refs[-1]
    w = jnp.concatenate([w_ref[...] for w_ref in w_refs], axis=1)
    for s in range(0, SSM_CHUNK, 2):
        xs = jnp.concatenate(
            [jnp.concatenate([x_ref[pl.ds(s + i, ch, stride=SSM_CHUNK), :] for x_ref in x_refs], axis=1)
             for i in range(2)], axis=0).astype(BF16)
        ut = lax.dot_general(w, xs, (((0,), (1,)), ((), ())), preferred_element_type=F32)
        for i in range(2):
            ut_ref[:, (s + i) * SSM_GROUP:(s + i + 1) * SSM_GROUP, :] = (
                ut[:, i * ch:(i + 1) * ch].reshape(n_grp, SSM_GROUP, ch).astype(BF16))


def _ssm_in(x1, w_in_b, u_lo, width, *, ch=LANES, wblk=256):
    t, d = x1.shape
    assert u_lo % wblk == 0 and width % wblk == 0
    n_grp = width // SSM_GROUP
    n_chunk_all = t // SSM_CHUNK
    tm = ch * SSM_CHUNK
    n_w = width // wblk
    slabs = [pl.BlockSpec((tm, LANES), lambda i, c=c: (i, c)) for c in range(d // LANES)]
    w_cols = [pl.BlockSpec((d, wblk), lambda i, c=c: (0, u_lo // wblk + c), pipeline_mode=pl.Buffered(1))
              for c in range(n_w)]
    return pl.pallas_call(
        functools.partial(_ssm_in_kernel, n_grp=n_grp, ch=ch, n_w=n_w),
        grid=(n_chunk_all // ch,),
        in_specs=slabs + w_cols,
        out_specs=pl.BlockSpec((n_grp, SSM_ROW, ch), lambda i: (0, 0, i)),
        out_shape=jax.ShapeDtypeStruct((n_grp, SSM_ROW, n_chunk_all), BF16),
        compiler_params=_params("parallel"),
        name="ssm_in",
    )(*([x1] * (d // LANES)), *([w_in_b] * n_w))


def _hdot(a, b):
    return jnp.dot(a, b, preferred_element_type=F32, precision=lax.Precision.HIGHEST)


def _cmul(ar, ai, br, bi):
    return ar * br - ai * bi, ar * bi + ai * br


def _cpowers(lbr, lbi, n, bits):
    out_r = jnp.ones(n.shape, F32)
    out_i = jnp.zeros(n.shape, F32)
    for b in range(bits):
        sel = (n & (1 << b)) != 0
        out_r, out_i = _cmul(out_r, out_i, jnp.where(sel, lbr, 1.0), jnp.where(sel, lbi, 0.0))
        lbr, lbi = _cmul(lbr, lbi, lbr, lbi)
    return out_r, out_i


def _ssm_disc_kernel(are_ref, aim_ref, ldt_ref, lbr_ref, lbi_ref, cfr_ref, cfi_ref):
    dt = jnp.exp(ldt_ref[...])
    lam_r = jnp.minimum(are_ref[...], -1e-4)
    lam_i = aim_ref[...]
    mag = jnp.exp(lam_r * dt)
    lbr = mag * jnp.cos(lam_i * dt)
    lbi = mag * jnp.sin(lam_i * dt)
    inv = 1.0 / (lam_r * lam_r + lam_i * lam_i)
    lbr_ref[...] = lbr
    lbi_ref[...] = lbi
    cfr_ref[...] = ((lbr - 1.0) * lam_r + lbi * lam_i) * inv
    cfi_ref[...] = (lbi * lam_r - (lbr - 1.0) * lam_i) * inv


def _ssm_prep_kernel(lbr_s, lbi_s, cfr_s, cfi_s, are_l, aim_l, ldt_ref, bre_ref, bim_ref, cre_ref, cim_ref,
                     tt_ref, win_ref, wout_ref, apr_ref, api_ref, *, gb):
    p = lbr_s.shape[1]
    expand = (lax.broadcasted_iota(jnp.int32, (SSM_CHUNK, SSM_ROW), 1) // SSM_GROUP
              == lax.broadcasted_iota(jnp.int32, (SSM_CHUNK, SSM_ROW), 0)).astype(F32)
    expand_t = (lax.broadcasted_iota(jnp.int32, (SSM_ROW, SSM_CHUNK), 0) // SSM_GROUP
                == lax.broadcasted_iota(jnp.int32, (SSM_ROW, SSM_CHUNK), 1)).astype(F32)
    rem = SSM_CHUNK - 1 - lax.broadcasted_iota(jnp.int32, (p, SSM_CHUNK), 1)
    tp1 = lax.broadcasted_iota(jnp.int32, (SSM_CHUNK, p), 0) + 1
    col = lax.broadcasted_iota(jnp.int32, (p, SSM_POWERS), 1)
    lane = lax.broadcasted_iota(jnp.int32, (SSM_GROUP, SSM_ROW), 1)
    chunk_bits = SSM_CHUNK.bit_length()

    for g in range(gb):
        lbr = lbr_s[g]
        lbi = lbi_s[g]
        bre = bre_ref[g]
        bim = bim_ref[g]
        bbar_r, bbar_i = _cmul(cfr_s[g], cfi_s[g], bre, bim)
        pw_r, pw_i = _cpowers(lbr, lbi, rem, chunk_bits - 1)
        win_r, win_i = _cmul(_hdot(pw_r, expand), _hdot(pw_i, expand), bbar_r, bbar_i)
        win_ref[g] = jnp.concatenate([win_r, win_i], axis=0).astype(BF16)

        a_r, a_i = _cmul(pw_r[:, 0:1], pw_i[:, 0:1], lbr, lbi)
        apr = jnp.zeros((p, SSM_POWERS), F32)
        api = jnp.zeros((p, SSM_POWERS), F32)
        for j in range(SSM_POWERS):
            apr = jnp.where(col == j, a_r, apr)
            api = jnp.where(col == j, a_i, api)
            a_r, a_i = _cmul(a_r, a_i, a_r, a_i)
        apr_ref[g] = apr
        api_ref[g] = api

        dt = jnp.exp(ldt_ref[g])
        mag_l = jnp.exp(jnp.minimum(are_l[g], -1e-4) * dt)
        e_r, e_i = _cpowers(mag_l * jnp.cos(aim_l[g] * dt), mag_l * jnp.sin(aim_l[g] * dt), tp1, chunk_bits)
        cre = cre_ref[g]
        cim = cim_ref[g]
        wo_r, wo_i = _cmul(_hdot(expand_t, e_r), _hdot(expand_t, e_i), cre, cim)
        wout_ref[g] = jnp.concatenate([wo_r, -wo_i], axis=1).astype(BF16)

        hrev = _hdot(cre[0:SSM_GROUP, :], win_r) - _hdot(cim[0:SSM_GROUP, :], win_i)
        blocks = []
        for t in range(SSM_CHUNK):
            hi = (t + 1) * SSM_GROUP
            rolled = hrev if hi == SSM_ROW else pltpu.roll(hrev, hi, axis=1)
            blocks.append(jnp.where(lane < hi, rolled, 0.0))
        tt_ref[g] = jnp.concatenate(blocks, axis=0).astype(BF16)


def _ssm_prep(a_re, a_im, log_dt, b_re, b_im, c_re, c_im, *, gb=4):
    g, p = a_re.shape
    row = SSM_ROW
    whole = lambda shape: pl.BlockSpec(shape, lambda: (0,) * len(shape))
    disc = pl.pallas_call(
        _ssm_disc_kernel,
        in_specs=[whole((p, g)), whole((p, g)), whole((1, g))],
        out_specs=[whole((p, g))] * 4,
        out_shape=[jax.ShapeDtypeStruct((p, g), F32)] * 4,
        name="ssm_disc",
    )(a_re.T, a_im.T, log_dt.reshape(1, g))
    lbr_s, lbi_s, cfr_s, cfi_s = [a.T.reshape(g, p, 1) for a in disc]

    blk = lambda shape: pl.BlockSpec((gb,) + shape, lambda i: (i, 0, 0))
    b_t = lambda b: jnp.tile(b, (1, 1, SSM_CHUNK))
    c_t = lambda c: jnp.tile(c, (1, SSM_CHUNK, 1))
    out_shapes = [
        jax.ShapeDtypeStruct((g, row, row), BF16),
        jax.ShapeDtypeStruct((g, 2 * p, row), BF16),
        jax.ShapeDtypeStruct((g, row, 2 * p), BF16),
        jax.ShapeDtypeStruct((g, p, SSM_POWERS), F32), jax.ShapeDtypeStruct((g, p, SSM_POWERS), F32),
    ]
    return pl.pallas_call(
        functools.partial(_ssm_prep_kernel, gb=gb),
        grid=(g // gb,),
        in_specs=[blk((p, 1))] * 4 + [blk((1, p)), blk((1, p)), blk((1, 1)),
                  blk((p, row)), blk((p, row)), blk((row, p)), blk((row, p))],
        out_specs=[blk(s.shape[1:]) for s in out_shapes],
        out_shape=out_shapes,
        compiler_params=_params("parallel"),
        name="ssm_prep",
    )(lbr_s, lbi_s, cfr_s, cfi_s, a_re.reshape(g, 1, p), a_im.reshape(g, 1, p),
      log_dt.reshape(g, 1, 1), b_t(b_re), b_t(b_im), c_t(c_re), c_t(c_im))


def _ssm_kernel(ut_ref, tt_ref, win_ref, wout_ref, apr_ref, api_ref, dsk_ref, y_ref, *, gb, n_chunk):
    nk = ut_ref.shape[-1]
    p = apr_ref.shape[1]
    pos = lax.broadcasted_iota(jnp.int32, (p, nk), 1) % n_chunk
    n_steps = (n_chunk - 1).bit_length()
    assert n_steps <= SSM_POWERS
    for g in range(gb):
        ut = ut_ref[g]
        sl = jnp.dot(win_ref[g], ut, preferred_element_type=F32)
        s_r, s_i = sl[:p], sl[p:]
        apr = apr_ref[g]
        api = api_ref[g]
        for j in range(n_steps):
            sh = 1 << j
            a_r = apr[:, j:j + 1]
            a_i = api[:, j:j + 1]
            keep = pos >= sh
            p_r = jnp.where(keep, pltpu.roll(s_r, sh, axis=1), 0.0)
            p_i = jnp.where(keep, pltpu.roll(s_i, sh, axis=1), 0.0)
            s_r, s_i = s_r + a_r * p_r - a_i * p_i, s_i + a_r * p_i + a_i * p_r
        keep = pos >= 1
        prev = jnp.concatenate([jnp.where(keep, pltpu.roll(s_r, 1, axis=1), 0.0),
                                jnp.where(keep, pltpu.roll(s_i, 1, axis=1), 0.0)], axis=0).astype(BF16)
        y = jnp.dot(tt_ref[g], ut, preferred_element_type=F32)
        y += jnp.dot(wout_ref[g], prev, preferred_element_type=F32)
        y += dsk_ref[g] * ut.astype(F32)
        y_ref[g] = _gelu_tanh(y).astype(BF16)


def _ssm(ut, prep, dsk, *, n_chunk, gb=4):
    tt, win, wout, apr, api = prep
    g, row, nk = ut.shape
    p = apr.shape[1]
    blk = lambda shape: pl.BlockSpec((gb,) + shape, lambda i: (i, 0, 0))
    kern = functools.partial(_ssm_kernel, gb=gb, n_chunk=n_chunk)
    return pl.pallas_call(
        kern,
        grid=(g // gb,),
        in_specs=[blk((row, nk)), blk((row, row)), blk((2 * p, row)), blk((row, 2 * p)),
                  blk((p, SSM_POWERS)), blk((p, SSM_POWERS)), blk((row, 1))],
        out_specs=blk((row, nk)),
        out_shape=jax.ShapeDtypeStruct((g, row, nk), BF16),
        compiler_params=_params("parallel"),
        name="ssm",
    )(ut, tt, win, wout, apr, api, dsk)


def _ssm_out_kernel(yt_ref, wglu_ref, bglu_ref, wso_ref, o_ref, stage_ref, *, t_per_pass):
    n_grp, _, ch = yt_ref.shape
    width = n_grp * SSM_GROUP
    n_col = o_ref.shape[1] // LANES
    for t0 in range(0, SSM_CHUNK, t_per_pass):
        y = jnp.concatenate(
            [yt_ref[:, t * SSM_GROUP:(t + 1) * SSM_GROUP, :].reshape(width, ch)
             for t in range(t0, t0 + t_per_pass)], axis=1)
        gate = jnp.dot(wglu_ref[...], y, preferred_element_type=F32) + bglu_ref[...]
        glu = (y.astype(F32) * _sigmoid(gate)).astype(BF16)
        yo = lax.dot_general(glu, wso_ref[...], (((0,), (0,)), ((), ())), preferred_element_type=F32)
        for i in range(t_per_pass):
            rows = pl.ds(t0 + i, ch, stride=CHUNK_PITCH)
            for c in range(n_col):
                stage_ref[c, rows, :] = yo[i * ch:(i + 1) * ch, c * LANES:(c + 1) * LANES]
    cols = [stage_ref[c].reshape(ch, CHUNK_PITCH, LANES)[:, :SSM_CHUNK, :].reshape(ch * SSM_CHUNK, LANES)
            for c in range(n_col)]
    o_ref[...] = jnp.concatenate(cols, axis=1).astype(BF16)


def _ssm_out(yt, w_glu, b_glu, w_ssm_out, *, ch=LANES, t_per_pass=4):
    n_grp, row, n_chunk_all = yt.shape
    width, d = w_ssm_out.shape
    tm = ch * SSM_CHUNK
    return pl.pallas_call(
        functools.partial(_ssm_out_kernel, t_per_pass=t_per_pass),
        grid=(n_chunk_all // ch,),
        in_specs=[pl.BlockSpec((n_grp, row, ch), lambda i: (0, 0, i)),
                  _resident((width, width)), _resident((width, 1)), _resident((width, d))],
        out_specs=pl.BlockSpec((tm, d), lambda i: (i, 0)),
        out_shape=jax.ShapeDtypeStruct((n_chunk_all * SSM_CHUNK, d), BF16),
        scratch_shapes=[pltpu.VMEM((d // LANES, ch * CHUNK_PITCH, LANES), F32)],
        compiler_params=_params("parallel"),
        name="ssm_out",
    )(yt, w_glu.T.astype(BF16), b_glu.reshape(width, 1), w_ssm_out.astype(BF16))


def _mix_out_kernel(x_ref, o0, o1, o2, st0, st1, st2, ys_ref, ga_ref, gs_ref,
                    wao_ref, wo_ref, g_ref, b_ref, out_ref, o_stage, st_stage, *, sub):
    lane = lax.broadcasted_iota(jnp.int32, (sub, LANES), 1)
    lo_half = lane < HEAD_DIM

    def token_order(ref, stage, r0):
        dil = ref.shape[0]
        n = sub // dil
        i0 = r0 // dil
        if dil == 1:
            return ref[0, i0:i0 + n, :].astype(F32)
        n_col = ref.shape[2] // LANES
        pitch = _scatter_pitch(dil)
        for r in range(dil):
            blk = ref[r, i0:i0 + n, :].astype(F32)
            for c in range(n_col):
                stage[c, pl.ds(r, n, stride=pitch), :] = blk[:, c * LANES:(c + 1) * LANES]
        cols = [stage[c, 0:n * pitch, :] for c in range(n_col)]
        if pitch != dil:
            cols = [a.reshape(n, pitch, LANES)[:, :dil, :].reshape(sub, LANES) for a in cols]
        return jnp.concatenate(cols, axis=1)

    def head_cols(vals):
        cols = []
        for hp in range(GROUP_WIDTH // LANES):
            cols.append(jnp.where(lo_half, vals[2 * hp], vals[2 * hp + 1]))
        return jnp.concatenate(cols, axis=1)

    def merge_groups(k):
        r0 = k * sub
        sts = [token_order(st, st_stage.at[k, i], r0) for i, st in enumerate((st0, st1, st2))]
        outs = [token_order(o, o_stage.at[k, i], r0) for i, o in enumerate((o0, o1, o2))]
        dens = [pltpu.roll(st, LANES - HEADS_PER_GROUP, axis=1) for st in sts]
        mx = jnp.maximum(jnp.maximum(sts[0], sts[1]), sts[2])
        w = [den * jnp.exp(st - mx) for den, st in zip(dens, sts)]
        tot = jnp.where(lane < HEADS_PER_GROUP, w[0] + w[1] + w[2], 1.0)
        att = None
        for g in range(N_ATTN_GROUPS):
            wt = w[g] / tot
            term = head_cols([wt[:, h:h + 1] for h in range(HEADS_PER_GROUP)]) * outs[g]
            att = term if att is None else att + term
        return att.astype(BF16)

    def gate(k, att):
        rows = slice(k * sub, (k + 1) * sub)
        y_attn = jnp.dot(att, wao_ref[...], preferred_element_type=F32)
        merged = (ga_ref[rows, :].astype(F32) * y_attn
                  + gs_ref[rows, :].astype(F32) * ys_ref[rows, :].astype(F32))
        return merged.astype(BF16)

    def project(k, merged):
        rows = slice(k * sub, (k + 1) * sub)
        mix = jnp.dot(merged, wo_ref[...], preferred_element_type=F32)
        out_ref[rows, :] = _layer_norm(DEEPNORM_ALPHA * x_ref[rows, :] + mix, g_ref[...], b_ref[...])

    n_sub = x_ref.shape[0] // sub
    atts, mergeds = {}, {}
    for step in range(n_sub + 2):
        if step < n_sub:
            atts[step] = merge_groups(step)
        if 0 <= step - 1 < n_sub:
            mergeds[step - 1] = gate(step - 1, atts.pop(step - 1))
        if 0 <= step - 2 < n_sub:
            project(step - 2, mergeds.pop(step - 2))


def _mix_out(x1, os_, sts, y_ssm, ga, gs, w_attn_out, w_o, g, b, *, seq, tm=1024, sub=256):
    t, d = x1.shape
    tiles = seq // tm
    tile = lambda w: pl.BlockSpec((tm, w), lambda i: (i, 0))

    def by_class(a):
        _, dil, n, w = a.shape
        return pl.BlockSpec((None, dil, tm // dil, w), lambda i: (i // tiles, 0, i % tiles, 0))

    n_grp = len(os_)
    stage_rows = max(sub // a.shape[1] * _scatter_pitch(a.shape[1]) for a in os_)
    return pl.pallas_call(
        functools.partial(_mix_out_kernel, sub=sub),
        grid=(t // tm,),
        in_specs=[tile(d)] + [by_class(a) for a in os_] + [by_class(a) for a in sts] + [tile(d), tile(d), tile(d),
                  _resident(w_attn_out.shape), _resident(w_o.shape), _resident((1, d)), _resident((1, d))],
        out_specs=tile(d),
        out_shape=jax.ShapeDtypeStruct((t, d), F32),
        scratch_shapes=[pltpu.VMEM((tm // sub, n_grp, GROUP_WIDTH // LANES, stage_rows, LANES), F32),
                        pltpu.VMEM((tm // sub, n_grp, 1, stage_rows, LANES), F32)],
        compiler_params=_params("parallel"),
        name="mix_out",
    )(x1, *os_, *sts, y_ssm, ga, gs, w_attn_out.astype(BF16), w_o.astype(BF16),
      g.reshape(1, d), b.reshape(1, d))


def _layer(x, positions, w_in, w_attn_out, a_re, a_im, log_dt, b_re, b_im, c_re, c_im, d_skip,
           w_glu, b_glu, w_ssm_out, w_o, ffn1, ffn2, ln1, ln2, ln3):
    batch, seq, d = x.shape
    t = batch * seq
    n_grp = a_re.shape[0]
    ssm_width = n_grp * SSM_GROUP
    u_lo = 3 * ATTN_WIDTH

    x1 = _ffn_ln(x.reshape(t, d), *ffn1, *ln1)
    w_in_b = w_in.astype(BF16)
    *qkvs, ga, gs = _in_proj(x1, positions, w_in_b, batch=batch, seq=seq, ssm_width=ssm_width)

    os_, sts = [], []
    for qkv, (window, dilation) in zip(qkvs, ATTN_PATTERNS):
        assert window == ATTN_BLK * dilation
        o, st = _dilated_attention(qkv)
        os_.append(o)
        sts.append(st)

    ut = _ssm_in(x1, w_in_b, u_lo, ssm_width)
    prep = _ssm_prep(a_re, a_im, log_dt, b_re, b_im, c_re, c_im)
    dsk = jnp.tile(d_skip.reshape(n_grp, 1, SSM_GROUP), (1, SSM_CHUNK, 1)).reshape(n_grp, SSM_ROW, 1)
    yt = _ssm(ut, prep, dsk, n_chunk=seq // SSM_CHUNK)
    y_ssm = _ssm_out(yt, w_glu, b_glu, w_ssm_out)

    x2 = _mix_out(x1, os_, sts, y_ssm, ga, gs, w_attn_out, w_o, *ln2, seq=seq)
    x3 = _ffn_ln(x2, *ffn2, *ln3)
    return x3.reshape(batch, seq, d)


def kernel(x, positions, w_in, w_attn_out, a_re, a_im, log_dt, b_re, b_im, c_re, c_im, d_skip, w_glu, b_glu, w_ssm_out, w_o, ffn1_wg, ffn1_wu, ffn1_wd, ffn2_wg, ffn2_wu, ffn2_wd, ln1_g, ln1_b, ln2_g, ln2_b, ln3_g, ln3_b):
    depth = w_in.shape[0]
    for i in range(depth):
        x = _layer(x, positions, w_in[i], w_attn_out[i], a_re[i], a_im[i], log_dt[i], b_re[i], b_im[i],
                   c_re[i], c_im[i], d_skip[i], w_glu[i], b_glu[i], w_ssm_out[i], w_o[i],
                   (ffn1_wg[i], ffn1_wu[i], ffn1_wd[i]), (ffn2_wg[i], ffn2_wu[i], ffn2_wd[i]),
                   (ln1_g[i], ln1_b[i]), (ln2_g[i], ln2_b[i]), (ln3_g[i], ln3_b[i]))
    return x
```

```python
import functools

import jax
import jax.numpy as jnp
from jax import lax
from jax.experimental import pallas as pl
from jax.experimental.pallas import tpu as pltpu

F32 = jnp.float32
BF16 = jnp.bfloat16

HEAD_DIM = 64
HEADS_PER_GROUP = 4
GROUP_WIDTH = HEADS_PER_GROUP * HEAD_DIM
ATTN_PATTERNS = ((128, 1), (512, 4), (2048, 16))
N_ATTN_GROUPS = len(ATTN_PATTERNS)
ATTN_WIDTH = N_ATTN_GROUPS * GROUP_WIDTH
ROT_DIM = HEAD_DIM // 4
ROPE_THETA = 500000.0
ROPE_ROWS = 32
ATTN_BLK = 128
NEG_INF = -1e30
SSM_GROUP = 16
SSM_STATE = 64
SSM_CHUNK = 16
SSM_ROW = SSM_CHUNK * SSM_GROUP
SSM_POWERS = 16
CHUNK_PITCH = 24


def _scatter_pitch(stride):
    return CHUNK_PITCH if stride == SSM_CHUNK else stride
DEEPNORM_ALPHA = 2.0 ** 0.25
LN_EPS = 1e-5

LANES = 128
SUBLANES = 8
VMEM_LIMIT_BYTES = 56 * 1024 * 1024


def _params(*semantics):
    return pltpu.CompilerParams(dimension_semantics=semantics, vmem_limit_bytes=VMEM_LIMIT_BYTES)


def _resident(shape):
    zeros = (0,) * len(shape)
    return pl.BlockSpec(shape, lambda *_: zeros, pipeline_mode=pl.Buffered(1))


def _layer_norm(y, g, b):
    mu = jnp.mean(y, axis=-1, keepdims=True)
    yc = y - mu
    var = jnp.mean(yc * yc, axis=-1, keepdims=True)
    return yc * lax.rsqrt(var + LN_EPS) * g + b


def _sigmoid(x):
    return 1.0 / (1.0 + jnp.exp(-x))


def _gelu_tanh(x):
    c = 0.7978845608028654
    return 0.5 * x * (1.0 + jnp.tanh(c * (x + 0.044715 * (x * x * x))))


def _ffn_ln_kernel(x_ref, wg_ref, wu_ref, wd_ref, g_ref, b_ref, o_ref, *, chunks, sub):
    for r0 in range(0, x_ref.shape[0], sub):
        x = x_ref[r0:r0 + sub, :]
        xb = x.astype(BF16)
        acc = None
        for c0, c1 in chunks:
            gate = jnp.dot(xb, wg_ref[:, c0:c1], preferred_element_type=F32)
            up = jnp.dot(xb, wu_ref[:, c0:c1], preferred_element_type=F32)
            h = (gate * _sigmoid(gate) * up).astype(BF16)
            part = jnp.dot(h, wd_ref[c0:c1, :], preferred_element_type=F32)
            acc = part if acc is None else acc + part
        y = DEEPNORM_ALPHA * x + 0.5 * acc
        o_ref[r0:r0 + sub, :] = _layer_norm(y, g_ref[...], b_ref[...])


def _ffn_chunks(d_ff, width):
    edges = list(range(0, d_ff, width)) + [d_ff]
    return tuple(zip(edges[:-1], edges[1:]))


def _ffn_ln(x, wg, wu, wd, g, b, *, tm=1024, sub=256, chunk=1024):
    t, d = x.shape
    d_ff = wg.shape[1]
    kern = functools.partial(_ffn_ln_kernel, chunks=_ffn_chunks(d_ff, chunk), sub=sub)
    return pl.pallas_call(
        kern,
        grid=(t // tm,),
        in_specs=[
            pl.BlockSpec((tm, d), lambda i: (i, 0)),
            _resident((d, d_ff)), _resident((d, d_ff)), _resident((d_ff, d)),
            _resident((1, d)), _resident((1, d)),
        ],
        out_specs=pl.BlockSpec((tm, d), lambda i: (i, 0)),
        out_shape=jax.ShapeDtypeStruct((t, d), F32),
        compiler_params=_params("parallel"),
        name="ffn_ln",
    )(x, wg.astype(BF16), wu.astype(BF16), wd.astype(BF16), g.reshape(1, d), b.reshape(1, d))


def _rope_spread():
    half = ROT_DIM // 2
    rows = lax.broadcasted_iota(jnp.int32, (ROPE_ROWS, 2 * LANES), 0)
    cols = lax.broadcasted_iota(jnp.int32, (ROPE_ROWS, 2 * LANES), 1)
    in_head = cols % HEAD_DIM
    freq = in_head % half
    is_cos = cols < LANES
    rot = in_head < ROT_DIM
    cos_part = is_cos & rot & (rows == freq)
    one_part = is_cos & jnp.logical_not(rot) & (rows == 2 * half)
    sin_part = jnp.logical_not(is_cos) & rot & (rows == half + freq)
    sign = jnp.where(in_head < half, 1.0, -1.0)
    return jnp.where(cos_part | one_part, 1.0, jnp.where(sin_part, sign, 0.0)).astype(BF16)


def _in_proj_kernel(x_ref, pos_ref, invf_ref, w_ref,
                    qkv0_ref, qkv1_ref, qkv2_ref, ga_ref, gs_ref, stage_ref, *, splits, sub):
    half = ROT_DIM // 2
    spread = _rope_spread()
    first = lax.broadcasted_iota(jnp.int32, (sub, LANES), 1) % HEAD_DIM < half
    s0, s1, s2, s3, s4, s5 = splits
    tn = (((0,), (0,)), ((), ()))

    for r0 in range(0, x_ref.shape[0], sub):
        xb = x_ref[r0:r0 + sub, :].astype(BF16)

        ang = invf_ref[...] * pos_ref[:, r0:r0 + sub].astype(F32)
        tab = jnp.concatenate([jnp.cos(ang), jnp.sin(ang), jnp.ones((half, sub), F32),
                               jnp.zeros((ROPE_ROWS - 3 * half, sub), F32)], axis=0)
        tab_hi = tab.astype(BF16)
        tab_lo = (tab - tab_hi.astype(F32)).astype(BF16)
        cs = (lax.dot_general(tab_hi, spread, tn, preferred_element_type=F32)
              + lax.dot_general(tab_lo, spread, tn, preferred_element_type=F32))
        cos = cs[:, :LANES]
        sin = cs[:, LANES:]

        def rotate(z):
            cols = []
            for c in range(z.shape[1] // LANES):
                zc = z[:, c * LANES:(c + 1) * LANES]
                zs = zc * sin
                up = pltpu.roll(jnp.where(first, zs, 0.0), half, axis=1)
                dn = pltpu.roll(jnp.where(first, 0.0, zs), LANES - half, axis=1)
                cols.append(zc * cos + up + dn)
            return jnp.concatenate(cols, axis=1)

        def proj(lo, hi):
            return jnp.dot(xb, w_ref[:, lo:hi], preferred_element_type=F32)

        q = rotate(proj(0, s0)) * (HEAD_DIM ** -0.5)
        k = rotate(proj(s0, s1))
        v = proj(s1, s2)
        for gi, out_ref in enumerate((qkv0_ref, qkv1_ref, qkv2_ref)):
            dil = ATTN_PATTERNS[gi][1]
            c0 = gi * GROUP_WIDTH
            qkv = jnp.concatenate([a[:, c0:c0 + GROUP_WIDTH] for a in (q, k, v)], axis=1)
            n = sub // dil
            i0 = r0 // dil
            if dil == 1:
                out_ref[0, i0:i0 + n, :] = qkv.astype(BF16)
            else:
                n_col = qkv.shape[1] // LANES
                pitch = _scatter_pitch(dil)
                stage = stage_ref.at[r0 // sub, gi - 1]
                for c in range(n_col):
                    col = qkv[:, c * LANES:(c + 1) * LANES]
                    if pitch != dil:
                        col = jnp.concatenate([col.reshape(n, dil, LANES),
                                               jnp.zeros((n, pitch - dil, LANES), F32)], axis=1)
                        col = col.reshape(n * pitch, LANES)
                    stage[c, 0:n * pitch, :] = col
                for r in range(dil):
                    rows = pl.ds(r, n, stride=pitch)
                    out_ref[r, i0:i0 + n, :] = jnp.concatenate(
                        [stage[c, rows, :] for c in range(n_col)], axis=1).astype(BF16)
        ga_ref[r0:r0 + sub, :] = _sigmoid(proj(s3, s4)).astype(BF16)
        gs_ref[r0:r0 + sub, :] = _sigmoid(proj(s4, s5)).astype(BF16)


def _in_proj(x1, positions, w_in, *, batch, seq, ssm_width, tm=1024, sub=256):
    t, d = x1.shape
    splits = (ATTN_WIDTH, 2 * ATTN_WIDTH, 3 * ATTN_WIDTH, 3 * ATTN_WIDTH + ssm_width,
              3 * ATTN_WIDTH + ssm_width + d, 3 * ATTN_WIDTH + ssm_width + 2 * d)
    assert splits[-1] == w_in.shape[1] and seq % tm == 0
    half = ROT_DIM // 2
    invf = (ROPE_THETA ** (-jnp.arange(half, dtype=F32) * 2.0 / ROT_DIM)).reshape(half, 1)
    tiles = seq // tm
    qkv_w = 3 * GROUP_WIDTH
    qkv_specs = [pl.BlockSpec((None, dil, tm // dil, qkv_w), lambda i: (i // tiles, 0, i % tiles, 0))
                 for _, dil in ATTN_PATTERNS]
    qkv_shapes = [jax.ShapeDtypeStruct((batch, dil, seq // dil, qkv_w), BF16) for _, dil in ATTN_PATTERNS]
    widths = (d, d)
    kern = functools.partial(_in_proj_kernel, splits=splits, sub=sub)
    n_strided = sum(dil > 1 for _, dil in ATTN_PATTERNS)
    stage_rows = max(sub // dil * _scatter_pitch(dil) for _, dil in ATTN_PATTERNS)
    return pl.pallas_call(
        kern,
        grid=(t // tm,),
        in_specs=[
            pl.BlockSpec((tm, d), lambda i: (i, 0)),
            pl.BlockSpec((1, tm), lambda i: (0, i)),
            _resident((half, 1)),
            _resident(w_in.shape),
        ],
        out_specs=qkv_specs + [pl.BlockSpec((tm, w), lambda i: (i, 0)) for w in widths],
        out_shape=qkv_shapes + [jax.ShapeDtypeStruct((t, w), BF16) for w in widths],
        scratch_shapes=[pltpu.VMEM((tm // sub, n_strided, qkv_w // LANES, stage_rows, LANES), F32)],
        compiler_params=_params("parallel"),
        name="in_proj",
    )(x1, positions.reshape(1, t), invf, w_in)


def _attn_kernel(q_ref, kc_ref, kp_ref, vc_ref, vp_ref, o_ref, st_ref, *, n_sub):
    first = pl.program_id(2) == 0
    row = lax.broadcasted_iota(jnp.int32, (ATTN_BLK, 2 * ATTN_BLK), 0)
    col = lax.broadcasted_iota(jnp.int32, (ATTN_BLK, 2 * ATTN_BLK), 1)
    band = (col >= row) & (col <= row + ATTN_BLK)
    lane = lax.broadcasted_iota(jnp.int32, (ATTN_BLK, LANES), 1)
    lo_half = lane < HEAD_DIM

    n_pair = GROUP_WIDTH // LANES
    tiles = [(j, hp, hh) for j in range(n_sub) for hp in range(n_pair) for hh in range(2)]

    def window(ref_prev, ref_cur, j, c0):
        r0 = j * ATTN_BLK
        if j == 0:
            return jnp.concatenate([ref_prev[:, c0:c0 + LANES], ref_cur[0:ATTN_BLK, c0:c0 + LANES]], axis=0)
        return ref_cur[r0 - ATTN_BLK:r0 + ATTN_BLK, c0:c0 + LANES]

    def scores(tile):
        j, hp, hh = tile
        q2 = q_ref[j * ATTN_BLK:(j + 1) * ATTN_BLK, hp * LANES:(hp + 1) * LANES]
        qm = jnp.where(lo_half if hh == 0 else jnp.logical_not(lo_half), q2, jnp.zeros_like(q2))
        return lax.dot_general(qm, window(kp_ref, kc_ref, j, hp * LANES), (((1,), (1,)), ((), ())),
                               preferred_element_type=F32)

    def softmax(tile, s):
        valid = band & ((col >= ATTN_BLK) | jnp.logical_not(first)) if tile[0] == 0 else band
        s = jnp.where(valid, s, NEG_INF)
        m = jnp.max(s, axis=1, keepdims=True)
        p = jnp.exp(s - m)
        return p.astype(BF16), m, jnp.sum(p, axis=1, keepdims=True)

    def weighted_values(tile, p, den):
        j, hp, _ = tile
        return jnp.dot(p, window(vp_ref, vc_ref, j, hp * LANES), preferred_element_type=F32) / den

    s_q, p_q, outs, stats = {}, {}, {}, {}
    for step in range(len(tiles) + 2):
        if step < len(tiles):
            s_q[step] = scores(tiles[step])
        if 0 <= step - 1 < len(tiles):
            p_q[step - 1] = softmax(tiles[step - 1], s_q.pop(step - 1))
        if 0 <= step - 2 < len(tiles):
            t = step - 2
            j, hp, hh = tiles[t]
            p, m, den = p_q.pop(t)
            outs[hh] = weighted_values(tiles[t], p, den)
            h = 2 * hp + hh
            st = stats.get(j, jnp.zeros((ATTN_BLK, LANES), F32))
            st = jnp.where(lane == h, m, st)
            stats[j] = jnp.where(lane == HEADS_PER_GROUP + h, den, st)
            rows = slice(j * ATTN_BLK, (j + 1) * ATTN_BLK)
            if hh == 1:
                o_ref[rows, hp * LANES:(hp + 1) * LANES] = jnp.where(lo_half, outs[0], outs[1]).astype(BF16)
                if hp == n_pair - 1:
                    st_ref[rows, :] = stats.pop(j)


def _dilated_attention(qkv):
    batch, dilation, n, _ = qkv.shape
    qb = min(1024, n)
    n_sub = qb // ATTN_BLK

    def cur(part):
        return pl.BlockSpec((None, None, qb, GROUP_WIDTH), lambda b, r, i: (b, r, i, part))

    def prev(part):
        return pl.BlockSpec((None, None, ATTN_BLK, GROUP_WIDTH),
                            lambda b, r, i: (b, r, jnp.maximum(i * n_sub - 1, 0), part))

    return pl.pallas_call(
        functools.partial(_attn_kernel, n_sub=n_sub),
        grid=(batch, dilation, n // qb),
        in_specs=[cur(0), cur(1), prev(1), cur(2), prev(2)],
        out_specs=[pl.BlockSpec((None, None, qb, GROUP_WIDTH), lambda b, r, i: (b, r, i, 0)),
                   pl.BlockSpec((None, None, qb, LANES), lambda b, r, i: (b, r, i, 0))],
        out_shape=[jax.ShapeDtypeStruct((batch, dilation, n, GROUP_WIDTH), BF16),
                   jax.ShapeDtypeStruct((batch, dilation, n, LANES), F32)],
        compiler_params=_params("parallel", "parallel", "parallel"),
        name=f"attn_d{dilation}",
    )(qkv, qkv, qkv, qkv, qkv)


def _ssm_in_kernel(*refs, n_grp, ch, n_w):
    x_refs, w_refs, ut_ref = refs[:-n_w - 1], refs[-n_w - 1:-1], refs[-1]
    w = jnp.concatenate([w_ref[...] for w_ref in w_refs], axis=1)
    for s in range(0, SSM_CHUNK, 2):
        xs = jnp.concatenate(
            [jnp.concatenate([x_ref[pl.ds(s + i, ch, stride=SSM_CHUNK), :] for x_ref in x_refs], axis=1)
             for i in range(2)], axis=0).astype(BF16)
        ut = lax.dot_general(w, xs, (((0,), (1,)), ((), ())), preferred_element_type=F32)
        for i in range(2):
            ut_ref[:, (s + i) * SSM_GROUP:(s + i + 1) * SSM_GROUP, :] = (
                ut[:, i * ch:(i + 1) * ch].reshape(n_grp, SSM_GROUP, ch).astype(BF16))


def _ssm_in(x1, w_in_b, u_lo, width, *, ch=LANES, wblk=256):
    t, d = x1.shape
    assert u_lo % wblk == 0 and width % wblk == 0
    n_grp = width // SSM_GROUP
    n_chunk_all = t // SSM_CHUNK
    tm = ch * SSM_CHUNK
    n_w = width // wblk
    slabs = [pl.BlockSpec((tm, LANES), lambda i, c=c: (i, c)) for c in range(d // LANES)]
    w_cols = [pl.BlockSpec((d, wblk), lambda i, c=c: (0, u_lo // wblk + c), pipeline_mode=pl.Buffered(1))
              for c in range(n_w)]
    return pl.pallas_call(
        functools.partial(_ssm_in_kernel, n_grp=n_grp, ch=ch, n_w=n_w),
        grid=(n_chunk_all // ch,),
        in_specs=slabs + w_cols,
        out_specs=pl.BlockSpec((n_grp, SSM_ROW, ch), lambda i: (0, 0, i)),
        out_shape=jax.ShapeDtypeStruct((n_grp, SSM_ROW, n_chunk_all), BF16),
        compiler_params=_params("parallel"),
        name="ssm_in",
    )(*([x1] * (d // LANES)), *([w_in_b] * n_w))


def _hdot(a, b):
    return jnp.dot(a, b, preferred_element_type=F32, precision=lax.Precision.HIGHEST)


def _cmul(ar, ai, br, bi):
    return ar * br - ai * bi, ar * bi + ai * br


def _cpowers(lbr, lbi, n, bits):
    out_r = jnp.ones(n.shape, F32)
    out_i = jnp.zeros(n.shape, F32)
    for b in range(bits):
        sel = (n & (1 << b)) != 0
        out_r, out_i = _cmul(out_r, out_i, jnp.where(sel, lbr, 1.0), jnp.where(sel, lbi, 0.0))
        lbr, lbi = _cmul(lbr, lbi, lbr, lbi)
    return out_r, out_i


def _ssm_disc_kernel(are_ref, aim_ref, ldt_ref, lbr_ref, lbi_ref, cfr_ref, cfi_ref):
    dt = jnp.exp(ldt_ref[...])
    lam_r = jnp.minimum(are_ref[...], -1e-4)
    lam_i = aim_ref[...]
    mag = jnp.exp(lam_r * dt)
    lbr = mag * jnp.cos(lam_i * dt)
    lbi = mag * jnp.sin(lam_i * dt)
    inv = 1.0 / (lam_r * lam_r + lam_i * lam_i)
    lbr_ref[...] = lbr
    lbi_ref[...] = lbi
    cfr_ref[...] = ((lbr - 1.0) * lam_r + lbi * lam_i) * inv
    cfi_ref[...] = (lbi * lam_r - (lbr - 1.0) * lam_i) * inv


def _ssm_prep_kernel(lbr_s, lbi_s, cfr_s, cfi_s, are_l, aim_l, ldt_ref, bre_ref, bim_ref, cre_ref, cim_ref,
                     tt_ref, win_ref, wout_ref, apr_ref, api_ref, *, gb):
    p = lbr_s.shape[1]
    expand = (lax.broadcasted_iota(jnp.int32, (SSM_CHUNK, SSM_ROW), 1) // SSM_GROUP
              == lax.broadcasted_iota(jnp.int32, (SSM_CHUNK, SSM_ROW), 0)).astype(F32)
    expand_t = (lax.broadcasted_iota(jnp.int32, (SSM_ROW, SSM_CHUNK), 0) // SSM_GROUP
                == lax.broadcasted_iota(jnp.int32, (SSM_ROW, SSM_CHUNK), 1)).astype(F32)
    rem = SSM_CHUNK - 1 - lax.broadcasted_iota(jnp.int32, (p, SSM_CHUNK), 1)
    tp1 = lax.broadcasted_iota(jnp.int32, (SSM_CHUNK, p), 0) + 1
    col = lax.broadcasted_iota(jnp.int32, (p, SSM_POWERS), 1)
    lane = lax.broadcasted_iota(jnp.int32, (SSM_GROUP, SSM_ROW), 1)
    chunk_bits = SSM_CHUNK.bit_length()

    for g in range(gb):
        lbr = lbr_s[g]
        lbi = lbi_s[g]
        bre = bre_ref[g]
        bim = bim_ref[g]
        bbar_r, bbar_i = _cmul(cfr_s[g], cfi_s[g], bre, bim)
        pw_r, pw_i = _cpowers(lbr, lbi, rem, chunk_bits - 1)
        win_r, win_i = _cmul(_hdot(pw_r, expand), _hdot(pw_i, expand), bbar_r, bbar_i)
        win_ref[g] = jnp.concatenate([win_r, win_i], axis=0).astype(BF16)

        a_r, a_i = _cmul(pw_r[:, 0:1], pw_i[:, 0:1], lbr, lbi)
        apr = jnp.zeros((p, SSM_POWERS), F32)
        api = jnp.zeros((p, SSM_POWERS), F32)
        for j in range(SSM_POWERS):
            apr = jnp.where(col == j, a_r, apr)
            api = jnp.where(col == j, a_i, api)
            a_r, a_i = _cmul(a_r, a_i, a_r, a_i)
        apr_ref[g] = apr
        api_ref[g] = api

        dt = jnp.exp(ldt_ref[g])
        mag_l = jnp.exp(jnp.minimum(are_l[g], -1e-4) * dt)
        e_r, e_i = _cpowers(mag_l * jnp.cos(aim_l[g] * dt), mag_l * jnp.sin(aim_l[g] * dt), tp1, chunk_bits)
        cre = cre_ref[g]
        cim = cim_ref[g]
        wo_r, wo_i = _cmul(_hdot(expand_t, e_r), _hdot(expand_t, e_i), cre, cim)
        wout_ref[g] = jnp.concatenate([wo_r, -wo_i], axis=1).astype(BF16)

        hrev = _hdot(cre[0:SSM_GROUP, :], win_r) - _hdot(cim[0:SSM_GROUP, :], win_i)
        blocks = []
        for t in range(SSM_CHUNK):
            hi = (t + 1) * SSM_GROUP
            rolled = hrev if hi == SSM_ROW else pltpu.roll(hrev, hi, axis=1)
            blocks.append(jnp.where(lane < hi, rolled, 0.0))
        tt_ref[g] = jnp.concatenate(blocks, axis=0).astype(BF16)


def _ssm_prep(a_re, a_im, log_dt, b_re, b_im, c_re, c_im, *, gb=4):
    g, p = a_re.shape
    row = SSM_ROW
    whole = lambda shape: pl.BlockSpec(shape, lambda: (0,) * len(shape))
    disc = pl.pallas_call(
        _ssm_disc_kernel,
        in_specs=[whole((p, g)), whole((p, g)), whole((1, g))],
        out_specs=[whole((p, g))] * 4,
        out_shape=[jax.ShapeDtypeStruct((p, g), F32)] * 4,
        name="ssm_disc",
    )(a_re.T, a_im.T, log_dt.reshape(1, g))
    lbr_s, lbi_s, cfr_s, cfi_s = [a.T.reshape(g, p, 1) for a in disc]

    blk = lambda shape: pl.BlockSpec((gb,) + shape, lambda i: (i, 0, 0))
    b_t = lambda b: jnp.tile(b, (1, 1, SSM_CHUNK))
    c_t = lambda c: jnp.tile(c, (1, SSM_CHUNK, 1))
    out_shapes = [
        jax.ShapeDtypeStruct((g, row, row), BF16),
        jax.ShapeDtypeStruct((g, 2 * p, row), BF16),
        jax.ShapeDtypeStruct((g, row, 2 * p), BF16),
        jax.ShapeDtypeStruct((g, p, SSM_POWERS), F32), jax.ShapeDtypeStruct((g, p, SSM_POWERS), F32),
    ]
    return pl.pallas_call(
        functools.partial(_ssm_prep_kernel, gb=gb),
        grid=(g // gb,),
        in_specs=[blk((p, 1))] * 4 + [blk((1, p)), blk((1, p)), blk((1, 1)),
                  blk((p, row)), blk((p, row)), blk((row, p)), blk((row, p))],
        out_specs=[blk(s.shape[1:]) for s in out_shapes],
        out_shape=out_shapes,
        compiler_params=_params("parallel"),
        name="ssm_prep",
    )(lbr_s, lbi_s, cfr_s, cfi_s, a_re.reshape(g, 1, p), a_im.reshape(g, 1, p),
      log_dt.reshape(g, 1, 1), b_t(b_re), b_t(b_im), c_t(c_re), c_t(c_im))


def _ssm_kernel(ut_ref, tt_ref, win_ref, wout_ref, apr_ref, api_ref, dsk_ref, y_ref, *, gb, n_chunk):
    nk = ut_ref.shape[-1]
    p = apr_ref.shape[1]
    pos = lax.broadcasted_iota(jnp.int32, (p, nk), 1) % n_chunk
    n_steps = (n_chunk - 1).bit_length()
    assert n_steps <= SSM_POWERS
    for g in range(gb):
        ut = ut_ref[g]
        sl = jnp.dot(win_ref[g], ut, preferred_element_type=F32)
        s_r, s_i = sl[:p], sl[p:]
        apr = apr_ref[g]
        api = api_ref[g]
        for j in range(n_steps):
            sh = 1 << j
            a_r = apr[:, j:j + 1]
            a_i = api[:, j:j + 1]
            keep = pos >= sh
            p_r = jnp.where(keep, pltpu.roll(s_r, sh, axis=1), 0.0)
            p_i = jnp.where(keep, pltpu.roll(s_i, sh, axis=1), 0.0)
            s_r, s_i = s_r + a_r * p_r - a_i * p_i, s_i + a_r * p_i + a_i * p_r
        keep = pos >= 1
        prev = jnp.concatenate([jnp.where(keep, pltpu.roll(s_r, 1, axis=1), 0.0),
                                jnp.where(keep, pltpu.roll(s_i, 1, axis=1), 0.0)], axis=0).astype(BF16)
        y = jnp.dot(tt_ref[g], ut, preferred_element_type=F32)
        y += jnp.dot(wout_ref[g], prev, preferred_element_type=F32)
        y += dsk_ref[g] * ut.astype(F32)
        y_ref[g] = _gelu_tanh(y).astype(BF16)


def _ssm(ut, prep, dsk, *, n_chunk, gb=4):
    tt, win, wout, apr, api = prep
    g, row, nk = ut.shape
    p = apr.shape[1]
    blk = lambda shape: pl.BlockSpec((gb,) + shape, lambda i: (i, 0, 0))
    kern = functools.partial(_ssm_kernel, gb=gb, n_chunk=n_chunk)
    return pl.pallas_call(
        kern,
        grid=(g // gb,),
        in_specs=[blk((row, nk)), blk((row, row)), blk((2 * p, row)), blk((row, 2 * p)),
                  blk((p, SSM_POWERS)), blk((p, SSM_POWERS)), blk((row, 1))],
        out_specs=blk((row, nk)),
        out_shape=jax.ShapeDtypeStruct((g, row, nk), BF16),
        compiler_params=_params("parallel"),
        name="ssm",
    )(ut, tt, win, wout, apr, api, dsk)


def _ssm_out_kernel(yt_ref, wglu_ref, bglu_ref, wso_ref, o_ref, stage_ref, *, t_per_pass):
    n_grp, _, ch = yt_ref.shape
    width = n_grp * SSM_GROUP
    n_col = o_ref.shape[1] // LANES
    for t0 in range(0, SSM_CHUNK, t_per_pass):
        y = jnp.concatenate(
            [yt_ref[:, t * SSM_GROUP:(t + 1) * SSM_GROUP, :].reshape(width, ch)
             for t in range(t0, t0 + t_per_pass)], axis=1)
        gate = jnp.dot(wglu_ref[...], y, preferred_element_type=F32) + bglu_ref[...]
        glu = (y.astype(F32) * _sigmoid(gate)).astype(BF16)
        yo = lax.dot_general(glu, wso_ref[...], (((0,), (0,)), ((), ())), preferred_element_type=F32)
        for i in range(t_per_pass):
            rows = pl.ds(t0 + i, ch, stride=CHUNK_PITCH)
            for c in range(n_col):
                stage_ref[c, rows, :] = yo[i * ch:(i + 1) * ch, c * LANES:(c + 1) * LANES]
    cols = [stage_ref[c].reshape(ch, CHUNK_PITCH, LANES)[:, :SSM_CHUNK, :].reshape(ch * SSM_CHUNK, LANES)
            for c in range(n_col)]
    o_ref[...] = jnp.concatenate(cols, axis=1).astype(BF16)


def _ssm_out(yt, w_glu, b_glu, w_ssm_out, *, ch=LANES, t_per_pass=4):
    n_grp, row, n_chunk_all = yt.shape
    width, d = w_ssm_out.shape
    tm = ch * SSM_CHUNK
    return pl.pallas_call(
        functools.partial(_ssm_out_kernel, t_per_pass=t_per_pass),
        grid=(n_chunk_all // ch,),
        in_specs=[pl.BlockSpec((n_grp, row, ch), lambda i: (0, 0, i)),
                  _resident((width, width)), _resident((width, 1)), _resident((width, d))],
        out_specs=pl.BlockSpec((tm, d), lambda i: (i, 0)),
        out_shape=jax.ShapeDtypeStruct((n_chunk_all * SSM_CHUNK, d), BF16),
        scratch_shapes=[pltpu.VMEM((d // LANES, ch * CHUNK_PITCH, LANES), F32)],
        compiler_params=_params("parallel"),
        name="ssm_out",
    )(yt, w_glu.T.astype(BF16), b_glu.reshape(width, 1), w_ssm_out.astype(BF16))


def _mix_ffn_kernel(x_ref, o0, o1, o2, st0, st1, st2, ys_ref, ga_ref, gs_ref,
                    wao_ref, wo_ref, g_ref, b_ref, wg_ref, wu_ref, wd_ref, g3_ref, b3_ref,
                    out_ref, o_stage, st_stage, x2_ref, *, sub, chunks):
    step = pl.program_id(0)
    slot_w = step % 2
    slot_r = 1 - slot_w

    @pl.when(step == 0)
    def _():
        x2_ref[...] = jnp.zeros(x2_ref.shape, F32)

    lane = lax.broadcasted_iota(jnp.int32, (sub, LANES), 1)
    lo_half = lane < HEAD_DIM

    def token_order(ref, stage, r0):
        dil = ref.shape[0]
        n = sub // dil
        i0 = r0 // dil
        if dil == 1:
            return ref[0, i0:i0 + n, :].astype(F32)
        n_col = ref.shape[2] // LANES
        pitch = _scatter_pitch(dil)
        for r in range(dil):
            blk = ref[r, i0:i0 + n, :].astype(F32)
            for c in range(n_col):
                stage[c, pl.ds(r, n, stride=pitch), :] = blk[:, c * LANES:(c + 1) * LANES]
        cols = [stage[c, 0:n * pitch, :] for c in range(n_col)]
        if pitch != dil:
            cols = [a.reshape(n, pitch, LANES)[:, :dil, :].reshape(sub, LANES) for a in cols]
        return jnp.concatenate(cols, axis=1)

    def head_cols(vals):
        cols = []
        for hp in range(GROUP_WIDTH // LANES):
            cols.append(jnp.where(lo_half, vals[2 * hp], vals[2 * hp + 1]))
        return jnp.concatenate(cols, axis=1)

    def merge_groups(k):
        r0 = k * sub
        sts = [token_order(st, st_stage.at[k, i], r0) for i, st in enumerate((st0, st1, st2))]
        outs = [token_order(o, o_stage.at[k, i], r0) for i, o in enumerate((o0, o1, o2))]
        dens = [pltpu.roll(st, LANES - HEADS_PER_GROUP, axis=1) for st in sts]
        mx = jnp.maximum(jnp.maximum(sts[0], sts[1]), sts[2])
        w = [den * jnp.exp(st - mx) for den, st in zip(dens, sts)]
        tot = jnp.where(lane < HEADS_PER_GROUP, w[0] + w[1] + w[2], 1.0)
        att = None
        for g in range(N_ATTN_GROUPS):
            wt = w[g] / tot
            term = head_cols([wt[:, h:h + 1] for h in range(HEADS_PER_GROUP)]) * outs[g]
            att = term if att is None else att + term
        return att.astype(BF16)

    def gate(k, att):
        rows = slice(k * sub, (k + 1) * sub)
        y_attn = jnp.dot(att, wao_ref[...], preferred_element_type=F32)
        merged = (ga_ref[rows, :].astype(F32) * y_attn
                  + gs_ref[rows, :].astype(F32) * ys_ref[rows, :].astype(F32))
        return merged.astype(BF16)

    def project(k, merged):
        rows = slice(k * sub, (k + 1) * sub)
        mix = jnp.dot(merged, wo_ref[...], preferred_element_type=F32)
        x2_ref[slot_w, rows, :] = _layer_norm(DEEPNORM_ALPHA * x_ref[rows, :] + mix, g_ref[...], b_ref[...])

    x2_prev = x2_ref[slot_r]
    ffn_state = {}

    def ffn_gate_up(k, c):
        rows = slice(k * sub, (k + 1) * sub)
        if c == chunks[0]:
            ffn_state[k] = {"xb": x2_prev[rows, :].astype(BF16), "acc": None}
        xb = ffn_state[k]["xb"]
        ffn_state[k]["gu"] = (jnp.dot(xb, wg_ref[:, c[0]:c[1]], preferred_element_type=F32),
                              jnp.dot(xb, wu_ref[:, c[0]:c[1]], preferred_element_type=F32))

    def ffn_down(k, c):
        rows = slice(k * sub, (k + 1) * sub)
        gate_v, up = ffn_state[k].pop("gu")
        h = (gate_v * _sigmoid(gate_v) * up).astype(BF16)
        part = jnp.dot(h, wd_ref[c[0]:c[1], :], preferred_element_type=F32)
        acc = ffn_state[k]["acc"]
        ffn_state[k]["acc"] = part if acc is None else acc + part
        if c == chunks[-1]:
            y = DEEPNORM_ALPHA * x2_prev[rows, :] + 0.5 * ffn_state.pop(k)["acc"]
            out_ref[rows, :] = _layer_norm(y, g3_ref[...], b3_ref[...])

    n_sub = x_ref.shape[0] // sub
    ffn_stages = [f for k in range(n_sub) for c in chunks
                  for f in (functools.partial(ffn_gate_up, k, c), functools.partial(ffn_down, k, c))]
    mix_stages = [f for k in range(n_sub) for f in (functools.partial(merge_groups, k),
                                                    functools.partial(gate, k),
                                                    functools.partial(project, k))]
    carry = ()
    for f in ffn_stages:
        f()
        if mix_stages:
            out = mix_stages.pop(0)(*carry)
            carry = () if out is None else (out,)
    assert not mix_stages


def _mix_ffn(x1, os_, sts, y_ssm, ga, gs, w_attn_out, w_o, g2, b2, wg, wu, wd, g3, b3, *,
             seq, tm=512, sub=256, chunk=1024):
    t, d = x1.shape
    d_ff = wg.shape[1]
    tiles = seq // tm
    n_tiles = t // tm
    cur = lambda i: jnp.minimum(i, n_tiles - 1)
    tile = lambda w: pl.BlockSpec((tm, w), lambda i: (cur(i), 0))

    def by_class(a):
        _, dil, n, w = a.shape
        return pl.BlockSpec((None, dil, tm // dil, w), lambda i: (cur(i) // tiles, 0, cur(i) % tiles, 0))

    n_grp = len(os_)
    stage_rows = max(sub // a.shape[1] * _scatter_pitch(a.shape[1]) for a in os_)
    vec = lambda v: v.reshape(1, d)
    return pl.pallas_call(
        functools.partial(_mix_ffn_kernel, sub=sub, chunks=_ffn_chunks(d_ff, chunk)),
        grid=(n_tiles + 1,),
        in_specs=[tile(d)] + [by_class(a) for a in os_] + [by_class(a) for a in sts] + [tile(d), tile(d), tile(d),
                  _resident(w_attn_out.shape), _resident(w_o.shape), _resident((1, d)), _resident((1, d)),
                  _resident((d, d_ff)), _resident((d, d_ff)), _resident((d_ff, d)),
                  _resident((1, d)), _resident((1, d))],
        out_specs=pl.BlockSpec((tm, d), lambda i: (jnp.maximum(i - 1, 0), 0)),
        out_shape=jax.ShapeDtypeStruct((t, d), F32),
        scratch_shapes=[pltpu.VMEM((tm // sub, n_grp, GROUP_WIDTH // LANES, stage_rows, LANES), F32),
                        pltpu.VMEM((tm // sub, n_grp, 1, stage_rows, LANES), F32),
                        pltpu.VMEM((2, tm, d), F32)],
        compiler_params=_params("arbitrary"),
        name="mix_ffn",
    )(x1, *os_, *sts, y_ssm, ga, gs, w_attn_out.astype(BF16), w_o.astype(BF16), vec(g2), vec(b2),
      wg.astype(BF16), wu.astype(BF16), wd.astype(BF16), vec(g3), vec(b3))


def _layer(x, positions, w_in, w_attn_out, a_re, a_im, log_dt, b_re, b_im, c_re, c_im, d_skip,
           w_glu, b_glu, w_ssm_out, w_o, ffn1, ffn2, ln1, ln2, ln3):
    batch, seq, d = x.shape
    t = batch * seq
    n_grp = a_re.shape[0]
    ssm_width = n_grp * SSM_GROUP
    u_lo = 3 * ATTN_WIDTH

    x1 = _ffn_ln(x.reshape(t, d), *ffn1, *ln1)
    w_in_b = w_in.astype(BF16)
    *qkvs, ga, gs = _in_proj(x1, positions, w_in_b, batch=batch, seq=seq, ssm_width=ssm_width)

    os_, sts = [], []
    for qkv, (window, dilation) in zip(qkvs, ATTN_PATTERNS):
        assert window == ATTN_BLK * dilation
        o, st = _dilated_attention(qkv)
        os_.append(o)
        sts.append(st)

    ut = _ssm_in(x1, w_in_b, u_lo, ssm_width)
    prep = _ssm_prep(a_re, a_im, log_dt, b_re, b_im, c_re, c_im)
    dsk = jnp.tile(d_skip.reshape(n_grp, 1, SSM_GROUP), (1, SSM_CHUNK, 1)).reshape(n_grp, SSM_ROW, 1)
    yt = _ssm(ut, prep, dsk, n_chunk=seq // SSM_CHUNK)
    y_ssm = _ssm_out(yt, w_glu, b_glu, w_ssm_out)

    x3 = _mix_ffn(x1, os_, sts, y_ssm, ga, gs, w_attn_out, w_o, *ln2, *ffn2, *ln3, seq=seq)
    return x3.reshape(batch, seq, d)


def kernel(x, positions, w_in, w_attn_out, a_re, a_im, log_dt, b_re, b_im, c_re, c_im, d_skip, w_glu, b_glu, w_ssm_out, w_o, ffn1_wg, ffn1_wu, ffn1_wd, ffn2_wg, ffn2_wu, ffn2_wd, ln1_g, ln1_b, ln2_g, ln2_b, ln3_g, ln3_b):
    depth = w_in.shape[0]
    for i in range(depth):
        x = _layer(x, positions, w_in[i], w_attn_out[i], a_re[i], a_im[i], log_dt[i], b_re[i], b_im[i],
                   c_re[i], c_im[i], d_skip[i], w_glu[i], b_glu[i], w_ssm_out[i], w_o[i],
                   (ffn1_wg[i], ffn1_wu[i], ffn1_wd[i]), (ffn2_wg[i], ffn2_wu[i], ffn2_wd[i]),
                   (ln1_g[i], ln1_b[i]), (ln2_g[i], ln2_b[i]), (ln3_g[i], ln3_b[i]))
    return x
```

```python
import functools

import jax
import jax.numpy as jnp
from jax import lax
from jax.experimental import pallas as pl
from jax.experimental.pallas import tpu as pltpu

F32 = jnp.float32
BF16 = jnp.bfloat16

HEAD_DIM = 64
HEADS_PER_GROUP = 4
GROUP_WIDTH = HEADS_PER_GROUP * HEAD_DIM
ATTN_PATTERNS = ((128, 1), (512, 4), (2048, 16))
N_ATTN_GROUPS = len(ATTN_PATTERNS)
ATTN_WIDTH = N_ATTN_GROUPS * GROUP_WIDTH
ROT_DIM = HEAD_DIM // 4
ROPE_THETA = 500000.0
ROPE_ROWS = 32
ATTN_BLK = 128
NEG_INF = -1e30
SSM_GROUP = 16
SSM_STATE = 64
SSM_CHUNK = 16
SSM_ROW = SSM_CHUNK * SSM_GROUP
SSM_POWERS = 16
SCAN_INTERLEAVE = 8
CHUNK_PITCH = 24


def _scatter_pitch(stride):
    return CHUNK_PITCH if stride == SSM_CHUNK else stride
DEEPNORM_ALPHA = 2.0 ** 0.25
LN_EPS = 1e-5

LANES = 128
SUBLANES = 8
VMEM_LIMIT_BYTES = 56 * 1024 * 1024


def _params(*semantics):
    return pltpu.CompilerParams(dimension_semantics=semantics, vmem_limit_bytes=VMEM_LIMIT_BYTES)


def _resident(shape):
    zeros = (0,) * len(shape)
    return pl.BlockSpec(shape, lambda *_: zeros, pipeline_mode=pl.Buffered(1))


def _layer_norm(y, g, b):
    mu = jnp.mean(y, axis=-1, keepdims=True)
    yc = y - mu
    var = jnp.mean(yc * yc, axis=-1, keepdims=True)
    return yc * lax.rsqrt(var + LN_EPS) * g + b


def _sigmoid(x):
    return 1.0 / (1.0 + jnp.exp(-x))


def _gelu_tanh(x):
    c = 0.7978845608028654
    return 0.5 * x * (1.0 + jnp.tanh(c * (x + 0.044715 * (x * x * x))))


def _ffn_ln_kernel(x_ref, wg_ref, wu_ref, wd_ref, g_ref, b_ref, o_ref, *, chunks, sub):
    for r0 in range(0, x_ref.shape[0], sub):
        x = x_ref[r0:r0 + sub, :]
        xb = x.astype(BF16)
        acc = None
        for c0, c1 in chunks:
            gate = jnp.dot(xb, wg_ref[:, c0:c1], preferred_element_type=F32)
            up = jnp.dot(xb, wu_ref[:, c0:c1], preferred_element_type=F32)
            h = (gate * _sigmoid(gate) * up).astype(BF16)
            part = jnp.dot(h, wd_ref[c0:c1, :], preferred_element_type=F32)
            acc = part if acc is None else acc + part
        y = DEEPNORM_ALPHA * x + 0.5 * acc
        o_ref[r0:r0 + sub, :] = _layer_norm(y, g_ref[...], b_ref[...])


def _ffn_chunks(d_ff, width):
    edges = list(range(0, d_ff, width)) + [d_ff]
    return tuple(zip(edges[:-1], edges[1:]))


def _ffn_ln(x, wg, wu, wd, g, b, *, tm=1024, sub=256, chunk=1024):
    t, d = x.shape
    d_ff = wg.shape[1]
    kern = functools.partial(_ffn_ln_kernel, chunks=_ffn_chunks(d_ff, chunk), sub=sub)
    return pl.pallas_call(
        kern,
        grid=(t // tm,),
        in_specs=[
            pl.BlockSpec((tm, d), lambda i: (i, 0)),
            _resident((d, d_ff)), _resident((d, d_ff)), _resident((d_ff, d)),
            _resident((1, d)), _resident((1, d)),
        ],
        out_specs=pl.BlockSpec((tm, d), lambda i: (i, 0)),
        out_shape=jax.ShapeDtypeStruct((t, d), F32),
        compiler_params=_params("parallel"),
        name="ffn_ln",
    )(x, wg.astype(BF16), wu.astype(BF16), wd.astype(BF16), g.reshape(1, d), b.reshape(1, d))


def _rope_spread():
    half = ROT_DIM // 2
    rows = lax.broadcasted_iota(jnp.int32, (ROPE_ROWS, 2 * LANES), 0)
    cols = lax.broadcasted_iota(jnp.int32, (ROPE_ROWS, 2 * LANES), 1)
    in_head = cols % HEAD_DIM
    freq = in_head % half
    is_cos = cols < LANES
    rot = in_head < ROT_DIM
    cos_part = is_cos & rot & (rows == freq)
    one_part = is_cos & jnp.logical_not(rot) & (rows == 2 * half)
    sin_part = jnp.logical_not(is_cos) & rot & (rows == half + freq)
    sign = jnp.where(in_head < half, 1.0, -1.0)
    return jnp.where(cos_part | one_part, 1.0, jnp.where(sin_part, sign, 0.0)).astype(BF16)


def _in_proj_kernel(x_ref, pos_ref, invf_ref, w_ref,
                    qkv0_ref, qkv1_ref, qkv2_ref, ga_ref, gs_ref, stage_ref, *, splits, sub):
    half = ROT_DIM // 2
    spread = _rope_spread()
    first = lax.broadcasted_iota(jnp.int32, (sub, LANES), 1) % HEAD_DIM < half
    s0, s1, s2, s3, s4, s5 = splits
    tn = (((0,), (0,)), ((), ()))

    for r0 in range(0, x_ref.shape[0], sub):
        xb = x_ref[r0:r0 + sub, :].astype(BF16)

        ang = invf_ref[...] * pos_ref[:, r0:r0 + sub].astype(F32)
        tab = jnp.concatenate([jnp.cos(ang), jnp.sin(ang), jnp.ones((half, sub), F32),
                               jnp.zeros((ROPE_ROWS - 3 * half, sub), F32)], axis=0)
        tab_hi = tab.astype(BF16)
        tab_lo = (tab - tab_hi.astype(F32)).astype(BF16)
        cs = (lax.dot_general(tab_hi, spread, tn, preferred_element_type=F32)
              + lax.dot_general(tab_lo, spread, tn, preferred_element_type=F32))
        cos = cs[:, :LANES]
        sin = cs[:, LANES:]

        def rotate(z):
            cols = []
            for c in range(z.shape[1] // LANES):
                zc = z[:, c * LANES:(c + 1) * LANES]
                zs = zc * sin
                up = pltpu.roll(jnp.where(first, zs, 0.0), half, axis=1)
                dn = pltpu.roll(jnp.where(first, 0.0, zs), LANES - half, axis=1)
                cols.append(zc * cos + up + dn)
            return jnp.concatenate(cols, axis=1)

        def proj(lo, hi):
            return jnp.dot(xb, w_ref[:, lo:hi], preferred_element_type=F32)

        q = rotate(proj(0, s0)) * (HEAD_DIM ** -0.5)
        k = rotate(proj(s0, s1))
        v = proj(s1, s2)
        for gi, out_ref in enumerate((qkv0_ref, qkv1_ref, qkv2_ref)):
            dil = ATTN_PATTERNS[gi][1]
            c0 = gi * GROUP_WIDTH
            qkv = jnp.concatenate([a[:, c0:c0 + GROUP_WIDTH] for a in (q, k, v)], axis=1)
            n = sub // dil
            i0 = r0 // dil
            if dil == 1:
                out_ref[0, i0:i0 + n, :] = qkv.astype(BF16)
            else:
                n_col = qkv.shape[1] // LANES
                pitch = _scatter_pitch(dil)
                stage = stage_ref.at[r0 // sub, gi - 1]
                for c in range(n_col):
                    col = qkv[:, c * LANES:(c + 1) * LANES]
                    if pitch != dil:
                        col = jnp.concatenate([col.reshape(n, dil, LANES),
                                               jnp.zeros((n, pitch - dil, LANES), F32)], axis=1)
                        col = col.reshape(n * pitch, LANES)
                    stage[c, 0:n * pitch, :] = col
                for r in range(dil):
                    rows = pl.ds(r, n, stride=pitch)
                    out_ref[r, i0:i0 + n, :] = jnp.concatenate(
                        [stage[c, rows, :] for c in range(n_col)], axis=1).astype(BF16)
        ga_ref[r0:r0 + sub, :] = _sigmoid(proj(s3, s4)).astype(BF16)
        gs_ref[r0:r0 + sub, :] = _sigmoid(proj(s4, s5)).astype(BF16)


def _in_proj(x1, positions, w_in, *, batch, seq, ssm_width, tm=1024, sub=256):
    t, d = x1.shape
    splits = (ATTN_WIDTH, 2 * ATTN_WIDTH, 3 * ATTN_WIDTH, 3 * ATTN_WIDTH + ssm_width,
              3 * ATTN_WIDTH + ssm_width + d, 3 * ATTN_WIDTH + ssm_width + 2 * d)
    assert splits[-1] == w_in.shape[1] and seq % tm == 0
    half = ROT_DIM // 2
    invf = (ROPE_THETA ** (-jnp.arange(half, dtype=F32) * 2.0 / ROT_DIM)).reshape(half, 1)
    tiles = seq // tm
    qkv_w = 3 * GROUP_WIDTH
    qkv_specs = [pl.BlockSpec((None, dil, tm // dil, qkv_w), lambda i: (i // tiles, 0, i % tiles, 0))
                 for _, dil in ATTN_PATTERNS]
    qkv_shapes = [jax.ShapeDtypeStruct((batch, dil, seq // dil, qkv_w), BF16) for _, dil in ATTN_PATTERNS]
    widths = (d, d)
    kern = functools.partial(_in_proj_kernel, splits=splits, sub=sub)
    n_strided = sum(dil > 1 for _, dil in ATTN_PATTERNS)
    stage_rows = max(sub // dil * _scatter_pitch(dil) for _, dil in ATTN_PATTERNS)
    return pl.pallas_call(
        kern,
        grid=(t // tm,),
        in_specs=[
            pl.BlockSpec((tm, d), lambda i: (i, 0)),
            pl.BlockSpec((1, tm), lambda i: (0, i)),
            _resident((half, 1)),
            _resident(w_in.shape),
        ],
        out_specs=qkv_specs + [pl.BlockSpec((tm, w), lambda i: (i, 0)) for w in widths],
        out_shape=qkv_shapes + [jax.ShapeDtypeStruct((t, w), BF16) for w in widths],
        scratch_shapes=[pltpu.VMEM((tm // sub, n_strided, qkv_w // LANES, stage_rows, LANES), F32)],
        compiler_params=_params("parallel"),
        name="in_proj",
    )(x1, positions.reshape(1, t), invf, w_in)


def _attn_kernel(q_ref, kc_ref, kp_ref, vc_ref, vp_ref, o_ref, st_ref, *, n_sub):
    first = pl.program_id(2) == 0
    row = lax.broadcasted_iota(jnp.int32, (ATTN_BLK, 2 * ATTN_BLK), 0)
    col = lax.broadcasted_iota(jnp.int32, (ATTN_BLK, 2 * ATTN_BLK), 1)
    band = (col >= row) & (col <= row + ATTN_BLK)
    lane = lax.broadcasted_iota(jnp.int32, (ATTN_BLK, LANES), 1)
    lo_half = lane < HEAD_DIM

    n_pair = GROUP_WIDTH // LANES
    tiles = [(j, hp, hh) for j in range(n_sub) for hp in range(n_pair) for hh in range(2)]

    def window(ref_prev, ref_cur, j, c0):
        r0 = j * ATTN_BLK
        if j == 0:
            return jnp.concatenate([ref_prev[:, c0:c0 + LANES], ref_cur[0:ATTN_BLK, c0:c0 + LANES]], axis=0)
        return ref_cur[r0 - ATTN_BLK:r0 + ATTN_BLK, c0:c0 + LANES]

    def scores(tile):
        j, hp, hh = tile
        q2 = q_ref[j * ATTN_BLK:(j + 1) * ATTN_BLK, hp * LANES:(hp + 1) * LANES]
        qm = jnp.where(lo_half if hh == 0 else jnp.logical_not(lo_half), q2, jnp.zeros_like(q2))
        return lax.dot_general(qm, window(kp_ref, kc_ref, j, hp * LANES), (((1,), (1,)), ((), ())),
                               preferred_element_type=F32)

    def softmax(tile, s):
        valid = band & ((col >= ATTN_BLK) | jnp.logical_not(first)) if tile[0] == 0 else band
        s = jnp.where(valid, s, NEG_INF)
        m = jnp.max(s, axis=1, keepdims=True)
        p = jnp.exp(s - m)
        return p.astype(BF16), m, jnp.sum(p, axis=1, keepdims=True)

    def weighted_values(tile, p, den):
        j, hp, _ = tile
        return jnp.dot(p, window(vp_ref, vc_ref, j, hp * LANES), preferred_element_type=F32) / den

    s_q, p_q, outs, stats = {}, {}, {}, {}
    for step in range(len(tiles) + 2):
        if step < len(tiles):
            s_q[step] = scores(tiles[step])
        if 0 <= step - 1 < len(tiles):
            p_q[step - 1] = softmax(tiles[step - 1], s_q.pop(step - 1))
        if 0 <= step - 2 < len(tiles):
            t = step - 2
            j, hp, hh = tiles[t]
            p, m, den = p_q.pop(t)
            outs[hh] = weighted_values(tiles[t], p, den)
            h = 2 * hp + hh
            st = stats.get(j, jnp.zeros((ATTN_BLK, LANES), F32))
            st = jnp.where(lane == h, m, st)
            stats[j] = jnp.where(lane == HEADS_PER_GROUP + h, den, st)
            rows = slice(j * ATTN_BLK, (j + 1) * ATTN_BLK)
            if hh == 1:
                o_ref[rows, hp * LANES:(hp + 1) * LANES] = jnp.where(lo_half, outs[0], outs[1]).astype(BF16)
                if hp == n_pair - 1:
                    st_ref[rows, :] = stats.pop(j)


def _dilated_attention(qkv):
    batch, dilation, n, _ = qkv.shape
    qb = min(1024, n)
    n_sub = qb // ATTN_BLK

    def cur(part):
        return pl.BlockSpec((None, None, qb, GROUP_WIDTH), lambda b, r, i: (b, r, i, part))

    def prev(part):
        return pl.BlockSpec((None, None, ATTN_BLK, GROUP_WIDTH),
                            lambda b, r, i: (b, r, jnp.maximum(i * n_sub - 1, 0), part))

    return pl.pallas_call(
        functools.partial(_attn_kernel, n_sub=n_sub),
        grid=(batch, dilation, n // qb),
        in_specs=[cur(0), cur(1), prev(1), cur(2), prev(2)],
        out_specs=[pl.BlockSpec((None, None, qb, GROUP_WIDTH), lambda b, r, i: (b, r, i, 0)),
                   pl.BlockSpec((None, None, qb, LANES), lambda b, r, i: (b, r, i, 0))],
        out_shape=[jax.ShapeDtypeStruct((batch, dilation, n, GROUP_WIDTH), BF16),
                   jax.ShapeDtypeStruct((batch, dilation, n, LANES), F32)],
        compiler_params=_params("parallel", "parallel", "parallel"),
        name=f"attn_d{dilation}",
    )(qkv, qkv, qkv, qkv, qkv)


def _ssm_in_kernel(*refs, n_grp, ch, n_w):
    x_refs, w_refs, ut_ref = refs[:-n_w - 1], refs[-n_w - 1:-1], refs[-1]
    w = jnp.concatenate([w_ref[...] for w_ref in w_refs], axis=1)
    for s in range(0, SSM_CHUNK, 2):
        xs = jnp.concatenate(
            [jnp.concatenate([x_ref[pl.ds(s + i, ch, stride=SSM_CHUNK), :] for x_ref in x_refs], axis=1)
             for i in range(2)], axis=0).astype(BF16)
        ut = lax.dot_general(w, xs, (((0,), (1,)), ((), ())), preferred_element_type=F32)
        for i in range(2):
            ut_ref[:, (s + i) * SSM_GROUP:(s + i + 1) * SSM_GROUP, :] = (
                ut[:, i * ch:(i + 1) * ch].reshape(n_grp, SSM_GROUP, ch).astype(BF16))


def _ssm_in(x1, w_in_b, u_lo, width, *, ch=LANES, wblk=256):
    t, d = x1.shape
    assert u_lo % wblk == 0 and width % wblk == 0
    n_grp = width // SSM_GROUP
    n_chunk_all = t // SSM_CHUNK
    tm = ch * SSM_CHUNK
    n_w = width // wblk
    slabs = [pl.BlockSpec((tm, LANES), lambda i, c=c: (i, c)) for c in range(d // LANES)]
    w_cols = [pl.BlockSpec((d, wblk), lambda i, c=c: (0, u_lo // wblk + c), pipeline_mode=pl.Buffered(1))
              for c in range(n_w)]
    return pl.pallas_call(
        functools.partial(_ssm_in_kernel, n_grp=n_grp, ch=ch, n_w=n_w),
        grid=(n_chunk_all // ch,),
        in_specs=slabs + w_cols,
        out_specs=pl.BlockSpec((n_grp, SSM_ROW, ch), lambda i: (0, 0, i)),
        out_shape=jax.ShapeDtypeStruct((n_grp, SSM_ROW, n_chunk_all), BF16),
        compiler_params=_params("parallel"),
        name="ssm_in",
    )(*([x1] * (d // LANES)), *([w_in_b] * n_w))


def _hdot(a, b):
    return jnp.dot(a, b, preferred_element_type=F32, precision=lax.Precision.HIGHEST)


def _cmul(ar, ai, br, bi):
    return ar * br - ai * bi, ar * bi + ai * br


def _cpowers(lbr, lbi, n, bits):
    out_r = jnp.ones(n.shape, F32)
    out_i = jnp.zeros(n.shape, F32)
    for b in range(bits):
        sel = (n & (1 << b)) != 0
        out_r, out_i = _cmul(out_r, out_i, jnp.where(sel, lbr, 1.0), jnp.where(sel, lbi, 0.0))
        lbr, lbi = _cmul(lbr, lbi, lbr, lbi)
    return out_r, out_i


def _ssm_disc_kernel(are_ref, aim_ref, ldt_ref, lbr_ref, lbi_ref, cfr_ref, cfi_ref):
    dt = jnp.exp(ldt_ref[...])
    lam_r = jnp.minimum(are_ref[...], -1e-4)
    lam_i = aim_ref[...]
    mag = jnp.exp(lam_r * dt)
    lbr = mag * jnp.cos(lam_i * dt)
    lbi = mag * jnp.sin(lam_i * dt)
    inv = 1.0 / (lam_r * lam_r + lam_i * lam_i)
    lbr_ref[...] = lbr
    lbi_ref[...] = lbi
    cfr_ref[...] = ((lbr - 1.0) * lam_r + lbi * lam_i) * inv
    cfi_ref[...] = (lbi * lam_r - (lbr - 1.0) * lam_i) * inv


def _ssm_prep_kernel(lbr_s, lbi_s, cfr_s, cfi_s, are_l, aim_l, ldt_ref, bre_ref, bim_ref, cre_ref, cim_ref,
                     tt_ref, win_ref, wout_ref, apr_ref, api_ref, *, gb):
    p = lbr_s.shape[1]
    expand = (lax.broadcasted_iota(jnp.int32, (SSM_CHUNK, SSM_ROW), 1) // SSM_GROUP
              == lax.broadcasted_iota(jnp.int32, (SSM_CHUNK, SSM_ROW), 0)).astype(F32)
    expand_t = (lax.broadcasted_iota(jnp.int32, (SSM_ROW, SSM_CHUNK), 0) // SSM_GROUP
                == lax.broadcasted_iota(jnp.int32, (SSM_ROW, SSM_CHUNK), 1)).astype(F32)
    rem = SSM_CHUNK - 1 - lax.broadcasted_iota(jnp.int32, (p, SSM_CHUNK), 1)
    tp1 = lax.broadcasted_iota(jnp.int32, (SSM_CHUNK, p), 0) + 1
    col = lax.broadcasted_iota(jnp.int32, (p, SSM_POWERS), 1)
    lane = lax.broadcasted_iota(jnp.int32, (SSM_GROUP, SSM_ROW), 1)
    chunk_bits = SSM_CHUNK.bit_length()

    for g in range(gb):
        lbr = lbr_s[g]
        lbi = lbi_s[g]
        bre = bre_ref[g]
        bim = bim_ref[g]
        bbar_r, bbar_i = _cmul(cfr_s[g], cfi_s[g], bre, bim)
        pw_r, pw_i = _cpowers(lbr, lbi, rem, chunk_bits - 1)
        win_r, win_i = _cmul(_hdot(pw_r, expand), _hdot(pw_i, expand), bbar_r, bbar_i)
        win_ref[g] = jnp.concatenate([win_r, win_i], axis=0).astype(BF16)

        a_r, a_i = _cmul(pw_r[:, 0:1], pw_i[:, 0:1], lbr, lbi)
        apr = jnp.zeros((p, SSM_POWERS), F32)
        api = jnp.zeros((p, SSM_POWERS), F32)
        for j in range(SSM_POWERS):
            apr = jnp.where(col == j, a_r, apr)
            api = jnp.where(col == j, a_i, api)
            a_r, a_i = _cmul(a_r, a_i, a_r, a_i)
        apr_ref[g] = apr
        api_ref[g] = api

        dt = jnp.exp(ldt_ref[g])
        mag_l = jnp.exp(jnp.minimum(are_l[g], -1e-4) * dt)
        e_r, e_i = _cpowers(mag_l * jnp.cos(aim_l[g] * dt), mag_l * jnp.sin(aim_l[g] * dt), tp1, chunk_bits)
        cre = cre_ref[g]
        cim = cim_ref[g]
        wo_r, wo_i = _cmul(_hdot(expand_t, e_r), _hdot(expand_t, e_i), cre, cim)
        wout_ref[g] = jnp.concatenate([wo_r, -wo_i], axis=1).astype(BF16)

        hrev = _hdot(cre[0:SSM_GROUP, :], win_r) - _hdot(cim[0:SSM_GROUP, :], win_i)
        blocks = []
        for t in range(SSM_CHUNK):
            hi = (t + 1) * SSM_GROUP
            rolled = hrev if hi == SSM_ROW else pltpu.roll(hrev, hi, axis=1)
            blocks.append(jnp.where(lane < hi, rolled, 0.0))
        tt_ref[g] = jnp.concatenate(blocks, axis=0).astype(BF16)


def _ssm_prep(a_re, a_im, log_dt, b_re, b_im, c_re, c_im, *, gb=4):
    g, p = a_re.shape
    row = SSM_ROW
    whole = lambda shape: pl.BlockSpec(shape, lambda: (0,) * len(shape))
    disc = pl.pallas_call(
        _ssm_disc_kernel,
        in_specs=[whole((p, g)), whole((p, g)), whole((1, g))],
        out_specs=[whole((p, g))] * 4,
        out_shape=[jax.ShapeDtypeStruct((p, g), F32)] * 4,
        name="ssm_disc",
    )(a_re.T, a_im.T, log_dt.reshape(1, g))
    lbr_s, lbi_s, cfr_s, cfi_s = [a.T.reshape(g, p, 1) for a in disc]

    blk = lambda shape: pl.BlockSpec((gb,) + shape, lambda i: (i, 0, 0))
    b_t = lambda b: jnp.tile(b, (1, 1, SSM_CHUNK))
    c_t = lambda c: jnp.tile(c, (1, SSM_CHUNK, 1))
    out_shapes = [
        jax.ShapeDtypeStruct((g, row, row), BF16),
        jax.ShapeDtypeStruct((g, 2 * p, row), BF16),
        jax.ShapeDtypeStruct((g, row, 2 * p), BF16),
        jax.ShapeDtypeStruct((g, p, SSM_POWERS), F32), jax.ShapeDtypeStruct((g, p, SSM_POWERS), F32),
    ]
    return pl.pallas_call(
        functools.partial(_ssm_prep_kernel, gb=gb),
        grid=(g // gb,),
        in_specs=[blk((p, 1))] * 4 + [blk((1, p)), blk((1, p)), blk((1, 1)),
                  blk((p, row)), blk((p, row)), blk((row, p)), blk((row, p))],
        out_specs=[blk(s.shape[1:]) for s in out_shapes],
        out_shape=out_shapes,
        compiler_params=_params("parallel"),
        name="ssm_prep",
    )(lbr_s, lbi_s, cfr_s, cfi_s, a_re.reshape(g, 1, p), a_im.reshape(g, 1, p),
      log_dt.reshape(g, 1, 1), b_t(b_re), b_t(b_im), c_t(c_re), c_t(c_im))


def _ssm_kernel(ut_ref, tt_ref, win_ref, wout_ref, apr_ref, api_ref, dsk_ref, y_ref, *, gb, n_chunk):
    nk = ut_ref.shape[-1]
    p = apr_ref.shape[1]
    pos = lax.broadcasted_iota(jnp.int32, (SUBLANES, nk), 1) % n_chunk
    n_steps = (n_chunk - 1).bit_length()
    assert n_steps <= SSM_POWERS

    def shift_whole_tiles(v, sh):
        segs = []
        for b0 in range(0, nk, n_chunk):
            segs += [jnp.zeros((v.shape[0], sh), F32), v[:, b0:b0 + n_chunk - sh]]
        return jnp.concatenate(segs, axis=1)

    for g in range(gb):
        ut = ut_ref[g]
        sl = jnp.dot(win_ref[g], ut, preferred_element_type=F32)
        apr = apr_ref[g]
        api = api_ref[g]
        prev_r, prev_i = [], []
        for b0 in range(0, p, SUBLANES * SCAN_INTERLEAVE):
            tiles = range(b0, b0 + SUBLANES * SCAN_INTERLEAVE, SUBLANES)
            s = {r0: (sl[r0:r0 + SUBLANES], sl[p + r0:p + r0 + SUBLANES]) for r0 in tiles}
            for j in range(n_steps):
                sh = 1 << j
                keep = pos >= sh
                for r0 in tiles:
                    s_r, s_i = s[r0]
                    a_r = apr[r0:r0 + SUBLANES, j:j + 1]
                    a_i = api[r0:r0 + SUBLANES, j:j + 1]
                    if sh % LANES == 0:
                        p_r, p_i = shift_whole_tiles(s_r, sh), shift_whole_tiles(s_i, sh)
                    else:
                        p_r = jnp.where(keep, pltpu.roll(s_r, sh, axis=1), 0.0)
                        p_i = jnp.where(keep, pltpu.roll(s_i, sh, axis=1), 0.0)
                    s[r0] = (s_r + a_r * p_r - a_i * p_i, s_i + a_r * p_i + a_i * p_r)
            keep = pos >= 1
            for r0 in tiles:
                prev_r.append(jnp.where(keep, pltpu.roll(s[r0][0], 1, axis=1), 0.0))
                prev_i.append(jnp.where(keep, pltpu.roll(s[r0][1], 1, axis=1), 0.0))
        prev = jnp.concatenate(prev_r + prev_i, axis=0).astype(BF16)
        y = jnp.dot(tt_ref[g], ut, preferred_element_type=F32)
        y += jnp.dot(wout_ref[g], prev, preferred_element_type=F32)
        y += dsk_ref[g] * ut.astype(F32)
        y_ref[g] = _gelu_tanh(y).astype(BF16)


def _ssm(ut, prep, dsk, *, n_chunk, gb=4):
    tt, win, wout, apr, api = prep
    g, row, nk = ut.shape
    p = apr.shape[1]
    blk = lambda shape: pl.BlockSpec((gb,) + shape, lambda i: (i, 0, 0))
    kern = functools.partial(_ssm_kernel, gb=gb, n_chunk=n_chunk)
    return pl.pallas_call(
        kern,
        grid=(g // gb,),
        in_specs=[blk((row, nk)), blk((row, row)), blk((2 * p, row)), blk((row, 2 * p)),
                  blk((p, SSM_POWERS)), blk((p, SSM_POWERS)), blk((row, 1))],
        out_specs=blk((row, nk)),
        out_shape=jax.ShapeDtypeStruct((g, row, nk), BF16),
        compiler_params=_params("parallel"),
        name="ssm",
    )(ut, tt, win, wout, apr, api, dsk)


def _ssm_out_kernel(yt_ref, wglu_ref, bglu_ref, wso_ref, o_ref, stage_ref, *, t_per_pass):
    n_grp, _, ch = yt_ref.shape
    width = n_grp * SSM_GROUP
    n_col = o_ref.shape[1] // LANES
    for t0 in range(0, SSM_CHUNK, t_per_pass):
        y = jnp.concatenate(
            [yt_ref[:, t * SSM_GROUP:(t + 1) * SSM_GROUP, :].reshape(width, ch)
             for t in range(t0, t0 + t_per_pass)], axis=1)
        gate = jnp.dot(wglu_ref[...], y, preferred_element_type=F32) + bglu_ref[...]
        glu = (y.astype(F32) * _sigmoid(gate)).astype(BF16)
        yo = lax.dot_general(glu, wso_ref[...], (((0,), (0,)), ((), ())), preferred_element_type=F32)
        for i in range(t_per_pass):
            rows = pl.ds(t0 + i, ch, stride=CHUNK_PITCH)
            for c in range(n_col):
                stage_ref[c, rows, :] = yo[i * ch:(i + 1) * ch, c * LANES:(c + 1) * LANES]
    cols = [stage_ref[c].reshape(ch, CHUNK_PITCH, LANES)[:, :SSM_CHUNK, :].reshape(ch * SSM_CHUNK, LANES)
            for c in range(n_col)]
    o_ref[...] = jnp.concatenate(cols, axis=1).astype(BF16)


def _ssm_out(yt, w_glu, b_glu, w_ssm_out, *, ch=LANES, t_per_pass=4):
    n_grp, row, n_chunk_all = yt.shape
    width, d = w_ssm_out.shape
    tm = ch * SSM_CHUNK
    return pl.pallas_call(
        functools.partial(_ssm_out_kernel, t_per_pass=t_per_pass),
        grid=(n_chunk_all // ch,),
        in_specs=[pl.BlockSpec((n_grp, row, ch), lambda i: (0, 0, i)),
                  _resident((width, width)), _resident((width, 1)), _resident((width, d))],
        out_specs=pl.BlockSpec((tm, d), lambda i: (i, 0)),
        out_shape=jax.ShapeDtypeStruct((n_chunk_all * SSM_CHUNK, d), BF16),
        scratch_shapes=[pltpu.VMEM((d // LANES, ch * CHUNK_PITCH, LANES), F32)],
        compiler_params=_params("parallel"),
        name="ssm_out",
    )(yt, w_glu.T.astype(BF16), b_glu.reshape(width, 1), w_ssm_out.astype(BF16))


def _mix_ffn_kernel(x_ref, o0, o1, o2, st0, st1, st2, ys_ref, ga_ref, gs_ref,
                    wao_ref, wo_ref, g_ref, b_ref, wg_ref, wu_ref, wd_ref, g3_ref, b3_ref,
                    out_ref, o_stage, st_stage, x2_ref, *, sub, chunks):
    step = pl.program_id(0)
    slot_w = step % 2
    slot_r = 1 - slot_w

    @pl.when(step == 0)
    def _():
        x2_ref[...] = jnp.zeros(x2_ref.shape, F32)

    lane = lax.broadcasted_iota(jnp.int32, (sub, LANES), 1)
    lo_half = lane < HEAD_DIM

    def token_order(ref, stage, r0):
        dil = ref.shape[0]
        n = sub // dil
        i0 = r0 // dil
        if dil == 1:
            return ref[0, i0:i0 + n, :].astype(F32)
        n_col = ref.shape[2] // LANES
        pitch = _scatter_pitch(dil)
        for r in range(dil):
            blk = ref[r, i0:i0 + n, :].astype(F32)
            for c in range(n_col):
                stage[c, pl.ds(r, n, stride=pitch), :] = blk[:, c * LANES:(c + 1) * LANES]
        cols = [stage[c, 0:n * pitch, :] for c in range(n_col)]
        if pitch != dil:
            cols = [a.reshape(n, pitch, LANES)[:, :dil, :].reshape(sub, LANES) for a in cols]
        return jnp.concatenate(cols, axis=1)

    def head_cols(vals):
        cols = []
        for hp in range(GROUP_WIDTH // LANES):
            cols.append(jnp.where(lo_half, vals[2 * hp], vals[2 * hp + 1]))
        return jnp.concatenate(cols, axis=1)

    def merge_groups(k):
        r0 = k * sub
        sts = [token_order(st, st_stage.at[k, i], r0) for i, st in enumerate((st0, st1, st2))]
        outs = [token_order(o, o_stage.at[k, i], r0) for i, o in enumerate((o0, o1, o2))]
        dens = [pltpu.roll(st, LANES - HEADS_PER_GROUP, axis=1) for st in sts]
        mx = jnp.maximum(jnp.maximum(sts[0], sts[1]), sts[2])
        w = [den * jnp.exp(st - mx) for den, st in zip(dens, sts)]
        tot = jnp.where(lane < HEADS_PER_GROUP, w[0] + w[1] + w[2], 1.0)
        att = None
        for g in range(N_ATTN_GROUPS):
            wt = w[g] / tot
            term = head_cols([wt[:, h:h + 1] for h in range(HEADS_PER_GROUP)]) * outs[g]
            att = term if att is None else att + term
        return att.astype(BF16)

    def gate(k, att):
        rows = slice(k * sub, (k + 1) * sub)
        y_attn = jnp.dot(att, wao_ref[...], preferred_element_type=F32)
        merged = (ga_ref[rows, :].astype(F32) * y_attn
                  + gs_ref[rows, :].astype(F32) * ys_ref[rows, :].astype(F32))
        return merged.astype(BF16)

    def project(k, merged):
        rows = slice(k * sub, (k + 1) * sub)
        mix = jnp.dot(merged, wo_ref[...], preferred_element_type=F32)
        x2_ref[slot_w, rows, :] = _layer_norm(DEEPNORM_ALPHA * x_ref[rows, :] + mix, g_ref[...], b_ref[...])

    x2_prev = x2_ref[slot_r]
    ffn_state = {}

    def ffn_gate_up(k, c):
        rows = slice(k * sub, (k + 1) * sub)
        if c == chunks[0]:
            ffn_state[k] = {"xb": x2_prev[rows, :].astype(BF16), "acc": None}
        xb = ffn_state[k]["xb"]
        ffn_state[k]["gu"] = (jnp.dot(xb, wg_ref[:, c[0]:c[1]], preferred_element_type=F32),
                              jnp.dot(xb, wu_ref[:, c[0]:c[1]], preferred_element_type=F32))

    def ffn_down(k, c):
        rows = slice(k * sub, (k + 1) * sub)
        gate_v, up = ffn_state[k].pop("gu")
        h = (gate_v * _sigmoid(gate_v) * up).astype(BF16)
        part = jnp.dot(h, wd_ref[c[0]:c[1], :], preferred_element_type=F32)
        acc = ffn_state[k]["acc"]
        ffn_state[k]["acc"] = part if acc is None else acc + part
        if c == chunks[-1]:
            y = DEEPNORM_ALPHA * x2_prev[rows, :] + 0.5 * ffn_state.pop(k)["acc"]
            out_ref[rows, :] = _layer_norm(y, g3_ref[...], b3_ref[...])

    n_sub = x_ref.shape[0] // sub
    ffn_stages = [f for k in range(n_sub) for c in chunks
                  for f in (functools.partial(ffn_gate_up, k, c), functools.partial(ffn_down, k, c))]
    mix_stages = [f for k in range(n_sub) for f in (functools.partial(merge_groups, k),
                                                    functools.partial(gate, k),
                                                    functools.partial(project, k))]
    carry = ()
    for f in ffn_stages:
        f()
        if mix_stages:
            out = mix_stages.pop(0)(*carry)
            carry = () if out is None else (out,)
    assert not mix_stages


def _mix_ffn(x1, os_, sts, y_ssm, ga, gs, w_attn_out, w_o, g2, b2, wg, wu, wd, g3, b3, *,
             seq, tm=512, sub=256, chunk=1024):
    t, d = x1.shape
    d_ff = wg.shape[1]
    tiles = seq // tm
    n_tiles = t // tm
    cur = lambda i: jnp.minimum(i, n_tiles - 1)
    tile = lambda w: pl.BlockSpec((tm, w), lambda i: (cur(i), 0))

    def by_class(a):
        _, dil, n, w = a.shape
        return pl.BlockSpec((None, dil, tm // dil, w), lambda i: (cur(i) // tiles, 0, cur(i) % tiles, 0))

    n_grp = len(os_)
    stage_rows = max(sub // a.shape[1] * _scatter_pitch(a.shape[1]) for a in os_)
    vec = lambda v: v.reshape(1, d)
    return pl.pallas_call(
        functools.partial(_mix_ffn_kernel, sub=sub, chunks=_ffn_chunks(d_ff, chunk)),
        grid=(n_tiles + 1,),
        in_specs=[tile(d)] + [by_class(a) for a in os_] + [by_class(a) for a in sts] + [tile(d), tile(d), tile(d),
                  _resident(w_attn_out.shape), _resident(w_o.shape), _resident((1, d)), _resident((1, d)),
                  _resident((d, d_ff)), _resident((d, d_ff)), _resident((d_ff, d)),
                  _resident((1, d)), _resident((1, d))],
        out_specs=pl.BlockSpec((tm, d), lambda i: (jnp.maximum(i - 1, 0), 0)),
        out_shape=jax.ShapeDtypeStruct((t, d), F32),
        scratch_shapes=[pltpu.VMEM((tm // sub, n_grp, GROUP_WIDTH // LANES, stage_rows, LANES), F32),
                        pltpu.VMEM((tm // sub, n_grp, 1, stage_rows, LANES), F32),
                        pltpu.VMEM((2, tm, d), F32)],
        compiler_params=_params("arbitrary"),
        name="mix_ffn",
    )(x1, *os_, *sts, y_ssm, ga, gs, w_attn_out.astype(BF16), w_o.astype(BF16), vec(g2), vec(b2),
      wg.astype(BF16), wu.astype(BF16), wd.astype(BF16), vec(g3), vec(b3))


def _layer(x, positions, w_in, w_attn_out, a_re, a_im, log_dt, b_re, b_im, c_re, c_im, d_skip,
           w_glu, b_glu, w_ssm_out, w_o, ffn1, ffn2, ln1, ln2, ln3):
    batch, seq, d = x.shape
    t = batch * seq
    n_grp = a_re.shape[0]
    ssm_width = n_grp * SSM_GROUP
    u_lo = 3 * ATTN_WIDTH

    x1 = _ffn_ln(x.reshape(t, d), *ffn1, *ln1)
    w_in_b = w_in.astype(BF16)
    *qkvs, ga, gs = _in_proj(x1, positions, w_in_b, batch=batch, seq=seq, ssm_width=ssm_width)

    os_, sts = [], []
    for qkv, (window, dilation) in zip(qkvs, ATTN_PATTERNS):
        assert window == ATTN_BLK * dilation
        o, st = _dilated_attention(qkv)
        os_.append(o)
        sts.append(st)

    ut = _ssm_in(x1, w_in_b, u_lo, ssm_width)
    prep = _ssm_prep(a_re, a_im, log_dt, b_re, b_im, c_re, c_im)
    dsk = jnp.tile(d_skip.reshape(n_grp, 1, SSM_GROUP), (1, SSM_CHUNK, 1)).reshape(n_grp, SSM_ROW, 1)
    yt = _ssm(ut, prep, dsk, n_chunk=seq // SSM_CHUNK)
    y_ssm = _ssm_out(yt, w_glu, b_glu, w_ssm_out)

    x3 = _mix_ffn(x1, os_, sts, y_ssm, ga, gs, w_attn_out, w_o, *ln2, *ffn2, *ln3, seq=seq)
    return x3.reshape(batch, seq, d)


def kernel(x, positions, w_in, w_attn_out, a_re, a_im, log_dt, b_re, b_im, c_re, c_im, d_skip, w_glu, b_glu, w_ssm_out, w_o, ffn1_wg, ffn1_wu, ffn1_wd, ffn2_wg, ffn2_wu, ffn2_wd, ln1_g, ln1_b, ln2_g, ln2_b, ln3_g, ln3_b):
    depth = w_in.shape[0]
    for i in range(depth):
        x = _layer(x, positions, w_in[i], w_attn_out[i], a_re[i], a_im[i], log_dt[i], b_re[i], b_im[i],
                   c_re[i], c_im[i], d_skip[i], w_glu[i], b_glu[i], w_ssm_out[i], w_o[i],
                   (ffn1_wg[i], ffn1_wu[i], ffn1_wd[i]), (ffn2_wg[i], ffn2_wu[i], ffn2_wd[i]),
                   (ln1_g[i], ln1_b[i]), (ln2_g[i], ln2_b[i]), (ln3_g[i], ln3_b[i]))
    return x
```

```python
import functools

import jax
import jax.numpy as jnp
from jax import lax
from jax.experimental import pallas as pl
from jax.experimental.pallas import tpu as pltpu

F32 = jnp.float32
BF16 = jnp.bfloat16

HEAD_DIM = 64
HEADS_PER_GROUP = 4
GROUP_WIDTH = HEADS_PER_GROUP * HEAD_DIM
ATTN_PATTERNS = ((128, 1), (512, 4), (2048, 16))
N_ATTN_GROUPS = len(ATTN_PATTERNS)
ATTN_WIDTH = N_ATTN_GROUPS * GROUP_WIDTH
ROT_DIM = HEAD_DIM // 4
ROPE_THETA = 500000.0
ROPE_ROWS = 32
ATTN_BLK = 128
NEG_INF = -1e30
LOG2_E = 1.4426950408889634
SSM_GROUP = 16
SSM_STATE = 64
SSM_CHUNK = 16
SSM_ROW = SSM_CHUNK * SSM_GROUP
SSM_POWERS = 16
SCAN_INTERLEAVE = 8
CHUNK_PITCH = 24


def _scatter_pitch(stride):
    return CHUNK_PITCH if stride == SSM_CHUNK else stride


def _rows_by_class(val, stage, dil):
    rows, w = val.shape
    n = rows // dil
    pitch = _scatter_pitch(dil)
    n_col = w // LANES
    for c in range(n_col):
        col = val[:, c * LANES:(c + 1) * LANES]
        if pitch != dil:
            col = jnp.concatenate([col.reshape(n, dil, LANES), jnp.zeros((n, pitch - dil, LANES), F32)], axis=1)
            col = col.reshape(n * pitch, LANES)
        stage[c, 0:n * pitch, :] = col

    def get(r):
        return jnp.concatenate([stage[c, pl.ds(r, n, stride=pitch), :] for c in range(n_col)], axis=1)

    return get
DEEPNORM_ALPHA = 2.0 ** 0.25
LN_EPS = 1e-5

LANES = 128
SUBLANES = 8
VMEM_LIMIT_BYTES = 56 * 1024 * 1024


def _params(*semantics):
    return pltpu.CompilerParams(dimension_semantics=semantics, vmem_limit_bytes=VMEM_LIMIT_BYTES)


def _resident(shape):
    zeros = (0,) * len(shape)
    return pl.BlockSpec(shape, lambda *_: zeros, pipeline_mode=pl.Buffered(1))


def _layer_norm(y, g, b):
    mu = jnp.mean(y, axis=-1, keepdims=True)
    yc = y - mu
    var = jnp.mean(yc * yc, axis=-1, keepdims=True)
    return yc * lax.rsqrt(var + LN_EPS) * g + b


def _sigmoid(x):
    return 1.0 / (1.0 + jnp.exp(-x))


def _gelu_tanh(x):
    c = 0.7978845608028654
    return 0.5 * x * (1.0 + jnp.tanh(c * (x + 0.044715 * (x * x * x))))


def _ffn_ln_kernel(x_ref, wg_ref, wu_ref, wd_ref, g_ref, b_ref, o_ref, ot_ref, stage_ref, *, chunks, sub):
    for r0 in range(0, x_ref.shape[0], sub):
        x = x_ref[r0:r0 + sub, :]
        xb = x.astype(BF16)
        acc = None
        for c0, c1 in chunks:
            gate = jnp.dot(xb, wg_ref[:, c0:c1], preferred_element_type=F32)
            up = jnp.dot(xb, wu_ref[:, c0:c1], preferred_element_type=F32)
            h = (gate * _sigmoid(gate) * up).astype(BF16)
            part = jnp.dot(h, wd_ref[c0:c1, :], preferred_element_type=F32)
            acc = part if acc is None else acc + part
        y = DEEPNORM_ALPHA * x + 0.5 * acc
        out = _layer_norm(y, g_ref[...], b_ref[...])
        o_ref[r0:r0 + sub, :] = out
        get = _rows_by_class(out, stage_ref.at[r0 // sub], SSM_CHUNK)
        k0, nk = r0 // SSM_CHUNK, sub // SSM_CHUNK
        for s in range(SSM_CHUNK):
            ot_ref[s, k0:k0 + nk, :] = get(s).astype(BF16)


def _ffn_chunks(d_ff, width):
    edges = list(range(0, d_ff, width)) + [d_ff]
    return tuple(zip(edges[:-1], edges[1:]))


def _ffn_ln(x, wg, wu, wd, g, b, *, tm=1024, sub=256, chunk=1024):
    t, d = x.shape
    d_ff = wg.shape[1]
    kern = functools.partial(_ffn_ln_kernel, chunks=_ffn_chunks(d_ff, chunk), sub=sub)
    stage_rows = sub // SSM_CHUNK * _scatter_pitch(SSM_CHUNK)
    return pl.pallas_call(
        kern,
        grid=(t // tm,),
        in_specs=[
            pl.BlockSpec((tm, d), lambda i: (i, 0)),
            _resident((d, d_ff)), _resident((d, d_ff)), _resident((d_ff, d)),
            _resident((1, d)), _resident((1, d)),
        ],
        out_specs=[pl.BlockSpec((tm, d), lambda i: (i, 0)),
                   pl.BlockSpec((None, SSM_CHUNK, tm // SSM_CHUNK, d), lambda i: (i, 0, 0, 0))],
        out_shape=[jax.ShapeDtypeStruct((t, d), F32),
                   jax.ShapeDtypeStruct((t // tm, SSM_CHUNK, tm // SSM_CHUNK, d), BF16)],
        scratch_shapes=[pltpu.VMEM((tm // sub, d // LANES, stage_rows, LANES), F32)],
        compiler_params=_params("parallel"),
        name="ffn_ln",
    )(x, wg.astype(BF16), wu.astype(BF16), wd.astype(BF16), g.reshape(1, d), b.reshape(1, d))


def _rope_spread():
    half = ROT_DIM // 2
    rows = lax.broadcasted_iota(jnp.int32, (ROPE_ROWS, 2 * LANES), 0)
    cols = lax.broadcasted_iota(jnp.int32, (ROPE_ROWS, 2 * LANES), 1)
    in_head = cols % HEAD_DIM
    freq = in_head % half
    is_cos = cols < LANES
    rot = in_head < ROT_DIM
    cos_part = is_cos & rot & (rows == freq)
    one_part = is_cos & jnp.logical_not(rot) & (rows == 2 * half)
    sin_part = jnp.logical_not(is_cos) & rot & (rows == half + freq)
    sign = jnp.where(in_head < half, 1.0, -1.0)
    return jnp.where(cos_part | one_part, 1.0, jnp.where(sin_part, sign, 0.0)).astype(BF16)


def _in_proj_kernel(x_ref, pos_ref, invf_ref, w_ref,
                    qkv0_ref, qkv1_ref, qkv2_ref, ga_ref, gs_ref, stage_ref, *, splits, sub):
    half = ROT_DIM // 2
    spread = _rope_spread()
    first = lax.broadcasted_iota(jnp.int32, (sub, LANES), 1) % HEAD_DIM < half
    s0, s1, s2, s3, s4, s5 = splits
    tn = (((0,), (0,)), ((), ()))

    for r0 in range(0, x_ref.shape[0], sub):
        xb = x_ref[r0:r0 + sub, :].astype(BF16)

        ang = invf_ref[...] * pos_ref[:, r0:r0 + sub].astype(F32)
        tab = jnp.concatenate([jnp.cos(ang), jnp.sin(ang), jnp.ones((half, sub), F32),
                               jnp.zeros((ROPE_ROWS - 3 * half, sub), F32)], axis=0)
        tab_hi = tab.astype(BF16)
        tab_lo = (tab - tab_hi.astype(F32)).astype(BF16)
        cs = (lax.dot_general(tab_hi, spread, tn, preferred_element_type=F32)
              + lax.dot_general(tab_lo, spread, tn, preferred_element_type=F32))
        cos = cs[:, :LANES]
        sin = cs[:, LANES:]

        def rotate(z):
            cols = []
            for c in range(z.shape[1] // LANES):
                zc = z[:, c * LANES:(c + 1) * LANES]
                zs = zc * sin
                up = pltpu.roll(jnp.where(first, zs, 0.0), half, axis=1)
                dn = pltpu.roll(jnp.where(first, 0.0, zs), LANES - half, axis=1)
                cols.append(zc * cos + up + dn)
            return jnp.concatenate(cols, axis=1)

        def proj(lo, hi):
            return jnp.dot(xb, w_ref[:, lo:hi], preferred_element_type=F32)

        q = rotate(proj(0, s0)) * (HEAD_DIM ** -0.5 * LOG2_E)
        k = rotate(proj(s0, s1))
        v = proj(s1, s2)
        for gi, out_ref in enumerate((qkv0_ref, qkv1_ref, qkv2_ref)):
            dil = ATTN_PATTERNS[gi][1]
            c0 = gi * GROUP_WIDTH
            qkv = jnp.concatenate([a[:, c0:c0 + GROUP_WIDTH] for a in (q, k, v)], axis=1)
            n = sub // dil
            i0 = r0 // dil
            if dil == 1:
                out_ref[0, i0:i0 + n, :] = qkv.astype(BF16)
            else:
                get = _rows_by_class(qkv, stage_ref.at[r0 // sub, gi - 1], dil)
                for r in range(dil):
                    out_ref[r, i0:i0 + n, :] = get(r).astype(BF16)
        ga_ref[r0:r0 + sub, :] = _sigmoid(proj(s3, s4)).astype(BF16)
        gs_ref[r0:r0 + sub, :] = _sigmoid(proj(s4, s5)).astype(BF16)


def _in_proj(x1, positions, w_in, *, batch, seq, ssm_width, tm=1024, sub=256):
    t, d = x1.shape
    splits = (ATTN_WIDTH, 2 * ATTN_WIDTH, 3 * ATTN_WIDTH, 3 * ATTN_WIDTH + ssm_width,
              3 * ATTN_WIDTH + ssm_width + d, 3 * ATTN_WIDTH + ssm_width + 2 * d)
    assert splits[-1] == w_in.shape[1] and seq % tm == 0
    half = ROT_DIM // 2
    invf = (ROPE_THETA ** (-jnp.arange(half, dtype=F32) * 2.0 / ROT_DIM)).reshape(half, 1)
    tiles = seq // tm
    qkv_w = 3 * GROUP_WIDTH
    qkv_specs = [pl.BlockSpec((None, dil, tm // dil, qkv_w), lambda i: (i // tiles, 0, i % tiles, 0))
                 for _, dil in ATTN_PATTERNS]
    qkv_shapes = [jax.ShapeDtypeStruct((batch, dil, seq // dil, qkv_w), BF16) for _, dil in ATTN_PATTERNS]
    widths = (d, d)
    kern = functools.partial(_in_proj_kernel, splits=splits, sub=sub)
    n_strided = sum(dil > 1 for _, dil in ATTN_PATTERNS)
    stage_rows = max(sub // dil * _scatter_pitch(dil) for _, dil in ATTN_PATTERNS)
    return pl.pallas_call(
        kern,
        grid=(t // tm,),
        in_specs=[
            pl.BlockSpec((tm, d), lambda i: (i, 0)),
            pl.BlockSpec((1, tm), lambda i: (0, i)),
            _resident((half, 1)),
            _resident(w_in.shape),
        ],
        out_specs=qkv_specs + [pl.BlockSpec((tm, w), lambda i: (i, 0)) for w in widths],
        out_shape=qkv_shapes + [jax.ShapeDtypeStruct((t, w), BF16) for w in widths],
        scratch_shapes=[pltpu.VMEM((tm // sub, n_strided, qkv_w // LANES, stage_rows, LANES), F32)],
        compiler_params=_params("parallel"),
        name="in_proj",
    )(x1, positions.reshape(1, t), invf, w_in)


def _attn_kernel(q_ref, kc_ref, kp_ref, vc_ref, vp_ref, o_ref, st_ref, *, n_sub):
    first = pl.program_id(2) == 0
    row = lax.broadcasted_iota(jnp.int32, (ATTN_BLK, 2 * ATTN_BLK), 0)
    col = lax.broadcasted_iota(jnp.int32, (ATTN_BLK, 2 * ATTN_BLK), 1)
    band = (col >= row) & (col <= row + ATTN_BLK)
    lane = lax.broadcasted_iota(jnp.int32, (ATTN_BLK, LANES), 1)
    lo_half = lane < HEAD_DIM

    n_pair = GROUP_WIDTH // LANES
    tiles = [(j, hp, hh) for j in range(n_sub) for hp in range(n_pair) for hh in range(2)]

    def window(ref_prev, ref_cur, j, c0):
        r0 = j * ATTN_BLK
        if j == 0:
            return jnp.concatenate([ref_prev[:, c0:c0 + LANES], ref_cur[0:ATTN_BLK, c0:c0 + LANES]], axis=0)
        return ref_cur[r0 - ATTN_BLK:r0 + ATTN_BLK, c0:c0 + LANES]

    def scores(tile):
        j, hp, hh = tile
        q2 = q_ref[j * ATTN_BLK:(j + 1) * ATTN_BLK, hp * LANES:(hp + 1) * LANES]
        qm = jnp.where(lo_half if hh == 0 else jnp.logical_not(lo_half), q2, jnp.zeros_like(q2))
        return lax.dot_general(qm, window(kp_ref, kc_ref, j, hp * LANES), (((1,), (1,)), ((), ())),
                               preferred_element_type=F32)

    def softmax(tile, s):
        valid = band & ((col >= ATTN_BLK) | jnp.logical_not(first)) if tile[0] == 0 else band
        s = jnp.where(valid, s, NEG_INF)
        m = jnp.max(s, axis=1, keepdims=True)
        p = jnp.exp2(s - m)
        return p.astype(BF16), m, jnp.sum(p, axis=1, keepdims=True)

    def weighted_values(tile, p, den):
        j, hp, _ = tile
        return jnp.dot(p, window(vp_ref, vc_ref, j, hp * LANES), preferred_element_type=F32) / den

    s_q, p_q, outs = {}, {}, {}
    for step in range(len(tiles) + 2):
        if step < len(tiles):
            s_q[step] = scores(tiles[step])
        if 0 <= step - 1 < len(tiles):
            p_q[step - 1] = softmax(tiles[step - 1], s_q.pop(step - 1))
        if 0 <= step - 2 < len(tiles):
            t = step - 2
            j, hp, hh = tiles[t]
            p, m, den = p_q.pop(t)
            outs[hh] = weighted_values(tiles[t], p, den)
            h = 2 * hp + hh
            rows = slice(j * ATTN_BLK, (j + 1) * ATTN_BLK)
            if h == 0:
                st_ref[rows, :] = jnp.zeros((ATTN_BLK, LANES), F32)
            st_ref[rows, h:h + 1] = m
            st_ref[rows, HEADS_PER_GROUP + h:HEADS_PER_GROUP + h + 1] = den
            if hh == 1:
                o_ref[rows, hp * LANES:(hp + 1) * LANES] = jnp.where(lo_half, outs[0], outs[1]).astype(BF16)


def _dilated_attention(qkv):
    batch, dilation, n, _ = qkv.shape
    qb = min(1024, n)
    n_sub = qb // ATTN_BLK

    def cur(part):
        return pl.BlockSpec((None, None, qb, GROUP_WIDTH), lambda b, r, i: (b, r, i, part))

    def prev(part):
        return pl.BlockSpec((None, None, ATTN_BLK, GROUP_WIDTH),
                            lambda b, r, i: (b, r, jnp.maximum(i * n_sub - 1, 0), part))

    return pl.pallas_call(
        functools.partial(_attn_kernel, n_sub=n_sub),
        grid=(batch, dilation, n // qb),
        in_specs=[cur(0), cur(1), prev(1), cur(2), prev(2)],
        out_specs=[pl.BlockSpec((None, None, qb, GROUP_WIDTH), lambda b, r, i: (b, r, i, 0)),
                   pl.BlockSpec((None, None, qb, LANES), lambda b, r, i: (b, r, i, 0))],
        out_shape=[jax.ShapeDtypeStruct((batch, dilation, n, GROUP_WIDTH), BF16),
                   jax.ShapeDtypeStruct((batch, dilation, n, LANES), F32)],
        compiler_params=_params("parallel", "parallel", "parallel"),
        name=f"attn_d{dilation}",
    )(qkv, qkv, qkv, qkv, qkv)


def _ssm_in_kernel(x_ref, *refs, n_grp, ch):
    w_refs, ut_ref = refs[:-1], refs[-1]
    w = jnp.concatenate([w_ref[...] for w_ref in w_refs], axis=1)
    for s in range(0, SSM_CHUNK, 2):
        xs = jnp.concatenate([x_ref[tile, s + i] for i in range(2) for tile in range(x_ref.shape[0])], axis=0)
        ut = lax.dot_general(w, xs, (((0,), (1,)), ((), ())), preferred_element_type=F32)
        for i in range(2):
            ut_ref[:, (s + i) * SSM_GROUP:(s + i + 1) * SSM_GROUP, :] = (
                ut[:, i * ch:(i + 1) * ch].reshape(n_grp, SSM_GROUP, ch).astype(BF16))


def _ssm_in(x1_t, w_in_b, u_lo, width, *, ch=LANES, wblk=256):
    n_tiles, _, per_tile, d = x1_t.shape
    assert u_lo % wblk == 0 and width % wblk == 0 and ch % per_tile == 0
    n_grp = width // SSM_GROUP
    n_chunk_all = n_tiles * per_tile
    n_w = width // wblk
    w_cols = [pl.BlockSpec((d, wblk), lambda i, c=c: (0, u_lo // wblk + c), pipeline_mode=pl.Buffered(1))
              for c in range(n_w)]
    return pl.pallas_call(
        functools.partial(_ssm_in_kernel, n_grp=n_grp, ch=ch),
        grid=(n_chunk_all // ch,),
        in_specs=[pl.BlockSpec((ch // per_tile, SSM_CHUNK, per_tile, d), lambda i: (i, 0, 0, 0))] + w_cols,
        out_specs=pl.BlockSpec((n_grp, SSM_ROW, ch), lambda i: (0, 0, i)),
        out_shape=jax.ShapeDtypeStruct((n_grp, SSM_ROW, n_chunk_all), BF16),
        compiler_params=_params("parallel"),
        name="ssm_in",
    )(x1_t, *([w_in_b] * n_w))


def _hdot(a, b):
    return jnp.dot(a, b, preferred_element_type=F32, precision=lax.Precision.HIGHEST)


def _cmul(ar, ai, br, bi):
    return ar * br - ai * bi, ar * bi + ai * br


def _cpowers(lbr, lbi, n, bits):
    out_r = jnp.ones(n.shape, F32)
    out_i = jnp.zeros(n.shape, F32)
    for b in range(bits):
        sel = (n & (1 << b)) != 0
        out_r, out_i = _cmul(out_r, out_i, jnp.where(sel, lbr, 1.0), jnp.where(sel, lbi, 0.0))
        lbr, lbi = _cmul(lbr, lbi, lbr, lbi)
    return out_r, out_i


def _ssm_disc_kernel(are_ref, aim_ref, ldt_ref, lbr_ref, lbi_ref, cfr_ref, cfi_ref):
    dt = jnp.exp(ldt_ref[...])
    lam_r = jnp.minimum(are_ref[...], -1e-4)
    lam_i = aim_ref[...]
    mag = jnp.exp(lam_r * dt)
    lbr = mag * jnp.cos(lam_i * dt)
    lbi = mag * jnp.sin(lam_i * dt)
    inv = 1.0 / (lam_r * lam_r + lam_i * lam_i)
    lbr_ref[...] = lbr
    lbi_ref[...] = lbi
    cfr_ref[...] = ((lbr - 1.0) * lam_r + lbi * lam_i) * inv
    cfi_ref[...] = (lbi * lam_r - (lbr - 1.0) * lam_i) * inv


def _ssm_prep_kernel(lbr_s, lbi_s, cfr_s, cfi_s, are_l, aim_l, ldt_ref, bre_ref, bim_ref, cre_ref, cim_ref,
                     tt_ref, win_ref, wout_ref, apr_ref, api_ref, *, gb):
    p = lbr_s.shape[1]
    expand = (lax.broadcasted_iota(jnp.int32, (SSM_CHUNK, SSM_ROW), 1) // SSM_GROUP
              == lax.broadcasted_iota(jnp.int32, (SSM_CHUNK, SSM_ROW), 0)).astype(F32)
    expand_t = (lax.broadcasted_iota(jnp.int32, (SSM_ROW, SSM_CHUNK), 0) // SSM_GROUP
                == lax.broadcasted_iota(jnp.int32, (SSM_ROW, SSM_CHUNK), 1)).astype(F32)
    rem = SSM_CHUNK - 1 - lax.broadcasted_iota(jnp.int32, (p, SSM_CHUNK), 1)
    tp1 = lax.broadcasted_iota(jnp.int32, (SSM_CHUNK, p), 0) + 1
    col = lax.broadcasted_iota(jnp.int32, (p, SSM_POWERS), 1)
    lane = lax.broadcasted_iota(jnp.int32, (SSM_GROUP, SSM_ROW), 1)
    chunk_bits = SSM_CHUNK.bit_length()

    for g in range(gb):
        lbr = lbr_s[g]
        lbi = lbi_s[g]
        bre = bre_ref[g]
        bim = bim_ref[g]
        bbar_r, bbar_i = _cmul(cfr_s[g], cfi_s[g], bre, bim)
        pw_r, pw_i = _cpowers(lbr, lbi, rem, chunk_bits - 1)
        win_r, win_i = _cmul(_hdot(pw_r, expand), _hdot(pw_i, expand), bbar_r, bbar_i)
        win_ref[g] = jnp.concatenate([win_r, win_i], axis=0).astype(BF16)

        a_r, a_i = _cmul(pw_r[:, 0:1], pw_i[:, 0:1], lbr, lbi)
        apr = jnp.zeros((p, SSM_POWERS), F32)
        api = jnp.zeros((p, SSM_POWERS), F32)
        for j in range(SSM_POWERS):
            apr = jnp.where(col == j, a_r, apr)
            api = jnp.where(col == j, a_i, api)
            a_r, a_i = _cmul(a_r, a_i, a_r, a_i)
        apr_ref[g] = apr
        api_ref[g] = api

        dt = jnp.exp(ldt_ref[g])
        mag_l = jnp.exp(jnp.minimum(are_l[g], -1e-4) * dt)
        e_r, e_i = _cpowers(mag_l * jnp.cos(aim_l[g] * dt), mag_l * jnp.sin(aim_l[g] * dt), tp1, chunk_bits)
        cre = cre_ref[g]
        cim = cim_ref[g]
        wo_r, wo_i = _cmul(_hdot(expand_t, e_r), _hdot(expand_t, e_i), cre, cim)
        wout_ref[g] = jnp.concatenate([wo_r, -wo_i], axis=1).astype(BF16)

        hrev = _hdot(cre[0:SSM_GROUP, :], win_r) - _hdot(cim[0:SSM_GROUP, :], win_i)
        blocks = []
        for t in range(SSM_CHUNK):
            hi = (t + 1) * SSM_GROUP
            rolled = hrev if hi == SSM_ROW else pltpu.roll(hrev, hi, axis=1)
            blocks.append(jnp.where(lane < hi, rolled, 0.0))
        tt_ref[g] = jnp.concatenate(blocks, axis=0).astype(BF16)


def _ssm_prep(a_re, a_im, log_dt, b_re, b_im, c_re, c_im, *, gb=4):
    g, p = a_re.shape
    row = SSM_ROW
    whole = lambda shape: pl.BlockSpec(shape, lambda: (0,) * len(shape))
    disc = pl.pallas_call(
        _ssm_disc_kernel,
        in_specs=[whole((p, g)), whole((p, g)), whole((1, g))],
        out_specs=[whole((p, g))] * 4,
        out_shape=[jax.ShapeDtypeStruct((p, g), F32)] * 4,
        name="ssm_disc",
    )(a_re.T, a_im.T, log_dt.reshape(1, g))
    lbr_s, lbi_s, cfr_s, cfi_s = [a.T.reshape(g, p, 1) for a in disc]

    blk = lambda shape: pl.BlockSpec((gb,) + shape, lambda i: (i, 0, 0))
    b_t = lambda b: jnp.tile(b, (1, 1, SSM_CHUNK))
    c_t = lambda c: jnp.tile(c, (1, SSM_CHUNK, 1))
    out_shapes = [
        jax.ShapeDtypeStruct((g, row, row), BF16),
        jax.ShapeDtypeStruct((g, 2 * p, row), BF16),
        jax.ShapeDtypeStruct((g, row, 2 * p), BF16),
        jax.ShapeDtypeStruct((g, p, SSM_POWERS), F32), jax.ShapeDtypeStruct((g, p, SSM_POWERS), F32),
    ]
    return pl.pallas_call(
        functools.partial(_ssm_prep_kernel, gb=gb),
        grid=(g // gb,),
        in_specs=[blk((p, 1))] * 4 + [blk((1, p)), blk((1, p)), blk((1, 1)),
                  blk((p, row)), blk((p, row)), blk((row, p)), blk((row, p))],
        out_specs=[blk(s.shape[1:]) for s in out_shapes],
        out_shape=out_shapes,
        compiler_params=_params("parallel"),
        name="ssm_prep",
    )(lbr_s, lbi_s, cfr_s, cfi_s, a_re.reshape(g, 1, p), a_im.reshape(g, 1, p),
      log_dt.reshape(g, 1, 1), b_t(b_re), b_t(b_im), c_t(c_re), c_t(c_im))


def _ssm_kernel(ut_ref, tt_ref, win_ref, wout_ref, apr_ref, api_ref, dsk_ref, y_ref, *, gb, n_chunk):
    nk = ut_ref.shape[-1]
    p = apr_ref.shape[1]
    pos = lax.broadcasted_iota(jnp.int32, (SUBLANES, nk), 1) % n_chunk
    n_steps = (n_chunk - 1).bit_length()
    assert n_steps <= SSM_POWERS

    def shift_whole_tiles(v, sh):
        segs = []
        for b0 in range(0, nk, n_chunk):
            segs += [jnp.zeros((v.shape[0], sh), F32), v[:, b0:b0 + n_chunk - sh]]
        return jnp.concatenate(segs, axis=1)

    for g in range(gb):
        ut = ut_ref[g]
        sl = jnp.dot(win_ref[g], ut, preferred_element_type=F32)
        apr = apr_ref[g]
        api = api_ref[g]
        prev_r, prev_i = [], []
        for b0 in range(0, p, SUBLANES * SCAN_INTERLEAVE):
            tiles = range(b0, b0 + SUBLANES * SCAN_INTERLEAVE, SUBLANES)
            s = {r0: (sl[r0:r0 + SUBLANES], sl[p + r0:p + r0 + SUBLANES]) for r0 in tiles}
            for j in range(n_steps):
                sh = 1 << j
                keep = pos >= sh
                for r0 in tiles:
                    s_r, s_i = s[r0]
                    a_r = apr[r0:r0 + SUBLANES, j:j + 1]
                    a_i = api[r0:r0 + SUBLANES, j:j + 1]
                    if sh % LANES == 0:
                        p_r, p_i = shift_whole_tiles(s_r, sh), shift_whole_tiles(s_i, sh)
                    else:
                        p_r = jnp.where(keep, pltpu.roll(s_r, sh, axis=1), 0.0)
                        p_i = jnp.where(keep, pltpu.roll(s_i, sh, axis=1), 0.0)
                    s[r0] = (s_r + a_r * p_r - a_i * p_i, s_i + a_r * p_i + a_i * p_r)
            keep = pos >= 1
            for r0 in tiles:
                prev_r.append(jnp.where(keep, pltpu.roll(s[r0][0], 1, axis=1), 0.0))
                prev_i.append(jnp.where(keep, pltpu.roll(s[r0][1], 1, axis=1), 0.0))
        prev = jnp.concatenate(prev_r + prev_i, axis=0).astype(BF16)
        y = jnp.dot(tt_ref[g], ut, preferred_element_type=F32)
        y += jnp.dot(wout_ref[g], prev, preferred_element_type=F32)
        y += dsk_ref[g] * ut.astype(F32)
        y_ref[g] = _gelu_tanh(y).astype(BF16)


def _ssm(ut, prep, dsk, *, n_chunk, gb=4):
    tt, win, wout, apr, api = prep
    g, row, nk = ut.shape
    p = apr.shape[1]
    blk = lambda shape: pl.BlockSpec((gb,) + shape, lambda i: (i, 0, 0))
    kern = functools.partial(_ssm_kernel, gb=gb, n_chunk=n_chunk)
    return pl.pallas_call(
        kern,
        grid=(g // gb,),
        in_specs=[blk((row, nk)), blk((row, row)), blk((2 * p, row)), blk((row, 2 * p)),
                  blk((p, SSM_POWERS)), blk((p, SSM_POWERS)), blk((row, 1))],
        out_specs=blk((row, nk)),
        out_shape=jax.ShapeDtypeStruct((g, row, nk), BF16),
        compiler_params=_params("parallel"),
        name="ssm",
    )(ut, tt, win, wout, apr, api, dsk)


def _ssm_out_kernel(yt_ref, wglu_ref, bglu_ref, wso_ref, o_ref, *, t_per_pass):
    n_grp, _, ch = yt_ref.shape
    width = n_grp * SSM_GROUP
    for t0 in range(0, SSM_CHUNK, t_per_pass):
        y = jnp.concatenate(
            [yt_ref[:, t * SSM_GROUP:(t + 1) * SSM_GROUP, :].reshape(width, ch)
             for t in range(t0, t0 + t_per_pass)], axis=1)
        gate = jnp.dot(wglu_ref[...], y, preferred_element_type=F32) + bglu_ref[...]
        glu = (y.astype(F32) * _sigmoid(gate)).astype(BF16)
        yo = lax.dot_general(glu, wso_ref[...], (((0,), (0,)), ((), ())), preferred_element_type=F32)
        for i in range(t_per_pass):
            o_ref[t0 + i] = yo[i * ch:(i + 1) * ch, :].astype(BF16)


def _ssm_out(yt, w_glu, b_glu, w_ssm_out, *, ch=LANES, t_per_pass=4):
    n_grp, row, n_chunk_all = yt.shape
    width, d = w_ssm_out.shape
    return pl.pallas_call(
        functools.partial(_ssm_out_kernel, t_per_pass=t_per_pass),
        grid=(n_chunk_all // ch,),
        in_specs=[pl.BlockSpec((n_grp, row, ch), lambda i: (0, 0, i)),
                  _resident((width, width)), _resident((width, 1)), _resident((width, d))],
        out_specs=pl.BlockSpec((None, SSM_CHUNK, ch, d), lambda i: (i, 0, 0, 0)),
        out_shape=jax.ShapeDtypeStruct((n_chunk_all // ch, SSM_CHUNK, ch, d), BF16),
        compiler_params=_params("parallel"),
        name="ssm_out",
    )(yt, w_glu.T.astype(BF16), b_glu.reshape(width, 1), w_ssm_out.astype(BF16))


def _mix_ffn_kernel(x_ref, o0, o1, o2, st0, st1, st2, ys_ref, ga_ref, gs_ref,
                    wao_ref, wo_ref, g_ref, b_ref, wg_ref, wu_ref, wd_ref, g3_ref, b3_ref,
                    out_ref, o_stage, st_stage, ys_stage, x2_ref, *, sub, chunks):
    step = pl.program_id(0)
    slot_w = step % 2
    slot_r = 1 - slot_w

    @pl.when(step == 0)
    def _():
        x2_ref[...] = jnp.zeros(x2_ref.shape, F32)

    lane = lax.broadcasted_iota(jnp.int32, (sub, LANES), 1)
    lo_half = lane < HEAD_DIM

    def token_order(ref, stage, r0):
        dil = ref.shape[0]
        n = sub // dil
        i0 = r0 // dil
        if dil == 1:
            return ref[0, i0:i0 + n, :].astype(F32)
        n_col = ref.shape[2] // LANES
        pitch = _scatter_pitch(dil)
        for r in range(dil):
            blk = ref[r, i0:i0 + n, :].astype(F32)
            for c in range(n_col):
                stage[c, pl.ds(r, n, stride=pitch), :] = blk[:, c * LANES:(c + 1) * LANES]
        cols = [stage[c, 0:n * pitch, :] for c in range(n_col)]
        if pitch != dil:
            cols = [a.reshape(n, pitch, LANES)[:, :dil, :].reshape(sub, LANES) for a in cols]
        return jnp.concatenate(cols, axis=1)

    def head_cols(vals):
        cols = []
        for hp in range(GROUP_WIDTH // LANES):
            cols.append(jnp.where(lo_half, vals[2 * hp], vals[2 * hp + 1]))
        return jnp.concatenate(cols, axis=1)

    def merge_groups(k):
        r0 = k * sub
        sts = [token_order(st, st_stage.at[k, i], r0) for i, st in enumerate((st0, st1, st2))]
        outs = [token_order(o, o_stage.at[k, i], r0) for i, o in enumerate((o0, o1, o2))]
        dens = [pltpu.roll(st, LANES - HEADS_PER_GROUP, axis=1) for st in sts]
        mx = jnp.maximum(jnp.maximum(sts[0], sts[1]), sts[2])
        w = [den * jnp.exp2(st - mx) for den, st in zip(dens, sts)]
        tot = jnp.where(lane < HEADS_PER_GROUP, w[0] + w[1] + w[2], 1.0)
        att = None
        for g in range(N_ATTN_GROUPS):
            wt = w[g] / tot
            term = head_cols([wt[:, h:h + 1] for h in range(HEADS_PER_GROUP)]) * outs[g]
            att = term if att is None else att + term
        return att.astype(BF16)

    def gate(k, att):
        rows = slice(k * sub, (k + 1) * sub)
        y_attn = jnp.dot(att, wao_ref[...], preferred_element_type=F32)
        y_ssm = token_order(ys_ref, ys_stage.at[k], k * sub)
        merged = ga_ref[rows, :].astype(F32) * y_attn + gs_ref[rows, :].astype(F32) * y_ssm
        return merged.astype(BF16)

    def project(k, merged):
        rows = slice(k * sub, (k + 1) * sub)
        mix = jnp.dot(merged, wo_ref[...], preferred_element_type=F32)
        x2_ref[slot_w, rows, :] = _layer_norm(DEEPNORM_ALPHA * x_ref[rows, :] + mix, g_ref[...], b_ref[...])

    x2_prev = x2_ref[slot_r]
    ffn_state = {}

    def ffn_gate_up(k, c):
        rows = slice(k * sub, (k + 1) * sub)
        if c == chunks[0]:
            ffn_state[k] = {"xb": x2_prev[rows, :].astype(BF16), "acc": None}
        xb = ffn_state[k]["xb"]
        ffn_state[k]["gu"] = (jnp.dot(xb, wg_ref[:, c[0]:c[1]], preferred_element_type=F32),
                              jnp.dot(xb, wu_ref[:, c[0]:c[1]], preferred_element_type=F32))

    def ffn_down(k, c):
        rows = slice(k * sub, (k + 1) * sub)
        gate_v, up = ffn_state[k].pop("gu")
        h = (gate_v * _sigmoid(gate_v) * up).astype(BF16)
        part = jnp.dot(h, wd_ref[c[0]:c[1], :], preferred_element_type=F32)
        acc = ffn_state[k]["acc"]
        ffn_state[k]["acc"] = part if acc is None else acc + part
        if c == chunks[-1]:
            y = DEEPNORM_ALPHA * x2_prev[rows, :] + 0.5 * ffn_state.pop(k)["acc"]
            out_ref[rows, :] = _layer_norm(y, g3_ref[...], b3_ref[...])

    n_sub = x_ref.shape[0] // sub
    ffn_stages = [f for k in range(n_sub) for c in chunks
                  for f in (functools.partial(ffn_gate_up, k, c), functools.partial(ffn_down, k, c))]
    mix_stages = [f for k in range(n_sub) for f in (functools.partial(merge_groups, k),
                                                    functools.partial(gate, k),
                                                    functools.partial(project, k))]
    carry = ()
    for f in ffn_stages:
        f()
        if mix_stages:
            out = mix_stages.pop(0)(*carry)
            carry = () if out is None else (out,)
    assert not mix_stages


def _mix_ffn(x1, os_, sts, y_ssm, ga, gs, w_attn_out, w_o, g2, b2, wg, wu, wd, g3, b3, *,
             seq, tm=512, sub=256, chunk=1024):
    t, d = x1.shape
    d_ff = wg.shape[1]
    tiles = seq // tm
    n_tiles = t // tm
    cur = lambda i: jnp.minimum(i, n_tiles - 1)
    tile = lambda w: pl.BlockSpec((tm, w), lambda i: (cur(i), 0))

    def by_class(a):
        _, dil, n, w = a.shape
        return pl.BlockSpec((None, dil, tm // dil, w), lambda i: (cur(i) // tiles, 0, cur(i) % tiles, 0))

    n_grp = len(os_)
    stage_rows = max(sub // a.shape[1] * _scatter_pitch(a.shape[1]) for a in os_)
    per_big = y_ssm.shape[2] * SSM_CHUNK // tm
    ys_spec = pl.BlockSpec((None, SSM_CHUNK, tm // SSM_CHUNK, d),
                           lambda i: (cur(i) // per_big, 0, cur(i) % per_big, 0))
    ys_rows = sub // SSM_CHUNK * _scatter_pitch(SSM_CHUNK)
    vec = lambda v: v.reshape(1, d)
    return pl.pallas_call(
        functools.partial(_mix_ffn_kernel, sub=sub, chunks=_ffn_chunks(d_ff, chunk)),
        grid=(n_tiles + 1,),
        in_specs=[tile(d)] + [by_class(a) for a in os_] + [by_class(a) for a in sts] + [ys_spec, tile(d), tile(d),
                  _resident(w_attn_out.shape), _resident(w_o.shape), _resident((1, d)), _resident((1, d)),
                  _resident((d, d_ff)), _resident((d, d_ff)), _resident((d_ff, d)),
                  _resident((1, d)), _resident((1, d))],
        out_specs=pl.BlockSpec((tm, d), lambda i: (jnp.maximum(i - 1, 0), 0)),
        out_shape=jax.ShapeDtypeStruct((t, d), F32),
        scratch_shapes=[pltpu.VMEM((tm // sub, n_grp, GROUP_WIDTH // LANES, stage_rows, LANES), F32),
                        pltpu.VMEM((tm // sub, n_grp, 1, stage_rows, LANES), F32),
                        pltpu.VMEM((tm // sub, d // LANES, ys_rows, LANES), F32),
                        pltpu.VMEM((2, tm, d), F32)],
        compiler_params=_params("arbitrary"),
        name="mix_ffn",
    )(x1, *os_, *sts, y_ssm, ga, gs, w_attn_out.astype(BF16), w_o.astype(BF16), vec(g2), vec(b2),
      wg.astype(BF16), wu.astype(BF16), wd.astype(BF16), vec(g3), vec(b3))


def _layer(x, positions, w_in, w_attn_out, a_re, a_im, log_dt, b_re, b_im, c_re, c_im, d_skip,
           w_glu, b_glu, w_ssm_out, w_o, ffn1, ffn2, ln1, ln2, ln3):
    batch, seq, d = x.shape
    t = batch * seq
    n_grp = a_re.shape[0]
    ssm_width = n_grp * SSM_GROUP
    u_lo = 3 * ATTN_WIDTH

    x1, x1_t = _ffn_ln(x.reshape(t, d), *ffn1, *ln1)
    w_in_b = w_in.astype(BF16)
    *qkvs, ga, gs = _in_proj(x1, positions, w_in_b, batch=batch, seq=seq, ssm_width=ssm_width)

    os_, sts = [], []
    for qkv, (window, dilation) in zip(qkvs, ATTN_PATTERNS):
        assert window == ATTN_BLK * dilation
        o, st = _dilated_attention(qkv)
        os_.append(o)
        sts.append(st)

    ut = _ssm_in(x1_t, w_in_b, u_lo, ssm_width)
    prep = _ssm_prep(a_re, a_im, log_dt, b_re, b_im, c_re, c_im)
    dsk = jnp.tile(d_skip.reshape(n_grp, 1, SSM_GROUP), (1, SSM_CHUNK, 1)).reshape(n_grp, SSM_ROW, 1)
    yt = _ssm(ut, prep, dsk, n_chunk=seq // SSM_CHUNK)
    y_ssm = _ssm_out(yt, w_glu, b_glu, w_ssm_out)

    x3 = _mix_ffn(x1, os_, sts, y_ssm, ga, gs, w_attn_out, w_o, *ln2, *ffn2, *ln3, seq=seq)
    return x3.reshape(batch, seq, d)


def kernel(x, positions, w_in, w_attn_out, a_re, a_im, log_dt, b_re, b_im, c_re, c_im, d_skip, w_glu, b_glu, w_ssm_out, w_o, ffn1_wg, ffn1_wu, ffn1_wd, ffn2_wg, ffn2_wu, ffn2_wd, ln1_g, ln1_b, ln2_g, ln2_b, ln3_g, ln3_b):
    depth = w_in.shape[0]
    for i in range(depth):
        x = _layer(x, positions, w_in[i], w_attn_out[i], a_re[i], a_im[i], log_dt[i], b_re[i], b_im[i],
                   c_re[i], c_im[i], d_skip[i], w_glu[i], b_glu[i], w_ssm_out[i], w_o[i],
                   (ffn1_wg[i], ffn1_wu[i], ffn1_wd[i]), (ffn2_wg[i], ffn2_wu[i], ffn2_wd[i]),
                   (ln1_g[i], ln1_b[i]), (ln2_g[i], ln2_b[i]), (ln3_g[i], ln3_b[i]))
    return x
```

```python
import functools

import jax
import jax.numpy as jnp
from jax import lax
from jax.experimental import pallas as pl
from jax.experimental.pallas import tpu as pltpu

F32 = jnp.float32
BF16 = jnp.bfloat16

HEAD_DIM = 64
HEADS_PER_GROUP = 4
GROUP_WIDTH = HEADS_PER_GROUP * HEAD_DIM
ATTN_PATTERNS = ((128, 1), (512, 4), (2048, 16))
N_ATTN_GROUPS = len(ATTN_PATTERNS)
ATTN_WIDTH = N_ATTN_GROUPS * GROUP_WIDTH
ROT_DIM = HEAD_DIM // 4
ROPE_THETA = 500000.0
ROPE_ROWS = 32
ATTN_BLK = 128
NEG_INF = -1e30
LOG2_E = 1.4426950408889634
SSM_GROUP = 16
SSM_STATE = 64
SSM_CHUNK = 16
SSM_ROW = SSM_CHUNK * SSM_GROUP
SSM_POWERS = 16
SCAN_INTERLEAVE = 8
CHUNK_PITCH = 24


def _scatter_pitch(stride):
    return CHUNK_PITCH if stride == SSM_CHUNK else stride


def _rows_by_class(val, stage, dil):
    rows, w = val.shape
    n = rows // dil
    pitch = _scatter_pitch(dil)
    n_col = w // LANES
    for c in range(n_col):
        col = val[:, c * LANES:(c + 1) * LANES]
        if pitch != dil:
            col = jnp.concatenate([col.reshape(n, dil, LANES), jnp.zeros((n, pitch - dil, LANES), F32)], axis=1)
            col = col.reshape(n * pitch, LANES)
        stage[c, 0:n * pitch, :] = col

    def get(r):
        return jnp.concatenate([stage[c, pl.ds(r, n, stride=pitch), :] for c in range(n_col)], axis=1)

    return get
DEEPNORM_ALPHA = 2.0 ** 0.25
LN_EPS = 1e-5

LANES = 128
SUBLANES = 8
VMEM_LIMIT_BYTES = 56 * 1024 * 1024


def _params(*semantics):
    return pltpu.CompilerParams(dimension_semantics=semantics, vmem_limit_bytes=VMEM_LIMIT_BYTES)


def _resident(shape):
    zeros = (0,) * len(shape)
    return pl.BlockSpec(shape, lambda *_: zeros, pipeline_mode=pl.Buffered(1))


def _layer_norm(y, g, b):
    mu = jnp.mean(y, axis=-1, keepdims=True)
    yc = y - mu
    var = jnp.mean(yc * yc, axis=-1, keepdims=True)
    return yc * lax.rsqrt(var + LN_EPS) * g + b


def _sigmoid(x):
    return 1.0 / (1.0 + jnp.exp(-x))


def _gelu_tanh(x):
    c = 0.7978845608028654
    return 0.5 * x * (1.0 + jnp.tanh(c * (x + 0.044715 * (x * x * x))))


def _load_ffn_weights(wg_hbm, wu_hbm, wd_hbm, wg_ref, wu_ref, wd_ref, col_stage, row_stage, sem):
    blk = col_stage.shape[2]
    d_ff = wg_hbm.shape[1]
    jobs = []
    for c0 in range(0, d_ff, blk):
        cols = pl.ds(c0, blk)
        jobs.append((wg_hbm.at[:, cols], col_stage, wg_ref.at[:, cols]))
        jobs.append((wu_hbm.at[:, cols], col_stage, wu_ref.at[:, cols]))
        jobs.append((wd_hbm.at[cols, :], row_stage, wd_ref.at[cols, :]))

    def copy(n):
        src, stage, _ = jobs[n]
        return pltpu.make_async_copy(src, stage.at[n % 2], sem.at[n % 2])

    copy(0).start()
    for n, (_, stage, dst) in enumerate(jobs):
        if n + 1 < len(jobs):
            copy(n + 1).start()
        copy(n).wait()
        dst[...] = stage[n % 2].astype(BF16)


def _ffn_ln_kernel(x_ref, wg_hbm, wu_hbm, wd_hbm, g_ref, b_ref, o_ref,
                   wg_ref, wu_ref, wd_ref, col_stage, row_stage, sem, *, chunks, sub):
    @pl.when(pl.program_id(0) == 0)
    def _():
        _load_ffn_weights(wg_hbm, wu_hbm, wd_hbm, wg_ref, wu_ref, wd_ref, col_stage, row_stage, sem)

    for r0 in range(0, x_ref.shape[0], sub):
        x = x_ref[r0:r0 + sub, :]
        xb = x.astype(BF16)
        acc = None
        for c0, c1 in chunks:
            gate = jnp.dot(xb, wg_ref[:, c0:c1], preferred_element_type=F32)
            up = jnp.dot(xb, wu_ref[:, c0:c1], preferred_element_type=F32)
            h = (gate * _sigmoid(gate) * up).astype(BF16)
            part = jnp.dot(h, wd_ref[c0:c1, :], preferred_element_type=F32)
            acc = part if acc is None else acc + part
        y = DEEPNORM_ALPHA * x + 0.5 * acc
        o_ref[r0:r0 + sub, :] = _layer_norm(y, g_ref[...], b_ref[...])


def _ffn_chunks(d_ff, width):
    edges = list(range(0, d_ff, width)) + [d_ff]
    return tuple(zip(edges[:-1], edges[1:]))


def _ffn_weight_scratch(d, d_ff, blk=256):
    assert d_ff % blk == 0
    return [pltpu.VMEM((d, d_ff), BF16), pltpu.VMEM((d, d_ff), BF16), pltpu.VMEM((d_ff, d), BF16),
            pltpu.VMEM((2, d, blk), F32), pltpu.VMEM((2, blk, d), F32), pltpu.SemaphoreType.DMA((2,))]


def _ffn_ln(x, wg, wu, wd, g, b, *, tm=1024, sub=256, chunk=1024):
    t, d = x.shape
    d_ff = wg.shape[1]
    kern = functools.partial(_ffn_ln_kernel, chunks=_ffn_chunks(d_ff, chunk), sub=sub)
    hbm = pl.BlockSpec(memory_space=pltpu.HBM)
    return pl.pallas_call(
        kern,
        grid=(t // tm,),
        in_specs=[pl.BlockSpec((tm, d), lambda i: (i, 0)), hbm, hbm, hbm, _resident((1, d)), _resident((1, d))],
        out_specs=pl.BlockSpec((tm, d), lambda i: (i, 0)),
        out_shape=jax.ShapeDtypeStruct((t, d), F32),
        scratch_shapes=_ffn_weight_scratch(d, d_ff),
        compiler_params=_params("arbitrary"),
        name="ffn_ln",
    )(x, wg, wu, wd, g.reshape(1, d), b.reshape(1, d))


def _rope_spread():
    half = ROT_DIM // 2
    rows = lax.broadcasted_iota(jnp.int32, (ROPE_ROWS, 2 * LANES), 0)
    cols = lax.broadcasted_iota(jnp.int32, (ROPE_ROWS, 2 * LANES), 1)
    in_head = cols % HEAD_DIM
    freq = in_head % half
    is_cos = cols < LANES
    rot = in_head < ROT_DIM
    cos_part = is_cos & rot & (rows == freq)
    one_part = is_cos & jnp.logical_not(rot) & (rows == 2 * half)
    sin_part = jnp.logical_not(is_cos) & rot & (rows == half + freq)
    sign = jnp.where(in_head < half, 1.0, -1.0)
    return jnp.where(cos_part | one_part, 1.0, jnp.where(sin_part, sign, 0.0)).astype(BF16)


def _in_proj_kernel(x_ref, pos_ref, invf_ref, w_ref,
                    qkv0_ref, qkv1_ref, qkv2_ref, ga_ref, gs_ref, stage_ref, *, splits, sub):
    half = ROT_DIM // 2
    spread = _rope_spread()
    first = lax.broadcasted_iota(jnp.int32, (sub, LANES), 1) % HEAD_DIM < half
    s0, s1, s2, s3, s4, s5 = splits
    tn = (((0,), (0,)), ((), ()))

    for r0 in range(0, x_ref.shape[0], sub):
        xb = x_ref[r0:r0 + sub, :].astype(BF16)

        ang = invf_ref[...] * pos_ref[:, r0:r0 + sub].astype(F32)
        tab = jnp.concatenate([jnp.cos(ang), jnp.sin(ang), jnp.ones((half, sub), F32),
                               jnp.zeros((ROPE_ROWS - 3 * half, sub), F32)], axis=0)
        tab_hi = tab.astype(BF16)
        tab_lo = (tab - tab_hi.astype(F32)).astype(BF16)
        cs = (lax.dot_general(tab_hi, spread, tn, preferred_element_type=F32)
              + lax.dot_general(tab_lo, spread, tn, preferred_element_type=F32))
        cos = cs[:, :LANES]
        sin = cs[:, LANES:]

        def rotate(z):
            cols = []
            for c in range(z.shape[1] // LANES):
                zc = z[:, c * LANES:(c + 1) * LANES]
                zs = zc * sin
                up = pltpu.roll(jnp.where(first, zs, 0.0), half, axis=1)
                dn = pltpu.roll(jnp.where(first, 0.0, zs), LANES - half, axis=1)
                cols.append(zc * cos + up + dn)
            return jnp.concatenate(cols, axis=1)

        def proj(lo, hi):
            return jnp.dot(xb, w_ref[:, lo:hi], preferred_element_type=F32)

        q = rotate(proj(0, s0)) * (HEAD_DIM ** -0.5 * LOG2_E)
        k = rotate(proj(s0, s1))
        v = proj(s1, s2)
        for gi, out_ref in enumerate((qkv0_ref, qkv1_ref, qkv2_ref)):
            dil = ATTN_PATTERNS[gi][1]
            c0 = gi * GROUP_WIDTH
            qkv = jnp.concatenate([a[:, c0:c0 + GROUP_WIDTH] for a in (q, k, v)], axis=1)
            n = sub // dil
            i0 = r0 // dil
            if dil == 1:
                out_ref[0, i0:i0 + n, :] = qkv.astype(BF16)
            else:
                get = _rows_by_class(qkv, stage_ref.at[r0 // sub, gi - 1], dil)
                for r in range(dil):
                    out_ref[r, i0:i0 + n, :] = get(r).astype(BF16)
        ga_ref[r0:r0 + sub, :] = _sigmoid(proj(s3, s4)).astype(BF16)
        gs_ref[r0:r0 + sub, :] = _sigmoid(proj(s4, s5)).astype(BF16)


def _in_proj(x1, positions, w_in, *, batch, seq, ssm_width, tm=1024, sub=256):
    t, d = x1.shape
    splits = (ATTN_WIDTH, 2 * ATTN_WIDTH, 3 * ATTN_WIDTH, 3 * ATTN_WIDTH + ssm_width,
              3 * ATTN_WIDTH + ssm_width + d, 3 * ATTN_WIDTH + ssm_width + 2 * d)
    assert splits[-1] == w_in.shape[1] and seq % tm == 0
    half = ROT_DIM // 2
    invf = (ROPE_THETA ** (-jnp.arange(half, dtype=F32) * 2.0 / ROT_DIM)).reshape(half, 1)
    tiles = seq // tm
    qkv_w = 3 * GROUP_WIDTH
    qkv_specs = [pl.BlockSpec((None, dil, tm // dil, qkv_w), lambda i: (i // tiles, 0, i % tiles, 0))
                 for _, dil in ATTN_PATTERNS]
    qkv_shapes = [jax.ShapeDtypeStruct((batch, dil, seq // dil, qkv_w), BF16) for _, dil in ATTN_PATTERNS]
    widths = (d, d)
    kern = functools.partial(_in_proj_kernel, splits=splits, sub=sub)
    n_strided = sum(dil > 1 for _, dil in ATTN_PATTERNS)
    stage_rows = max(sub // dil * _scatter_pitch(dil) for _, dil in ATTN_PATTERNS)
    return pl.pallas_call(
        kern,
        grid=(t // tm,),
        in_specs=[
            pl.BlockSpec((tm, d), lambda i: (i, 0)),
            pl.BlockSpec((1, tm), lambda i: (0, i)),
            _resident((half, 1)),
            _resident(w_in.shape),
        ],
        out_specs=qkv_specs + [pl.BlockSpec((tm, w), lambda i: (i, 0)) for w in widths],
        out_shape=qkv_shapes + [jax.ShapeDtypeStruct((t, w), BF16) for w in widths],
        scratch_shapes=[pltpu.VMEM((tm // sub, n_strided, qkv_w // LANES, stage_rows, LANES), F32)],
        compiler_params=_params("parallel"),
        name="in_proj",
    )(x1, positions.reshape(1, t), invf, w_in)


def _attn_kernel(q_ref, kc_ref, kp_ref, vc_ref, vp_ref, o_ref, st_ref, *, n_sub):
    first = pl.program_id(2) == 0
    row = lax.broadcasted_iota(jnp.int32, (ATTN_BLK, 2 * ATTN_BLK), 0)
    col = lax.broadcasted_iota(jnp.int32, (ATTN_BLK, 2 * ATTN_BLK), 1)
    band = (col >= row) & (col <= row + ATTN_BLK)
    lane = lax.broadcasted_iota(jnp.int32, (ATTN_BLK, LANES), 1)
    lo_half = lane < HEAD_DIM

    n_pair = GROUP_WIDTH // LANES
    tiles = [(rc, j, hp, hh) for rc in range(q_ref.shape[0]) for j in range(n_sub)
             for hp in range(n_pair) for hh in range(2)]

    def window(ref_prev, ref_cur, j, c0):
        r0 = j * ATTN_BLK
        if j == 0:
            return jnp.concatenate([ref_prev[:, c0:c0 + LANES], ref_cur[0:ATTN_BLK, c0:c0 + LANES]], axis=0)
        return ref_cur[r0 - ATTN_BLK:r0 + ATTN_BLK, c0:c0 + LANES]

    def scores(tile):
        rc, j, hp, hh = tile
        q2 = q_ref[rc, j * ATTN_BLK:(j + 1) * ATTN_BLK, hp * LANES:(hp + 1) * LANES]
        qm = jnp.where(lo_half if hh == 0 else jnp.logical_not(lo_half), q2, jnp.zeros_like(q2))
        return lax.dot_general(qm, window(kp_ref.at[rc], kc_ref.at[rc], j, hp * LANES), (((1,), (1,)), ((), ())),
                               preferred_element_type=F32)

    def softmax(tile, s):
        valid = band & ((col >= ATTN_BLK) | jnp.logical_not(first)) if tile[1] == 0 else band
        s = jnp.where(valid, s, NEG_INF)
        m = jnp.max(s, axis=1, keepdims=True)
        p = jnp.exp2(s - m)
        return p.astype(BF16), m, jnp.sum(p, axis=1, keepdims=True)

    def weighted_values(tile, p, den):
        rc, j, hp, _ = tile
        return jnp.dot(p, window(vp_ref.at[rc], vc_ref.at[rc], j, hp * LANES), preferred_element_type=F32) / den

    s_q, p_q, outs = {}, {}, {}
    for step in range(len(tiles) + 2):
        if step < len(tiles):
            s_q[step] = scores(tiles[step])
        if 0 <= step - 1 < len(tiles):
            p_q[step - 1] = softmax(tiles[step - 1], s_q.pop(step - 1))
        if 0 <= step - 2 < len(tiles):
            t = step - 2
            rc, j, hp, hh = tiles[t]
            p, m, den = p_q.pop(t)
            outs[hh] = weighted_values(tiles[t], p, den)
            h = 2 * hp + hh
            rows = slice(j * ATTN_BLK, (j + 1) * ATTN_BLK)
            if h == 0:
                st_ref[rc, rows, :] = jnp.zeros((ATTN_BLK, LANES), F32)
            st_ref[rc, rows, h:h + 1] = m
            st_ref[rc, rows, HEADS_PER_GROUP + h:HEADS_PER_GROUP + h + 1] = den
            if hh == 1:
                o_ref[rc, rows, hp * LANES:(hp + 1) * LANES] = (
                    jnp.where(lo_half, outs[0], outs[1]).astype(BF16))


def _dilated_attention(qkv):
    batch, dilation, n, _ = qkv.shape
    rows_per_step = 1024
    qb = min(rows_per_step, n)
    n_sub = qb // ATTN_BLK
    rb = min(max(rows_per_step // qb, 1), dilation)

    def cur(part):
        return pl.BlockSpec((None, rb, qb, GROUP_WIDTH), lambda b, r, i: (b, r, i, part))

    def prev(part):
        return pl.BlockSpec((None, rb, ATTN_BLK, GROUP_WIDTH),
                            lambda b, r, i: (b, r, jnp.maximum(i * n_sub - 1, 0), part))

    return pl.pallas_call(
        functools.partial(_attn_kernel, n_sub=n_sub),
        grid=(batch, dilation // rb, n // qb),
        in_specs=[cur(0), cur(1), prev(1), cur(2), prev(2)],
        out_specs=[pl.BlockSpec((None, rb, qb, GROUP_WIDTH), lambda b, r, i: (b, r, i, 0)),
                   pl.BlockSpec((None, rb, qb, LANES), lambda b, r, i: (b, r, i, 0))],
        out_shape=[jax.ShapeDtypeStruct((batch, dilation, n, GROUP_WIDTH), BF16),
                   jax.ShapeDtypeStruct((batch, dilation, n, LANES), F32)],
        compiler_params=_params("parallel", "parallel", "parallel"),
        name=f"attn_d{dilation}",
    )(qkv, qkv, qkv, qkv, qkv)


def _ssm_in_kernel(*refs, n_grp, ch, n_w):
    x_refs, w_refs, ut_ref = refs[:-n_w - 1], refs[-n_w - 1:-1], refs[-1]
    w = jnp.concatenate([w_ref[...] for w_ref in w_refs], axis=1)
    for s in range(0, SSM_CHUNK, 2):
        xs = jnp.concatenate(
            [jnp.concatenate([x_ref[pl.ds(s + i, ch, stride=SSM_CHUNK), :] for x_ref in x_refs], axis=1)
             for i in range(2)], axis=0).astype(BF16)
        ut = lax.dot_general(w, xs, (((0,), (1,)), ((), ())), preferred_element_type=F32)
        for i in range(2):
            ut_ref[:, (s + i) * SSM_GROUP:(s + i + 1) * SSM_GROUP, :] = (
                ut[:, i * ch:(i + 1) * ch].reshape(n_grp, SSM_GROUP, ch).astype(BF16))


def _ssm_in(x1, w_in_b, u_lo, width, *, ch=LANES, wblk=256):
    t, d = x1.shape
    assert u_lo % wblk == 0 and width % wblk == 0
    n_grp = width // SSM_GROUP
    n_chunk_all = t // SSM_CHUNK
    tm = ch * SSM_CHUNK
    n_w = width // wblk
    slabs = [pl.BlockSpec((tm, LANES), lambda i, c=c: (i, c)) for c in range(d // LANES)]
    w_cols = [pl.BlockSpec((d, wblk), lambda i, c=c: (0, u_lo // wblk + c), pipeline_mode=pl.Buffered(1))
              for c in range(n_w)]
    return pl.pallas_call(
        functools.partial(_ssm_in_kernel, n_grp=n_grp, ch=ch, n_w=n_w),
        grid=(n_chunk_all // ch,),
        in_specs=slabs + w_cols,
        out_specs=pl.BlockSpec((n_grp, SSM_ROW, ch), lambda i: (0, 0, i)),
        out_shape=jax.ShapeDtypeStruct((n_grp, SSM_ROW, n_chunk_all), BF16),
        compiler_params=_params("parallel"),
        name="ssm_in",
    )(*([x1] * (d // LANES)), *([w_in_b] * n_w))


def _hdot(a, b):
    return jnp.dot(a, b, preferred_element_type=F32, precision=lax.Precision.HIGHEST)


def _cmul(ar, ai, br, bi):
    return ar * br - ai * bi, ar * bi + ai * br


def _cpowers(lbr, lbi, n, bits):
    out_r = jnp.ones(n.shape, F32)
    out_i = jnp.zeros(n.shape, F32)
    for b in range(bits):
        sel = (n & (1 << b)) != 0
        out_r, out_i = _cmul(out_r, out_i, jnp.where(sel, lbr, 1.0), jnp.where(sel, lbi, 0.0))
        lbr, lbi = _cmul(lbr, lbi, lbr, lbi)
    return out_r, out_i


def _ssm_disc_kernel(are_ref, aim_ref, ldt_ref, lbr_ref, lbi_ref, cfr_ref, cfi_ref):
    dt = jnp.exp(ldt_ref[...])
    lam_r = jnp.minimum(are_ref[...], -1e-4)
    lam_i = aim_ref[...]
    mag = jnp.exp(lam_r * dt)
    lbr = mag * jnp.cos(lam_i * dt)
    lbi = mag * jnp.sin(lam_i * dt)
    inv = 1.0 / (lam_r * lam_r + lam_i * lam_i)
    lbr_ref[...] = lbr
    lbi_ref[...] = lbi
    cfr_ref[...] = ((lbr - 1.0) * lam_r + lbi * lam_i) * inv
    cfi_ref[...] = (lbi * lam_r - (lbr - 1.0) * lam_i) * inv


def _ssm_prep_kernel(lbr_s, lbi_s, cfr_s, cfi_s, are_l, aim_l, ldt_ref, bre_ref, bim_ref, cre_ref, cim_ref,
                     tt_ref, win_ref, wout_ref, apr_ref, api_ref, *, gb):
    p = lbr_s.shape[1]
    expand = (lax.broadcasted_iota(jnp.int32, (SSM_CHUNK, SSM_ROW), 1) // SSM_GROUP
              == lax.broadcasted_iota(jnp.int32, (SSM_CHUNK, SSM_ROW), 0)).astype(F32)
    expand_t = (lax.broadcasted_iota(jnp.int32, (SSM_ROW, SSM_CHUNK), 0) // SSM_GROUP
                == lax.broadcasted_iota(jnp.int32, (SSM_ROW, SSM_CHUNK), 1)).astype(F32)
    rem = SSM_CHUNK - 1 - lax.broadcasted_iota(jnp.int32, (p, SSM_CHUNK), 1)
    tp1 = lax.broadcasted_iota(jnp.int32, (SSM_CHUNK, p), 0) + 1
    col = lax.broadcasted_iota(jnp.int32, (p, SSM_POWERS), 1)
    lane = lax.broadcasted_iota(jnp.int32, (SSM_GROUP, SSM_ROW), 1)
    chunk_bits = SSM_CHUNK.bit_length()

    for g in range(gb):
        lbr = lbr_s[g]
        lbi = lbi_s[g]
        bre = bre_ref[g]
        bim = bim_ref[g]
        bbar_r, bbar_i = _cmul(cfr_s[g], cfi_s[g], bre, bim)
        pw_r, pw_i = _cpowers(lbr, lbi, rem, chunk_bits - 1)
        win_r, win_i = _cmul(_hdot(pw_r, expand), _hdot(pw_i, expand), bbar_r, bbar_i)
        win_ref[g] = jnp.concatenate([win_r, win_i], axis=0).astype(BF16)

        a_r, a_i = _cmul(pw_r[:, 0:1], pw_i[:, 0:1], lbr, lbi)
        apr = jnp.zeros((p, SSM_POWERS), F32)
        api = jnp.zeros((p, SSM_POWERS), F32)
        for j in range(SSM_POWERS):
            apr = jnp.where(col == j, a_r, apr)
            api = jnp.where(col == j, a_i, api)
            a_r, a_i = _cmul(a_r, a_i, a_r, a_i)
        apr_ref[g] = apr
        api_ref[g] = api

        dt = jnp.exp(ldt_ref[g])
        mag_l = jnp.exp(jnp.minimum(are_l[g], -1e-4) * dt)
        e_r, e_i = _cpowers(mag_l * jnp.cos(aim_l[g] * dt), mag_l * jnp.sin(aim_l[g] * dt), tp1, chunk_bits)
        cre = cre_ref[g]
        cim = cim_ref[g]
        wo_r, wo_i = _cmul(_hdot(expand_t, e_r), _hdot(expand_t, e_i), cre, cim)
        wout_ref[g] = jnp.concatenate([wo_r, -wo_i], axis=1).astype(BF16)

        hrev = _hdot(cre[0:SSM_GROUP, :], win_r) - _hdot(cim[0:SSM_GROUP, :], win_i)
        blocks = []
        for t in range(SSM_CHUNK):
            hi = (t + 1) * SSM_GROUP
            rolled = hrev if hi == SSM_ROW else pltpu.roll(hrev, hi, axis=1)
            blocks.append(jnp.where(lane < hi, rolled, 0.0))
        tt_ref[g] = jnp.concatenate(blocks, axis=0).astype(BF16)


def _ssm_prep(a_re, a_im, log_dt, b_re, b_im, c_re, c_im, *, gb=4):
    g, p = a_re.shape
    row = SSM_ROW
    whole = lambda shape: pl.BlockSpec(shape, lambda: (0,) * len(shape))
    disc = pl.pallas_call(
        _ssm_disc_kernel,
        in_specs=[whole((p, g)), whole((p, g)), whole((1, g))],
        out_specs=[whole((p, g))] * 4,
        out_shape=[jax.ShapeDtypeStruct((p, g), F32)] * 4,
        name="ssm_disc",
    )(a_re.T, a_im.T, log_dt.reshape(1, g))
    lbr_s, lbi_s, cfr_s, cfi_s = [a.T.reshape(g, p, 1) for a in disc]

    blk = lambda shape: pl.BlockSpec((gb,) + shape, lambda i: (i, 0, 0))
    b_t = lambda b: jnp.tile(b, (1, 1, SSM_CHUNK))
    c_t = lambda c: jnp.tile(c, (1, SSM_CHUNK, 1))
    out_shapes = [
        jax.ShapeDtypeStruct((g, row, row), BF16),
        jax.ShapeDtypeStruct((g, 2 * p, row), BF16),
        jax.ShapeDtypeStruct((g, row, 2 * p), BF16),
        jax.ShapeDtypeStruct((g, p, SSM_POWERS), F32), jax.ShapeDtypeStruct((g, p, SSM_POWERS), F32),
    ]
    return pl.pallas_call(
        functools.partial(_ssm_prep_kernel, gb=gb),
        grid=(g // gb,),
        in_specs=[blk((p, 1))] * 4 + [blk((1, p)), blk((1, p)), blk((1, 1)),
                  blk((p, row)), blk((p, row)), blk((row, p)), blk((row, p))],
        out_specs=[blk(s.shape[1:]) for s in out_shapes],
        out_shape=out_shapes,
        compiler_params=_params("parallel"),
        name="ssm_prep",
    )(lbr_s, lbi_s, cfr_s, cfi_s, a_re.reshape(g, 1, p), a_im.reshape(g, 1, p),
      log_dt.reshape(g, 1, 1), b_t(b_re), b_t(b_im), c_t(c_re), c_t(c_im))


def _ssm_kernel(ut_ref, tt_ref, win_ref, wout_ref, apr_ref, api_ref, dsk_ref, y_ref, *, gb, n_chunk):
    nk = ut_ref.shape[-1]
    p = apr_ref.shape[1]
    pos = lax.broadcasted_iota(jnp.int32, (SUBLANES, nk), 1) % n_chunk
    n_steps = (n_chunk - 1).bit_length()
    assert n_steps <= SSM_POWERS

    def shift_whole_tiles(v, sh):
        segs = []
        for b0 in range(0, nk, n_chunk):
            segs += [jnp.zeros((v.shape[0], sh), F32), v[:, b0:b0 + n_chunk - sh]]
        return jnp.concatenate(segs, axis=1)

    for g in range(gb):
        ut = ut_ref[g]
        sl = jnp.dot(win_ref[g], ut, preferred_element_type=F32)
        apr = apr_ref[g]
        api = api_ref[g]
        prev_r, prev_i = [], []
        for b0 in range(0, p, SUBLANES * SCAN_INTERLEAVE):
            tiles = range(b0, b0 + SUBLANES * SCAN_INTERLEAVE, SUBLANES)
            s = {r0: (sl[r0:r0 + SUBLANES], sl[p + r0:p + r0 + SUBLANES]) for r0 in tiles}
            for j in range(n_steps):
                sh = 1 << j
                keep = pos >= sh
                for r0 in tiles:
                    s_r, s_i = s[r0]
                    a_r = apr[r0:r0 + SUBLANES, j:j + 1]
                    a_i = api[r0:r0 + SUBLANES, j:j + 1]
                    if sh % LANES == 0:
                        p_r, p_i = shift_whole_tiles(s_r, sh), shift_whole_tiles(s_i, sh)
                    else:
                        p_r = jnp.where(keep, pltpu.roll(s_r, sh, axis=1), 0.0)
                        p_i = jnp.where(keep, pltpu.roll(s_i, sh, axis=1), 0.0)
                    s[r0] = (s_r + a_r * p_r - a_i * p_i, s_i + a_r * p_i + a_i * p_r)
            keep = pos >= 1
            for r0 in tiles:
                prev_r.append(jnp.where(keep, pltpu.roll(s[r0][0], 1, axis=1), 0.0))
                prev_i.append(jnp.where(keep, pltpu.roll(s[r0][1], 1, axis=1), 0.0))
        prev = jnp.concatenate(prev_r + prev_i, axis=0).astype(BF16)
        y = jnp.dot(tt_ref[g], ut, preferred_element_type=F32)
        y += jnp.dot(wout_ref[g], prev, preferred_element_type=F32)
        y += dsk_ref[g] * ut.astype(F32)
        y_ref[g] = _gelu_tanh(y).astype(BF16)


def _ssm(ut, prep, dsk, *, n_chunk, gb=4):
    tt, win, wout, apr, api = prep
    g, row, nk = ut.shape
    p = apr.shape[1]
    blk = lambda shape: pl.BlockSpec((gb,) + shape, lambda i: (i, 0, 0))
    kern = functools.partial(_ssm_kernel, gb=gb, n_chunk=n_chunk)
    return pl.pallas_call(
        kern,
        grid=(g // gb,),
        in_specs=[blk((row, nk)), blk((row, row)), blk((2 * p, row)), blk((row, 2 * p)),
                  blk((p, SSM_POWERS)), blk((p, SSM_POWERS)), blk((row, 1))],
        out_specs=blk((row, nk)),
        out_shape=jax.ShapeDtypeStruct((g, row, nk), BF16),
        compiler_params=_params("parallel"),
        name="ssm",
    )(ut, tt, win, wout, apr, api, dsk)


def _ssm_out_kernel(yt_ref, wglu_ref, bglu_ref, wso_ref, o_ref, *, t_per_pass):
    n_grp, _, ch = yt_ref.shape
    width = n_grp * SSM_GROUP
    for t0 in range(0, SSM_CHUNK, t_per_pass):
        y = jnp.concatenate(
            [yt_ref[:, t * SSM_GROUP:(t + 1) * SSM_GROUP, :].reshape(width, ch)
             for t in range(t0, t0 + t_per_pass)], axis=1)
        gate = jnp.dot(wglu_ref[...], y, preferred_element_type=F32) + bglu_ref[...]
        glu = (y.astype(F32) * _sigmoid(gate)).astype(BF16)
        yo = lax.dot_general(glu, wso_ref[...], (((0,), (0,)), ((), ())), preferred_element_type=F32)
        for i in range(t_per_pass):
            o_ref[t0 + i] = yo[i * ch:(i + 1) * ch, :].astype(BF16)


def _ssm_out(yt, w_glu, b_glu, w_ssm_out, *, ch=LANES, t_per_pass=4):
    n_grp, row, n_chunk_all = yt.shape
    width, d = w_ssm_out.shape
    return pl.pallas_call(
        functools.partial(_ssm_out_kernel, t_per_pass=t_per_pass),
        grid=(n_chunk_all // ch,),
        in_specs=[pl.BlockSpec((n_grp, row, ch), lambda i: (0, 0, i)),
                  _resident((width, width)), _resident((width, 1)), _resident((width, d))],
        out_specs=pl.BlockSpec((None, SSM_CHUNK, ch, d), lambda i: (i, 0, 0, 0)),
        out_shape=jax.ShapeDtypeStruct((n_chunk_all // ch, SSM_CHUNK, ch, d), BF16),
        compiler_params=_params("parallel"),
        name="ssm_out",
    )(yt, w_glu.T.astype(BF16), b_glu.reshape(width, 1), w_ssm_out.astype(BF16))


def _mix_ffn_kernel(x_ref, o0, o1, o2, st0, st1, st2, ys_ref, ga_ref, gs_ref,
                    wao_ref, wo_ref, g_ref, b_ref, wg_hbm, wu_hbm, wd_hbm, g3_ref, b3_ref,
                    out_ref, o_stage, st_stage, ys_stage, x2_ref,
                    wg_ref, wu_ref, wd_ref, col_stage, row_stage, sem, *, sub, chunks):
    step = pl.program_id(0)
    slot_w = step % 2
    slot_r = 1 - slot_w

    @pl.when(step == 0)
    def _():
        x2_ref[...] = jnp.zeros(x2_ref.shape, F32)
        _load_ffn_weights(wg_hbm, wu_hbm, wd_hbm, wg_ref, wu_ref, wd_ref, col_stage, row_stage, sem)

    lane = lax.broadcasted_iota(jnp.int32, (sub, LANES), 1)
    lo_half = lane < HEAD_DIM

    def token_order(ref, stage, r0):
        dil = ref.shape[0]
        n = sub // dil
        i0 = r0 // dil
        if dil == 1:
            return ref[0, i0:i0 + n, :].astype(F32)
        n_col = ref.shape[2] // LANES
        pitch = _scatter_pitch(dil)
        for r in range(dil):
            blk = ref[r, i0:i0 + n, :].astype(F32)
            for c in range(n_col):
                stage[c, pl.ds(r, n, stride=pitch), :] = blk[:, c * LANES:(c + 1) * LANES]
        cols = [stage[c, 0:n * pitch, :] for c in range(n_col)]
        if pitch != dil:
            cols = [a.reshape(n, pitch, LANES)[:, :dil, :].reshape(sub, LANES) for a in cols]
        return jnp.concatenate(cols, axis=1)

    def head_cols(vals):
        cols = []
        for hp in range(GROUP_WIDTH // LANES):
            cols.append(jnp.where(lo_half, vals[2 * hp], vals[2 * hp + 1]))
        return jnp.concatenate(cols, axis=1)

    def merge_groups(k):
        r0 = k * sub
        sts = [token_order(st, st_stage.at[k, i], r0) for i, st in enumerate((st0, st1, st2))]
        outs = [token_order(o, o_stage.at[k, i], r0) for i, o in enumerate((o0, o1, o2))]
        dens = [pltpu.roll(st, LANES - HEADS_PER_GROUP, axis=1) for st in sts]
        mx = jnp.maximum(jnp.maximum(sts[0], sts[1]), sts[2])
        w = [den * jnp.exp2(st - mx) for den, st in zip(dens, sts)]
        tot = jnp.where(lane < HEADS_PER_GROUP, w[0] + w[1] + w[2], 1.0)
        att = None
        for g in range(N_ATTN_GROUPS):
            wt = w[g] / tot
            term = head_cols([wt[:, h:h + 1] for h in range(HEADS_PER_GROUP)]) * outs[g]
            att = term if att is None else att + term
        return att.astype(BF16)

    def gate(k, att):
        rows = slice(k * sub, (k + 1) * sub)
        y_attn = jnp.dot(att, wao_ref[...], preferred_element_type=F32)
        y_ssm = token_order(ys_ref, ys_stage.at[k], k * sub)
        merged = ga_ref[rows, :].astype(F32) * y_attn + gs_ref[rows, :].astype(F32) * y_ssm
        return merged.astype(BF16)

    def project(k, merged):
        rows = slice(k * sub, (k + 1) * sub)
        mix = jnp.dot(merged, wo_ref[...], preferred_element_type=F32)
        x2_ref[slot_w, rows, :] = _layer_norm(DEEPNORM_ALPHA * x_ref[rows, :] + mix, g_ref[...], b_ref[...])

    x2_prev = x2_ref[slot_r]
    ffn_state = {}

    def ffn_gate_up(k, c):
        rows = slice(k * sub, (k + 1) * sub)
        if c == chunks[0]:
            ffn_state[k] = {"xb": x2_prev[rows, :].astype(BF16), "acc": None}
        xb = ffn_state[k]["xb"]
        ffn_state[k]["gu"] = (jnp.dot(xb, wg_ref[:, c[0]:c[1]], preferred_element_type=F32),
                              jnp.dot(xb, wu_ref[:, c[0]:c[1]], preferred_element_type=F32))

    def ffn_down(k, c):
        rows = slice(k * sub, (k + 1) * sub)
        gate_v, up = ffn_state[k].pop("gu")
        h = (gate_v * _sigmoid(gate_v) * up).astype(BF16)
        part = jnp.dot(h, wd_ref[c[0]:c[1], :], preferred_element_type=F32)
        acc = ffn_state[k]["acc"]
        ffn_state[k]["acc"] = part if acc is None else acc + part
        if c == chunks[-1]:
            y = DEEPNORM_ALPHA * x2_prev[rows, :] + 0.5 * ffn_state.pop(k)["acc"]
            out_ref[rows, :] = _layer_norm(y, g3_ref[...], b3_ref[...])

    n_sub = x_ref.shape[0] // sub
    ffn_stages = [f for k in range(n_sub) for c in chunks
                  for f in (functools.partial(ffn_gate_up, k, c), functools.partial(ffn_down, k, c))]
    mix_stages = [f for k in range(n_sub) for f in (functools.partial(merge_groups, k),
                                                    functools.partial(gate, k),
                                                    functools.partial(project, k))]
    carry = ()
    for f in ffn_stages:
        f()
        if mix_stages:
            out = mix_stages.pop(0)(*carry)
            carry = () if out is None else (out,)
    assert not mix_stages


def _mix_ffn(x1, os_, sts, y_ssm, ga, gs, w_attn_out, w_o, g2, b2, wg, wu, wd, g3, b3, *,
             seq, tm=512, sub=256, chunk=1024):
    t, d = x1.shape
    d_ff = wg.shape[1]
    tiles = seq // tm
    n_tiles = t // tm
    cur = lambda i: jnp.minimum(i, n_tiles - 1)
    tile = lambda w: pl.BlockSpec((tm, w), lambda i: (cur(i), 0))

    def by_class(a):
        _, dil, n, w = a.shape
        return pl.BlockSpec((None, dil, tm // dil, w), lambda i: (cur(i) // tiles, 0, cur(i) % tiles, 0))

    n_grp = len(os_)
    stage_rows = max(sub // a.shape[1] * _scatter_pitch(a.shape[1]) for a in os_)
    per_big = y_ssm.shape[2] * SSM_CHUNK // tm
    ys_spec = pl.BlockSpec((None, SSM_CHUNK, tm // SSM_CHUNK, d),
                           lambda i: (cur(i) // per_big, 0, cur(i) % per_big, 0))
    ys_rows = sub // SSM_CHUNK * _scatter_pitch(SSM_CHUNK)
    vec = lambda v: v.reshape(1, d)
    hbm = pl.BlockSpec(memory_space=pltpu.HBM)
    return pl.pallas_call(
        functools.partial(_mix_ffn_kernel, sub=sub, chunks=_ffn_chunks(d_ff, chunk)),
        grid=(n_tiles + 1,),
        in_specs=[tile(d)] + [by_class(a) for a in os_] + [by_class(a) for a in sts] + [ys_spec, tile(d), tile(d),
                  _resident(w_attn_out.shape), _resident(w_o.shape), _resident((1, d)), _resident((1, d)),
                  hbm, hbm, hbm, _resident((1, d)), _resident((1, d))],
        out_specs=pl.BlockSpec((tm, d), lambda i: (jnp.maximum(i - 1, 0), 0)),
        out_shape=jax.ShapeDtypeStruct((t, d), F32),
        scratch_shapes=[pltpu.VMEM((tm // sub, n_grp, GROUP_WIDTH // LANES, stage_rows, LANES), F32),
                        pltpu.VMEM((tm // sub, n_grp, 1, stage_rows, LANES), F32),
                        pltpu.VMEM((tm // sub, d // LANES, ys_rows, LANES), F32),
                        pltpu.VMEM((2, tm, d), F32)] + _ffn_weight_scratch(d, d_ff),
        compiler_params=_params("arbitrary"),
        name="mix_ffn",
    )(x1, *os_, *sts, y_ssm, ga, gs, w_attn_out.astype(BF16), w_o.astype(BF16), vec(g2), vec(b2),
      wg, wu, wd, vec(g3), vec(b3))


def _layer(x, positions, w_in, w_attn_out, a_re, a_im, log_dt, b_re, b_im, c_re, c_im, d_skip,
           w_glu, b_glu, w_ssm_out, w_o, ffn1, ffn2, ln1, ln2, ln3):
    batch, seq, d = x.shape
    t = batch * seq
    n_grp = a_re.shape[0]
    ssm_width = n_grp * SSM_GROUP
    u_lo = 3 * ATTN_WIDTH

    x1 = _ffn_ln(x.reshape(t, d), *ffn1, *ln1)
    w_in_b = w_in.astype(BF16)
    *qkvs, ga, gs = _in_proj(x1, positions, w_in_b, batch=batch, seq=seq, ssm_width=ssm_width)

    os_, sts = [], []
    for qkv, (window, dilation) in zip(qkvs, ATTN_PATTERNS):
        assert window == ATTN_BLK * dilation
        o, st = _dilated_attention(qkv)
        os_.append(o)
        sts.append(st)

    ut = _ssm_in(x1, w_in_b, u_lo, ssm_width)
    prep = _ssm_prep(a_re, a_im, log_dt, b_re, b_im, c_re, c_im)
    dsk = jnp.tile(d_skip.reshape(n_grp, 1, SSM_GROUP), (1, SSM_CHUNK, 1)).reshape(n_grp, SSM_ROW, 1)
    yt = _ssm(ut, prep, dsk, n_chunk=seq // SSM_CHUNK)
    y_ssm = _ssm_out(yt, w_glu, b_glu, w_ssm_out)

    x3 = _mix_ffn(x1, os_, sts, y_ssm, ga, gs, w_attn_out, w_o, *ln2, *ffn2, *ln3, seq=seq)
    return x3.reshape(batch, seq, d)


def kernel(x, positions, w_in, w_attn_out, a_re, a_im, log_dt, b_re, b_im, c_re, c_im, d_skip, w_glu, b_glu, w_ssm_out, w_o, ffn1_wg, ffn1_wu, ffn1_wd, ffn2_wg, ffn2_wu, ffn2_wd, ln1_g, ln1_b, ln2_g, ln2_b, ln3_g, ln3_b):
    depth = w_in.shape[0]
    for i in range(depth):
        x = _layer(x, positions, w_in[i], w_attn_out[i], a_re[i], a_im[i], log_dt[i], b_re[i], b_im[i],
                   c_re[i], c_im[i], d_skip[i], w_glu[i], b_glu[i], w_ssm_out[i], w_o[i],
                   (ffn1_wg[i], ffn1_wu[i], ffn1_wd[i]), (ffn2_wg[i], ffn2_wu[i], ffn2_wd[i]),
                   (ln1_g[i], ln1_b[i]), (ln2_g[i], ln2_b[i]), (ln3_g[i], ln3_b[i]))
    return x
```

```python
import functools

import jax
import jax.numpy as jnp
from jax import lax
from jax.experimental import pallas as pl
from jax.experimental.pallas import tpu as pltpu

F32 = jnp.float32
BF16 = jnp.bfloat16

HEAD_DIM = 64
HEADS_PER_GROUP = 4
GROUP_WIDTH = HEADS_PER_GROUP * HEAD_DIM
ATTN_PATTERNS = ((128, 1), (512, 4), (2048, 16))
N_ATTN_GROUPS = len(ATTN_PATTERNS)
ATTN_WIDTH = N_ATTN_GROUPS * GROUP_WIDTH
ROT_DIM = HEAD_DIM // 4
ROPE_THETA = 500000.0
ROPE_ROWS = 32
ATTN_BLK = 128
NEG_INF = -1e30
LOG2_E = 1.4426950408889634
SSM_GROUP = 16
SSM_STATE = 64
SSM_CHUNK = 16
SSM_ROW = SSM_CHUNK * SSM_GROUP
SSM_POWERS = 16
SCAN_INTERLEAVE = 8
CHUNK_PITCH = 24


def _scatter_pitch(stride):
    return CHUNK_PITCH if stride == SSM_CHUNK else stride


def _rows_by_class(val, stage, dil):
    rows, w = val.shape
    n = rows // dil
    pitch = _scatter_pitch(dil)
    n_col = w // LANES
    for c in range(n_col):
        col = val[:, c * LANES:(c + 1) * LANES]
        if pitch != dil:
            col = jnp.concatenate([col.reshape(n, dil, LANES), jnp.zeros((n, pitch - dil, LANES), F32)], axis=1)
            col = col.reshape(n * pitch, LANES)
        stage[c, 0:n * pitch, :] = col

    def get(r):
        return jnp.concatenate([stage[c, pl.ds(r, n, stride=pitch), :] for c in range(n_col)], axis=1)

    return get
DEEPNORM_ALPHA = 2.0 ** 0.25
LN_EPS = 1e-5

LANES = 128
SUBLANES = 8
VMEM_LIMIT_BYTES = 56 * 1024 * 1024


def _params(*semantics):
    return pltpu.CompilerParams(dimension_semantics=semantics, vmem_limit_bytes=VMEM_LIMIT_BYTES)


def _resident(shape):
    zeros = (0,) * len(shape)
    return pl.BlockSpec(shape, lambda *_: zeros, pipeline_mode=pl.Buffered(1))


def _layer_norm(y, g, b):
    mu = jnp.mean(y, axis=-1, keepdims=True)
    yc = y - mu
    var = jnp.mean(yc * yc, axis=-1, keepdims=True)
    return yc * lax.rsqrt(var + LN_EPS) * g + b


def _sigmoid(x):
    return 1.0 / (1.0 + jnp.exp(-x))


def _gelu_tanh(x):
    c = 0.7978845608028654
    return 0.5 * x * (1.0 + jnp.tanh(c * (x + 0.044715 * (x * x * x))))


def _ffn_ln_kernel(x_ref, wg_ref, wu_ref, wd_ref, g_ref, b_ref, o_ref, *, chunks, sub):
    for r0 in range(0, x_ref.shape[0], sub):
        x = x_ref[r0:r0 + sub, :]
        xb = x.astype(BF16)
        acc = None
        for c0, c1 in chunks:
            gate = jnp.dot(xb, wg_ref[:, c0:c1], preferred_element_type=F32)
            up = jnp.dot(xb, wu_ref[:, c0:c1], preferred_element_type=F32)
            h = (gate * _sigmoid(gate) * up).astype(BF16)
            part = jnp.dot(h, wd_ref[c0:c1, :], preferred_element_type=F32)
            acc = part if acc is None else acc + part
        y = DEEPNORM_ALPHA * x + 0.5 * acc
        o_ref[r0:r0 + sub, :] = _layer_norm(y, g_ref[...], b_ref[...])


def _ffn_chunks(d_ff, width):
    edges = list(range(0, d_ff, width)) + [d_ff]
    return tuple(zip(edges[:-1], edges[1:]))


def _ffn_ln(x, wg, wu, wd, g, b, *, tm=1024, sub=256, chunk=1024):
    t, d = x.shape
    d_ff = wg.shape[1]
    kern = functools.partial(_ffn_ln_kernel, chunks=_ffn_chunks(d_ff, chunk), sub=sub)
    return pl.pallas_call(
        kern,
        grid=(t // tm,),
        in_specs=[
            pl.BlockSpec((tm, d), lambda i: (i, 0)),
            _resident((d, d_ff)), _resident((d, d_ff)), _resident((d_ff, d)),
            _resident((1, d)), _resident((1, d)),
        ],
        out_specs=pl.BlockSpec((tm, d), lambda i: (i, 0)),
        out_shape=jax.ShapeDtypeStruct((t, d), F32),
        compiler_params=_params("parallel"),
        name="ffn_ln",
    )(x, wg.astype(BF16), wu.astype(BF16), wd.astype(BF16), g.reshape(1, d), b.reshape(1, d))


def _rope_spread():
    half = ROT_DIM // 2
    rows = lax.broadcasted_iota(jnp.int32, (ROPE_ROWS, 2 * LANES), 0)
    cols = lax.broadcasted_iota(jnp.int32, (ROPE_ROWS, 2 * LANES), 1)
    in_head = cols % HEAD_DIM
    freq = in_head % half
    is_cos = cols < LANES
    rot = in_head < ROT_DIM
    cos_part = is_cos & rot & (rows == freq)
    one_part = is_cos & jnp.logical_not(rot) & (rows == 2 * half)
    sin_part = jnp.logical_not(is_cos) & rot & (rows == half + freq)
    sign = jnp.where(in_head < half, 1.0, -1.0)
    return jnp.where(cos_part | one_part, 1.0, jnp.where(sin_part, sign, 0.0)).astype(BF16)


def _in_proj_kernel(x_ref, pos_ref, invf_ref, w_ref,
                    qkv0_ref, qkv1_ref, qkv2_ref, ga_ref, gs_ref, stage_ref, *, splits, sub):
    half = ROT_DIM // 2
    spread = _rope_spread()
    first = lax.broadcasted_iota(jnp.int32, (sub, LANES), 1) % HEAD_DIM < half
    s0, s1, s2, s3, s4, s5 = splits
    tn = (((0,), (0,)), ((), ()))

    for r0 in range(0, x_ref.shape[0], sub):
        xb = x_ref[r0:r0 + sub, :].astype(BF16)

        ang = invf_ref[...] * pos_ref[:, r0:r0 + sub].astype(F32)
        tab = jnp.concatenate([jnp.cos(ang), jnp.sin(ang), jnp.ones((half, sub), F32),
                               jnp.zeros((ROPE_ROWS - 3 * half, sub), F32)], axis=0)
        tab_hi = tab.astype(BF16)
        tab_lo = (tab - tab_hi.astype(F32)).astype(BF16)
        cs = (lax.dot_general(tab_hi, spread, tn, preferred_element_type=F32)
              + lax.dot_general(tab_lo, spread, tn, preferred_element_type=F32))
        cos = cs[:, :LANES]
        sin = cs[:, LANES:]

        def rotate(z):
            cols = []
            for c in range(z.shape[1] // LANES):
                zc = z[:, c * LANES:(c + 1) * LANES]
                zs = zc * sin
                up = pltpu.roll(jnp.where(first, zs, 0.0), half, axis=1)
                dn = pltpu.roll(jnp.where(first, 0.0, zs), LANES - half, axis=1)
                cols.append(zc * cos + up + dn)
            return jnp.concatenate(cols, axis=1)

        def proj(lo, hi):
            return jnp.dot(xb, w_ref[:, lo:hi], preferred_element_type=F32)

        q = rotate(proj(0, s0)) * (HEAD_DIM ** -0.5 * LOG2_E)
        k = rotate(proj(s0, s1))
        v = proj(s1, s2)
        for gi, out_ref in enumerate((qkv0_ref, qkv1_ref, qkv2_ref)):
            dil = ATTN_PATTERNS[gi][1]
            c0 = gi * GROUP_WIDTH
            qkv = jnp.concatenate([a[:, c0:c0 + GROUP_WIDTH] for a in (q, k, v)], axis=1)
            n = sub // dil
            i0 = r0 // dil
            if dil == 1:
                out_ref[0, i0:i0 + n, :] = qkv.astype(BF16)
            else:
                get = _rows_by_class(qkv, stage_ref.at[r0 // sub, gi - 1], dil)
                for r in range(dil):
                    out_ref[r, i0:i0 + n, :] = get(r).astype(BF16)
        ga_ref[r0:r0 + sub, :] = _sigmoid(proj(s3, s4)).astype(BF16)
        gs_ref[r0:r0 + sub, :] = _sigmoid(proj(s4, s5)).astype(BF16)


def _in_proj(x1, positions, w_in, *, batch, seq, ssm_width, tm=1024, sub=256):
    t, d = x1.shape
    splits = (ATTN_WIDTH, 2 * ATTN_WIDTH, 3 * ATTN_WIDTH, 3 * ATTN_WIDTH + ssm_width,
              3 * ATTN_WIDTH + ssm_width + d, 3 * ATTN_WIDTH + ssm_width + 2 * d)
    assert splits[-1] == w_in.shape[1] and seq % tm == 0
    half = ROT_DIM // 2
    invf = (ROPE_THETA ** (-jnp.arange(half, dtype=F32) * 2.0 / ROT_DIM)).reshape(half, 1)
    tiles = seq // tm
    qkv_w = 3 * GROUP_WIDTH
    qkv_specs = [pl.BlockSpec((None, dil, tm // dil, qkv_w), lambda i: (i // tiles, 0, i % tiles, 0))
                 for _, dil in ATTN_PATTERNS]
    qkv_shapes = [jax.ShapeDtypeStruct((batch, dil, seq // dil, qkv_w), BF16) for _, dil in ATTN_PATTERNS]
    widths = (d, d)
    kern = functools.partial(_in_proj_kernel, splits=splits, sub=sub)
    n_strided = sum(dil > 1 for _, dil in ATTN_PATTERNS)
    stage_rows = max(sub // dil * _scatter_pitch(dil) for _, dil in ATTN_PATTERNS)
    return pl.pallas_call(
        kern,
        grid=(t // tm,),
        in_specs=[
            pl.BlockSpec((tm, d), lambda i: (i, 0)),
            pl.BlockSpec((1, tm), lambda i: (0, i)),
            _resident((half, 1)),
            _resident(w_in.shape),
        ],
        out_specs=qkv_specs + [pl.BlockSpec((tm, w), lambda i: (i, 0)) for w in widths],
        out_shape=qkv_shapes + [jax.ShapeDtypeStruct((t, w), BF16) for w in widths],
        scratch_shapes=[pltpu.VMEM((tm // sub, n_strided, qkv_w // LANES, stage_rows, LANES), F32)],
        compiler_params=_params("parallel"),
        name="in_proj",
    )(x1, positions.reshape(1, t), invf, w_in)


def _attn_kernel(q_ref, kc_ref, kp_ref, vc_ref, vp_ref, o_ref, st_ref, *, n_sub):
    first = pl.program_id(2) == 0
    row = lax.broadcasted_iota(jnp.int32, (ATTN_BLK, 2 * ATTN_BLK), 0)
    col = lax.broadcasted_iota(jnp.int32, (ATTN_BLK, 2 * ATTN_BLK), 1)
    band = (col >= row) & (col <= row + ATTN_BLK)
    lane = lax.broadcasted_iota(jnp.int32, (ATTN_BLK, LANES), 1)
    lo_half = lane < HEAD_DIM

    n_pair = GROUP_WIDTH // LANES
    tiles = [(rc, j, hp, hh) for rc in range(q_ref.shape[0]) for j in range(n_sub)
             for hp in range(n_pair) for hh in range(2)]

    def window(ref_prev, ref_cur, j, c0):
        r0 = j * ATTN_BLK
        if j == 0:
            return jnp.concatenate([ref_prev[:, c0:c0 + LANES], ref_cur[0:ATTN_BLK, c0:c0 + LANES]], axis=0)
        return ref_cur[r0 - ATTN_BLK:r0 + ATTN_BLK, c0:c0 + LANES]

    def scores(tile):
        rc, j, hp, hh = tile
        q2 = q_ref[rc, j * ATTN_BLK:(j + 1) * ATTN_BLK, hp * LANES:(hp + 1) * LANES]
        qm = jnp.where(lo_half if hh == 0 else jnp.logical_not(lo_half), q2, jnp.zeros_like(q2))
        return lax.dot_general(qm, window(kp_ref.at[rc], kc_ref.at[rc], j, hp * LANES), (((1,), (1,)), ((), ())),
                               preferred_element_type=F32)

    def softmax(tile, s):
        valid = band & ((col >= ATTN_BLK) | jnp.logical_not(first)) if tile[1] == 0 else band
        s = jnp.where(valid, s, NEG_INF)
        m = jnp.max(s, axis=1, keepdims=True)
        p = jnp.exp2(s - m)
        return p.astype(BF16), m, jnp.sum(p, axis=1, keepdims=True)

    def weighted_values(tile, p, den):
        rc, j, hp, _ = tile
        return jnp.dot(p, window(vp_ref.at[rc], vc_ref.at[rc], j, hp * LANES), preferred_element_type=F32) / den

    s_q, p_q, outs = {}, {}, {}
    for step in range(len(tiles) + 2):
        if step < len(tiles):
            s_q[step] = scores(tiles[step])
        if 0 <= step - 1 < len(tiles):
            p_q[step - 1] = softmax(tiles[step - 1], s_q.pop(step - 1))
        if 0 <= step - 2 < len(tiles):
            t = step - 2
            rc, j, hp, hh = tiles[t]
            p, m, den = p_q.pop(t)
            outs[hh] = weighted_values(tiles[t], p, den)
            h = 2 * hp + hh
            rows = slice(j * ATTN_BLK, (j + 1) * ATTN_BLK)
            if h == 0:
                st_ref[rc, rows, :] = jnp.zeros((ATTN_BLK, LANES), F32)
            st_ref[rc, rows, h:h + 1] = m
            st_ref[rc, rows, HEADS_PER_GROUP + h:HEADS_PER_GROUP + h + 1] = den
            if hh == 1:
                o_ref[rc, rows, hp * LANES:(hp + 1) * LANES] = (
                    jnp.where(lo_half, outs[0], outs[1]).astype(BF16))


def _dilated_attention(qkv):
    batch, dilation, n, _ = qkv.shape
    rows_per_step = 2048
    qb = min(rows_per_step, n)
    n_sub = qb // ATTN_BLK
    rb = min(max(rows_per_step // qb, 1), dilation)

    def cur(part):
        return pl.BlockSpec((None, rb, qb, GROUP_WIDTH), lambda b, r, i: (b, r, i, part))

    def prev(part):
        return pl.BlockSpec((None, rb, ATTN_BLK, GROUP_WIDTH),
                            lambda b, r, i: (b, r, jnp.maximum(i * n_sub - 1, 0), part))

    return pl.pallas_call(
        functools.partial(_attn_kernel, n_sub=n_sub),
        grid=(batch, dilation // rb, n // qb),
        in_specs=[cur(0), cur(1), prev(1), cur(2), prev(2)],
        out_specs=[pl.BlockSpec((None, rb, qb, GROUP_WIDTH), lambda b, r, i: (b, r, i, 0)),
                   pl.BlockSpec((None, rb, qb, LANES), lambda b, r, i: (b, r, i, 0))],
        out_shape=[jax.ShapeDtypeStruct((batch, dilation, n, GROUP_WIDTH), BF16),
                   jax.ShapeDtypeStruct((batch, dilation, n, LANES), F32)],
        compiler_params=_params("parallel", "parallel", "parallel"),
        name=f"attn_d{dilation}",
    )(qkv, qkv, qkv, qkv, qkv)


def _ssm_in_kernel(*refs, n_grp, ch, n_w):
    x_refs, w_refs, ut_ref = refs[:-n_w - 1], refs[-n_w - 1:-1], refs[-1]
    w = jnp.concatenate([w_ref[...] for w_ref in w_refs], axis=1)
    for s in range(0, SSM_CHUNK, 2):
        xs = jnp.concatenate(
            [jnp.concatenate([x_ref[pl.ds(s + i, ch, stride=SSM_CHUNK), :] for x_ref in x_refs], axis=1)
             for i in range(2)], axis=0).astype(BF16)
        ut = lax.dot_general(w, xs, (((0,), (1,)), ((), ())), preferred_element_type=F32)
        for i in range(2):
            ut_ref[:, (s + i) * SSM_GROUP:(s + i + 1) * SSM_GROUP, :] = (
                ut[:, i * ch:(i + 1) * ch].reshape(n_grp, SSM_GROUP, ch).astype(BF16))


def _ssm_in(x1, w_in_b, u_lo, width, *, ch=LANES, wblk=256):
    t, d = x1.shape
    assert u_lo % wblk == 0 and width % wblk == 0
    n_grp = width // SSM_GROUP
    n_chunk_all = t // SSM_CHUNK
    tm = ch * SSM_CHUNK
    n_w = width // wblk
    slabs = [pl.BlockSpec((tm, LANES), lambda i, c=c: (i, c)) for c in range(d // LANES)]
    w_cols = [pl.BlockSpec((d, wblk), lambda i, c=c: (0, u_lo // wblk + c), pipeline_mode=pl.Buffered(1))
              for c in range(n_w)]
    return pl.pallas_call(
        functools.partial(_ssm_in_kernel, n_grp=n_grp, ch=ch, n_w=n_w),
        grid=(n_chunk_all // ch,),
        in_specs=slabs + w_cols,
        out_specs=pl.BlockSpec((n_grp, SSM_ROW, ch), lambda i: (0, 0, i)),
        out_shape=jax.ShapeDtypeStruct((n_grp, SSM_ROW, n_chunk_all), BF16),
        compiler_params=_params("parallel"),
        name="ssm_in",
    )(*([x1] * (d // LANES)), *([w_in_b] * n_w))


def _hdot(a, b):
    return jnp.dot(a, b, preferred_element_type=F32, precision=lax.Precision.HIGHEST)


def _cmul(ar, ai, br, bi):
    return ar * br - ai * bi, ar * bi + ai * br


def _cpowers(lbr, lbi, n, bits):
    out_r = jnp.ones(n.shape, F32)
    out_i = jnp.zeros(n.shape, F32)
    for b in range(bits):
        sel = (n & (1 << b)) != 0
        out_r, out_i = _cmul(out_r, out_i, jnp.where(sel, lbr, 1.0), jnp.where(sel, lbi, 0.0))
        lbr, lbi = _cmul(lbr, lbi, lbr, lbi)
    return out_r, out_i


def _ssm_prep_kernel(are_l, aim_l, ldt_ref, bre_ref, bim_ref, cre_ref, cim_ref,
                     tt_ref, win_ref, wout_ref, apr_ref, api_ref, *, gb):
    p = are_l.shape[2]
    eye = (lax.broadcasted_iota(jnp.int32, (p, p), 0) == lax.broadcasted_iota(jnp.int32, (p, p), 1)).astype(F32)
    expand = (lax.broadcasted_iota(jnp.int32, (SSM_CHUNK, SSM_ROW), 1) // SSM_GROUP
              == lax.broadcasted_iota(jnp.int32, (SSM_CHUNK, SSM_ROW), 0)).astype(F32)
    expand_t = (lax.broadcasted_iota(jnp.int32, (SSM_ROW, SSM_CHUNK), 0) // SSM_GROUP
                == lax.broadcasted_iota(jnp.int32, (SSM_ROW, SSM_CHUNK), 1)).astype(F32)
    rem = SSM_CHUNK - 1 - lax.broadcasted_iota(jnp.int32, (p, SSM_CHUNK), 1)
    tp1 = lax.broadcasted_iota(jnp.int32, (SSM_CHUNK, p), 0) + 1
    col = lax.broadcasted_iota(jnp.int32, (p, SSM_POWERS), 1)
    lane = lax.broadcasted_iota(jnp.int32, (SSM_GROUP, SSM_ROW), 1)
    chunk_bits = SSM_CHUNK.bit_length()

    for g in range(gb):
        dt = jnp.exp(ldt_ref[g])
        lam_r = jnp.minimum(are_l[g], -1e-4)
        lam_i = aim_l[g]
        mag_l = jnp.exp(lam_r * dt)
        lbr_l = mag_l * jnp.cos(lam_i * dt)
        lbi_l = mag_l * jnp.sin(lam_i * dt)
        inv = 1.0 / (lam_r * lam_r + lam_i * lam_i)
        cfr_l = ((lbr_l - 1.0) * lam_r + lbi_l * lam_i) * inv
        cfi_l = (lbi_l * lam_r - (lbr_l - 1.0) * lam_i) * inv
        stacked = jnp.concatenate([lbr_l, lbi_l, cfr_l, cfi_l, jnp.zeros((SUBLANES - 4, p), F32)], axis=0)
        cols = lax.dot_general(eye, stacked, (((1,), (1,)), ((), ())), preferred_element_type=F32,
                               precision=lax.Precision.HIGHEST)
        lbr, lbi = cols[:, 0:1], cols[:, 1:2]

        bre = bre_ref[g]
        bim = bim_ref[g]
        bbar_r, bbar_i = _cmul(cols[:, 2:3], cols[:, 3:4], bre, bim)
        pw_r, pw_i = _cpowers(lbr, lbi, rem, chunk_bits - 1)
        win_r, win_i = _cmul(_hdot(pw_r, expand), _hdot(pw_i, expand), bbar_r, bbar_i)
        win_ref[g] = jnp.concatenate([win_r, win_i], axis=0).astype(BF16)

        a_r, a_i = _cmul(pw_r[:, 0:1], pw_i[:, 0:1], lbr, lbi)
        apr = jnp.zeros((p, SSM_POWERS), F32)
        api = jnp.zeros((p, SSM_POWERS), F32)
        for j in range(SSM_POWERS):
            apr = jnp.where(col == j, a_r, apr)
            api = jnp.where(col == j, a_i, api)
            a_r, a_i = _cmul(a_r, a_i, a_r, a_i)
        apr_ref[g] = apr
        api_ref[g] = api

        e_r, e_i = _cpowers(lbr_l, lbi_l, tp1, chunk_bits)
        cre = cre_ref[g]
        cim = cim_ref[g]
        wo_r, wo_i = _cmul(_hdot(expand_t, e_r), _hdot(expand_t, e_i), cre, cim)
        wout_ref[g] = jnp.concatenate([wo_r, -wo_i], axis=1).astype(BF16)

        hrev = _hdot(cre[0:SSM_GROUP, :], win_r) - _hdot(cim[0:SSM_GROUP, :], win_i)
        blocks = []
        for t in range(SSM_CHUNK):
            hi = (t + 1) * SSM_GROUP
            rolled = hrev if hi == SSM_ROW else pltpu.roll(hrev, hi, axis=1)
            blocks.append(jnp.where(lane < hi, rolled, 0.0))
        tt_ref[g] = jnp.concatenate(blocks, axis=0).astype(BF16)


def _ssm_prep(a_re, a_im, log_dt, b_re, b_im, c_re, c_im, *, gb=4):
    g, p = a_re.shape
    row = SSM_ROW
    blk = lambda shape: pl.BlockSpec((gb,) + shape, lambda i: (i, 0, 0))
    b_t = lambda b: jnp.tile(b, (1, 1, SSM_CHUNK))
    c_t = lambda c: jnp.tile(c, (1, SSM_CHUNK, 1))
    out_shapes = [
        jax.ShapeDtypeStruct((g, row, row), BF16),
        jax.ShapeDtypeStruct((g, 2 * p, row), BF16),
        jax.ShapeDtypeStruct((g, row, 2 * p), BF16),
        jax.ShapeDtypeStruct((g, p, SSM_POWERS), F32), jax.ShapeDtypeStruct((g, p, SSM_POWERS), F32),
    ]
    return pl.pallas_call(
        functools.partial(_ssm_prep_kernel, gb=gb),
        grid=(g // gb,),
        in_specs=[blk((1, p)), blk((1, p)), blk((1, 1)),
                  blk((p, row)), blk((p, row)), blk((row, p)), blk((row, p))],
        out_specs=[blk(s.shape[1:]) for s in out_shapes],
        out_shape=out_shapes,
        compiler_params=_params("parallel"),
        name="ssm_prep",
    )(a_re.reshape(g, 1, p), a_im.reshape(g, 1, p),
      log_dt.reshape(g, 1, 1), b_t(b_re), b_t(b_im), c_t(c_re), c_t(c_im))


def _ssm_kernel(ut_ref, tt_ref, win_ref, wout_ref, apr_ref, api_ref, dsk_ref, y_ref, *, gb, n_chunk):
    nk = ut_ref.shape[-1]
    p = apr_ref.shape[1]
    pos = lax.broadcasted_iota(jnp.int32, (SUBLANES, nk), 1) % n_chunk
    n_steps = (n_chunk - 1).bit_length()
    assert n_steps <= SSM_POWERS

    def shift_whole_tiles(v, sh):
        segs = []
        for b0 in range(0, nk, n_chunk):
            segs += [jnp.zeros((v.shape[0], sh), F32), v[:, b0:b0 + n_chunk - sh]]
        return jnp.concatenate(segs, axis=1)

    for g in range(gb):
        ut = ut_ref[g]
        sl = jnp.dot(win_ref[g], ut, preferred_element_type=F32)
        apr = apr_ref[g]
        api = api_ref[g]
        prev_r, prev_i = [], []
        for b0 in range(0, p, SUBLANES * SCAN_INTERLEAVE):
            tiles = range(b0, b0 + SUBLANES * SCAN_INTERLEAVE, SUBLANES)
            s = {r0: (sl[r0:r0 + SUBLANES], sl[p + r0:p + r0 + SUBLANES]) for r0 in tiles}
            for j in range(n_steps):
                sh = 1 << j
                keep = pos >= sh
                for r0 in tiles:
                    s_r, s_i = s[r0]
                    a_r = apr[r0:r0 + SUBLANES, j:j + 1]
                    a_i = api[r0:r0 + SUBLANES, j:j + 1]
                    if sh % LANES == 0:
                        p_r, p_i = shift_whole_tiles(s_r, sh), shift_whole_tiles(s_i, sh)
                    else:
                        p_r = jnp.where(keep, pltpu.roll(s_r, sh, axis=1), 0.0)
                        p_i = jnp.where(keep, pltpu.roll(s_i, sh, axis=1), 0.0)
                    s[r0] = (s_r + a_r * p_r - a_i * p_i, s_i + a_r * p_i + a_i * p_r)
            keep = pos >= 1
            for r0 in tiles:
                prev_r.append(jnp.where(keep, pltpu.roll(s[r0][0], 1, axis=1), 0.0))
                prev_i.append(jnp.where(keep, pltpu.roll(s[r0][1], 1, axis=1), 0.0))
        prev = jnp.concatenate(prev_r + prev_i, axis=0).astype(BF16)
        y = jnp.dot(tt_ref[g], ut, preferred_element_type=F32)
        y += jnp.dot(wout_ref[g], prev, preferred_element_type=F32)
        y += dsk_ref[g] * ut.astype(F32)
        y_ref[g] = _gelu_tanh(y).astype(BF16)


def _ssm(ut, prep, dsk, *, n_chunk, gb=4):
    tt, win, wout, apr, api = prep
    g, row, nk = ut.shape
    p = apr.shape[1]
    blk = lambda shape: pl.BlockSpec((gb,) + shape, lambda i: (i, 0, 0))
    kern = functools.partial(_ssm_kernel, gb=gb, n_chunk=n_chunk)
    return pl.pallas_call(
        kern,
        grid=(g // gb,),
        in_specs=[blk((row, nk)), blk((row, row)), blk((2 * p, row)), blk((row, 2 * p)),
                  blk((p, SSM_POWERS)), blk((p, SSM_POWERS)), blk((row, 1))],
        out_specs=blk((row, nk)),
        out_shape=jax.ShapeDtypeStruct((g, row, nk), BF16),
        compiler_params=_params("parallel"),
        name="ssm",
    )(ut, tt, win, wout, apr, api, dsk)


def _ssm_out_kernel(yt_ref, wglu_ref, bglu_ref, wso_ref, o_ref, *, t_per_pass):
    n_grp, _, ch = yt_ref.shape
    width = n_grp * SSM_GROUP
    for t0 in range(0, SSM_CHUNK, t_per_pass):
        y = jnp.concatenate(
            [yt_ref[:, t * SSM_GROUP:(t + 1) * SSM_GROUP, :].reshape(width, ch)
             for t in range(t0, t0 + t_per_pass)], axis=1)
        gate = jnp.dot(wglu_ref[...], y, preferred_element_type=F32) + bglu_ref[...]
        glu = (y.astype(F32) * _sigmoid(gate)).astype(BF16)
        yo = lax.dot_general(glu, wso_ref[...], (((0,), (0,)), ((), ())), preferred_element_type=F32)
        for i in range(t_per_pass):
            o_ref[t0 + i] = yo[i * ch:(i + 1) * ch, :].astype(BF16)


def _ssm_out(yt, w_glu, b_glu, w_ssm_out, *, ch=LANES, t_per_pass=4):
    n_grp, row, n_chunk_all = yt.shape
    width, d = w_ssm_out.shape
    return pl.pallas_call(
        functools.partial(_ssm_out_kernel, t_per_pass=t_per_pass),
        grid=(n_chunk_all // ch,),
        in_specs=[pl.BlockSpec((n_grp, row, ch), lambda i: (0, 0, i)),
                  _resident((width, width)), _resident((width, 1)), _resident((width, d))],
        out_specs=pl.BlockSpec((None, SSM_CHUNK, ch, d), lambda i: (i, 0, 0, 0)),
        out_shape=jax.ShapeDtypeStruct((n_chunk_all // ch, SSM_CHUNK, ch, d), BF16),
        compiler_params=_params("parallel"),
        name="ssm_out",
    )(yt, w_glu.T.astype(BF16), b_glu.reshape(width, 1), w_ssm_out.astype(BF16))


def _mix_ffn_kernel(x_ref, o0, o1, o2, st0, st1, st2, ys_ref, ga_ref, gs_ref,
                    wao_ref, wo_ref, g_ref, b_ref, wg_ref, wu_ref, wd_ref, g3_ref, b3_ref,
                    out_ref, o_stage, st_stage, ys_stage, x2_ref, *, sub, chunks):
    step = pl.program_id(0)
    slot_w = step % 2
    slot_r = 1 - slot_w

    @pl.when(step == 0)
    def _():
        x2_ref[...] = jnp.zeros(x2_ref.shape, F32)

    lane = lax.broadcasted_iota(jnp.int32, (sub, LANES), 1)
    lo_half = lane < HEAD_DIM

    def token_order(ref, stage, r0):
        dil = ref.shape[0]
        n = sub // dil
        i0 = r0 // dil
        if dil == 1:
            return ref[0, i0:i0 + n, :].astype(F32)
        n_col = ref.shape[2] // LANES
        pitch = _scatter_pitch(dil)
        for r in range(dil):
            blk = ref[r, i0:i0 + n, :].astype(F32)
            for c in range(n_col):
                stage[c, pl.ds(r, n, stride=pitch), :] = blk[:, c * LANES:(c + 1) * LANES]
        cols = [stage[c, 0:n * pitch, :] for c in range(n_col)]
        if pitch != dil:
            cols = [a.reshape(n, pitch, LANES)[:, :dil, :].reshape(sub, LANES) for a in cols]
        return jnp.concatenate(cols, axis=1)

    def head_cols(vals):
        cols = []
        for hp in range(GROUP_WIDTH // LANES):
            cols.append(jnp.where(lo_half, vals[2 * hp], vals[2 * hp + 1]))
        return jnp.concatenate(cols, axis=1)

    def merge_groups(k):
        r0 = k * sub
        sts = [token_order(st, st_stage.at[k, i], r0) for i, st in enumerate((st0, st1, st2))]
        outs = [token_order(o, o_stage.at[k, i], r0) for i, o in enumerate((o0, o1, o2))]
        dens = [pltpu.roll(st, LANES - HEADS_PER_GROUP, axis=1) for st in sts]
        mx = jnp.maximum(jnp.maximum(sts[0], sts[1]), sts[2])
        w = [den * jnp.exp2(st - mx) for den, st in zip(dens, sts)]
        tot = jnp.where(lane < HEADS_PER_GROUP, w[0] + w[1] + w[2], 1.0)
        att = None
        for g in range(N_ATTN_GROUPS):
            wt = w[g] / tot
            term = head_cols([wt[:, h:h + 1] for h in range(HEADS_PER_GROUP)]) * outs[g]
            att = term if att is None else att + term
        return att.astype(BF16)

    def gate(k, att):
        rows = slice(k * sub, (k + 1) * sub)
        y_attn = jnp.dot(att, wao_ref[...], preferred_element_type=F32)
        y_ssm = token_order(ys_ref, ys_stage.at[k], k * sub)
        merged = ga_ref[rows, :].astype(F32) * y_attn + gs_ref[rows, :].astype(F32) * y_ssm
        return merged.astype(BF16)

    def project(k, merged):
        rows = slice(k * sub, (k + 1) * sub)
        mix = jnp.dot(merged, wo_ref[...], preferred_element_type=F32)
        x2_ref[slot_w, rows, :] = _layer_norm(DEEPNORM_ALPHA * x_ref[rows, :] + mix, g_ref[...], b_ref[...])

    x2_prev = x2_ref[slot_r]
    ffn_state = {}

    def ffn_gate_up(k, c):
        rows = slice(k * sub, (k + 1) * sub)
        if c == chunks[0]:
            ffn_state[k] = {"xb": x2_prev[rows, :].astype(BF16), "acc": None}
        xb = ffn_state[k]["xb"]
        ffn_state[k]["gu"] = (jnp.dot(xb, wg_ref[:, c[0]:c[1]], preferred_element_type=F32),
                              jnp.dot(xb, wu_ref[:, c[0]:c[1]], preferred_element_type=F32))

    def ffn_down(k, c):
        rows = slice(k * sub, (k + 1) * sub)
        gate_v, up = ffn_state[k].pop("gu")
        h = (gate_v * _sigmoid(gate_v) * up).astype(BF16)
        part = jnp.dot(h, wd_ref[c[0]:c[1], :], preferred_element_type=F32)
        acc = ffn_state[k]["acc"]
        ffn_state[k]["acc"] = part if acc is None else acc + part
        if c == chunks[-1]:
            y = DEEPNORM_ALPHA * x2_prev[rows, :] + 0.5 * ffn_state.pop(k)["acc"]
            out_ref[rows, :] = _layer_norm(y, g3_ref[...], b3_ref[...])

    n_sub = x_ref.shape[0] // sub
    ffn_stages = [f for k in range(n_sub) for c in chunks
                  for f in (functools.partial(ffn_gate_up, k, c), functools.partial(ffn_down, k, c))]
    mix_stages = [f for k in range(n_sub) for f in (functools.partial(merge_groups, k),
                                                    functools.partial(gate, k),
                                                    functools.partial(project, k))]
    carry = ()
    for f in ffn_stages:
        f()
        if mix_stages:
            out = mix_stages.pop(0)(*carry)
            carry = () if out is None else (out,)
    assert not mix_stages


def _mix_ffn(x1, os_, sts, y_ssm, ga, gs, w_attn_out, w_o, g2, b2, wg, wu, wd, g3, b3, *,
             seq, tm=512, sub=256, chunk=1024):
    t, d = x1.shape
    d_ff = wg.shape[1]
    tiles = seq // tm
    n_tiles = t // tm
    cur = lambda i: jnp.minimum(i, n_tiles - 1)
    tile = lambda w: pl.BlockSpec((tm, w), lambda i: (cur(i), 0))

    def by_class(a):
        _, dil, n, w = a.shape
        return pl.BlockSpec((None, dil, tm // dil, w), lambda i: (cur(i) // tiles, 0, cur(i) % tiles, 0))

    n_grp = len(os_)
    stage_rows = max(sub // a.shape[1] * _scatter_pitch(a.shape[1]) for a in os_)
    per_big = y_ssm.shape[2] * SSM_CHUNK // tm
    ys_spec = pl.BlockSpec((None, SSM_CHUNK, tm // SSM_CHUNK, d),
                           lambda i: (cur(i) // per_big, 0, cur(i) % per_big, 0))
    ys_rows = sub // SSM_CHUNK * _scatter_pitch(SSM_CHUNK)
    vec = lambda v: v.reshape(1, d)
    return pl.pallas_call(
        functools.partial(_mix_ffn_kernel, sub=sub, chunks=_ffn_chunks(d_ff, chunk)),
        grid=(n_tiles + 1,),
        in_specs=[tile(d)] + [by_class(a) for a in os_] + [by_class(a) for a in sts] + [ys_spec, tile(d), tile(d),
                  _resident(w_attn_out.shape), _resident(w_o.shape), _resident((1, d)), _resident((1, d)),
                  _resident((d, d_ff)), _resident((d, d_ff)), _resident((d_ff, d)),
                  _resident((1, d)), _resident((1, d))],
        out_specs=pl.BlockSpec((tm, d), lambda i: (jnp.maximum(i - 1, 0), 0)),
        out_shape=jax.ShapeDtypeStruct((t, d), F32),
        scratch_shapes=[pltpu.VMEM((tm // sub, n_grp, GROUP_WIDTH // LANES, stage_rows, LANES), F32),
                        pltpu.VMEM((tm // sub, n_grp, 1, stage_rows, LANES), F32),
                        pltpu.VMEM((tm // sub, d // LANES, ys_rows, LANES), F32),
                        pltpu.VMEM((2, tm, d), F32)],
        compiler_params=_params("arbitrary"),
        name="mix_ffn",
    )(x1, *os_, *sts, y_ssm, ga, gs, w_attn_out.astype(BF16), w_o.astype(BF16), vec(g2), vec(b2),
      wg.astype(BF16), wu.astype(BF16), wd.astype(BF16), vec(g3), vec(b3))


def _layer(x, positions, w_in, w_attn_out, a_re, a_im, log_dt, b_re, b_im, c_re, c_im, d_skip,
           w_glu, b_glu, w_ssm_out, w_o, ffn1, ffn2, ln1, ln2, ln3):
    batch, seq, d = x.shape
    t = batch * seq
    n_grp = a_re.shape[0]
    ssm_width = n_grp * SSM_GROUP
    u_lo = 3 * ATTN_WIDTH

    x1 = _ffn_ln(x.reshape(t, d), *ffn1, *ln1)
    w_in_b = w_in.astype(BF16)
    *qkvs, ga, gs = _in_proj(x1, positions, w_in_b, batch=batch, seq=seq, ssm_width=ssm_width)

    os_, sts = [], []
    for qkv, (window, dilation) in zip(qkvs, ATTN_PATTERNS):
        assert window == ATTN_BLK * dilation
        o, st = _dilated_attention(qkv)
        os_.append(o)
        sts.append(st)

    ut = _ssm_in(x1, w_in_b, u_lo, ssm_width)
    prep = _ssm_prep(a_re, a_im, log_dt, b_re, b_im, c_re, c_im)
    dsk = jnp.tile(d_skip.reshape(n_grp, 1, SSM_GROUP), (1, SSM_CHUNK, 1)).reshape(n_grp, SSM_ROW, 1)
    yt = _ssm(ut, prep, dsk, n_chunk=seq // SSM_CHUNK)
    y_ssm = _ssm_out(yt, w_glu, b_glu, w_ssm_out)

    x3 = _mix_ffn(x1, os_, sts, y_ssm, ga, gs, w_attn_out, w_o, *ln2, *ffn2, *ln3, seq=seq)
    return x3.reshape(batch, seq, d)


def kernel(x, positions, w_in, w_attn_out, a_re, a_im, log_dt, b_re, b_im, c_re, c_im, d_skip, w_glu, b_glu, w_ssm_out, w_o, ffn1_wg, ffn1_wu, ffn1_wd, ffn2_wg, ffn2_wu, ffn2_wd, ln1_g, ln1_b, ln2_g, ln2_b, ln3_g, ln3_b):
    depth = w_in.shape[0]
    for i in range(depth):
        x = _layer(x, positions, w_in[i], w_attn_out[i], a_re[i], a_im[i], log_dt[i], b_re[i], b_im[i],
                   c_re[i], c_im[i], d_skip[i], w_glu[i], b_glu[i], w_ssm_out[i], w_o[i],
                   (ffn1_wg[i], ffn1_wu[i], ffn1_wd[i]), (ffn2_wg[i], ffn2_wu[i], ffn2_wd[i]),
                   (ln1_g[i], ln1_b[i]), (ln2_g[i], ln2_b[i]), (ln3_g[i], ln3_b[i]))
    return x
```

```python
import functools

import jax
import jax.numpy as jnp
from jax import lax
from jax.experimental import pallas as pl
from jax.experimental.pallas import tpu as pltpu

F32 = jnp.float32
BF16 = jnp.bfloat16

HEAD_DIM = 64
HEADS_PER_GROUP = 4
GROUP_WIDTH = HEADS_PER_GROUP * HEAD_DIM
ATTN_PATTERNS = ((128, 1), (512, 4), (2048, 16))
N_ATTN_GROUPS = len(ATTN_PATTERNS)
ATTN_WIDTH = N_ATTN_GROUPS * GROUP_WIDTH
ROT_DIM = HEAD_DIM // 4
ROPE_THETA = 500000.0
ROPE_ROWS = 32
ATTN_BLK = 128
NEG_INF = -1e30
LOG2_E = 1.4426950408889634
SSM_GROUP = 16
SSM_CHUNK = 16
SSM_ROW = SSM_CHUNK * SSM_GROUP
SSM_POWERS = 16
SCAN_INTERLEAVE = 8
CHUNK_PITCH = 24


def _scatter_pitch(stride):
    return CHUNK_PITCH if stride == SSM_CHUNK else stride


def _rows_by_class(val, stage, dil):
    rows, w = val.shape
    n = rows // dil
    pitch = _scatter_pitch(dil)
    n_col = w // LANES
    for c in range(n_col):
        col = val[:, c * LANES:(c + 1) * LANES]
        if pitch != dil:
            col = jnp.concatenate([col.reshape(n, dil, LANES), jnp.zeros((n, pitch - dil, LANES), F32)], axis=1)
            col = col.reshape(n * pitch, LANES)
        stage[c, 0:n * pitch, :] = col

    def get(r):
        return jnp.concatenate([stage[c, pl.ds(r, n, stride=pitch), :] for c in range(n_col)], axis=1)

    return get
LN_EPS = 1e-5

LANES = 128
SUBLANES = 8
VMEM_LIMIT_BYTES = 56 * 1024 * 1024


def _params(*semantics):
    return pltpu.CompilerParams(dimension_semantics=semantics, vmem_limit_bytes=VMEM_LIMIT_BYTES)


def _resident(shape):
    zeros = (0,) * len(shape)
    return pl.BlockSpec(shape, lambda *_: zeros, pipeline_mode=pl.Buffered(1))


def _layer_norm(y, g, b):
    mu = jnp.mean(y, axis=-1, keepdims=True)
    yc = y - mu
    var = jnp.mean(yc * yc, axis=-1, keepdims=True)
    return yc * lax.rsqrt(var + LN_EPS) * g + b


def _sigmoid(x):
    return 1.0 / (1.0 + jnp.exp(-x))


def _gelu_tanh(x):
    c = 0.7978845608028654
    return 0.5 * x * (1.0 + jnp.tanh(c * (x + 0.044715 * (x * x * x))))


def _ffn_ln_kernel(x_ref, wg_ref, wu_ref, wd_ref, g_ref, b_ref, o_ref, *, chunks, sub, alpha):
    for r0 in range(0, x_ref.shape[0], sub):
        x = x_ref[r0:r0 + sub, :]
        xb = x.astype(BF16)
        acc = None
        for c0, c1 in chunks:
            gate = jnp.dot(xb, wg_ref[:, c0:c1], preferred_element_type=F32)
            up = jnp.dot(xb, wu_ref[:, c0:c1], preferred_element_type=F32)
            h = (gate * _sigmoid(gate) * up).astype(BF16)
            part = jnp.dot(h, wd_ref[c0:c1, :], preferred_element_type=F32)
            acc = part if acc is None else acc + part
        y = alpha * x + 0.5 * acc
        o_ref[r0:r0 + sub, :] = _layer_norm(y, g_ref[...], b_ref[...])


def _ffn_chunks(d_ff, width):
    edges = list(range(0, d_ff, width)) + [d_ff]
    return tuple(zip(edges[:-1], edges[1:]))


def _ffn_ln(x, wg, wu, wd, g, b, *, alpha, tm=1024, sub=256, chunk=1024):
    t, d = x.shape
    d_ff = wg.shape[1]
    kern = functools.partial(_ffn_ln_kernel, chunks=_ffn_chunks(d_ff, chunk), sub=sub, alpha=alpha)
    return pl.pallas_call(
        kern,
        grid=(t // tm,),
        in_specs=[
            pl.BlockSpec((tm, d), lambda i: (i, 0)),
            _resident((d, d_ff)), _resident((d, d_ff)), _resident((d_ff, d)),
            _resident((1, d)), _resident((1, d)),
        ],
        out_specs=pl.BlockSpec((tm, d), lambda i: (i, 0)),
        out_shape=jax.ShapeDtypeStruct((t, d), F32),
        compiler_params=_params("parallel"),
        name="ffn_ln",
    )(x, wg.astype(BF16), wu.astype(BF16), wd.astype(BF16), g.reshape(1, d), b.reshape(1, d))


def _rope_spread():
    half = ROT_DIM // 2
    rows = lax.broadcasted_iota(jnp.int32, (ROPE_ROWS, 2 * LANES), 0)
    cols = lax.broadcasted_iota(jnp.int32, (ROPE_ROWS, 2 * LANES), 1)
    in_head = cols % HEAD_DIM
    freq = in_head % half
    is_cos = cols < LANES
    rot = in_head < ROT_DIM
    cos_part = is_cos & rot & (rows == freq)
    one_part = is_cos & jnp.logical_not(rot) & (rows == 2 * half)
    sin_part = jnp.logical_not(is_cos) & rot & (rows == half + freq)
    sign = jnp.where(in_head < half, 1.0, -1.0)
    return jnp.where(cos_part | one_part, 1.0, jnp.where(sin_part, sign, 0.0)).astype(BF16)


def _in_proj_kernel(x_ref, pos_ref, invf_ref, w_ref,
                    qkv0_ref, qkv1_ref, qkv2_ref, ga_ref, gs_ref, stage_ref, *, splits, sub):
    half = ROT_DIM // 2
    spread = _rope_spread()
    first = lax.broadcasted_iota(jnp.int32, (sub, LANES), 1) % HEAD_DIM < half
    s0, s1, s2, s3, s4, s5 = splits
    tn = (((0,), (0,)), ((), ()))

    for r0 in range(0, x_ref.shape[0], sub):
        xb = x_ref[r0:r0 + sub, :].astype(BF16)

        ang = invf_ref[...] * pos_ref[:, r0:r0 + sub].astype(F32)
        tab = jnp.concatenate([jnp.cos(ang), jnp.sin(ang), jnp.ones((half, sub), F32),
                               jnp.zeros((ROPE_ROWS - 3 * half, sub), F32)], axis=0)
        tab_hi = tab.astype(BF16)
        tab_lo = (tab - tab_hi.astype(F32)).astype(BF16)
        cs = (lax.dot_general(tab_hi, spread, tn, preferred_element_type=F32)
              + lax.dot_general(tab_lo, spread, tn, preferred_element_type=F32))
        cos = cs[:, :LANES]
        sin = cs[:, LANES:]

        def rotate(z):
            cols = []
            for c in range(z.shape[1] // LANES):
                zc = z[:, c * LANES:(c + 1) * LANES]
                zs = zc * sin
                up = pltpu.roll(jnp.where(first, zs, 0.0), half, axis=1)
                dn = pltpu.roll(jnp.where(first, 0.0, zs), LANES - half, axis=1)
                cols.append(zc * cos + up + dn)
            return jnp.concatenate(cols, axis=1)

        def proj(lo, hi):
            return jnp.dot(xb, w_ref[:, lo:hi], preferred_element_type=F32)

        q = rotate(proj(0, s0)) * (HEAD_DIM ** -0.5 * LOG2_E)
        k = rotate(proj(s0, s1))
        v = proj(s1, s2)
        for gi, out_ref in enumerate((qkv0_ref, qkv1_ref, qkv2_ref)):
            dil = ATTN_PATTERNS[gi][1]
            c0 = gi * GROUP_WIDTH
            qkv = jnp.concatenate([a[:, c0:c0 + GROUP_WIDTH] for a in (q, k, v)], axis=1)
            n = sub // dil
            i0 = r0 // dil
            if dil == 1:
                out_ref[0, i0:i0 + n, :] = qkv.astype(BF16)
            else:
                get = _rows_by_class(qkv, stage_ref.at[r0 // sub, gi - 1], dil)
                for r in range(dil):
                    out_ref[r, i0:i0 + n, :] = get(r).astype(BF16)
        ga_ref[r0:r0 + sub, :] = _sigmoid(proj(s3, s4)).astype(BF16)
        gs_ref[r0:r0 + sub, :] = _sigmoid(proj(s4, s5)).astype(BF16)


def _in_proj(x1, positions, w_in, *, batch, seq, ssm_width, tm=1024, sub=256):
    t, d = x1.shape
    splits = (ATTN_WIDTH, 2 * ATTN_WIDTH, 3 * ATTN_WIDTH, 3 * ATTN_WIDTH + ssm_width,
              3 * ATTN_WIDTH + ssm_width + d, 3 * ATTN_WIDTH + ssm_width + 2 * d)
    assert splits[-1] == w_in.shape[1] and seq % tm == 0
    half = ROT_DIM // 2
    invf = (ROPE_THETA ** (-jnp.arange(half, dtype=F32) * 2.0 / ROT_DIM)).reshape(half, 1)
    tiles = seq // tm
    qkv_w = 3 * GROUP_WIDTH
    qkv_specs = [pl.BlockSpec((None, dil, tm // dil, qkv_w), lambda i: (i // tiles, 0, i % tiles, 0))
                 for _, dil in ATTN_PATTERNS]
    qkv_shapes = [jax.ShapeDtypeStruct((batch, dil, seq // dil, qkv_w), BF16) for _, dil in ATTN_PATTERNS]
    widths = (d, d)
    kern = functools.partial(_in_proj_kernel, splits=splits, sub=sub)
    n_strided = sum(dil > 1 for _, dil in ATTN_PATTERNS)
    stage_rows = max(sub // dil * _scatter_pitch(dil) for _, dil in ATTN_PATTERNS)
    return pl.pallas_call(
        kern,
        grid=(t // tm,),
        in_specs=[
            pl.BlockSpec((tm, d), lambda i: (i, 0)),
            pl.BlockSpec((1, tm), lambda i: (0, i)),
            _resident((half, 1)),
            _resident(w_in.shape),
        ],
        out_specs=qkv_specs + [pl.BlockSpec((tm, w), lambda i: (i, 0)) for w in widths],
        out_shape=qkv_shapes + [jax.ShapeDtypeStruct((t, w), BF16) for w in widths],
        scratch_shapes=[pltpu.VMEM((tm // sub, n_strided, qkv_w // LANES, stage_rows, LANES), F32)],
        compiler_params=_params("parallel"),
        name="in_proj",
    )(x1, positions.reshape(1, t), invf, w_in)


def _attn_kernel(q_ref, kc_ref, kp_ref, vc_ref, vp_ref, o_ref, st_ref, *, n_sub):
    first = pl.program_id(2) == 0
    row = lax.broadcasted_iota(jnp.int32, (ATTN_BLK, 2 * ATTN_BLK), 0)
    col = lax.broadcasted_iota(jnp.int32, (ATTN_BLK, 2 * ATTN_BLK), 1)
    band = (col >= row) & (col <= row + ATTN_BLK)
    lane = lax.broadcasted_iota(jnp.int32, (ATTN_BLK, LANES), 1)
    lo_half = lane < HEAD_DIM

    n_pair = GROUP_WIDTH // LANES
    tiles = [(rc, j, hp, hh) for rc in range(q_ref.shape[0]) for j in range(n_sub)
             for hp in range(n_pair) for hh in range(2)]

    def window(ref_prev, ref_cur, j, c0):
        r0 = j * ATTN_BLK
        if j == 0:
            return jnp.concatenate([ref_prev[:, c0:c0 + LANES], ref_cur[0:ATTN_BLK, c0:c0 + LANES]], axis=0)
        return ref_cur[r0 - ATTN_BLK:r0 + ATTN_BLK, c0:c0 + LANES]

    def scores(tile):
        rc, j, hp, hh = tile
        q2 = q_ref[rc, j * ATTN_BLK:(j + 1) * ATTN_BLK, hp * LANES:(hp + 1) * LANES]
        qm = jnp.where(lo_half if hh == 0 else jnp.logical_not(lo_half), q2, jnp.zeros_like(q2))
        return lax.dot_general(qm, window(kp_ref.at[rc], kc_ref.at[rc], j, hp * LANES), (((1,), (1,)), ((), ())),
                               preferred_element_type=F32)

    def softmax(tile, s):
        valid = band & ((col >= ATTN_BLK) | jnp.logical_not(first)) if tile[1] == 0 else band
        s = jnp.where(valid, s, NEG_INF)
        m = jnp.max(s, axis=1, keepdims=True)
        p = jnp.exp2(s - m)
        return p.astype(BF16), m, jnp.sum(p, axis=1, keepdims=True)

    def weighted_values(tile, p, den):
        rc, j, hp, _ = tile
        return jnp.dot(p, window(vp_ref.at[rc], vc_ref.at[rc], j, hp * LANES), preferred_element_type=F32) / den

    s_q, p_q, outs = {}, {}, {}
    for step in range(len(tiles) + 2):
        if step < len(tiles):
            s_q[step] = scores(tiles[step])
        if 0 <= step - 1 < len(tiles):
            p_q[step - 1] = softmax(tiles[step - 1], s_q.pop(step - 1))
        if 0 <= step - 2 < len(tiles):
            t = step - 2
            rc, j, hp, hh = tiles[t]
            p, m, den = p_q.pop(t)
            outs[hh] = weighted_values(tiles[t], p, den)
            h = 2 * hp + hh
            rows = slice(j * ATTN_BLK, (j + 1) * ATTN_BLK)
            if h == 0:
                st_ref[rc, rows, :] = jnp.zeros((ATTN_BLK, LANES), F32)
            st_ref[rc, rows, h:h + 1] = m
            st_ref[rc, rows, HEADS_PER_GROUP + h:HEADS_PER_GROUP + h + 1] = den
            if hh == 1:
                o_ref[rc, rows, hp * LANES:(hp + 1) * LANES] = (
                    jnp.where(lo_half, outs[0], outs[1]).astype(BF16))


def _dilated_attention(qkv):
    batch, dilation, n, _ = qkv.shape
    rows_per_step = 2048
    qb = min(rows_per_step, n)
    n_sub = qb // ATTN_BLK
    rb = min(max(rows_per_step // qb, 1), dilation)

    def cur(part):
        return pl.BlockSpec((None, rb, qb, GROUP_WIDTH), lambda b, r, i: (b, r, i, part))

    def prev(part):
        return pl.BlockSpec((None, rb, ATTN_BLK, GROUP_WIDTH),
                            lambda b, r, i: (b, r, jnp.maximum(i * n_sub - 1, 0), part))

    return pl.pallas_call(
        functools.partial(_attn_kernel, n_sub=n_sub),
        grid=(batch, dilation // rb, n // qb),
        in_specs=[cur(0), cur(1), prev(1), cur(2), prev(2)],
        out_specs=[pl.BlockSpec((None, rb, qb, GROUP_WIDTH), lambda b, r, i: (b, r, i, 0)),
                   pl.BlockSpec((None, rb, qb, LANES), lambda b, r, i: (b, r, i, 0))],
        out_shape=[jax.ShapeDtypeStruct((batch, dilation, n, GROUP_WIDTH), BF16),
                   jax.ShapeDtypeStruct((batch, dilation, n, LANES), F32)],
        compiler_params=_params("parallel", "parallel", "parallel"),
        name=f"attn_d{dilation}",
    )(qkv, qkv, qkv, qkv, qkv)


def _ssm_in_kernel(*refs, n_grp, ch, n_w):
    x_refs, w_refs, ut_ref = refs[:-n_w - 1], refs[-n_w - 1:-1], refs[-1]
    w = jnp.concatenate([w_ref[...] for w_ref in w_refs], axis=1)
    for s in range(0, SSM_CHUNK, 2):
        xs = jnp.concatenate(
            [jnp.concatenate([x_ref[pl.ds(s + i, ch, stride=SSM_CHUNK), :] for x_ref in x_refs], axis=1)
             for i in range(2)], axis=0).astype(BF16)
        ut = lax.dot_general(w, xs, (((0,), (1,)), ((), ())), preferred_element_type=F32)
        for i in range(2):
            ut_ref[:, (s + i) * SSM_GROUP:(s + i + 1) * SSM_GROUP, :] = (
                ut[:, i * ch:(i + 1) * ch].reshape(n_grp, SSM_GROUP, ch).astype(BF16))


def _ssm_in(x1, w_in_b, u_lo, width, *, ch=LANES, wblk=256):
    t, d = x1.shape
    assert u_lo % wblk == 0 and width % wblk == 0
    n_grp = width // SSM_GROUP
    n_chunk_all = t // SSM_CHUNK
    tm = ch * SSM_CHUNK
    n_w = width // wblk
    slabs = [pl.BlockSpec((tm, LANES), lambda i, c=c: (i, c)) for c in range(d // LANES)]
    w_cols = [pl.BlockSpec((d, wblk), lambda i, c=c: (0, u_lo // wblk + c), pipeline_mode=pl.Buffered(1))
              for c in range(n_w)]
    return pl.pallas_call(
        functools.partial(_ssm_in_kernel, n_grp=n_grp, ch=ch, n_w=n_w),
        grid=(n_chunk_all // ch,),
        in_specs=slabs + w_cols,
        out_specs=pl.BlockSpec((n_grp, SSM_ROW, ch), lambda i: (0, 0, i)),
        out_shape=jax.ShapeDtypeStruct((n_grp, SSM_ROW, n_chunk_all), BF16),
        compiler_params=_params("parallel"),
        name="ssm_in",
    )(*([x1] * (d // LANES)), *([w_in_b] * n_w))


def _hdot(a, b):
    return jnp.dot(a, b, preferred_element_type=F32, precision=lax.Precision.HIGHEST)


def _cmul(ar, ai, br, bi):
    return ar * br - ai * bi, ar * bi + ai * br


def _cpowers(lbr, lbi, n, bits):
    out_r = jnp.ones(n.shape, F32)
    out_i = jnp.zeros(n.shape, F32)
    for b in range(bits):
        sel = (n & (1 << b)) != 0
        out_r, out_i = _cmul(out_r, out_i, jnp.where(sel, lbr, 1.0), jnp.where(sel, lbi, 0.0))
        lbr, lbi = _cmul(lbr, lbi, lbr, lbi)
    return out_r, out_i


def _ssm_prep_kernel(are_l, aim_l, ldt_ref, bre_ref, bim_ref, cre_ref, cim_ref,
                     tt_ref, win_ref, wout_ref, apr_ref, api_ref, *, gb):
    p = are_l.shape[2]
    eye = (lax.broadcasted_iota(jnp.int32, (p, p), 0) == lax.broadcasted_iota(jnp.int32, (p, p), 1)).astype(F32)
    expand = (lax.broadcasted_iota(jnp.int32, (SSM_CHUNK, SSM_ROW), 1) // SSM_GROUP
              == lax.broadcasted_iota(jnp.int32, (SSM_CHUNK, SSM_ROW), 0)).astype(F32)
    expand_t = (lax.broadcasted_iota(jnp.int32, (SSM_ROW, SSM_CHUNK), 0) // SSM_GROUP
                == lax.broadcasted_iota(jnp.int32, (SSM_ROW, SSM_CHUNK), 1)).astype(F32)
    rem = SSM_CHUNK - 1 - lax.broadcasted_iota(jnp.int32, (p, SSM_CHUNK), 1)
    tp1 = lax.broadcasted_iota(jnp.int32, (SSM_CHUNK, p), 0) + 1
    col = lax.broadcasted_iota(jnp.int32, (p, SSM_POWERS), 1)
    lane = lax.broadcasted_iota(jnp.int32, (SSM_GROUP, SSM_ROW), 1)
    chunk_bits = SSM_CHUNK.bit_length()

    for g in range(gb):
        dt = jnp.exp(ldt_ref[g])
        lam_r = jnp.minimum(are_l[g], -1e-4)
        lam_i = aim_l[g]
        mag_l = jnp.exp(lam_r * dt)
        lbr_l = mag_l * jnp.cos(lam_i * dt)
        lbi_l = mag_l * jnp.sin(lam_i * dt)
        inv = 1.0 / (lam_r * lam_r + lam_i * lam_i)
        cfr_l = ((lbr_l - 1.0) * lam_r + lbi_l * lam_i) * inv
        cfi_l = (lbi_l * lam_r - (lbr_l - 1.0) * lam_i) * inv
        stacked = jnp.concatenate([lbr_l, lbi_l, cfr_l, cfi_l, jnp.zeros((SUBLANES - 4, p), F32)], axis=0)
        cols = lax.dot_general(eye, stacked, (((1,), (1,)), ((), ())), preferred_element_type=F32,
                               precision=lax.Precision.HIGHEST)
        lbr, lbi = cols[:, 0:1], cols[:, 1:2]

        bre = bre_ref[g]
        bim = bim_ref[g]
        bbar_r, bbar_i = _cmul(cols[:, 2:3], cols[:, 3:4], bre, bim)
        pw_r, pw_i = _cpowers(lbr, lbi, rem, chunk_bits - 1)
        win_r, win_i = _cmul(_hdot(pw_r, expand), _hdot(pw_i, expand), bbar_r, bbar_i)
        win_ref[g] = jnp.concatenate([win_r, win_i], axis=0).astype(BF16)

        a_r, a_i = _cmul(pw_r[:, 0:1], pw_i[:, 0:1], lbr, lbi)
        apr = jnp.zeros((p, SSM_POWERS), F32)
        api = jnp.zeros((p, SSM_POWERS), F32)
        for j in range(SSM_POWERS):
            apr = jnp.where(col == j, a_r, apr)
            api = jnp.where(col == j, a_i, api)
            a_r, a_i = _cmul(a_r, a_i, a_r, a_i)
        apr_ref[g] = apr
        api_ref[g] = api

        e_r, e_i = _cpowers(lbr_l, lbi_l, tp1, chunk_bits)
        cre = cre_ref[g]
        cim = cim_ref[g]
        wo_r, wo_i = _cmul(_hdot(expand_t, e_r), _hdot(expand_t, e_i), cre, cim)
        wout_ref[g] = jnp.concatenate([wo_r, -wo_i], axis=1).astype(BF16)

        hrev = _hdot(cre[0:SSM_GROUP, :], win_r) - _hdot(cim[0:SSM_GROUP, :], win_i)
        blocks = []
        for t in range(SSM_CHUNK):
            hi = (t + 1) * SSM_GROUP
            rolled = hrev if hi == SSM_ROW else pltpu.roll(hrev, hi, axis=1)
            blocks.append(jnp.where(lane < hi, rolled, 0.0))
        tt_ref[g] = jnp.concatenate(blocks, axis=0).astype(BF16)


def _ssm_prep(a_re, a_im, log_dt, b_re, b_im, c_re, c_im, *, gb=4):
    g, p = a_re.shape
    row = SSM_ROW
    blk = lambda shape: pl.BlockSpec((gb,) + shape, lambda i: (i, 0, 0))
    b_t = lambda b: jnp.tile(b, (1, 1, SSM_CHUNK))
    c_t = lambda c: jnp.tile(c, (1, SSM_CHUNK, 1))
    out_shapes = [
        jax.ShapeDtypeStruct((g, row, row), BF16),
        jax.ShapeDtypeStruct((g, 2 * p, row), BF16),
        jax.ShapeDtypeStruct((g, row, 2 * p), BF16),
        jax.ShapeDtypeStruct((g, p, SSM_POWERS), F32), jax.ShapeDtypeStruct((g, p, SSM_POWERS), F32),
    ]
    return pl.pallas_call(
        functools.partial(_ssm_prep_kernel, gb=gb),
        grid=(g // gb,),
        in_specs=[blk((1, p)), blk((1, p)), blk((1, 1)),
                  blk((p, row)), blk((p, row)), blk((row, p)), blk((row, p))],
        out_specs=[blk(s.shape[1:]) for s in out_shapes],
        out_shape=out_shapes,
        compiler_params=_params("parallel"),
        name="ssm_prep",
    )(a_re.reshape(g, 1, p), a_im.reshape(g, 1, p),
      log_dt.reshape(g, 1, 1), b_t(b_re), b_t(b_im), c_t(c_re), c_t(c_im))


def _ssm_kernel(ut_ref, tt_ref, win_ref, wout_ref, apr_ref, api_ref, dsk_ref, y_ref, *, gb, n_chunk):
    nk = ut_ref.shape[-1]
    p = apr_ref.shape[1]
    pos = lax.broadcasted_iota(jnp.int32, (SUBLANES, nk), 1) % n_chunk
    n_steps = (n_chunk - 1).bit_length()
    assert n_steps <= SSM_POWERS

    def shift_whole_tiles(v, sh):
        segs = []
        for b0 in range(0, nk, n_chunk):
            segs += [jnp.zeros((v.shape[0], sh), F32), v[:, b0:b0 + n_chunk - sh]]
        return jnp.concatenate(segs, axis=1)

    for g in range(gb):
        ut = ut_ref[g]
        sl = jnp.dot(win_ref[g], ut, preferred_element_type=F32)
        apr = apr_ref[g]
        api = api_ref[g]
        prev_r, prev_i = [], []
        for b0 in range(0, p, SUBLANES * SCAN_INTERLEAVE):
            tiles = range(b0, b0 + SUBLANES * SCAN_INTERLEAVE, SUBLANES)
            s = {r0: (sl[r0:r0 + SUBLANES], sl[p + r0:p + r0 + SUBLANES]) for r0 in tiles}
            for j in range(n_steps):
                sh = 1 << j
                keep = pos >= sh
                for r0 in tiles:
                    s_r, s_i = s[r0]
                    a_r = apr[r0:r0 + SUBLANES, j:j + 1]
                    a_i = api[r0:r0 + SUBLANES, j:j + 1]
                    if sh % LANES == 0:
                        p_r, p_i = shift_whole_tiles(s_r, sh), shift_whole_tiles(s_i, sh)
                    else:
                        p_r = jnp.where(keep, pltpu.roll(s_r, sh, axis=1), 0.0)
                        p_i = jnp.where(keep, pltpu.roll(s_i, sh, axis=1), 0.0)
                    s[r0] = (s_r + a_r * p_r - a_i * p_i, s_i + a_r * p_i + a_i * p_r)
            keep = pos >= 1
            for r0 in tiles:
                prev_r.append(jnp.where(keep, pltpu.roll(s[r0][0], 1, axis=1), 0.0))
                prev_i.append(jnp.where(keep, pltpu.roll(s[r0][1], 1, axis=1), 0.0))
        prev = jnp.concatenate(prev_r + prev_i, axis=0).astype(BF16)
        y = jnp.dot(tt_ref[g], ut, preferred_element_type=F32)
        y += jnp.dot(wout_ref[g], prev, preferred_element_type=F32)
        y += dsk_ref[g] * ut.astype(F32)
        y_ref[g] = _gelu_tanh(y).astype(BF16)


def _ssm(ut, prep, dsk, *, n_chunk, gb=4):
    tt, win, wout, apr, api = prep
    g, row, nk = ut.shape
    p = apr.shape[1]
    blk = lambda shape: pl.BlockSpec((gb,) + shape, lambda i: (i, 0, 0))
    kern = functools.partial(_ssm_kernel, gb=gb, n_chunk=n_chunk)
    return pl.pallas_call(
        kern,
        grid=(g // gb,),
        in_specs=[blk((row, nk)), blk((row, row)), blk((2 * p, row)), blk((row, 2 * p)),
                  blk((p, SSM_POWERS)), blk((p, SSM_POWERS)), blk((row, 1))],
        out_specs=blk((row, nk)),
        out_shape=jax.ShapeDtypeStruct((g, row, nk), BF16),
        compiler_params=_params("parallel"),
        name="ssm",
    )(ut, tt, win, wout, apr, api, dsk)


def _ssm_out_kernel(yt_ref, wglu_ref, bglu_ref, wso_ref, o_ref, *, t_per_pass):
    n_grp, _, ch = yt_ref.shape
    width = n_grp * SSM_GROUP
    def gated(t0):
        y = jnp.concatenate(
            [yt_ref[:, t * SSM_GROUP:(t + 1) * SSM_GROUP, :].reshape(width, ch)
             for t in range(t0, t0 + t_per_pass)], axis=1)
        return y, jnp.dot(wglu_ref[...], y, preferred_element_type=F32) + bglu_ref[...]

    def project(t0, y, gate):
        glu = (y.astype(F32) * _sigmoid(gate)).astype(BF16)
        yo = lax.dot_general(glu, wso_ref[...], (((0,), (0,)), ((), ())), preferred_element_type=F32)
        for i in range(t_per_pass):
            o_ref[t0 + i] = yo[i * ch:(i + 1) * ch, :].astype(BF16)

    starts = list(range(0, SSM_CHUNK, t_per_pass))
    pending = gated(starts[0])
    for n, t0 in enumerate(starts):
        nxt = gated(starts[n + 1]) if n + 1 < len(starts) else None
        project(t0, *pending)
        pending = nxt


def _ssm_out(yt, w_glu, b_glu, w_ssm_out, *, ch=LANES, t_per_pass=4):
    n_grp, row, n_chunk_all = yt.shape
    width, d = w_ssm_out.shape
    return pl.pallas_call(
        functools.partial(_ssm_out_kernel, t_per_pass=t_per_pass),
        grid=(n_chunk_all // ch,),
        in_specs=[pl.BlockSpec((n_grp, row, ch), lambda i: (0, 0, i)),
                  _resident((width, width)), _resident((width, 1)), _resident((width, d))],
        out_specs=pl.BlockSpec((None, SSM_CHUNK, ch, d), lambda i: (i, 0, 0, 0)),
        out_shape=jax.ShapeDtypeStruct((n_chunk_all // ch, SSM_CHUNK, ch, d), BF16),
        compiler_params=_params("parallel"),
        name="ssm_out",
    )(yt, w_glu.T.astype(BF16), b_glu.reshape(width, 1), w_ssm_out.astype(BF16))


def _mix_ffn_kernel(x_ref, o0, o1, o2, st0, st1, st2, ys_ref, ga_ref, gs_ref,
                    wao_ref, wo_ref, g_ref, b_ref, wg_ref, wu_ref, wd_ref, g3_ref, b3_ref,
                    out_ref, o_stage, st_stage, ys_stage, x2_ref, *, sub, chunks, alpha):
    step = pl.program_id(0)
    slot_w = step % 2
    slot_r = 1 - slot_w

    @pl.when(step == 0)
    def _():
        x2_ref[...] = jnp.zeros(x2_ref.shape, F32)

    lane = lax.broadcasted_iota(jnp.int32, (sub, LANES), 1)
    lo_half = lane < HEAD_DIM

    def token_order(ref, stage, r0):
        dil = ref.shape[0]
        n = sub // dil
        i0 = r0 // dil
        if dil == 1:
            return ref[0, i0:i0 + n, :].astype(F32)
        n_col = ref.shape[2] // LANES
        pitch = _scatter_pitch(dil)
        for r in range(dil):
            blk = ref[r, i0:i0 + n, :].astype(F32)
            for c in range(n_col):
                stage[c, pl.ds(r, n, stride=pitch), :] = blk[:, c * LANES:(c + 1) * LANES]
        cols = [stage[c, 0:n * pitch, :] for c in range(n_col)]
        if pitch != dil:
            cols = [a.reshape(n, pitch, LANES)[:, :dil, :].reshape(sub, LANES) for a in cols]
        return jnp.concatenate(cols, axis=1)

    def head_cols(vals):
        cols = []
        for hp in range(GROUP_WIDTH // LANES):
            cols.append(jnp.where(lo_half, vals[2 * hp], vals[2 * hp + 1]))
        return jnp.concatenate(cols, axis=1)

    def merge_groups(k):
        r0 = k * sub
        sts = [token_order(st, st_stage.at[k, i], r0) for i, st in enumerate((st0, st1, st2))]
        outs = [token_order(o, o_stage.at[k, i], r0) for i, o in enumerate((o0, o1, o2))]
        dens = [pltpu.roll(st, LANES - HEADS_PER_GROUP, axis=1) for st in sts]
        mx = jnp.maximum(jnp.maximum(sts[0], sts[1]), sts[2])
        w = [den * jnp.exp2(st - mx) for den, st in zip(dens, sts)]
        tot = jnp.where(lane < HEADS_PER_GROUP, w[0] + w[1] + w[2], 1.0)
        att = None
        for g in range(N_ATTN_GROUPS):
            wt = w[g] / tot
            term = head_cols([wt[:, h:h + 1] for h in range(HEADS_PER_GROUP)]) * outs[g]
            att = term if att is None else att + term
        return att.astype(BF16)

    def gate(k, att):
        rows = slice(k * sub, (k + 1) * sub)
        y_attn = jnp.dot(att, wao_ref[...], preferred_element_type=F32)
        y_ssm = token_order(ys_ref, ys_stage.at[k], k * sub)
        merged = ga_ref[rows, :].astype(F32) * y_attn + gs_ref[rows, :].astype(F32) * y_ssm
        return merged.astype(BF16)

    def project(k, merged):
        rows = slice(k * sub, (k + 1) * sub)
        mix = jnp.dot(merged, wo_ref[...], preferred_element_type=F32)
        x2_ref[slot_w, rows, :] = _layer_norm(alpha * x_ref[rows, :] + mix, g_ref[...], b_ref[...])

    x2_prev = x2_ref[slot_r]
    ffn_state = {}

    def ffn_gate_up(k, c):
        rows = slice(k * sub, (k + 1) * sub)
        if c == chunks[0]:
            ffn_state[k] = {"xb": x2_prev[rows, :].astype(BF16), "acc": None}
        xb = ffn_state[k]["xb"]
        ffn_state[k]["gu"] = (jnp.dot(xb, wg_ref[:, c[0]:c[1]], preferred_element_type=F32),
                              jnp.dot(xb, wu_ref[:, c[0]:c[1]], preferred_element_type=F32))

    def ffn_down(k, c):
        rows = slice(k * sub, (k + 1) * sub)
        gate_v, up = ffn_state[k].pop("gu")
        h = (gate_v * _sigmoid(gate_v) * up).astype(BF16)
        part = jnp.dot(h, wd_ref[c[0]:c[1], :], preferred_element_type=F32)
        acc = ffn_state[k]["acc"]
        ffn_state[k]["acc"] = part if acc is None else acc + part
        if c == chunks[-1]:
            y = alpha * x2_prev[rows, :] + 0.5 * ffn_state.pop(k)["acc"]
            out_ref[rows, :] = _layer_norm(y, g3_ref[...], b3_ref[...])

    n_sub = x_ref.shape[0] // sub
    ffn_stages = [f for k in range(n_sub) for c in chunks
                  for f in (functools.partial(ffn_gate_up, k, c), functools.partial(ffn_down, k, c))]
    mix_stages = [f for k in range(n_sub) for f in (functools.partial(merge_groups, k),
                                                    functools.partial(gate, k),
                                                    functools.partial(project, k))]
    carry = ()
    for f in ffn_stages:
        f()
        if mix_stages:
            out = mix_stages.pop(0)(*carry)
            carry = () if out is None else (out,)
    assert not mix_stages


def _mix_ffn(x1, os_, sts, y_ssm, ga, gs, w_attn_out, w_o, g2, b2, wg, wu, wd, g3, b3, *,
             seq, alpha, tm=512, sub=256, chunk=1024):
    t, d = x1.shape
    d_ff = wg.shape[1]
    tiles = seq // tm
    n_tiles = t // tm
    cur = lambda i: jnp.minimum(i, n_tiles - 1)
    tile = lambda w: pl.BlockSpec((tm, w), lambda i: (cur(i), 0))

    def by_class(a):
        _, dil, n, w = a.shape
        return pl.BlockSpec((None, dil, tm // dil, w), lambda i: (cur(i) // tiles, 0, cur(i) % tiles, 0))

    n_grp = len(os_)
    stage_rows = max(sub // a.shape[1] * _scatter_pitch(a.shape[1]) for a in os_)
    per_big = y_ssm.shape[2] * SSM_CHUNK // tm
    ys_spec = pl.BlockSpec((None, SSM_CHUNK, tm // SSM_CHUNK, d),
                           lambda i: (cur(i) // per_big, 0, cur(i) % per_big, 0))
    ys_rows = sub // SSM_CHUNK * _scatter_pitch(SSM_CHUNK)
    vec = lambda v: v.reshape(1, d)
    return pl.pallas_call(
        functools.partial(_mix_ffn_kernel, sub=sub, chunks=_ffn_chunks(d_ff, chunk), alpha=alpha),
        grid=(n_tiles + 1,),
        in_specs=[tile(d)] + [by_class(a) for a in os_] + [by_class(a) for a in sts] + [ys_spec, tile(d), tile(d),
                  _resident(w_attn_out.shape), _resident(w_o.shape), _resident((1, d)), _resident((1, d)),
                  _resident((d, d_ff)), _resident((d, d_ff)), _resident((d_ff, d)),
                  _resident((1, d)), _resident((1, d))],
        out_specs=pl.BlockSpec((tm, d), lambda i: (jnp.maximum(i - 1, 0), 0)),
        out_shape=jax.ShapeDtypeStruct((t, d), F32),
        scratch_shapes=[pltpu.VMEM((tm // sub, n_grp, GROUP_WIDTH // LANES, stage_rows, LANES), F32),
                        pltpu.VMEM((tm // sub, n_grp, 1, stage_rows, LANES), F32),
                        pltpu.VMEM((tm // sub, d // LANES, ys_rows, LANES), F32),
                        pltpu.VMEM((2, tm, d), F32)],
        compiler_params=_params("arbitrary"),
        name="mix_ffn",
    )(x1, *os_, *sts, y_ssm, ga, gs, w_attn_out.astype(BF16), w_o.astype(BF16), vec(g2), vec(b2),
      wg.astype(BF16), wu.astype(BF16), wd.astype(BF16), vec(g3), vec(b3))


def _layer(x, positions, w_in, w_attn_out, a_re, a_im, log_dt, b_re, b_im, c_re, c_im, d_skip,
           w_glu, b_glu, w_ssm_out, w_o, ffn1, ffn2, ln1, ln2, ln3, *, alpha):
    batch, seq, d = x.shape
    t = batch * seq
    n_grp = a_re.shape[0]
    ssm_width = n_grp * SSM_GROUP
    u_lo = 3 * ATTN_WIDTH

    x1 = _ffn_ln(x.reshape(t, d), *ffn1, *ln1, alpha=alpha)
    w_in_b = w_in.astype(BF16)
    *qkvs, ga, gs = _in_proj(x1, positions, w_in_b, batch=batch, seq=seq, ssm_width=ssm_width)

    os_, sts = [], []
    for qkv, (window, dilation) in zip(qkvs, ATTN_PATTERNS):
        assert window == ATTN_BLK * dilation
        o, st = _dilated_attention(qkv)
        os_.append(o)
        sts.append(st)

    ut = _ssm_in(x1, w_in_b, u_lo, ssm_width)
    prep = _ssm_prep(a_re, a_im, log_dt, b_re, b_im, c_re, c_im)
    dsk = jnp.tile(d_skip.reshape(n_grp, 1, SSM_GROUP), (1, SSM_CHUNK, 1)).reshape(n_grp, SSM_ROW, 1)
    yt = _ssm(ut, prep, dsk, n_chunk=seq // SSM_CHUNK)
    y_ssm = _ssm_out(yt, w_glu, b_glu, w_ssm_out)

    x3 = _mix_ffn(x1, os_, sts, y_ssm, ga, gs, w_attn_out, w_o, *ln2, *ffn2, *ln3, seq=seq, alpha=alpha)
    return x3.reshape(batch, seq, d)


def kernel(x, positions, w_in, w_attn_out, a_re, a_im, log_dt, b_re, b_im, c_re, c_im, d_skip, w_glu, b_glu, w_ssm_out, w_o, ffn1_wg, ffn1_wu, ffn1_wd, ffn2_wg, ffn2_wu, ffn2_wd, ln1_g, ln1_b, ln2_g, ln2_b, ln3_g, ln3_b):
    depth = w_in.shape[0]
    alpha = (2.0 * depth) ** 0.25
    for i in range(depth):
        x = _layer(x, positions, w_in[i], w_attn_out[i], a_re[i], a_im[i], log_dt[i], b_re[i], b_im[i],
                   c_re[i], c_im[i], d_skip[i], w_glu[i], b_glu[i], w_ssm_out[i], w_o[i],
                   (ffn1_wg[i], ffn1_wu[i], ffn1_wd[i]), (ffn2_wg[i], ffn2_wu[i], ffn2_wd[i]),
                   (ln1_g[i], ln1_b[i]), (ln2_g[i], ln2_b[i]), (ln3_g[i], ln3_b[i]), alpha=alpha)
    return x
```

```python
import functools

import jax
import jax.numpy as jnp
from jax import lax
from jax.experimental import pallas as pl
from jax.experimental.pallas import tpu as pltpu

F32 = jnp.float32
BF16 = jnp.bfloat16

HEAD_DIM = 64
HEADS_PER_GROUP = 4
GROUP_WIDTH = HEADS_PER_GROUP * HEAD_DIM
ATTN_PATTERNS = ((128, 1), (512, 4), (2048, 16))
N_ATTN_GROUPS = len(ATTN_PATTERNS)
ATTN_WIDTH = N_ATTN_GROUPS * GROUP_WIDTH
ROT_DIM = HEAD_DIM // 4
ROPE_THETA = 500000.0
ROPE_ROWS = 32
ATTN_BLK = 128
NEG_INF = -1e30
LOG2_E = 1.4426950408889634
SSM_GROUP = 16
SSM_CHUNK = 16
SSM_ROW = SSM_CHUNK * SSM_GROUP
SSM_POWERS = 16
SCAN_INTERLEAVE = 8
CHUNK_PITCH = 24


def _scatter_pitch(stride):
    return CHUNK_PITCH if stride == SSM_CHUNK else stride


def _rows_by_class(val, stage, dil):
    rows, w = val.shape
    n = rows // dil
    pitch = _scatter_pitch(dil)
    n_col = w // LANES
    for c in range(n_col):
        col = val[:, c * LANES:(c + 1) * LANES]
        if pitch != dil:
            col = jnp.concatenate([col.reshape(n, dil, LANES), jnp.zeros((n, pitch - dil, LANES), F32)], axis=1)
            col = col.reshape(n * pitch, LANES)
        stage[c, 0:n * pitch, :] = col

    def get(r):
        return jnp.concatenate([stage[c, pl.ds(r, n, stride=pitch), :] for c in range(n_col)], axis=1)

    return get
LN_EPS = 1e-5

LANES = 128
SUBLANES = 8
VMEM_LIMIT_BYTES = 56 * 1024 * 1024


def _params(*semantics):
    return pltpu.CompilerParams(dimension_semantics=semantics, vmem_limit_bytes=VMEM_LIMIT_BYTES)


def _resident(shape):
    zeros = (0,) * len(shape)
    return pl.BlockSpec(shape, lambda *_: zeros, pipeline_mode=pl.Buffered(1))


def _layer_norm(y, g, b):
    mu = jnp.mean(y, axis=-1, keepdims=True)
    yc = y - mu
    var = jnp.mean(yc * yc, axis=-1, keepdims=True)
    return yc * lax.rsqrt(var + LN_EPS) * g + b


def _sigmoid(x):
    return 1.0 / (1.0 + jnp.exp(-x))


def _gelu_tanh(x):
    c = 0.7978845608028654
    return 0.5 * x * (1.0 + jnp.tanh(c * (x + 0.044715 * (x * x * x))))


def _ffn_ln_kernel(x_ref, wg_ref, wu_ref, wd_ref, g_ref, b_ref, o_ref, *, chunks, sub, alpha):
    for r0 in range(0, x_ref.shape[0], sub):
        x = x_ref[r0:r0 + sub, :]
        xb = x.astype(BF16)
        acc = None
        for c0, c1 in chunks:
            gate = jnp.dot(xb, wg_ref[:, c0:c1], preferred_element_type=F32)
            up = jnp.dot(xb, wu_ref[:, c0:c1], preferred_element_type=F32)
            h = (gate * _sigmoid(gate) * up).astype(BF16)
            part = jnp.dot(h, wd_ref[c0:c1, :], preferred_element_type=F32)
            acc = part if acc is None else acc + part
        y = alpha * x + 0.5 * acc
        o_ref[r0:r0 + sub, :] = _layer_norm(y, g_ref[...], b_ref[...])


def _ffn_chunks(d_ff, width):
    edges = list(range(0, d_ff, width)) + [d_ff]
    return tuple(zip(edges[:-1], edges[1:]))


def _ffn_ln(x, wg, wu, wd, g, b, *, alpha, tm=1024, sub=256, chunk=1024):
    t, d = x.shape
    d_ff = wg.shape[1]
    kern = functools.partial(_ffn_ln_kernel, chunks=_ffn_chunks(d_ff, chunk), sub=sub, alpha=alpha)
    return pl.pallas_call(
        kern,
        grid=(t // tm,),
        in_specs=[
            pl.BlockSpec((tm, d), lambda i: (i, 0)),
            _resident((d, d_ff)), _resident((d, d_ff)), _resident((d_ff, d)),
            _resident((1, d)), _resident((1, d)),
        ],
        out_specs=pl.BlockSpec((tm, d), lambda i: (i, 0)),
        out_shape=jax.ShapeDtypeStruct((t, d), F32),
        compiler_params=_params("parallel"),
        name="ffn_ln",
    )(x, wg.astype(BF16), wu.astype(BF16), wd.astype(BF16), g.reshape(1, d), b.reshape(1, d))


def _rope_spread():
    half = ROT_DIM // 2
    rows = lax.broadcasted_iota(jnp.int32, (ROPE_ROWS, 2 * LANES), 0)
    cols = lax.broadcasted_iota(jnp.int32, (ROPE_ROWS, 2 * LANES), 1)
    in_head = cols % HEAD_DIM
    freq = in_head % half
    is_cos = cols < LANES
    rot = in_head < ROT_DIM
    cos_part = is_cos & rot & (rows == freq)
    one_part = is_cos & jnp.logical_not(rot) & (rows == 2 * half)
    sin_part = jnp.logical_not(is_cos) & rot & (rows == half + freq)
    sign = jnp.where(in_head < half, 1.0, -1.0)
    return jnp.where(cos_part | one_part, 1.0, jnp.where(sin_part, sign, 0.0)).astype(BF16)


def _in_proj_kernel(x_ref, pos_ref, invf_ref, w_ref,
                    qkv0_ref, qkv1_ref, qkv2_ref, ga_ref, gs_ref, stage_ref, *, splits, sub):
    half = ROT_DIM // 2
    spread = _rope_spread()
    first = lax.broadcasted_iota(jnp.int32, (sub, LANES), 1) % HEAD_DIM < half
    s0, s1, s2, s3, s4, s5 = splits
    tn = (((0,), (0,)), ((), ()))

    for r0 in range(0, x_ref.shape[0], sub):
        xb = x_ref[r0:r0 + sub, :].astype(BF16)

        ang = invf_ref[...] * pos_ref[:, r0:r0 + sub].astype(F32)
        tab = jnp.concatenate([jnp.cos(ang), jnp.sin(ang), jnp.ones((half, sub), F32),
                               jnp.zeros((ROPE_ROWS - 3 * half, sub), F32)], axis=0)
        tab_hi = tab.astype(BF16)
        tab_lo = (tab - tab_hi.astype(F32)).astype(BF16)
        cs = (lax.dot_general(tab_hi, spread, tn, preferred_element_type=F32)
              + lax.dot_general(tab_lo, spread, tn, preferred_element_type=F32))
        cos = cs[:, :LANES]
        sin = cs[:, LANES:]

        def rotate(z):
            cols = []
            for c in range(z.shape[1] // LANES):
                zc = z[:, c * LANES:(c + 1) * LANES]
                zs = zc * sin
                up = pltpu.roll(jnp.where(first, zs, 0.0), half, axis=1)
                dn = pltpu.roll(jnp.where(first, 0.0, zs), LANES - half, axis=1)
                cols.append(zc * cos + up + dn)
            return jnp.concatenate(cols, axis=1)

        def proj(lo, hi):
            return jnp.dot(xb, w_ref[:, lo:hi], preferred_element_type=F32)

        q = rotate(proj(0, s0)) * (HEAD_DIM ** -0.5 * LOG2_E)
        k = rotate(proj(s0, s1))
        v = proj(s1, s2)
        for gi, out_ref in enumerate((qkv0_ref, qkv1_ref, qkv2_ref)):
            dil = ATTN_PATTERNS[gi][1]
            c0 = gi * GROUP_WIDTH
            qkv = jnp.concatenate([a[:, c0:c0 + GROUP_WIDTH] for a in (q, k, v)], axis=1)
            n = sub // dil
            i0 = r0 // dil
            if dil == 1:
                out_ref[0, i0:i0 + n, :] = qkv.astype(BF16)
            else:
                get = _rows_by_class(qkv, stage_ref.at[r0 // sub, gi - 1], dil)
                for r in range(dil):
                    out_ref[r, i0:i0 + n, :] = get(r).astype(BF16)
        ga_ref[r0:r0 + sub, :] = _sigmoid(proj(s3, s4)).astype(BF16)
        gs_ref[r0:r0 + sub, :] = _sigmoid(proj(s4, s5)).astype(BF16)


def _in_proj(x1, positions, w_in, *, batch, seq, ssm_width, tm=1024, sub=256):
    t, d = x1.shape
    splits = (ATTN_WIDTH, 2 * ATTN_WIDTH, 3 * ATTN_WIDTH, 3 * ATTN_WIDTH + ssm_width,
              3 * ATTN_WIDTH + ssm_width + d, 3 * ATTN_WIDTH + ssm_width + 2 * d)
    assert splits[-1] == w_in.shape[1] and seq % tm == 0
    half = ROT_DIM // 2
    invf = (ROPE_THETA ** (-jnp.arange(half, dtype=F32) * 2.0 / ROT_DIM)).reshape(half, 1)
    tiles = seq // tm
    qkv_w = 3 * GROUP_WIDTH
    qkv_specs = [pl.BlockSpec((None, dil, tm // dil, qkv_w), lambda i: (i // tiles, 0, i % tiles, 0))
                 for _, dil in ATTN_PATTERNS]
    qkv_shapes = [jax.ShapeDtypeStruct((batch, dil, seq // dil, qkv_w), BF16) for _, dil in ATTN_PATTERNS]
    widths = (d, d)
    kern = functools.partial(_in_proj_kernel, splits=splits, sub=sub)
    n_strided = sum(dil > 1 for _, dil in ATTN_PATTERNS)
    stage_rows = max(sub // dil * _scatter_pitch(dil) for _, dil in ATTN_PATTERNS)
    return pl.pallas_call(
        kern,
        grid=(t // tm,),
        in_specs=[
            pl.BlockSpec((tm, d), lambda i: (i, 0)),
            pl.BlockSpec((1, tm), lambda i: (0, i)),
            _resident((half, 1)),
            _resident(w_in.shape),
        ],
        out_specs=qkv_specs + [pl.BlockSpec((tm, w), lambda i: (i, 0)) for w in widths],
        out_shape=qkv_shapes + [jax.ShapeDtypeStruct((t, w), BF16) for w in widths],
        scratch_shapes=[pltpu.VMEM((tm // sub, n_strided, qkv_w // LANES, stage_rows, LANES), F32)],
        compiler_params=_params("parallel"),
        name="in_proj",
    )(x1, positions.reshape(1, t), invf, w_in)


def _attn_kernel(q_ref, kc_ref, kp_ref, vc_ref, vp_ref, o_ref, st_ref, *, n_sub):
    first = pl.program_id(2) == 0
    row = lax.broadcasted_iota(jnp.int32, (ATTN_BLK, 2 * ATTN_BLK), 0)
    col = lax.broadcasted_iota(jnp.int32, (ATTN_BLK, 2 * ATTN_BLK), 1)
    band = (col >= row) & (col <= row + ATTN_BLK)
    lane = lax.broadcasted_iota(jnp.int32, (ATTN_BLK, LANES), 1)
    lo_half = lane < HEAD_DIM

    n_pair = GROUP_WIDTH // LANES
    tiles = [(rc, j, hp, hh) for rc in range(q_ref.shape[0]) for j in range(n_sub)
             for hp in range(n_pair) for hh in range(2)]

    def window(ref_prev, ref_cur, j, c0):
        r0 = j * ATTN_BLK
        if j == 0:
            return jnp.concatenate([ref_prev[:, c0:c0 + LANES], ref_cur[0:ATTN_BLK, c0:c0 + LANES]], axis=0)
        return ref_cur[r0 - ATTN_BLK:r0 + ATTN_BLK, c0:c0 + LANES]

    def scores(tile):
        rc, j, hp, hh = tile
        q2 = q_ref[rc, j * ATTN_BLK:(j + 1) * ATTN_BLK, hp * LANES:(hp + 1) * LANES]
        qm = jnp.where(lo_half if hh == 0 else jnp.logical_not(lo_half), q2, jnp.zeros_like(q2))
        return lax.dot_general(qm, window(kp_ref.at[rc], kc_ref.at[rc], j, hp * LANES), (((1,), (1,)), ((), ())),
                               preferred_element_type=F32)

    def softmax(tile, s):
        valid = band & ((col >= ATTN_BLK) | jnp.logical_not(first)) if tile[1] == 0 else band
        s = jnp.where(valid, s, NEG_INF)
        m = jnp.max(s, axis=1, keepdims=True)
        p = jnp.exp2(s - m)
        return p.astype(BF16), m, jnp.sum(p, axis=1, keepdims=True)

    def weighted_values(tile, p, den):
        rc, j, hp, _ = tile
        return jnp.dot(p, window(vp_ref.at[rc], vc_ref.at[rc], j, hp * LANES), preferred_element_type=F32) / den

    s_q, p_q, outs = {}, {}, {}
    for step in range(len(tiles) + 2):
        if step < len(tiles):
            s_q[step] = scores(tiles[step])
        if 0 <= step - 1 < len(tiles):
            p_q[step - 1] = softmax(tiles[step - 1], s_q.pop(step - 1))
        if 0 <= step - 2 < len(tiles):
            t = step - 2
            rc, j, hp, hh = tiles[t]
            p, m, den = p_q.pop(t)
            outs[hh] = weighted_values(tiles[t], p, den)
            h = 2 * hp + hh
            rows = slice(j * ATTN_BLK, (j + 1) * ATTN_BLK)
            if h == 0:
                st_ref[rc, rows, :] = jnp.zeros((ATTN_BLK, LANES), F32)
            st_ref[rc, rows, h:h + 1] = m
            st_ref[rc, rows, HEADS_PER_GROUP + h:HEADS_PER_GROUP + h + 1] = den
            if hh == 1:
                o_ref[rc, rows, hp * LANES:(hp + 1) * LANES] = (
                    jnp.where(lo_half, outs[0], outs[1]).astype(BF16))


def _dilated_attention(qkv):
    batch, dilation, n, _ = qkv.shape
    rows_per_step = 2048
    qb = min(rows_per_step, n)
    n_sub = qb // ATTN_BLK
    rb = min(max(rows_per_step // qb, 1), dilation)

    def cur(part):
        return pl.BlockSpec((None, rb, qb, GROUP_WIDTH), lambda b, r, i: (b, r, i, part))

    def prev(part):
        return pl.BlockSpec((None, rb, ATTN_BLK, GROUP_WIDTH),
                            lambda b, r, i: (b, r, jnp.maximum(i * n_sub - 1, 0), part))

    return pl.pallas_call(
        functools.partial(_attn_kernel, n_sub=n_sub),
        grid=(batch, dilation // rb, n // qb),
        in_specs=[cur(0), cur(1), prev(1), cur(2), prev(2)],
        out_specs=[pl.BlockSpec((None, rb, qb, GROUP_WIDTH), lambda b, r, i: (b, r, i, 0)),
                   pl.BlockSpec((None, rb, qb, LANES), lambda b, r, i: (b, r, i, 0))],
        out_shape=[jax.ShapeDtypeStruct((batch, dilation, n, GROUP_WIDTH), BF16),
                   jax.ShapeDtypeStruct((batch, dilation, n, LANES), F32)],
        compiler_params=_params("parallel", "parallel", "parallel"),
        name=f"attn_d{dilation}",
    )(qkv, qkv, qkv, qkv, qkv)


def _ssm_in_kernel(*refs, n_grp, ch, n_w):
    x_refs, w_refs, ut_ref = refs[:-n_w - 1], refs[-n_w - 1:-1], refs[-1]
    w = jnp.concatenate([w_ref[...] for w_ref in w_refs], axis=1)
    for s in range(0, SSM_CHUNK, 2):
        xs = jnp.concatenate(
            [jnp.concatenate([x_ref[pl.ds(s + i, ch, stride=SSM_CHUNK), :] for x_ref in x_refs], axis=1)
             for i in range(2)], axis=0).astype(BF16)
        ut = lax.dot_general(w, xs, (((0,), (1,)), ((), ())), preferred_element_type=F32)
        for i in range(2):
            ut_ref[:, (s + i) * SSM_GROUP:(s + i + 1) * SSM_GROUP, :] = (
                ut[:, i * ch:(i + 1) * ch].reshape(n_grp, SSM_GROUP, ch).astype(BF16))


def _ssm_in(x1, w_in_b, u_lo, width, *, ch=LANES, wblk=256):
    t, d = x1.shape
    assert u_lo % wblk == 0 and width % wblk == 0
    n_grp = width // SSM_GROUP
    n_chunk_all = t // SSM_CHUNK
    tm = ch * SSM_CHUNK
    n_w = width // wblk
    slabs = [pl.BlockSpec((tm, LANES), lambda i, c=c: (i, c)) for c in range(d // LANES)]
    w_cols = [pl.BlockSpec((d, wblk), lambda i, c=c: (0, u_lo // wblk + c), pipeline_mode=pl.Buffered(1))
              for c in range(n_w)]
    return pl.pallas_call(
        functools.partial(_ssm_in_kernel, n_grp=n_grp, ch=ch, n_w=n_w),
        grid=(n_chunk_all // ch,),
        in_specs=slabs + w_cols,
        out_specs=pl.BlockSpec((n_grp, SSM_ROW, ch), lambda i: (0, 0, i)),
        out_shape=jax.ShapeDtypeStruct((n_grp, SSM_ROW, n_chunk_all), BF16),
        compiler_params=_params("parallel"),
        name="ssm_in",
    )(*([x1] * (d // LANES)), *([w_in_b] * n_w))


def _hdot(a, b):
    return jnp.dot(a, b, preferred_element_type=F32, precision=lax.Precision.HIGHEST)


def _cmul(ar, ai, br, bi):
    return ar * br - ai * bi, ar * bi + ai * br


def _cpowers(lbr, lbi, n, bits):
    out_r = jnp.ones(n.shape, F32)
    out_i = jnp.zeros(n.shape, F32)
    for b in range(bits):
        sel = (n & (1 << b)) != 0
        out_r, out_i = _cmul(out_r, out_i, jnp.where(sel, lbr, 1.0), jnp.where(sel, lbi, 0.0))
        lbr, lbi = _cmul(lbr, lbi, lbr, lbi)
    return out_r, out_i


def _ssm_prep_kernel(are_l, aim_l, ldt_ref, bre_ref, bim_ref, cre_ref, cim_ref,
                     tt_ref, win_ref, wout_ref, apr_ref, api_ref, *, gb):
    p = are_l.shape[2]
    eye = (lax.broadcasted_iota(jnp.int32, (p, p), 0) == lax.broadcasted_iota(jnp.int32, (p, p), 1)).astype(F32)
    expand = (lax.broadcasted_iota(jnp.int32, (SSM_CHUNK, SSM_ROW), 1) // SSM_GROUP
              == lax.broadcasted_iota(jnp.int32, (SSM_CHUNK, SSM_ROW), 0)).astype(F32)
    expand_t = (lax.broadcasted_iota(jnp.int32, (SSM_ROW, SSM_CHUNK), 0) // SSM_GROUP
                == lax.broadcasted_iota(jnp.int32, (SSM_ROW, SSM_CHUNK), 1)).astype(F32)
    rem = SSM_CHUNK - 1 - lax.broadcasted_iota(jnp.int32, (p, SSM_CHUNK), 1)
    tp1 = lax.broadcasted_iota(jnp.int32, (SSM_CHUNK, p), 0) + 1
    col = lax.broadcasted_iota(jnp.int32, (p, SSM_POWERS), 1)
    lane = lax.broadcasted_iota(jnp.int32, (SSM_GROUP, SSM_ROW), 1)
    chunk_bits = SSM_CHUNK.bit_length()

    def group(g):
        dt = jnp.exp(ldt_ref[g])
        lam_r = jnp.minimum(are_l[g], -1e-4)
        lam_i = aim_l[g]
        mag_l = jnp.exp(lam_r * dt)
        lbr_l = mag_l * jnp.cos(lam_i * dt)
        lbi_l = mag_l * jnp.sin(lam_i * dt)
        inv = 1.0 / (lam_r * lam_r + lam_i * lam_i)
        cfr_l = ((lbr_l - 1.0) * lam_r + lbi_l * lam_i) * inv
        cfi_l = (lbi_l * lam_r - (lbr_l - 1.0) * lam_i) * inv
        stacked = jnp.concatenate([lbr_l, lbi_l, cfr_l, cfi_l, jnp.zeros((SUBLANES - 4, p), F32)], axis=0)
        cols = lax.dot_general(eye, stacked, (((1,), (1,)), ((), ())), preferred_element_type=F32,
                               precision=lax.Precision.HIGHEST)
        yield
        lbr, lbi = cols[:, 0:1], cols[:, 1:2]

        bre = bre_ref[g]
        bim = bim_ref[g]
        bbar_r, bbar_i = _cmul(cols[:, 2:3], cols[:, 3:4], bre, bim)
        pw_r, pw_i = _cpowers(lbr, lbi, rem, chunk_bits - 1)
        win_r, win_i = _cmul(_hdot(pw_r, expand), _hdot(pw_i, expand), bbar_r, bbar_i)
        win_ref[g] = jnp.concatenate([win_r, win_i], axis=0).astype(BF16)
        yield

        a_r, a_i = _cmul(pw_r[:, 0:1], pw_i[:, 0:1], lbr, lbi)
        apr = jnp.zeros((p, SSM_POWERS), F32)
        api = jnp.zeros((p, SSM_POWERS), F32)
        for j in range(SSM_POWERS):
            apr = jnp.where(col == j, a_r, apr)
            api = jnp.where(col == j, a_i, api)
            a_r, a_i = _cmul(a_r, a_i, a_r, a_i)
        apr_ref[g] = apr
        api_ref[g] = api
        yield

        e_r, e_i = _cpowers(lbr_l, lbi_l, tp1, chunk_bits)
        cre = cre_ref[g]
        cim = cim_ref[g]
        wo_r, wo_i = _cmul(_hdot(expand_t, e_r), _hdot(expand_t, e_i), cre, cim)
        wout_ref[g] = jnp.concatenate([wo_r, -wo_i], axis=1).astype(BF16)
        yield

        hrev = _hdot(cre[0:SSM_GROUP, :], win_r) - _hdot(cim[0:SSM_GROUP, :], win_i)
        yield
        blocks = []
        for t in range(SSM_CHUNK):
            hi = (t + 1) * SSM_GROUP
            rolled = hrev if hi == SSM_ROW else pltpu.roll(hrev, hi, axis=1)
            blocks.append(jnp.where(lane < hi, rolled, 0.0))
        tt_ref[g] = jnp.concatenate(blocks, axis=0).astype(BF16)

    active = [group(g) for g in range(gb)]
    while active:
        for gen in list(active):
            try:
                next(gen)
            except StopIteration:
                active.remove(gen)


def _ssm_prep(a_re, a_im, log_dt, b_re, b_im, c_re, c_im, *, gb=8):
    g, p = a_re.shape
    row = SSM_ROW
    blk = lambda shape: pl.BlockSpec((gb,) + shape, lambda i: (i, 0, 0))
    b_t = lambda b: jnp.tile(b, (1, 1, SSM_CHUNK))
    c_t = lambda c: jnp.tile(c, (1, SSM_CHUNK, 1))
    out_shapes = [
        jax.ShapeDtypeStruct((g, row, row), BF16),
        jax.ShapeDtypeStruct((g, 2 * p, row), BF16),
        jax.ShapeDtypeStruct((g, row, 2 * p), BF16),
        jax.ShapeDtypeStruct((g, p, SSM_POWERS), F32), jax.ShapeDtypeStruct((g, p, SSM_POWERS), F32),
    ]
    return pl.pallas_call(
        functools.partial(_ssm_prep_kernel, gb=gb),
        grid=(g // gb,),
        in_specs=[blk((1, p)), blk((1, p)), blk((1, 1)),
                  blk((p, row)), blk((p, row)), blk((row, p)), blk((row, p))],
        out_specs=[blk(s.shape[1:]) for s in out_shapes],
        out_shape=out_shapes,
        compiler_params=_params("parallel"),
        name="ssm_prep",
    )(a_re.reshape(g, 1, p), a_im.reshape(g, 1, p),
      log_dt.reshape(g, 1, 1), b_t(b_re), b_t(b_im), c_t(c_re), c_t(c_im))


def _ssm_kernel(ut_ref, tt_ref, win_ref, wout_ref, apr_ref, api_ref, dsk_ref, y_ref, *, gb, n_chunk):
    nk = ut_ref.shape[-1]
    p = apr_ref.shape[1]
    pos = lax.broadcasted_iota(jnp.int32, (SUBLANES, nk), 1) % n_chunk
    n_steps = (n_chunk - 1).bit_length()
    assert n_steps <= SSM_POWERS

    def shift_whole_tiles(v, sh):
        segs = []
        for b0 in range(0, nk, n_chunk):
            segs += [jnp.zeros((v.shape[0], sh), F32), v[:, b0:b0 + n_chunk - sh]]
        return jnp.concatenate(segs, axis=1)

    def group(g):
        ut = ut_ref[g]
        sl = jnp.dot(win_ref[g], ut, preferred_element_type=F32)
        yield
        apr = apr_ref[g]
        api = api_ref[g]
        prev_r, prev_i = [], []
        for b0 in range(0, p, SUBLANES * SCAN_INTERLEAVE):
            tiles = range(b0, b0 + SUBLANES * SCAN_INTERLEAVE, SUBLANES)
            s = {r0: (sl[r0:r0 + SUBLANES], sl[p + r0:p + r0 + SUBLANES]) for r0 in tiles}
            for j in range(n_steps):
                sh = 1 << j
                keep = pos >= sh
                for r0 in tiles:
                    s_r, s_i = s[r0]
                    a_r = apr[r0:r0 + SUBLANES, j:j + 1]
                    a_i = api[r0:r0 + SUBLANES, j:j + 1]
                    if sh % LANES == 0:
                        p_r, p_i = shift_whole_tiles(s_r, sh), shift_whole_tiles(s_i, sh)
                    else:
                        p_r = jnp.where(keep, pltpu.roll(s_r, sh, axis=1), 0.0)
                        p_i = jnp.where(keep, pltpu.roll(s_i, sh, axis=1), 0.0)
                    s[r0] = (s_r + a_r * p_r - a_i * p_i, s_i + a_r * p_i + a_i * p_r)
                yield
            keep = pos >= 1
            for r0 in tiles:
                prev_r.append(jnp.where(keep, pltpu.roll(s[r0][0], 1, axis=1), 0.0))
                prev_i.append(jnp.where(keep, pltpu.roll(s[r0][1], 1, axis=1), 0.0))
        prev = jnp.concatenate(prev_r + prev_i, axis=0).astype(BF16)
        yield
        y = jnp.dot(tt_ref[g], ut, preferred_element_type=F32)
        y += jnp.dot(wout_ref[g], prev, preferred_element_type=F32)
        yield
        y += dsk_ref[g] * ut.astype(F32)
        y_ref[g] = _gelu_tanh(y).astype(BF16)

    active = [group(g) for g in range(gb)]
    while active:
        for gen in list(active):
            try:
                next(gen)
            except StopIteration:
                active.remove(gen)


def _ssm(ut, prep, dsk, *, n_chunk, gb=4):
    tt, win, wout, apr, api = prep
    g, row, nk = ut.shape
    p = apr.shape[1]
    blk = lambda shape: pl.BlockSpec((gb,) + shape, lambda i: (i, 0, 0))
    kern = functools.partial(_ssm_kernel, gb=gb, n_chunk=n_chunk)
    return pl.pallas_call(
        kern,
        grid=(g // gb,),
        in_specs=[blk((row, nk)), blk((row, row)), blk((2 * p, row)), blk((row, 2 * p)),
                  blk((p, SSM_POWERS)), blk((p, SSM_POWERS)), blk((row, 1))],
        out_specs=blk((row, nk)),
        out_shape=jax.ShapeDtypeStruct((g, row, nk), BF16),
        compiler_params=_params("parallel"),
        name="ssm",
    )(ut, tt, win, wout, apr, api, dsk)


def _ssm_out_kernel(yt_ref, wglu_ref, bglu_ref, wso_ref, o_ref, *, t_per_pass):
    n_grp, _, ch = yt_ref.shape
    width = n_grp * SSM_GROUP
    def gated(t0):
        y = jnp.concatenate(
            [yt_ref[:, t * SSM_GROUP:(t + 1) * SSM_GROUP, :].reshape(width, ch)
             for t in range(t0, t0 + t_per_pass)], axis=1)
        return y, jnp.dot(wglu_ref[...], y, preferred_element_type=F32) + bglu_ref[...]

    def project(t0, y, gate):
        glu = (y.astype(F32) * _sigmoid(gate)).astype(BF16)
        yo = lax.dot_general(glu, wso_ref[...], (((0,), (0,)), ((), ())), preferred_element_type=F32)
        for i in range(t_per_pass):
            o_ref[t0 + i] = yo[i * ch:(i + 1) * ch, :].astype(BF16)

    starts = list(range(0, SSM_CHUNK, t_per_pass))
    pending = gated(starts[0])
    for n, t0 in enumerate(starts):
        nxt = gated(starts[n + 1]) if n + 1 < len(starts) else None
        project(t0, *pending)
        pending = nxt


def _ssm_out(yt, w_glu, b_glu, w_ssm_out, *, ch=LANES, t_per_pass=4):
    n_grp, row, n_chunk_all = yt.shape
    width, d = w_ssm_out.shape
    return pl.pallas_call(
        functools.partial(_ssm_out_kernel, t_per_pass=t_per_pass),
        grid=(n_chunk_all // ch,),
        in_specs=[pl.BlockSpec((n_grp, row, ch), lambda i: (0, 0, i)),
                  _resident((width, width)), _resident((width, 1)), _resident((width, d))],
        out_specs=pl.BlockSpec((None, SSM_CHUNK, ch, d), lambda i: (i, 0, 0, 0)),
        out_shape=jax.ShapeDtypeStruct((n_chunk_all // ch, SSM_CHUNK, ch, d), BF16),
        compiler_params=_params("parallel"),
        name="ssm_out",
    )(yt, w_glu.T.astype(BF16), b_glu.reshape(width, 1), w_ssm_out.astype(BF16))


def _mix_ffn_kernel(x_ref, o0, o1, o2, st0, st1, st2, ys_ref, ga_ref, gs_ref,
                    wao_ref, wo_ref, g_ref, b_ref, wg_ref, wu_ref, wd_ref, g3_ref, b3_ref,
                    out_ref, o_stage, st_stage, ys_stage, x2_ref, *, sub, chunks, alpha):
    step = pl.program_id(0)
    slot_w = step % 2
    slot_r = 1 - slot_w

    @pl.when(step == 0)
    def _():
        x2_ref[...] = jnp.zeros(x2_ref.shape, F32)

    lane = lax.broadcasted_iota(jnp.int32, (sub, LANES), 1)
    lo_half = lane < HEAD_DIM

    def token_order(ref, stage, r0):
        dil = ref.shape[0]
        n = sub // dil
        i0 = r0 // dil
        if dil == 1:
            return ref[0, i0:i0 + n, :].astype(F32)
        n_col = ref.shape[2] // LANES
        pitch = _scatter_pitch(dil)
        for r in range(dil):
            blk = ref[r, i0:i0 + n, :].astype(F32)
            for c in range(n_col):
                stage[c, pl.ds(r, n, stride=pitch), :] = blk[:, c * LANES:(c + 1) * LANES]
        cols = [stage[c, 0:n * pitch, :] for c in range(n_col)]
        if pitch != dil:
            cols = [a.reshape(n, pitch, LANES)[:, :dil, :].reshape(sub, LANES) for a in cols]
        return jnp.concatenate(cols, axis=1)

    def head_cols(vals):
        cols = []
        for hp in range(GROUP_WIDTH // LANES):
            cols.append(jnp.where(lo_half, vals[2 * hp], vals[2 * hp + 1]))
        return jnp.concatenate(cols, axis=1)

    def merge_groups(k):
        r0 = k * sub
        sts = [token_order(st, st_stage.at[k, i], r0) for i, st in enumerate((st0, st1, st2))]
        outs = [token_order(o, o_stage.at[k, i], r0) for i, o in enumerate((o0, o1, o2))]
        dens = [pltpu.roll(st, LANES - HEADS_PER_GROUP, axis=1) for st in sts]
        mx = jnp.maximum(jnp.maximum(sts[0], sts[1]), sts[2])
        w = [den * jnp.exp2(st - mx) for den, st in zip(dens, sts)]
        tot = jnp.where(lane < HEADS_PER_GROUP, w[0] + w[1] + w[2], 1.0)
        att = None
        for g in range(N_ATTN_GROUPS):
            wt = w[g] / tot
            term = head_cols([wt[:, h:h + 1] for h in range(HEADS_PER_GROUP)]) * outs[g]
            att = term if att is None else att + term
        return att.astype(BF16)

    def gate(k, att):
        rows = slice(k * sub, (k + 1) * sub)
        y_attn = jnp.dot(att, wao_ref[...], preferred_element_type=F32)
        y_ssm = token_order(ys_ref, ys_stage.at[k], k * sub)
        merged = ga_ref[rows, :].astype(F32) * y_attn + gs_ref[rows, :].astype(F32) * y_ssm
        return merged.astype(BF16)

    def project(k, merged):
        rows = slice(k * sub, (k + 1) * sub)
        mix = jnp.dot(merged, wo_ref[...], preferred_element_type=F32)
        x2_ref[slot_w, rows, :] = _layer_norm(alpha * x_ref[rows, :] + mix, g_ref[...], b_ref[...])

    x2_prev = x2_ref[slot_r]
    ffn_state = {}

    def ffn_gate_up(k, c):
        rows = slice(k * sub, (k + 1) * sub)
        if c == chunks[0]:
            ffn_state[k] = {"xb": x2_prev[rows, :].astype(BF16), "acc": None}
        xb = ffn_state[k]["xb"]
        ffn_state[k]["gu"] = (jnp.dot(xb, wg_ref[:, c[0]:c[1]], preferred_element_type=F32),
                              jnp.dot(xb, wu_ref[:, c[0]:c[1]], preferred_element_type=F32))

    def ffn_down(k, c):
        rows = slice(k * sub, (k + 1) * sub)
        gate_v, up = ffn_state[k].pop("gu")
        h = (gate_v * _sigmoid(gate_v) * up).astype(BF16)
        part = jnp.dot(h, wd_ref[c[0]:c[1], :], preferred_element_type=F32)
        acc = ffn_state[k]["acc"]
        ffn_state[k]["acc"] = part if acc is None else acc + part
        if c == chunks[-1]:
            y = alpha * x2_prev[rows, :] + 0.5 * ffn_state.pop(k)["acc"]
            out_ref[rows, :] = _layer_norm(y, g3_ref[...], b3_ref[...])

    n_sub = x_ref.shape[0] // sub
    ffn_stages = [f for k in range(n_sub) for c in chunks
                  for f in (functools.partial(ffn_gate_up, k, c), functools.partial(ffn_down, k, c))]
    mix_stages = [f for k in range(n_sub) for f in (functools.partial(merge_groups, k),
                                                    functools.partial(gate, k),
                                                    functools.partial(project, k))]
    carry = ()
    for f in ffn_stages:
        f()
        if mix_stages:
            out = mix_stages.pop(0)(*carry)
            carry = () if out is None else (out,)
    assert not mix_stages


def _mix_ffn(x1, os_, sts, y_ssm, ga, gs, w_attn_out, w_o, g2, b2, wg, wu, wd, g3, b3, *,
             seq, alpha, tm=512, sub=256, chunk=1024):
    t, d = x1.shape
    d_ff = wg.shape[1]
    tiles = seq // tm
    n_tiles = t // tm
    cur = lambda i: jnp.minimum(i, n_tiles - 1)
    tile = lambda w: pl.BlockSpec((tm, w), lambda i: (cur(i), 0))

    def by_class(a):
        _, dil, n, w = a.shape
        return pl.BlockSpec((None, dil, tm // dil, w), lambda i: (cur(i) // tiles, 0, cur(i) % tiles, 0))

    n_grp = len(os_)
    stage_rows = max(sub // a.shape[1] * _scatter_pitch(a.shape[1]) for a in os_)
    per_big = y_ssm.shape[2] * SSM_CHUNK // tm
    ys_spec = pl.BlockSpec((None, SSM_CHUNK, tm // SSM_CHUNK, d),
                           lambda i: (cur(i) // per_big, 0, cur(i) % per_big, 0))
    ys_rows = sub // SSM_CHUNK * _scatter_pitch(SSM_CHUNK)
    vec = lambda v: v.reshape(1, d)
    return pl.pallas_call(
        functools.partial(_mix_ffn_kernel, sub=sub, chunks=_ffn_chunks(d_ff, chunk), alpha=alpha),
        grid=(n_tiles + 1,),
        in_specs=[tile(d)] + [by_class(a) for a in os_] + [by_class(a) for a in sts] + [ys_spec, tile(d), tile(d),
                  _resident(w_attn_out.shape), _resident(w_o.shape), _resident((1, d)), _resident((1, d)),
                  _resident((d, d_ff)), _resident((d, d_ff)), _resident((d_ff, d)),
                  _resident((1, d)), _resident((1, d))],
        out_specs=pl.BlockSpec((tm, d), lambda i: (jnp.maximum(i - 1, 0), 0)),
        out_shape=jax.ShapeDtypeStruct((t, d), F32),
        scratch_shapes=[pltpu.VMEM((tm // sub, n_grp, GROUP_WIDTH // LANES, stage_rows, LANES), F32),
                        pltpu.VMEM((tm // sub, n_grp, 1, stage_rows, LANES), F32),
                        pltpu.VMEM((tm // sub, d // LANES, ys_rows, LANES), F32),
                        pltpu.VMEM((2, tm, d), F32)],
        compiler_params=_params("arbitrary"),
        name="mix_ffn",
    )(x1, *os_, *sts, y_ssm, ga, gs, w_attn_out.astype(BF16), w_o.astype(BF16), vec(g2), vec(b2),
      wg.astype(BF16), wu.astype(BF16), wd.astype(BF16), vec(g3), vec(b3))


def _layer(x, positions, w_in, w_attn_out, a_re, a_im, log_dt, b_re, b_im, c_re, c_im, d_skip,
           w_glu, b_glu, w_ssm_out, w_o, ffn1, ffn2, ln1, ln2, ln3, *, alpha):
    batch, seq, d = x.shape
    t = batch * seq
    n_grp = a_re.shape[0]
    ssm_width = n_grp * SSM_GROUP
    u_lo = 3 * ATTN_WIDTH

    x1 = _ffn_ln(x.reshape(t, d), *ffn1, *ln1, alpha=alpha)
    w_in_b = w_in.astype(BF16)
    *qkvs, ga, gs = _in_proj(x1, positions, w_in_b, batch=batch, seq=seq, ssm_width=ssm_width)

    os_, sts = [], []
    for qkv, (window, dilation) in zip(qkvs, ATTN_PATTERNS):
        assert window == ATTN_BLK * dilation
        o, st = _dilated_attention(qkv)
        os_.append(o)
        sts.append(st)

    ut = _ssm_in(x1, w_in_b, u_lo, ssm_width)
    prep = _ssm_prep(a_re, a_im, log_dt, b_re, b_im, c_re, c_im)
    dsk = jnp.tile(d_skip.reshape(n_grp, 1, SSM_GROUP), (1, SSM_CHUNK, 1)).reshape(n_grp, SSM_ROW, 1)
    yt = _ssm(ut, prep, dsk, n_chunk=seq // SSM_CHUNK)
    y_ssm = _ssm_out(yt, w_glu, b_glu, w_ssm_out)

    x3 = _mix_ffn(x1, os_, sts, y_ssm, ga, gs, w_attn_out, w_o, *ln2, *ffn2, *ln3, seq=seq, alpha=alpha)
    return x3.reshape(batch, seq, d)


def kernel(x, positions, w_in, w_attn_out, a_re, a_im, log_dt, b_re, b_im, c_re, c_im, d_skip, w_glu, b_glu, w_ssm_out, w_o, ffn1_wg, ffn1_wu, ffn1_wd, ffn2_wg, ffn2_wu, ffn2_wd, ln1_g, ln1_b, ln2_g, ln2_b, ln3_g, ln3_b):
    depth = w_in.shape[0]
    alpha = (2.0 * depth) ** 0.25
    for i in range(depth):
        x = _layer(x, positions, w_in[i], w_attn_out[i], a_re[i], a_im[i], log_dt[i], b_re[i], b_im[i],
                   c_re[i], c_im[i], d_skip[i], w_glu[i], b_glu[i], w_ssm_out[i], w_o[i],
                   (ffn1_wg[i], ffn1_wu[i], ffn1_wd[i]), (ffn2_wg[i], ffn2_wu[i], ffn2_wd[i]),
                   (ln1_g[i], ln1_b[i]), (ln2_g[i], ln2_b[i]), (ln3_g[i], ln3_b[i]), alpha=alpha)
    return x
```

```python
import functools

import jax
import jax.numpy as jnp
from jax import lax
from jax.experimental import pallas as pl
from jax.experimental.pallas import tpu as pltpu

F32 = jnp.float32
BF16 = jnp.bfloat16

HEAD_DIM = 64
HEADS_PER_GROUP = 4
GROUP_WIDTH = HEADS_PER_GROUP * HEAD_DIM
ATTN_PATTERNS = ((128, 1), (512, 4), (2048, 16))
N_ATTN_GROUPS = len(ATTN_PATTERNS)
ATTN_WIDTH = N_ATTN_GROUPS * GROUP_WIDTH
ROT_DIM = HEAD_DIM // 4
ROPE_THETA = 500000.0
ROPE_ROWS = 32
ATTN_BLK = 128
NEG_INF = -1e30
LOG2_E = 1.4426950408889634
SSM_GROUP = 16
SSM_CHUNK = 16
SSM_ROW = SSM_CHUNK * SSM_GROUP
SSM_POWERS = 16
SCAN_INTERLEAVE = 8
CHUNK_PITCH = 24


def _scatter_pitch(stride):
    return CHUNK_PITCH if stride == SSM_CHUNK else stride


def _rows_by_class(val, stage, dil):
    rows, w = val.shape
    n = rows // dil
    pitch = _scatter_pitch(dil)
    n_col = w // LANES
    for c in range(n_col):
        col = val[:, c * LANES:(c + 1) * LANES]
        if pitch != dil:
            col = jnp.concatenate([col.reshape(n, dil, LANES), jnp.zeros((n, pitch - dil, LANES), F32)], axis=1)
            col = col.reshape(n * pitch, LANES)
        stage[c, 0:n * pitch, :] = col

    def get(r):
        return jnp.concatenate([stage[c, pl.ds(r, n, stride=pitch), :] for c in range(n_col)], axis=1)

    return get
LN_EPS = 1e-5

LANES = 128
SUBLANES = 8
VMEM_LIMIT_BYTES = 56 * 1024 * 1024


def _params(*semantics):
    return pltpu.CompilerParams(dimension_semantics=semantics, vmem_limit_bytes=VMEM_LIMIT_BYTES)


def _resident(shape):
    zeros = (0,) * len(shape)
    return pl.BlockSpec(shape, lambda *_: zeros, pipeline_mode=pl.Buffered(1))


def _layer_norm(y, g, b):
    mu = jnp.mean(y, axis=-1, keepdims=True)
    yc = y - mu
    var = jnp.mean(yc * yc, axis=-1, keepdims=True)
    return yc * lax.rsqrt(var + LN_EPS) * g + b


def _sigmoid(x):
    return 1.0 / (1.0 + jnp.exp(-x))


def _gelu_tanh(x):
    c = 0.7978845608028654
    return 0.5 * x * (1.0 + jnp.tanh(c * (x + 0.044715 * (x * x * x))))


def _ffn_ln_kernel(x_ref, wg_ref, wu_ref, wd_ref, g_ref, b_ref, o_ref, *, chunks, sub, alpha):
    for r0 in range(0, x_ref.shape[0], sub):
        x = x_ref[r0:r0 + sub, :]
        xb = x.astype(BF16)
        acc = None
        for c0, c1 in chunks:
            gate = jnp.dot(xb, wg_ref[:, c0:c1], preferred_element_type=F32)
            up = jnp.dot(xb, wu_ref[:, c0:c1], preferred_element_type=F32)
            h = (gate * _sigmoid(gate) * up).astype(BF16)
            part = jnp.dot(h, wd_ref[c0:c1, :], preferred_element_type=F32)
            acc = part if acc is None else acc + part
        y = alpha * x + 0.5 * acc
        o_ref[r0:r0 + sub, :] = _layer_norm(y, g_ref[...], b_ref[...])


def _ffn_chunks(d_ff, width):
    edges = list(range(0, d_ff, width)) + [d_ff]
    return tuple(zip(edges[:-1], edges[1:]))


def _ffn_ln(x, wg, wu, wd, g, b, *, alpha, tm=2048, sub=256, chunk=1024):
    t, d = x.shape
    d_ff = wg.shape[1]
    kern = functools.partial(_ffn_ln_kernel, chunks=_ffn_chunks(d_ff, chunk), sub=sub, alpha=alpha)
    return pl.pallas_call(
        kern,
        grid=(t // tm,),
        in_specs=[
            pl.BlockSpec((tm, d), lambda i: (i, 0)),
            _resident((d, d_ff)), _resident((d, d_ff)), _resident((d_ff, d)),
            _resident((1, d)), _resident((1, d)),
        ],
        out_specs=pl.BlockSpec((tm, d), lambda i: (i, 0)),
        out_shape=jax.ShapeDtypeStruct((t, d), F32),
        compiler_params=_params("parallel"),
        name="ffn_ln",
    )(x, wg.astype(BF16), wu.astype(BF16), wd.astype(BF16), g.reshape(1, d), b.reshape(1, d))


def _rope_spread():
    half = ROT_DIM // 2
    rows = lax.broadcasted_iota(jnp.int32, (ROPE_ROWS, 2 * LANES), 0)
    cols = lax.broadcasted_iota(jnp.int32, (ROPE_ROWS, 2 * LANES), 1)
    in_head = cols % HEAD_DIM
    freq = in_head % half
    is_cos = cols < LANES
    rot = in_head < ROT_DIM
    cos_part = is_cos & rot & (rows == freq)
    one_part = is_cos & jnp.logical_not(rot) & (rows == 2 * half)
    sin_part = jnp.logical_not(is_cos) & rot & (rows == half + freq)
    sign = jnp.where(in_head < half, 1.0, -1.0)
    return jnp.where(cos_part | one_part, 1.0, jnp.where(sin_part, sign, 0.0)).astype(BF16)


def _in_proj_kernel(x_ref, pos_ref, invf_ref, w_ref,
                    qkv0_ref, qkv1_ref, qkv2_ref, ga_ref, gs_ref, stage_ref, *, splits, sub):
    half = ROT_DIM // 2
    spread = _rope_spread()
    first = lax.broadcasted_iota(jnp.int32, (sub, LANES), 1) % HEAD_DIM < half
    s0, s1, s2, s3, s4, s5 = splits
    tn = (((0,), (0,)), ((), ()))

    for r0 in range(0, x_ref.shape[0], sub):
        xb = x_ref[r0:r0 + sub, :].astype(BF16)

        ang = invf_ref[...] * pos_ref[:, r0:r0 + sub].astype(F32)
        tab = jnp.concatenate([jnp.cos(ang), jnp.sin(ang), jnp.ones((half, sub), F32),
                               jnp.zeros((ROPE_ROWS - 3 * half, sub), F32)], axis=0)
        tab_hi = tab.astype(BF16)
        tab_lo = (tab - tab_hi.astype(F32)).astype(BF16)
        cs = (lax.dot_general(tab_hi, spread, tn, preferred_element_type=F32)
              + lax.dot_general(tab_lo, spread, tn, preferred_element_type=F32))
        cos = cs[:, :LANES]
        sin = cs[:, LANES:]

        def rotate(z):
            cols = []
            for c in range(z.shape[1] // LANES):
                zc = z[:, c * LANES:(c + 1) * LANES]
                zs = zc * sin
                up = pltpu.roll(jnp.where(first, zs, 0.0), half, axis=1)
                dn = pltpu.roll(jnp.where(first, 0.0, zs), LANES - half, axis=1)
                cols.append(zc * cos + up + dn)
            return jnp.concatenate(cols, axis=1)

        def proj(lo, hi):
            return jnp.dot(xb, w_ref[:, lo:hi], preferred_element_type=F32)

        q = rotate(proj(0, s0)) * (HEAD_DIM ** -0.5 * LOG2_E)
        k = rotate(proj(s0, s1))
        v = proj(s1, s2)
        for gi, out_ref in enumerate((qkv0_ref, qkv1_ref, qkv2_ref)):
            dil = ATTN_PATTERNS[gi][1]
            c0 = gi * GROUP_WIDTH
            qkv = jnp.concatenate([a[:, c0:c0 + GROUP_WIDTH] for a in (q, k, v)], axis=1)
            n = sub // dil
            i0 = r0 // dil
            if dil == 1:
                out_ref[0, i0:i0 + n, :] = qkv.astype(BF16)
            else:
                get = _rows_by_class(qkv, stage_ref.at[r0 // sub, gi - 1], dil)
                for r in range(dil):
                    out_ref[r, i0:i0 + n, :] = get(r).astype(BF16)
        ga_ref[r0:r0 + sub, :] = _sigmoid(proj(s3, s4)).astype(BF16)
        gs_ref[r0:r0 + sub, :] = _sigmoid(proj(s4, s5)).astype(BF16)


def _in_proj(x1, positions, w_in, *, batch, seq, ssm_width, tm=1024, sub=256):
    t, d = x1.shape
    splits = (ATTN_WIDTH, 2 * ATTN_WIDTH, 3 * ATTN_WIDTH, 3 * ATTN_WIDTH + ssm_width,
              3 * ATTN_WIDTH + ssm_width + d, 3 * ATTN_WIDTH + ssm_width + 2 * d)
    assert splits[-1] == w_in.shape[1] and seq % tm == 0
    half = ROT_DIM // 2
    invf = (ROPE_THETA ** (-jnp.arange(half, dtype=F32) * 2.0 / ROT_DIM)).reshape(half, 1)
    tiles = seq // tm
    qkv_w = 3 * GROUP_WIDTH
    qkv_specs = [pl.BlockSpec((None, dil, tm // dil, qkv_w), lambda i: (i // tiles, 0, i % tiles, 0))
                 for _, dil in ATTN_PATTERNS]
    qkv_shapes = [jax.ShapeDtypeStruct((batch, dil, seq // dil, qkv_w), BF16) for _, dil in ATTN_PATTERNS]
    widths = (d, d)
    kern = functools.partial(_in_proj_kernel, splits=splits, sub=sub)
    n_strided = sum(dil > 1 for _, dil in ATTN_PATTERNS)
    stage_rows = max(sub // dil * _scatter_pitch(dil) for _, dil in ATTN_PATTERNS)
    return pl.pallas_call(
        kern,
        grid=(t // tm,),
        in_specs=[
            pl.BlockSpec((tm, d), lambda i: (i, 0)),
            pl.BlockSpec((1, tm), lambda i: (0, i)),
            _resident((half, 1)),
            _resident(w_in.shape),
        ],
        out_specs=qkv_specs + [pl.BlockSpec((tm, w), lambda i: (i, 0)) for w in widths],
        out_shape=qkv_shapes + [jax.ShapeDtypeStruct((t, w), BF16) for w in widths],
        scratch_shapes=[pltpu.VMEM((tm // sub, n_strided, qkv_w // LANES, stage_rows, LANES), F32)],
        compiler_params=_params("parallel"),
        name="in_proj",
    )(x1, positions.reshape(1, t), invf, w_in)


def _attn_kernel(q_ref, kc_ref, kp_ref, vc_ref, vp_ref, o_ref, st_ref, *, n_sub):
    first = pl.program_id(2) == 0
    row = lax.broadcasted_iota(jnp.int32, (ATTN_BLK, 2 * ATTN_BLK), 0)
    col = lax.broadcasted_iota(jnp.int32, (ATTN_BLK, 2 * ATTN_BLK), 1)
    band = (col >= row) & (col <= row + ATTN_BLK)
    lane = lax.broadcasted_iota(jnp.int32, (ATTN_BLK, LANES), 1)
    lo_half = lane < HEAD_DIM

    n_pair = GROUP_WIDTH // LANES
    tiles = [(rc, j, hp, hh) for rc in range(q_ref.shape[0]) for j in range(n_sub)
             for hp in range(n_pair) for hh in range(2)]

    def window(ref_prev, ref_cur, j, c0):
        r0 = j * ATTN_BLK
        if j == 0:
            return jnp.concatenate([ref_prev[:, c0:c0 + LANES], ref_cur[0:ATTN_BLK, c0:c0 + LANES]], axis=0)
        return ref_cur[r0 - ATTN_BLK:r0 + ATTN_BLK, c0:c0 + LANES]

    def scores(tile):
        rc, j, hp, hh = tile
        q2 = q_ref[rc, j * ATTN_BLK:(j + 1) * ATTN_BLK, hp * LANES:(hp + 1) * LANES]
        qm = jnp.where(lo_half if hh == 0 else jnp.logical_not(lo_half), q2, jnp.zeros_like(q2))
        return lax.dot_general(qm, window(kp_ref.at[rc], kc_ref.at[rc], j, hp * LANES), (((1,), (1,)), ((), ())),
                               preferred_element_type=F32)

    def softmax(tile, s):
        valid = band & ((col >= ATTN_BLK) | jnp.logical_not(first)) if tile[1] == 0 else band
        s = jnp.where(valid, s, NEG_INF)
        m = jnp.max(s, axis=1, keepdims=True)
        p = jnp.exp2(s - m)
        return p.astype(BF16), m, jnp.sum(p, axis=1, keepdims=True)

    def weighted_values(tile, p, den):
        rc, j, hp, _ = tile
        return jnp.dot(p, window(vp_ref.at[rc], vc_ref.at[rc], j, hp * LANES), preferred_element_type=F32) / den

    s_q, p_q, outs = {}, {}, {}
    for step in range(len(tiles) + 2):
        if step < len(tiles):
            s_q[step] = scores(tiles[step])
        if 0 <= step - 1 < len(tiles):
            p_q[step - 1] = softmax(tiles[step - 1], s_q.pop(step - 1))
        if 0 <= step - 2 < len(tiles):
            t = step - 2
            rc, j, hp, hh = tiles[t]
            p, m, den = p_q.pop(t)
            outs[hh] = weighted_values(tiles[t], p, den)
            h = 2 * hp + hh
            rows = slice(j * ATTN_BLK, (j + 1) * ATTN_BLK)
            if h == 0:
                st_ref[rc, rows, :] = jnp.zeros((ATTN_BLK, LANES), F32)
            st_ref[rc, rows, h:h + 1] = m
            st_ref[rc, rows, HEADS_PER_GROUP + h:HEADS_PER_GROUP + h + 1] = den
            if hh == 1:
                o_ref[rc, rows, hp * LANES:(hp + 1) * LANES] = (
                    jnp.where(lo_half, outs[0], outs[1]).astype(BF16))


def _dilated_attention(qkv):
    batch, dilation, n, _ = qkv.shape
    rows_per_step = 2048
    qb = min(rows_per_step, n)
    n_sub = qb // ATTN_BLK
    rb = min(max(rows_per_step // qb, 1), dilation)

    def cur(part):
        return pl.BlockSpec((None, rb, qb, GROUP_WIDTH), lambda b, r, i: (b, r, i, part))

    def prev(part):
        return pl.BlockSpec((None, rb, ATTN_BLK, GROUP_WIDTH),
                            lambda b, r, i: (b, r, jnp.maximum(i * n_sub - 1, 0), part))

    return pl.pallas_call(
        functools.partial(_attn_kernel, n_sub=n_sub),
        grid=(batch, dilation // rb, n // qb),
        in_specs=[cur(0), cur(1), prev(1), cur(2), prev(2)],
        out_specs=[pl.BlockSpec((None, rb, qb, GROUP_WIDTH), lambda b, r, i: (b, r, i, 0)),
                   pl.BlockSpec((None, rb, qb, LANES), lambda b, r, i: (b, r, i, 0))],
        out_shape=[jax.ShapeDtypeStruct((batch, dilation, n, GROUP_WIDTH), BF16),
                   jax.ShapeDtypeStruct((batch, dilation, n, LANES), F32)],
        compiler_params=_params("parallel", "parallel", "parallel"),
        name=f"attn_d{dilation}",
    )(qkv, qkv, qkv, qkv, qkv)


def _ssm_in_kernel(*refs, n_grp, ch, n_w):
    x_refs, w_refs, ut_ref = refs[:-n_w - 1], refs[-n_w - 1:-1], refs[-1]
    w = jnp.concatenate([w_ref[...] for w_ref in w_refs], axis=1)
    for s in range(0, SSM_CHUNK, 2):
        xs = jnp.concatenate(
            [jnp.concatenate([x_ref[pl.ds(s + i, ch, stride=SSM_CHUNK), :] for x_ref in x_refs], axis=1)
             for i in range(2)], axis=0).astype(BF16)
        ut = lax.dot_general(w, xs, (((0,), (1,)), ((), ())), preferred_element_type=F32)
        for i in range(2):
            ut_ref[:, (s + i) * SSM_GROUP:(s + i + 1) * SSM_GROUP, :] = (
                ut[:, i * ch:(i + 1) * ch].reshape(n_grp, SSM_GROUP, ch).astype(BF16))


def _ssm_in(x1, w_in_b, u_lo, width, *, ch=2 * LANES, wblk=256):
    t, d = x1.shape
    assert u_lo % wblk == 0 and width % wblk == 0
    n_grp = width // SSM_GROUP
    n_chunk_all = t // SSM_CHUNK
    tm = ch * SSM_CHUNK
    n_w = width // wblk
    slabs = [pl.BlockSpec((tm, LANES), lambda i, c=c: (i, c)) for c in range(d // LANES)]
    w_cols = [pl.BlockSpec((d, wblk), lambda i, c=c: (0, u_lo // wblk + c), pipeline_mode=pl.Buffered(1))
              for c in range(n_w)]
    return pl.pallas_call(
        functools.partial(_ssm_in_kernel, n_grp=n_grp, ch=ch, n_w=n_w),
        grid=(n_chunk_all // ch,),
        in_specs=slabs + w_cols,
        out_specs=pl.BlockSpec((n_grp, SSM_ROW, ch), lambda i: (0, 0, i)),
        out_shape=jax.ShapeDtypeStruct((n_grp, SSM_ROW, n_chunk_all), BF16),
        compiler_params=_params("parallel"),
        name="ssm_in",
    )(*([x1] * (d // LANES)), *([w_in_b] * n_w))


def _hdot(a, b):
    return jnp.dot(a, b, preferred_element_type=F32, precision=lax.Precision.HIGHEST)


def _cmul(ar, ai, br, bi):
    return ar * br - ai * bi, ar * bi + ai * br


def _cpowers(lbr, lbi, n, bits):
    out_r = jnp.ones(n.shape, F32)
    out_i = jnp.zeros(n.shape, F32)
    for b in range(bits):
        sel = (n & (1 << b)) != 0
        out_r, out_i = _cmul(out_r, out_i, jnp.where(sel, lbr, 1.0), jnp.where(sel, lbi, 0.0))
        lbr, lbi = _cmul(lbr, lbi, lbr, lbi)
    return out_r, out_i


def _ssm_prep_kernel(are_l, aim_l, ldt_ref, bre_ref, bim_ref, cre_ref, cim_ref,
                     tt_ref, win_ref, wout_ref, apr_ref, api_ref, *, gb):
    p = are_l.shape[2]
    eye = (lax.broadcasted_iota(jnp.int32, (p, p), 0) == lax.broadcasted_iota(jnp.int32, (p, p), 1)).astype(F32)
    expand = (lax.broadcasted_iota(jnp.int32, (SSM_CHUNK, SSM_ROW), 1) // SSM_GROUP
              == lax.broadcasted_iota(jnp.int32, (SSM_CHUNK, SSM_ROW), 0)).astype(F32)
    expand_t = (lax.broadcasted_iota(jnp.int32, (SSM_ROW, SSM_CHUNK), 0) // SSM_GROUP
                == lax.broadcasted_iota(jnp.int32, (SSM_ROW, SSM_CHUNK), 1)).astype(F32)
    rem = SSM_CHUNK - 1 - lax.broadcasted_iota(jnp.int32, (p, SSM_CHUNK), 1)
    tp1 = lax.broadcasted_iota(jnp.int32, (SSM_CHUNK, p), 0) + 1
    col = lax.broadcasted_iota(jnp.int32, (p, SSM_POWERS), 1)
    lane = lax.broadcasted_iota(jnp.int32, (SSM_GROUP, SSM_ROW), 1)
    chunk_bits = SSM_CHUNK.bit_length()

    def group(g):
        dt = jnp.exp(ldt_ref[g])
        lam_r = jnp.minimum(are_l[g], -1e-4)
        lam_i = aim_l[g]
        mag_l = jnp.exp(lam_r * dt)
        lbr_l = mag_l * jnp.cos(lam_i * dt)
        lbi_l = mag_l * jnp.sin(lam_i * dt)
        inv = 1.0 / (lam_r * lam_r + lam_i * lam_i)
        cfr_l = ((lbr_l - 1.0) * lam_r + lbi_l * lam_i) * inv
        cfi_l = (lbi_l * lam_r - (lbr_l - 1.0) * lam_i) * inv
        stacked = jnp.concatenate([lbr_l, lbi_l, cfr_l, cfi_l, jnp.zeros((SUBLANES - 4, p), F32)], axis=0)
        cols = lax.dot_general(eye, stacked, (((1,), (1,)), ((), ())), preferred_element_type=F32,
                               precision=lax.Precision.HIGHEST)
        yield
        lbr, lbi = cols[:, 0:1], cols[:, 1:2]

        bre = bre_ref[g]
        bim = bim_ref[g]
        bbar_r, bbar_i = _cmul(cols[:, 2:3], cols[:, 3:4], bre, bim)
        pw_r, pw_i = _cpowers(lbr, lbi, rem, chunk_bits - 1)
        win_r, win_i = _cmul(_hdot(pw_r, expand), _hdot(pw_i, expand), bbar_r, bbar_i)
        win_ref[g] = jnp.concatenate([win_r, win_i], axis=0).astype(BF16)
        yield

        a_r, a_i = _cmul(pw_r[:, 0:1], pw_i[:, 0:1], lbr, lbi)
        apr = jnp.zeros((p, SSM_POWERS), F32)
        api = jnp.zeros((p, SSM_POWERS), F32)
        for j in range(SSM_POWERS):
            apr = jnp.where(col == j, a_r, apr)
            api = jnp.where(col == j, a_i, api)
            a_r, a_i = _cmul(a_r, a_i, a_r, a_i)
        apr_ref[g] = apr
        api_ref[g] = api
        yield

        e_r, e_i = _cpowers(lbr_l, lbi_l, tp1, chunk_bits)
        cre = cre_ref[g]
        cim = cim_ref[g]
        wo_r, wo_i = _cmul(_hdot(expand_t, e_r), _hdot(expand_t, e_i), cre, cim)
        wout_ref[g] = jnp.concatenate([wo_r, -wo_i], axis=1).astype(BF16)
        yield

        hrev = _hdot(cre[0:SSM_GROUP, :], win_r) - _hdot(cim[0:SSM_GROUP, :], win_i)
        yield
        blocks = []
        for t in range(SSM_CHUNK):
            hi = (t + 1) * SSM_GROUP
            rolled = hrev if hi == SSM_ROW else pltpu.roll(hrev, hi, axis=1)
            blocks.append(jnp.where(lane < hi, rolled, 0.0))
        tt_ref[g] = jnp.concatenate(blocks, axis=0).astype(BF16)

    active = [group(g) for g in range(gb)]
    while active:
        for gen in list(active):
            try:
                next(gen)
            except StopIteration:
                active.remove(gen)


def _ssm_prep(a_re, a_im, log_dt, b_re, b_im, c_re, c_im, *, gb=8):
    g, p = a_re.shape
    row = SSM_ROW
    blk = lambda shape: pl.BlockSpec((gb,) + shape, lambda i: (i, 0, 0))
    b_t = lambda b: jnp.tile(b, (1, 1, SSM_CHUNK))
    c_t = lambda c: jnp.tile(c, (1, SSM_CHUNK, 1))
    out_shapes = [
        jax.ShapeDtypeStruct((g, row, row), BF16),
        jax.ShapeDtypeStruct((g, 2 * p, row), BF16),
        jax.ShapeDtypeStruct((g, row, 2 * p), BF16),
        jax.ShapeDtypeStruct((g, p, SSM_POWERS), F32), jax.ShapeDtypeStruct((g, p, SSM_POWERS), F32),
    ]
    return pl.pallas_call(
        functools.partial(_ssm_prep_kernel, gb=gb),
        grid=(g // gb,),
        in_specs=[blk((1, p)), blk((1, p)), blk((1, 1)),
                  blk((p, row)), blk((p, row)), blk((row, p)), blk((row, p))],
        out_specs=[blk(s.shape[1:]) for s in out_shapes],
        out_shape=out_shapes,
        compiler_params=_params("parallel"),
        name="ssm_prep",
    )(a_re.reshape(g, 1, p), a_im.reshape(g, 1, p),
      log_dt.reshape(g, 1, 1), b_t(b_re), b_t(b_im), c_t(c_re), c_t(c_im))


def _ssm_kernel(ut_ref, tt_ref, win_ref, wout_ref, apr_ref, api_ref, dsk_ref, y_ref, *, gb, n_chunk):
    nk = ut_ref.shape[-1]
    p = apr_ref.shape[1]
    pos = lax.broadcasted_iota(jnp.int32, (SUBLANES, nk), 1) % n_chunk
    n_steps = (n_chunk - 1).bit_length()
    assert n_steps <= SSM_POWERS

    def shift_whole_tiles(v, sh):
        segs = []
        for b0 in range(0, nk, n_chunk):
            segs += [jnp.zeros((v.shape[0], sh), F32), v[:, b0:b0 + n_chunk - sh]]
        return jnp.concatenate(segs, axis=1)

    def group(g):
        ut = ut_ref[g]
        sl = jnp.dot(win_ref[g], ut, preferred_element_type=F32)
        yield
        apr = apr_ref[g]
        api = api_ref[g]
        prev_r, prev_i = [], []
        for b0 in range(0, p, SUBLANES * SCAN_INTERLEAVE):
            tiles = range(b0, b0 + SUBLANES * SCAN_INTERLEAVE, SUBLANES)
            s = {r0: (sl[r0:r0 + SUBLANES], sl[p + r0:p + r0 + SUBLANES]) for r0 in tiles}
            for j in range(n_steps):
                sh = 1 << j
                keep = pos >= sh
                for r0 in tiles:
                    s_r, s_i = s[r0]
                    a_r = apr[r0:r0 + SUBLANES, j:j + 1]
                    a_i = api[r0:r0 + SUBLANES, j:j + 1]
                    if sh % LANES == 0:
                        p_r, p_i = shift_whole_tiles(s_r, sh), shift_whole_tiles(s_i, sh)
                    else:
                        p_r = jnp.where(keep, pltpu.roll(s_r, sh, axis=1), 0.0)
                        p_i = jnp.where(keep, pltpu.roll(s_i, sh, axis=1), 0.0)
                    s[r0] = (s_r + a_r * p_r - a_i * p_i, s_i + a_r * p_i + a_i * p_r)
                yield
            keep = pos >= 1
            for r0 in tiles:
                prev_r.append(jnp.where(keep, pltpu.roll(s[r0][0], 1, axis=1), 0.0))
                prev_i.append(jnp.where(keep, pltpu.roll(s[r0][1], 1, axis=1), 0.0))
        prev = jnp.concatenate(prev_r + prev_i, axis=0).astype(BF16)
        yield
        y = jnp.dot(tt_ref[g], ut, preferred_element_type=F32)
        y += jnp.dot(wout_ref[g], prev, preferred_element_type=F32)
        yield
        y += dsk_ref[g] * ut.astype(F32)
        y_ref[g] = _gelu_tanh(y).astype(BF16)

    active = [group(g) for g in range(gb)]
    while active:
        for gen in list(active):
            try:
                next(gen)
            except StopIteration:
                active.remove(gen)


def _ssm(ut, prep, dsk, *, n_chunk, gb=4):
    tt, win, wout, apr, api = prep
    g, row, nk = ut.shape
    p = apr.shape[1]
    blk = lambda shape: pl.BlockSpec((gb,) + shape, lambda i: (i, 0, 0))
    kern = functools.partial(_ssm_kernel, gb=gb, n_chunk=n_chunk)
    return pl.pallas_call(
        kern,
        grid=(g // gb,),
        in_specs=[blk((row, nk)), blk((row, row)), blk((2 * p, row)), blk((row, 2 * p)),
                  blk((p, SSM_POWERS)), blk((p, SSM_POWERS)), blk((row, 1))],
        out_specs=blk((row, nk)),
        out_shape=jax.ShapeDtypeStruct((g, row, nk), BF16),
        compiler_params=_params("parallel"),
        name="ssm",
    )(ut, tt, win, wout, apr, api, dsk)


def _ssm_out_kernel(yt_ref, wglu_ref, bglu_ref, wso_ref, o_ref, *, t_per_pass):
    n_grp, _, ch = yt_ref.shape
    width = n_grp * SSM_GROUP
    def gated(t0):
        y = jnp.concatenate(
            [yt_ref[:, t * SSM_GROUP:(t + 1) * SSM_GROUP, :].reshape(width, ch)
             for t in range(t0, t0 + t_per_pass)], axis=1)
        return y, jnp.dot(wglu_ref[...], y, preferred_element_type=F32) + bglu_ref[...]

    def project(t0, y, gate):
        glu = (y.astype(F32) * _sigmoid(gate)).astype(BF16)
        yo = lax.dot_general(glu, wso_ref[...], (((0,), (0,)), ((), ())), preferred_element_type=F32)
        for i in range(t_per_pass):
            o_ref[t0 + i] = yo[i * ch:(i + 1) * ch, :].astype(BF16)

    starts = list(range(0, SSM_CHUNK, t_per_pass))
    pending = gated(starts[0])
    for n, t0 in enumerate(starts):
        nxt = gated(starts[n + 1]) if n + 1 < len(starts) else None
        project(t0, *pending)
        pending = nxt


def _ssm_out(yt, w_glu, b_glu, w_ssm_out, *, ch=LANES, t_per_pass=4):
    n_grp, row, n_chunk_all = yt.shape
    width, d = w_ssm_out.shape
    return pl.pallas_call(
        functools.partial(_ssm_out_kernel, t_per_pass=t_per_pass),
        grid=(n_chunk_all // ch,),
        in_specs=[pl.BlockSpec((n_grp, row, ch), lambda i: (0, 0, i)),
                  _resident((width, width)), _resident((width, 1)), _resident((width, d))],
        out_specs=pl.BlockSpec((None, SSM_CHUNK, ch, d), lambda i: (i, 0, 0, 0)),
        out_shape=jax.ShapeDtypeStruct((n_chunk_all // ch, SSM_CHUNK, ch, d), BF16),
        compiler_params=_params("parallel"),
        name="ssm_out",
    )(yt, w_glu.T.astype(BF16), b_glu.reshape(width, 1), w_ssm_out.astype(BF16))


def _mix_ffn_kernel(x_ref, o0, o1, o2, st0, st1, st2, ys_ref, ga_ref, gs_ref,
                    wao_ref, wo_ref, g_ref, b_ref, wg_ref, wu_ref, wd_ref, g3_ref, b3_ref,
                    out_ref, o_stage, st_stage, ys_stage, x2_ref, *, sub, chunks, alpha):
    step = pl.program_id(0)
    slot_w = step % 2
    slot_r = 1 - slot_w

    @pl.when(step == 0)
    def _():
        x2_ref[...] = jnp.zeros(x2_ref.shape, F32)

    lane = lax.broadcasted_iota(jnp.int32, (sub, LANES), 1)
    lo_half = lane < HEAD_DIM

    def token_order(ref, stage, r0):
        dil = ref.shape[0]
        n = sub // dil
        i0 = r0 // dil
        if dil == 1:
            return ref[0, i0:i0 + n, :].astype(F32)
        n_col = ref.shape[2] // LANES
        pitch = _scatter_pitch(dil)
        for r in range(dil):
            blk = ref[r, i0:i0 + n, :].astype(F32)
            for c in range(n_col):
                stage[c, pl.ds(r, n, stride=pitch), :] = blk[:, c * LANES:(c + 1) * LANES]
        cols = [stage[c, 0:n * pitch, :] for c in range(n_col)]
        if pitch != dil:
            cols = [a.reshape(n, pitch, LANES)[:, :dil, :].reshape(sub, LANES) for a in cols]
        return jnp.concatenate(cols, axis=1)

    def head_cols(vals):
        cols = []
        for hp in range(GROUP_WIDTH // LANES):
            cols.append(jnp.where(lo_half, vals[2 * hp], vals[2 * hp + 1]))
        return jnp.concatenate(cols, axis=1)

    def merge_groups(k):
        r0 = k * sub
        sts = [token_order(st, st_stage.at[k, i], r0) for i, st in enumerate((st0, st1, st2))]
        outs = [token_order(o, o_stage.at[k, i], r0) for i, o in enumerate((o0, o1, o2))]
        dens = [pltpu.roll(st, LANES - HEADS_PER_GROUP, axis=1) for st in sts]
        mx = jnp.maximum(jnp.maximum(sts[0], sts[1]), sts[2])
        w = [den * jnp.exp2(st - mx) for den, st in zip(dens, sts)]
        tot = jnp.where(lane < HEADS_PER_GROUP, w[0] + w[1] + w[2], 1.0)
        att = None
        for g in range(N_ATTN_GROUPS):
            wt = w[g] / tot
            term = head_cols([wt[:, h:h + 1] for h in range(HEADS_PER_GROUP)]) * outs[g]
            att = term if att is None else att + term
        return att.astype(BF16)

    def gate(k, att):
        rows = slice(k * sub, (k + 1) * sub)
        y_attn = jnp.dot(att, wao_ref[...], preferred_element_type=F32)
        y_ssm = token_order(ys_ref, ys_stage.at[k], k * sub)
        merged = ga_ref[rows, :].astype(F32) * y_attn + gs_ref[rows, :].astype(F32) * y_ssm
        return merged.astype(BF16)

    def project(k, merged):
        rows = slice(k * sub, (k + 1) * sub)
        mix = jnp.dot(merged, wo_ref[...], preferred_element_type=F32)
        x2_ref[slot_w, rows, :] = _layer_norm(alpha * x_ref[rows, :] + mix, g_ref[...], b_ref[...])

    x2_prev = x2_ref[slot_r]
    ffn_state = {}

    def ffn_gate_up(k, c):
        rows = slice(k * sub, (k + 1) * sub)
        if c == chunks[0]:
            ffn_state[k] = {"xb": x2_prev[rows, :].astype(BF16), "acc": None}
        xb = ffn_state[k]["xb"]
        ffn_state[k]["gu"] = (jnp.dot(xb, wg_ref[:, c[0]:c[1]], preferred_element_type=F32),
                              jnp.dot(xb, wu_ref[:, c[0]:c[1]], preferred_element_type=F32))

    def ffn_down(k, c):
        rows = slice(k * sub, (k + 1) * sub)
        gate_v, up = ffn_state[k].pop("gu")
        h = (gate_v * _sigmoid(gate_v) * up).astype(BF16)
        part = jnp.dot(h, wd_ref[c[0]:c[1], :], preferred_element_type=F32)
        acc = ffn_state[k]["acc"]
        ffn_state[k]["acc"] = part if acc is None else acc + part
        if c == chunks[-1]:
            y = alpha * x2_prev[rows, :] + 0.5 * ffn_state.pop(k)["acc"]
            out_ref[rows, :] = _layer_norm(y, g3_ref[...], b3_ref[...])

    n_sub = x_ref.shape[0] // sub
    ffn_stages = [f for k in range(n_sub) for c in chunks
                  for f in (functools.partial(ffn_gate_up, k, c), functools.partial(ffn_down, k, c))]
    mix_stages = [f for k in range(n_sub) for f in (functools.partial(merge_groups, k),
                                                    functools.partial(gate, k),
                                                    functools.partial(project, k))]
    carry = ()
    for f in ffn_stages:
        f()
        if mix_stages:
            out = mix_stages.pop(0)(*carry)
            carry = () if out is None else (out,)
    assert not mix_stages


def _mix_ffn(x1, os_, sts, y_ssm, ga, gs, w_attn_out, w_o, g2, b2, wg, wu, wd, g3, b3, *,
             seq, alpha, tm=512, sub=256, chunk=1024):
    t, d = x1.shape
    d_ff = wg.shape[1]
    tiles = seq // tm
    n_tiles = t // tm
    cur = lambda i: jnp.minimum(i, n_tiles - 1)
    tile = lambda w: pl.BlockSpec((tm, w), lambda i: (cur(i), 0))

    def by_class(a):
        _, dil, n, w = a.shape
        return pl.BlockSpec((None, dil, tm // dil, w), lambda i: (cur(i) // tiles, 0, cur(i) % tiles, 0))

    n_grp = len(os_)
    stage_rows = max(sub // a.shape[1] * _scatter_pitch(a.shape[1]) for a in os_)
    per_big = y_ssm.shape[2] * SSM_CHUNK // tm
    ys_spec = pl.BlockSpec((None, SSM_CHUNK, tm // SSM_CHUNK, d),
                           lambda i: (cur(i) // per_big, 0, cur(i) % per_big, 0))
    ys_rows = sub // SSM_CHUNK * _scatter_pitch(SSM_CHUNK)
    vec = lambda v: v.reshape(1, d)
    return pl.pallas_call(
        functools.partial(_mix_ffn_kernel, sub=sub, chunks=_ffn_chunks(d_ff, chunk), alpha=alpha),
        grid=(n_tiles + 1,),
        in_specs=[tile(d)] + [by_class(a) for a in os_] + [by_class(a) for a in sts] + [ys_spec, tile(d), tile(d),
                  _resident(w_attn_out.shape), _resident(w_o.shape), _resident((1, d)), _resident((1, d)),
                  _resident((d, d_ff)), _resident((d, d_ff)), _resident((d_ff, d)),
                  _resident((1, d)), _resident((1, d))],
        out_specs=pl.BlockSpec((tm, d), lambda i: (jnp.maximum(i - 1, 0), 0)),
        out_shape=jax.ShapeDtypeStruct((t, d), F32),
        scratch_shapes=[pltpu.VMEM((tm // sub, n_grp, GROUP_WIDTH // LANES, stage_rows, LANES), F32),
                        pltpu.VMEM((tm // sub, n_grp, 1, stage_rows, LANES), F32),
                        pltpu.VMEM((tm // sub, d // LANES, ys_rows, LANES), F32),
                        pltpu.VMEM((2, tm, d), F32)],
        compiler_params=_params("arbitrary"),
        name="mix_ffn",
    )(x1, *os_, *sts, y_ssm, ga, gs, w_attn_out.astype(BF16), w_o.astype(BF16), vec(g2), vec(b2),
      wg.astype(BF16), wu.astype(BF16), wd.astype(BF16), vec(g3), vec(b3))


def _layer(x, positions, w_in, w_attn_out, a_re, a_im, log_dt, b_re, b_im, c_re, c_im, d_skip,
           w_glu, b_glu, w_ssm_out, w_o, ffn1, ffn2, ln1, ln2, ln3, *, alpha):
    batch, seq, d = x.shape
    t = batch * seq
    n_grp = a_re.shape[0]
    ssm_width = n_grp * SSM_GROUP
    u_lo = 3 * ATTN_WIDTH

    x1 = _ffn_ln(x.reshape(t, d), *ffn1, *ln1, alpha=alpha)
    w_in_b = w_in.astype(BF16)
    *qkvs, ga, gs = _in_proj(x1, positions, w_in_b, batch=batch, seq=seq, ssm_width=ssm_width)

    os_, sts = [], []
    for qkv, (window, dilation) in zip(qkvs, ATTN_PATTERNS):
        assert window == ATTN_BLK * dilation
        o, st = _dilated_attention(qkv)
        os_.append(o)
        sts.append(st)

    ut = _ssm_in(x1, w_in_b, u_lo, ssm_width)
    prep = _ssm_prep(a_re, a_im, log_dt, b_re, b_im, c_re, c_im)
    dsk = jnp.tile(d_skip.reshape(n_grp, 1, SSM_GROUP), (1, SSM_CHUNK, 1)).reshape(n_grp, SSM_ROW, 1)
    yt = _ssm(ut, prep, dsk, n_chunk=seq // SSM_CHUNK)
    y_ssm = _ssm_out(yt, w_glu, b_glu, w_ssm_out)

    x3 = _mix_ffn(x1, os_, sts, y_ssm, ga, gs, w_attn_out, w_o, *ln2, *ffn2, *ln3, seq=seq, alpha=alpha)
    return x3.reshape(batch, seq, d)


def kernel(x, positions, w_in, w_attn_out, a_re, a_im, log_dt, b_re, b_im, c_re, c_im, d_skip, w_glu, b_glu, w_ssm_out, w_o, ffn1_wg, ffn1_wu, ffn1_wd, ffn2_wg, ffn2_wu, ffn2_wd, ln1_g, ln1_b, ln2_g, ln2_b, ln3_g, ln3_b):
    depth = w_in.shape[0]
    alpha = (2.0 * depth) ** 0.25
    for i in range(depth):
        x = _layer(x, positions, w_in[i], w_attn_out[i], a_re[i], a_im[i], log_dt[i], b_re[i], b_im[i],
                   c_re[i], c_im[i], d_skip[i], w_glu[i], b_glu[i], w_ssm_out[i], w_o[i],
                   (ffn1_wg[i], ffn1_wu[i], ffn1_wd[i]), (ffn2_wg[i], ffn2_wu[i], ffn2_wd[i]),
                   (ln1_g[i], ln1_b[i]), (ln2_g[i], ln2_b[i]), (ln3_g[i], ln3_b[i]), alpha=alpha)
    return x
```

```python
import functools

import jax
import jax.numpy as jnp
from jax import lax
from jax.experimental import pallas as pl
from jax.experimental.pallas import tpu as pltpu

F32 = jnp.float32
BF16 = jnp.bfloat16

HEAD_DIM = 64
HEADS_PER_GROUP = 4
GROUP_WIDTH = HEADS_PER_GROUP * HEAD_DIM
ATTN_PATTERNS = ((128, 1), (512, 4), (2048, 16))
N_ATTN_GROUPS = len(ATTN_PATTERNS)
ATTN_WIDTH = N_ATTN_GROUPS * GROUP_WIDTH
ROT_DIM = HEAD_DIM // 4
ROPE_THETA = 500000.0
ROPE_ROWS = 32
ATTN_BLK = 128
NEG_INF = -1e30
LOG2_E = 1.4426950408889634
SSM_GROUP = 16
SSM_CHUNK = 16
SSM_ROW = SSM_CHUNK * SSM_GROUP
SSM_POWERS = 16
SCAN_INTERLEAVE = 8
CHUNK_PITCH = 24
LN_EPS = 1e-5

LANES = 128
SUBLANES = 8
VMEM_LIMIT_BYTES = 56 * 1024 * 1024


def _scatter_pitch(stride):
    return CHUNK_PITCH if stride == SSM_CHUNK else stride


def _rows_by_class(val, stage, dil):
    rows, w = val.shape
    n = rows // dil
    pitch = _scatter_pitch(dil)
    n_col = w // LANES
    for c in range(n_col):
        col = val[:, c * LANES:(c + 1) * LANES]
        if pitch != dil:
            col = jnp.concatenate([col.reshape(n, dil, LANES), jnp.zeros((n, pitch - dil, LANES), F32)], axis=1)
            col = col.reshape(n * pitch, LANES)
        stage[c, 0:n * pitch, :] = col

    def get(r):
        return jnp.concatenate([stage[c, pl.ds(r, n, stride=pitch), :] for c in range(n_col)], axis=1)

    return get


def _params(*semantics):
    return pltpu.CompilerParams(dimension_semantics=semantics, vmem_limit_bytes=VMEM_LIMIT_BYTES)


def _resident(shape):
    zeros = (0,) * len(shape)
    return pl.BlockSpec(shape, lambda *_: zeros, pipeline_mode=pl.Buffered(1))


def _layer_norm(y, g, b):
    mu = jnp.mean(y, axis=-1, keepdims=True)
    yc = y - mu
    var = jnp.mean(yc * yc, axis=-1, keepdims=True)
    return yc * lax.rsqrt(var + LN_EPS) * g + b


def _sigmoid(x):
    return 1.0 / (1.0 + jnp.exp(-x))


def _gelu_tanh(x):
    c = 0.7978845608028654
    return 0.5 * x * (1.0 + jnp.tanh(c * (x + 0.044715 * (x * x * x))))


def _ffn_ln_kernel(x_ref, wg_ref, wu_ref, wd_ref, g_ref, b_ref, o_ref, *, chunks, sub, alpha):
    for r0 in range(0, x_ref.shape[0], sub):
        x = x_ref[r0:r0 + sub, :]
        xb = x.astype(BF16)
        acc = None
        for c0, c1 in chunks:
            gate = jnp.dot(xb, wg_ref[:, c0:c1], preferred_element_type=F32)
            up = jnp.dot(xb, wu_ref[:, c0:c1], preferred_element_type=F32)
            h = (gate * _sigmoid(gate) * up).astype(BF16)
            part = jnp.dot(h, wd_ref[c0:c1, :], preferred_element_type=F32)
            acc = part if acc is None else acc + part
        y = alpha * x + 0.5 * acc
        o_ref[r0:r0 + sub, :] = _layer_norm(y, g_ref[...], b_ref[...])


def _ffn_chunks(d_ff, width):
    edges = list(range(0, d_ff, width)) + [d_ff]
    return tuple(zip(edges[:-1], edges[1:]))


def _ffn_ln(x, wg, wu, wd, g, b, *, alpha, tm=1024, sub=256, chunk=1024):
    t, d = x.shape
    d_ff = wg.shape[1]
    kern = functools.partial(_ffn_ln_kernel, chunks=_ffn_chunks(d_ff, chunk), sub=sub, alpha=alpha)
    return pl.pallas_call(
        kern,
        grid=(t // tm,),
        in_specs=[
            pl.BlockSpec((tm, d), lambda i: (i, 0)),
            _resident((d, d_ff)), _resident((d, d_ff)), _resident((d_ff, d)),
            _resident((1, d)), _resident((1, d)),
        ],
        out_specs=pl.BlockSpec((tm, d), lambda i: (i, 0)),
        out_shape=jax.ShapeDtypeStruct((t, d), F32),
        compiler_params=_params("parallel"),
        name="ffn_ln",
    )(x, wg.astype(BF16), wu.astype(BF16), wd.astype(BF16), g.reshape(1, d), b.reshape(1, d))


def _rope_spread():
    half = ROT_DIM // 2
    rows = lax.broadcasted_iota(jnp.int32, (ROPE_ROWS, 2 * LANES), 0)
    cols = lax.broadcasted_iota(jnp.int32, (ROPE_ROWS, 2 * LANES), 1)
    in_head = cols % HEAD_DIM
    freq = in_head % half
    is_cos = cols < LANES
    rot = in_head < ROT_DIM
    cos_part = is_cos & rot & (rows == freq)
    one_part = is_cos & jnp.logical_not(rot) & (rows == 2 * half)
    sin_part = jnp.logical_not(is_cos) & rot & (rows == half + freq)
    sign = jnp.where(in_head < half, 1.0, -1.0)
    return jnp.where(cos_part | one_part, 1.0, jnp.where(sin_part, sign, 0.0)).astype(BF16)


def _in_proj_kernel(x_ref, pos_ref, invf_ref, w_ref,
                    qkv0_ref, qkv1_ref, qkv2_ref, ga_ref, gs_ref, stage_ref, *, splits, sub):
    half = ROT_DIM // 2
    spread = _rope_spread()
    first = lax.broadcasted_iota(jnp.int32, (sub, LANES), 1) % HEAD_DIM < half
    s0, s1, s2, s3, s4, s5 = splits
    tn = (((0,), (0,)), ((), ()))

    for r0 in range(0, x_ref.shape[0], sub):
        xb = x_ref[r0:r0 + sub, :].astype(BF16)

        ang = invf_ref[...] * pos_ref[:, r0:r0 + sub].astype(F32)
        tab = jnp.concatenate([jnp.cos(ang), jnp.sin(ang), jnp.ones((half, sub), F32),
                               jnp.zeros((ROPE_ROWS - 3 * half, sub), F32)], axis=0)
        tab_hi = tab.astype(BF16)
        tab_lo = (tab - tab_hi.astype(F32)).astype(BF16)
        cs = (lax.dot_general(tab_hi, spread, tn, preferred_element_type=F32)
              + lax.dot_general(tab_lo, spread, tn, preferred_element_type=F32))
        cos = cs[:, :LANES]
        sin = cs[:, LANES:]

        def rotate(z):
            cols = []
            for c in range(z.shape[1] // LANES):
                zc = z[:, c * LANES:(c + 1) * LANES]
                zs = zc * sin
                up = pltpu.roll(jnp.where(first, zs, 0.0), half, axis=1)
                dn = pltpu.roll(jnp.where(first, 0.0, zs), LANES - half, axis=1)
                cols.append(zc * cos + up + dn)
            return jnp.concatenate(cols, axis=1)

        def proj(lo, hi):
            return jnp.dot(xb, w_ref[:, lo:hi], preferred_element_type=F32)

        q = rotate(proj(0, s0)) * (HEAD_DIM ** -0.5 * LOG2_E)
        k = rotate(proj(s0, s1))
        v = proj(s1, s2)
        for gi, out_ref in enumerate((qkv0_ref, qkv1_ref, qkv2_ref)):
            dil = ATTN_PATTERNS[gi][1]
            c0 = gi * GROUP_WIDTH
            qkv = jnp.concatenate([a[:, c0:c0 + GROUP_WIDTH] for a in (q, k, v)], axis=1)
            n = sub // dil
            i0 = r0 // dil
            if dil == 1:
                out_ref[0, i0:i0 + n, :] = qkv.astype(BF16)
            else:
                get = _rows_by_class(qkv, stage_ref.at[r0 // sub, gi - 1], dil)
                for r in range(dil):
                    out_ref[r, i0:i0 + n, :] = get(r).astype(BF16)
        ga_ref[r0:r0 + sub, :] = _sigmoid(proj(s3, s4)).astype(BF16)
        gs_ref[r0:r0 + sub, :] = _sigmoid(proj(s4, s5)).astype(BF16)


def _in_proj(x1, positions, w_in, *, batch, seq, ssm_width, tm=1024, sub=256):
    t, d = x1.shape
    splits = (ATTN_WIDTH, 2 * ATTN_WIDTH, 3 * ATTN_WIDTH, 3 * ATTN_WIDTH + ssm_width,
              3 * ATTN_WIDTH + ssm_width + d, 3 * ATTN_WIDTH + ssm_width + 2 * d)
    assert splits[-1] == w_in.shape[1] and seq % tm == 0
    half = ROT_DIM // 2
    invf = (ROPE_THETA ** (-jnp.arange(half, dtype=F32) * 2.0 / ROT_DIM)).reshape(half, 1)
    tiles = seq // tm
    qkv_w = 3 * GROUP_WIDTH
    qkv_specs = [pl.BlockSpec((None, dil, tm // dil, qkv_w), lambda i: (i // tiles, 0, i % tiles, 0))
                 for _, dil in ATTN_PATTERNS]
    qkv_shapes = [jax.ShapeDtypeStruct((batch, dil, seq // dil, qkv_w), BF16) for _, dil in ATTN_PATTERNS]
    widths = (d, d)
    kern = functools.partial(_in_proj_kernel, splits=splits, sub=sub)
    n_strided = sum(dil > 1 for _, dil in ATTN_PATTERNS)
    stage_rows = max(sub // dil * _scatter_pitch(dil) for _, dil in ATTN_PATTERNS)
    return pl.pallas_call(
        kern,
        grid=(t // tm,),
        in_specs=[
            pl.BlockSpec((tm, d), lambda i: (i, 0)),
            pl.BlockSpec((1, tm), lambda i: (0, i)),
            _resident((half, 1)),
            _resident(w_in.shape),
        ],
        out_specs=qkv_specs + [pl.BlockSpec((tm, w), lambda i: (i, 0)) for w in widths],
        out_shape=qkv_shapes + [jax.ShapeDtypeStruct((t, w), BF16) for w in widths],
        scratch_shapes=[pltpu.VMEM((tm // sub, n_strided, qkv_w // LANES, stage_rows, LANES), F32)],
        compiler_params=_params("parallel"),
        name="in_proj",
    )(x1, positions.reshape(1, t), invf, w_in)


def _attn_kernel(q_ref, kc_ref, kp_ref, vc_ref, vp_ref, o_ref, st_ref, *, n_sub):
    first = pl.program_id(2) == 0
    row = lax.broadcasted_iota(jnp.int32, (ATTN_BLK, 2 * ATTN_BLK), 0)
    col = lax.broadcasted_iota(jnp.int32, (ATTN_BLK, 2 * ATTN_BLK), 1)
    band = (col >= row) & (col <= row + ATTN_BLK)
    lane = lax.broadcasted_iota(jnp.int32, (ATTN_BLK, LANES), 1)
    lo_half = lane < HEAD_DIM

    n_pair = GROUP_WIDTH // LANES
    tiles = [(rc, j, hp, hh) for rc in range(q_ref.shape[0]) for j in range(n_sub)
             for hp in range(n_pair) for hh in range(2)]

    def window(ref_prev, ref_cur, j, c0):
        r0 = j * ATTN_BLK
        if j == 0:
            return jnp.concatenate([ref_prev[:, c0:c0 + LANES], ref_cur[0:ATTN_BLK, c0:c0 + LANES]], axis=0)
        return ref_cur[r0 - ATTN_BLK:r0 + ATTN_BLK, c0:c0 + LANES]

    def scores(tile):
        rc, j, hp, hh = tile
        q2 = q_ref[rc, j * ATTN_BLK:(j + 1) * ATTN_BLK, hp * LANES:(hp + 1) * LANES]
        qm = jnp.where(lo_half if hh == 0 else jnp.logical_not(lo_half), q2, jnp.zeros_like(q2))
        return lax.dot_general(qm, window(kp_ref.at[rc], kc_ref.at[rc], j, hp * LANES), (((1,), (1,)), ((), ())),
                               preferred_element_type=F32)

    def softmax(tile, s):
        valid = band & ((col >= ATTN_BLK) | jnp.logical_not(first)) if tile[1] == 0 else band
        s = jnp.where(valid, s, NEG_INF)
        m = jnp.max(s, axis=1, keepdims=True)
        p = jnp.exp2(s - m)
        return p.astype(BF16), m, jnp.sum(p, axis=1, keepdims=True)

    def weighted_values(tile, p, den):
        rc, j, hp, _ = tile
        return jnp.dot(p, window(vp_ref.at[rc], vc_ref.at[rc], j, hp * LANES), preferred_element_type=F32) / den

    s_q, p_q, outs = {}, {}, {}
    for step in range(len(tiles) + 2):
        if step < len(tiles):
            s_q[step] = scores(tiles[step])
        if 0 <= step - 1 < len(tiles):
            p_q[step - 1] = softmax(tiles[step - 1], s_q.pop(step - 1))
        if 0 <= step - 2 < len(tiles):
            t = step - 2
            rc, j, hp, hh = tiles[t]
            p, m, den = p_q.pop(t)
            outs[hh] = weighted_values(tiles[t], p, den)
            h = 2 * hp + hh
            rows = slice(j * ATTN_BLK, (j + 1) * ATTN_BLK)
            if h == 0:
                st_ref[rc, rows, :] = jnp.zeros((ATTN_BLK, LANES), F32)
            st_ref[rc, rows, h:h + 1] = m
            st_ref[rc, rows, HEADS_PER_GROUP + h:HEADS_PER_GROUP + h + 1] = den
            if hh == 1:
                o_ref[rc, rows, hp * LANES:(hp + 1) * LANES] = (
                    jnp.where(lo_half, outs[0], outs[1]).astype(BF16))


def _dilated_attention(qkv):
    batch, dilation, n, _ = qkv.shape
    rows_per_step = 2048
    qb = min(rows_per_step, n)
    n_sub = qb // ATTN_BLK
    rb = min(max(rows_per_step // qb, 1), dilation)

    def cur(part):
        return pl.BlockSpec((None, rb, qb, GROUP_WIDTH), lambda b, r, i: (b, r, i, part))

    def prev(part):
        return pl.BlockSpec((None, rb, ATTN_BLK, GROUP_WIDTH),
                            lambda b, r, i: (b, r, jnp.maximum(i * n_sub - 1, 0), part))

    return pl.pallas_call(
        functools.partial(_attn_kernel, n_sub=n_sub),
        grid=(batch, dilation // rb, n // qb),
        in_specs=[cur(0), cur(1), prev(1), cur(2), prev(2)],
        out_specs=[pl.BlockSpec((None, rb, qb, GROUP_WIDTH), lambda b, r, i: (b, r, i, 0)),
                   pl.BlockSpec((None, rb, qb, LANES), lambda b, r, i: (b, r, i, 0))],
        out_shape=[jax.ShapeDtypeStruct((batch, dilation, n, GROUP_WIDTH), BF16),
                   jax.ShapeDtypeStruct((batch, dilation, n, LANES), F32)],
        compiler_params=_params("parallel", "parallel", "parallel"),
        name=f"attn_d{dilation}",
    )(qkv, qkv, qkv, qkv, qkv)


def _ssm_in_kernel(*refs, n_grp, ch, n_w):
    x_refs, w_refs, ut_ref = refs[:-n_w - 1], refs[-n_w - 1:-1], refs[-1]
    w = jnp.concatenate([w_ref[...] for w_ref in w_refs], axis=1)
    for s in range(0, SSM_CHUNK, 2):
        xs = jnp.concatenate(
            [jnp.concatenate([x_ref[pl.ds(s + i, ch, stride=SSM_CHUNK), :] for x_ref in x_refs], axis=1)
             for i in range(2)], axis=0).astype(BF16)
        ut = lax.dot_general(w, xs, (((0,), (1,)), ((), ())), preferred_element_type=F32)
        for i in range(2):
            ut_ref[:, (s + i) * SSM_GROUP:(s + i + 1) * SSM_GROUP, :] = (
                ut[:, i * ch:(i + 1) * ch].reshape(n_grp, SSM_GROUP, ch).astype(BF16))


def _ssm_in(x1, w_in_b, u_lo, width, *, ch=LANES, wblk=256):
    t, d = x1.shape
    assert u_lo % wblk == 0 and width % wblk == 0
    n_grp = width // SSM_GROUP
    n_chunk_all = t // SSM_CHUNK
    tm = ch * SSM_CHUNK
    n_w = width // wblk
    slabs = [pl.BlockSpec((tm, LANES), lambda i, c=c: (i, c)) for c in range(d // LANES)]
    w_cols = [pl.BlockSpec((d, wblk), lambda i, c=c: (0, u_lo // wblk + c), pipeline_mode=pl.Buffered(1))
              for c in range(n_w)]
    return pl.pallas_call(
        functools.partial(_ssm_in_kernel, n_grp=n_grp, ch=ch, n_w=n_w),
        grid=(n_chunk_all // ch,),
        in_specs=slabs + w_cols,
        out_specs=pl.BlockSpec((n_grp, SSM_ROW, ch), lambda i: (0, 0, i)),
        out_shape=jax.ShapeDtypeStruct((n_grp, SSM_ROW, n_chunk_all), BF16),
        compiler_params=_params("parallel"),
        name="ssm_in",
    )(*([x1] * (d // LANES)), *([w_in_b] * n_w))


def _hdot(a, b):
    return jnp.dot(a, b, preferred_element_type=F32, precision=lax.Precision.HIGHEST)


def _cmul(ar, ai, br, bi):
    return ar * br - ai * bi, ar * bi + ai * br


def _cpowers(lbr, lbi, n, bits):
    out_r = jnp.ones(n.shape, F32)
    out_i = jnp.zeros(n.shape, F32)
    for b in range(bits):
        sel = (n & (1 << b)) != 0
        out_r, out_i = _cmul(out_r, out_i, jnp.where(sel, lbr, 1.0), jnp.where(sel, lbi, 0.0))
        lbr, lbi = _cmul(lbr, lbi, lbr, lbi)
    return out_r, out_i


def _ssm_prep_kernel(are_l, aim_l, ldt_ref, bre_ref, bim_ref, cre_ref, cim_ref,
                     tt_ref, win_ref, wout_ref, apr_ref, api_ref, *, gb):
    p = are_l.shape[2]
    eye = (lax.broadcasted_iota(jnp.int32, (p, p), 0) == lax.broadcasted_iota(jnp.int32, (p, p), 1)).astype(F32)
    expand = (lax.broadcasted_iota(jnp.int32, (SSM_CHUNK, SSM_ROW), 1) // SSM_GROUP
              == lax.broadcasted_iota(jnp.int32, (SSM_CHUNK, SSM_ROW), 0)).astype(F32)
    expand_t = (lax.broadcasted_iota(jnp.int32, (SSM_ROW, SSM_CHUNK), 0) // SSM_GROUP
                == lax.broadcasted_iota(jnp.int32, (SSM_ROW, SSM_CHUNK), 1)).astype(F32)
    rem = SSM_CHUNK - 1 - lax.broadcasted_iota(jnp.int32, (p, SSM_CHUNK), 1)
    tp1 = lax.broadcasted_iota(jnp.int32, (SSM_CHUNK, p), 0) + 1
    col = lax.broadcasted_iota(jnp.int32, (p, SSM_POWERS), 1)
    lane = lax.broadcasted_iota(jnp.int32, (SSM_GROUP, SSM_ROW), 1)
    chunk_bits = SSM_CHUNK.bit_length()

    def group(g):
        dt = jnp.exp(ldt_ref[g])
        lam_r = jnp.minimum(are_l[g], -1e-4)
        lam_i = aim_l[g]
        mag_l = jnp.exp(lam_r * dt)
        lbr_l = mag_l * jnp.cos(lam_i * dt)
        lbi_l = mag_l * jnp.sin(lam_i * dt)
        inv = 1.0 / (lam_r * lam_r + lam_i * lam_i)
        cfr_l = ((lbr_l - 1.0) * lam_r + lbi_l * lam_i) * inv
        cfi_l = (lbi_l * lam_r - (lbr_l - 1.0) * lam_i) * inv
        stacked = jnp.concatenate([lbr_l, lbi_l, cfr_l, cfi_l, jnp.zeros((SUBLANES - 4, p), F32)], axis=0)
        cols = lax.dot_general(eye, stacked, (((1,), (1,)), ((), ())), preferred_element_type=F32,
                               precision=lax.Precision.HIGHEST)
        yield
        lbr, lbi = cols[:, 0:1], cols[:, 1:2]

        bre = bre_ref[g]
        bim = bim_ref[g]
        bbar_r, bbar_i = _cmul(cols[:, 2:3], cols[:, 3:4], bre, bim)
        pw_r, pw_i = _cpowers(lbr, lbi, rem, chunk_bits - 1)
        win_r, win_i = _cmul(_hdot(pw_r, expand), _hdot(pw_i, expand), bbar_r, bbar_i)
        win_ref[g] = jnp.concatenate([win_r, win_i], axis=0).astype(BF16)
        yield

        a_r, a_i = _cmul(pw_r[:, 0:1], pw_i[:, 0:1], lbr, lbi)
        apr = jnp.zeros((p, SSM_POWERS), F32)
        api = jnp.zeros((p, SSM_POWERS), F32)
        for j in range(SSM_POWERS):
            apr = jnp.where(col == j, a_r, apr)
            api = jnp.where(col == j, a_i, api)
            a_r, a_i = _cmul(a_r, a_i, a_r, a_i)
        apr_ref[g] = apr
        api_ref[g] = api
        yield

        e_r, e_i = _cpowers(lbr_l, lbi_l, tp1, chunk_bits)
        cre = cre_ref[g]
        cim = cim_ref[g]
        wo_r, wo_i = _cmul(_hdot(expand_t, e_r), _hdot(expand_t, e_i), cre, cim)
        wout_ref[g] = jnp.concatenate([wo_r, -wo_i], axis=1).astype(BF16)
        yield

        hrev = _hdot(cre[0:SSM_GROUP, :], win_r) - _hdot(cim[0:SSM_GROUP, :], win_i)
        yield
        blocks = []
        for t in range(SSM_CHUNK):
            hi = (t + 1) * SSM_GROUP
            rolled = hrev if hi == SSM_ROW else pltpu.roll(hrev, hi, axis=1)
            blocks.append(jnp.where(lane < hi, rolled, 0.0))
        tt_ref[g] = jnp.concatenate(blocks, axis=0).astype(BF16)

    active = [group(g) for g in range(gb)]
    while active:
        for gen in list(active):
            try:
                next(gen)
            except StopIteration:
                active.remove(gen)


def _ssm_prep(a_re, a_im, log_dt, b_re, b_im, c_re, c_im, *, gb=8):
    g, p = a_re.shape
    row = SSM_ROW
    blk = lambda shape: pl.BlockSpec((gb,) + shape, lambda i: (i, 0, 0))
    b_t = lambda b: jnp.tile(b, (1, 1, SSM_CHUNK))
    c_t = lambda c: jnp.tile(c, (1, SSM_CHUNK, 1))
    out_shapes = [
        jax.ShapeDtypeStruct((g, row, row), BF16),
        jax.ShapeDtypeStruct((g, 2 * p, row), BF16),
        jax.ShapeDtypeStruct((g, row, 2 * p), BF16),
        jax.ShapeDtypeStruct((g, p, SSM_POWERS), F32), jax.ShapeDtypeStruct((g, p, SSM_POWERS), F32),
    ]
    return pl.pallas_call(
        functools.partial(_ssm_prep_kernel, gb=gb),
        grid=(g // gb,),
        in_specs=[blk((1, p)), blk((1, p)), blk((1, 1)),
                  blk((p, row)), blk((p, row)), blk((row, p)), blk((row, p))],
        out_specs=[blk(s.shape[1:]) for s in out_shapes],
        out_shape=out_shapes,
        compiler_params=_params("parallel"),
        name="ssm_prep",
    )(a_re.reshape(g, 1, p), a_im.reshape(g, 1, p),
      log_dt.reshape(g, 1, 1), b_t(b_re), b_t(b_im), c_t(c_re), c_t(c_im))


def _ssm_kernel(ut_ref, tt_ref, win_ref, wout_ref, apr_ref, api_ref, dsk_ref, y_ref, *, gb, n_chunk):
    nk = ut_ref.shape[-1]
    p = apr_ref.shape[1]
    pos = lax.broadcasted_iota(jnp.int32, (SUBLANES, nk), 1) % n_chunk
    n_steps = (n_chunk - 1).bit_length()
    assert n_steps <= SSM_POWERS

    def shift_whole_tiles(v, sh):
        segs = []
        for b0 in range(0, nk, n_chunk):
            segs += [jnp.zeros((v.shape[0], sh), F32), v[:, b0:b0 + n_chunk - sh]]
        return jnp.concatenate(segs, axis=1)

    def group(g):
        ut = ut_ref[g]
        sl = jnp.dot(win_ref[g], ut, preferred_element_type=F32)
        yield
        apr = apr_ref[g]
        api = api_ref[g]
        prev_r, prev_i = [], []
        for b0 in range(0, p, SUBLANES * SCAN_INTERLEAVE):
            tiles = range(b0, b0 + SUBLANES * SCAN_INTERLEAVE, SUBLANES)
            s = {r0: (sl[r0:r0 + SUBLANES], sl[p + r0:p + r0 + SUBLANES]) for r0 in tiles}
            for j in range(n_steps):
                sh = 1 << j
                keep = pos >= sh
                for r0 in tiles:
                    s_r, s_i = s[r0]
                    a_r = apr[r0:r0 + SUBLANES, j:j + 1]
                    a_i = api[r0:r0 + SUBLANES, j:j + 1]
                    if sh % LANES == 0:
                        p_r, p_i = shift_whole_tiles(s_r, sh), shift_whole_tiles(s_i, sh)
                    else:
                        p_r = jnp.where(keep, pltpu.roll(s_r, sh, axis=1), 0.0)
                        p_i = jnp.where(keep, pltpu.roll(s_i, sh, axis=1), 0.0)
                    s[r0] = (s_r + a_r * p_r - a_i * p_i, s_i + a_r * p_i + a_i * p_r)
                yield
            keep = pos >= 1
            for r0 in tiles:
                prev_r.append(jnp.where(keep, pltpu.roll(s[r0][0], 1, axis=1), 0.0))
                prev_i.append(jnp.where(keep, pltpu.roll(s[r0][1], 1, axis=1), 0.0))
        prev = jnp.concatenate(prev_r + prev_i, axis=0).astype(BF16)
        yield
        y = jnp.dot(tt_ref[g], ut, preferred_element_type=F32)
        y += jnp.dot(wout_ref[g], prev, preferred_element_type=F32)
        yield
        y += dsk_ref[g] * ut.astype(F32)
        y_ref[g] = _gelu_tanh(y).astype(BF16)

    active = [group(g) for g in range(gb)]
    while active:
        for gen in list(active):
            try:
                next(gen)
            except StopIteration:
                active.remove(gen)


def _ssm(ut, prep, dsk, *, n_chunk, gb=4):
    tt, win, wout, apr, api = prep
    g, row, nk = ut.shape
    p = apr.shape[1]
    blk = lambda shape: pl.BlockSpec((gb,) + shape, lambda i: (i, 0, 0))
    kern = functools.partial(_ssm_kernel, gb=gb, n_chunk=n_chunk)
    return pl.pallas_call(
        kern,
        grid=(g // gb,),
        in_specs=[blk((row, nk)), blk((row, row)), blk((2 * p, row)), blk((row, 2 * p)),
                  blk((p, SSM_POWERS)), blk((p, SSM_POWERS)), blk((row, 1))],
        out_specs=blk((row, nk)),
        out_shape=jax.ShapeDtypeStruct((g, row, nk), BF16),
        compiler_params=_params("parallel"),
        name="ssm",
    )(ut, tt, win, wout, apr, api, dsk)


def _ssm_out_kernel(yt_ref, wglu_ref, bglu_ref, wso_ref, o_ref, *, t_per_pass):
    n_grp, _, ch = yt_ref.shape
    width = n_grp * SSM_GROUP
    def gated(t0):
        y = jnp.concatenate(
            [yt_ref[:, t * SSM_GROUP:(t + 1) * SSM_GROUP, :].reshape(width, ch)
             for t in range(t0, t0 + t_per_pass)], axis=1)
        return y, jnp.dot(wglu_ref[...], y, preferred_element_type=F32) + bglu_ref[...]

    def project(t0, y, gate):
        glu = (y.astype(F32) * _sigmoid(gate)).astype(BF16)
        yo = lax.dot_general(glu, wso_ref[...], (((0,), (0,)), ((), ())), preferred_element_type=F32)
        for i in range(t_per_pass):
            o_ref[t0 + i] = yo[i * ch:(i + 1) * ch, :].astype(BF16)

    starts = list(range(0, SSM_CHUNK, t_per_pass))
    pending = gated(starts[0])
    for n, t0 in enumerate(starts):
        nxt = gated(starts[n + 1]) if n + 1 < len(starts) else None
        project(t0, *pending)
        pending = nxt


def _ssm_out(yt, w_glu, b_glu, w_ssm_out, *, ch=LANES, t_per_pass=4):
    n_grp, row, n_chunk_all = yt.shape
    width, d = w_ssm_out.shape
    return pl.pallas_call(
        functools.partial(_ssm_out_kernel, t_per_pass=t_per_pass),
        grid=(n_chunk_all // ch,),
        in_specs=[pl.BlockSpec((n_grp, row, ch), lambda i: (0, 0, i)),
                  _resident((width, width)), _resident((width, 1)), _resident((width, d))],
        out_specs=pl.BlockSpec((None, SSM_CHUNK, ch, d), lambda i: (i, 0, 0, 0)),
        out_shape=jax.ShapeDtypeStruct((n_chunk_all // ch, SSM_CHUNK, ch, d), BF16),
        compiler_params=_params("parallel"),
        name="ssm_out",
    )(yt, w_glu.T.astype(BF16), b_glu.reshape(width, 1), w_ssm_out.astype(BF16))


def _mix_ffn_kernel(x_ref, o0, o1, o2, st0, st1, st2, ys_ref, ga_ref, gs_ref,
                    wao_ref, wo_ref, g_ref, b_ref, wg_ref, wu_ref, wd_ref, g3_ref, b3_ref,
                    out_ref, o_stage, st_stage, ys_stage, x2_ref, *, sub, chunks, alpha):
    step = pl.program_id(0)
    slot_w = step % 2
    slot_r = 1 - slot_w

    @pl.when(step == 0)
    def _():
        x2_ref[...] = jnp.zeros(x2_ref.shape, F32)

    lane = lax.broadcasted_iota(jnp.int32, (sub, LANES), 1)
    lo_half = lane < HEAD_DIM

    def token_order(ref, stage, r0):
        dil = ref.shape[0]
        n = sub // dil
        i0 = r0 // dil
        if dil == 1:
            return ref[0, i0:i0 + n, :].astype(F32)
        n_col = ref.shape[2] // LANES
        pitch = _scatter_pitch(dil)
        for r in range(dil):
            blk = ref[r, i0:i0 + n, :].astype(F32)
            for c in range(n_col):
                stage[c, pl.ds(r, n, stride=pitch), :] = blk[:, c * LANES:(c + 1) * LANES]
        cols = [stage[c, 0:n * pitch, :] for c in range(n_col)]
        if pitch != dil:
            cols = [a.reshape(n, pitch, LANES)[:, :dil, :].reshape(sub, LANES) for a in cols]
        return jnp.concatenate(cols, axis=1)

    def head_cols(vals):
        cols = []
        for hp in range(GROUP_WIDTH // LANES):
            cols.append(jnp.where(lo_half, vals[2 * hp], vals[2 * hp + 1]))
        return jnp.concatenate(cols, axis=1)

    def merge_groups(k):
        r0 = k * sub
        sts = [token_order(st, st_stage.at[k, i], r0) for i, st in enumerate((st0, st1, st2))]
        outs = [token_order(o, o_stage.at[k, i], r0) for i, o in enumerate((o0, o1, o2))]
        dens = [pltpu.roll(st, LANES - HEADS_PER_GROUP, axis=1) for st in sts]
        mx = jnp.maximum(jnp.maximum(sts[0], sts[1]), sts[2])
        w = [den * jnp.exp2(st - mx) for den, st in zip(dens, sts)]
        tot = jnp.where(lane < HEADS_PER_GROUP, w[0] + w[1] + w[2], 1.0)
        att = None
        for g in range(N_ATTN_GROUPS):
            wt = w[g] / tot
            term = head_cols([wt[:, h:h + 1] for h in range(HEADS_PER_GROUP)]) * outs[g]
            att = term if att is None else att + term
        return att.astype(BF16)

    def gate(k, att):
        rows = slice(k * sub, (k + 1) * sub)
        y_attn = jnp.dot(att, wao_ref[...], preferred_element_type=F32)
        y_ssm = token_order(ys_ref, ys_stage.at[k], k * sub)
        merged = ga_ref[rows, :].astype(F32) * y_attn + gs_ref[rows, :].astype(F32) * y_ssm
        return merged.astype(BF16)

    def project(k, merged):
        rows = slice(k * sub, (k + 1) * sub)
        mix = jnp.dot(merged, wo_ref[...], preferred_element_type=F32)
        x2_ref[slot_w, rows, :] = _layer_norm(alpha * x_ref[rows, :] + mix, g_ref[...], b_ref[...])

    x2_prev = x2_ref[slot_r]
    ffn_state = {}

    def ffn_gate_up(k, c):
        rows = slice(k * sub, (k + 1) * sub)
        if c == chunks[0]:
            ffn_state[k] = {"xb": x2_prev[rows, :].astype(BF16), "acc": None}
        xb = ffn_state[k]["xb"]
        ffn_state[k]["gu"] = (jnp.dot(xb, wg_ref[:, c[0]:c[1]], preferred_element_type=F32),
                              jnp.dot(xb, wu_ref[:, c[0]:c[1]], preferred_element_type=F32))

    def ffn_down(k, c):
        rows = slice(k * sub, (k + 1) * sub)
        gate_v, up = ffn_state[k].pop("gu")
        h = (gate_v * _sigmoid(gate_v) * up).astype(BF16)
        part = jnp.dot(h, wd_ref[c[0]:c[1], :], preferred_element_type=F32)
        acc = ffn_state[k]["acc"]
        ffn_state[k]["acc"] = part if acc is None else acc + part
        if c == chunks[-1]:
            y = alpha * x2_prev[rows, :] + 0.5 * ffn_state.pop(k)["acc"]
            out_ref[rows, :] = _layer_norm(y, g3_ref[...], b3_ref[...])

    n_sub = x_ref.shape[0] // sub
    ffn_stages = [f for k in range(n_sub) for c in chunks
                  for f in (functools.partial(ffn_gate_up, k, c), functools.partial(ffn_down, k, c))]
    mix_stages = [f for k in range(n_sub) for f in (functools.partial(merge_groups, k),
                                                    functools.partial(gate, k),
                                                    functools.partial(project, k))]
    carry = ()
    for f in ffn_stages:
        f()
        if mix_stages:
            out = mix_stages.pop(0)(*carry)
            carry = () if out is None else (out,)
    assert not mix_stages


def _mix_ffn(x1, os_, sts, y_ssm, ga, gs, w_attn_out, w_o, g2, b2, wg, wu, wd, g3, b3, *,
             seq, alpha, tm=512, sub=256, chunk=1024):
    t, d = x1.shape
    d_ff = wg.shape[1]
    tiles = seq // tm
    n_tiles = t // tm
    cur = lambda i: jnp.minimum(i, n_tiles - 1)
    tile = lambda w: pl.BlockSpec((tm, w), lambda i: (cur(i), 0))

    def by_class(a):
        _, dil, n, w = a.shape
        return pl.BlockSpec((None, dil, tm // dil, w), lambda i: (cur(i) // tiles, 0, cur(i) % tiles, 0))

    n_grp = len(os_)
    stage_rows = max(sub // a.shape[1] * _scatter_pitch(a.shape[1]) for a in os_)
    per_big = y_ssm.shape[2] * SSM_CHUNK // tm
    ys_spec = pl.BlockSpec((None, SSM_CHUNK, tm // SSM_CHUNK, d),
                           lambda i: (cur(i) // per_big, 0, cur(i) % per_big, 0))
    ys_rows = sub // SSM_CHUNK * _scatter_pitch(SSM_CHUNK)
    vec = lambda v: v.reshape(1, d)
    return pl.pallas_call(
        functools.partial(_mix_ffn_kernel, sub=sub, chunks=_ffn_chunks(d_ff, chunk), alpha=alpha),
        grid=(n_tiles + 1,),
        in_specs=[tile(d)] + [by_class(a) for a in os_] + [by_class(a) for a in sts] + [ys_spec, tile(d), tile(d),
                  _resident(w_attn_out.shape), _resident(w_o.shape), _resident((1, d)), _resident((1, d)),
                  _resident((d, d_ff)), _resident((d, d_ff)), _resident((d_ff, d)),
                  _resident((1, d)), _resident((1, d))],
        out_specs=pl.BlockSpec((tm, d), lambda i: (jnp.maximum(i - 1, 0), 0)),
        out_shape=jax.ShapeDtypeStruct((t, d), F32),
        scratch_shapes=[pltpu.VMEM((tm // sub, n_grp, GROUP_WIDTH // LANES, stage_rows, LANES), F32),
                        pltpu.VMEM((tm // sub, n_grp, 1, stage_rows, LANES), F32),
                        pltpu.VMEM((tm // sub, d // LANES, ys_rows, LANES), F32),
                        pltpu.VMEM((2, tm, d), F32)],
        compiler_params=_params("arbitrary"),
        name="mix_ffn",
    )(x1, *os_, *sts, y_ssm, ga, gs, w_attn_out.astype(BF16), w_o.astype(BF16), vec(g2), vec(b2),
      wg.astype(BF16), wu.astype(BF16), wd.astype(BF16), vec(g3), vec(b3))


def _layer(x, positions, w_in, w_attn_out, a_re, a_im, log_dt, b_re, b_im, c_re, c_im, d_skip,
           w_glu, b_glu, w_ssm_out, w_o, ffn1, ffn2, ln1, ln2, ln3, *, alpha):
    batch, seq, d = x.shape
    t = batch * seq
    n_grp = a_re.shape[0]
    ssm_width = n_grp * SSM_GROUP
    u_lo = 3 * ATTN_WIDTH

    x1 = _ffn_ln(x.reshape(t, d), *ffn1, *ln1, alpha=alpha)
    w_in_b = w_in.astype(BF16)
    *qkvs, ga, gs = _in_proj(x1, positions, w_in_b, batch=batch, seq=seq, ssm_width=ssm_width)

    os_, sts = [], []
    for qkv, (window, dilation) in zip(qkvs, ATTN_PATTERNS):
        assert window == ATTN_BLK * dilation
        o, st = _dilated_attention(qkv)
        os_.append(o)
        sts.append(st)

    ut = _ssm_in(x1, w_in_b, u_lo, ssm_width)
    prep = _ssm_prep(a_re, a_im, log_dt, b_re, b_im, c_re, c_im)
    dsk = jnp.tile(d_skip.reshape(n_grp, 1, SSM_GROUP), (1, SSM_CHUNK, 1)).reshape(n_grp, SSM_ROW, 1)
    yt = _ssm(ut, prep, dsk, n_chunk=seq // SSM_CHUNK)
    y_ssm = _ssm_out(yt, w_glu, b_glu, w_ssm_out)

    x3 = _mix_ffn(x1, os_, sts, y_ssm, ga, gs, w_attn_out, w_o, *ln2, *ffn2, *ln3, seq=seq, alpha=alpha)
    return x3.reshape(batch, seq, d)


def kernel(x, positions, w_in, w_attn_out, a_re, a_im, log_dt, b_re, b_im, c_re, c_im, d_skip, w_glu, b_glu, w_ssm_out, w_o, ffn1_wg, ffn1_wu, ffn1_wd, ffn2_wg, ffn2_wu, ffn2_wd, ln1_g, ln1_b, ln2_g, ln2_b, ln3_g, ln3_b):
    depth = w_in.shape[0]
    alpha = (2.0 * depth) ** 0.25
    for i in range(depth):
        x = _layer(x, positions, w_in[i], w_attn_out[i], a_re[i], a_im[i], log_dt[i], b_re[i], b_im[i],
                   c_re[i], c_im[i], d_skip[i], w_glu[i], b_glu[i], w_ssm_out[i], w_o[i],
                   (ffn1_wg[i], ffn1_wu[i], ffn1_wd[i]), (ffn2_wg[i], ffn2_wu[i], ffn2_wd[i]),
                   (ln1_g[i], ln1_b[i]), (ln2_g[i], ln2_b[i]), (ln3_g[i], ln3_b[i]), alpha=alpha)
    return x
```

```python
import functools

import jax
import jax.numpy as jnp
from jax import lax
from jax.experimental import pallas as pl
from jax.experimental.pallas import tpu as pltpu

F32 = jnp.float32
BF16 = jnp.bfloat16

HEAD_DIM = 64
HEADS_PER_GROUP = 4
GROUP_WIDTH = HEADS_PER_GROUP * HEAD_DIM
ATTN_PATTERNS = ((128, 1), (512, 4), (2048, 16))
N_ATTN_GROUPS = len(ATTN_PATTERNS)
ATTN_WIDTH = N_ATTN_GROUPS * GROUP_WIDTH
ROT_DIM = HEAD_DIM // 4
ROPE_THETA = 500000.0
ROPE_ROWS = 32
ATTN_BLK = 128
NEG_INF = -1e30
LOG2_E = 1.4426950408889634
SSM_GROUP = 16
SSM_CHUNK = 16
SSM_ROW = SSM_CHUNK * SSM_GROUP
SSM_POWERS = 16
SCAN_INTERLEAVE = 8
SSM_GATHER_HOP = 4
CHUNK_PITCH = 24
LN_EPS = 1e-5

LANES = 128
SUBLANES = 8
VMEM_LIMIT_BYTES = 56 * 1024 * 1024


def _scatter_pitch(stride):
    return CHUNK_PITCH if stride == SSM_CHUNK else stride


def _rows_by_class(val, stage, dil):
    rows, w = val.shape
    n = rows // dil
    pitch = _scatter_pitch(dil)
    n_col = w // LANES
    for c in range(n_col):
        col = val[:, c * LANES:(c + 1) * LANES]
        if pitch != dil:
            col = jnp.concatenate([col.reshape(n, dil, LANES), jnp.zeros((n, pitch - dil, LANES), F32)], axis=1)
            col = col.reshape(n * pitch, LANES)
        stage[c, 0:n * pitch, :] = col

    def get(r):
        return jnp.concatenate([stage[c, pl.ds(r, n, stride=pitch), :] for c in range(n_col)], axis=1)

    return get


def _params(*semantics):
    return pltpu.CompilerParams(dimension_semantics=semantics, vmem_limit_bytes=VMEM_LIMIT_BYTES)


def _resident(shape):
    zeros = (0,) * len(shape)
    return pl.BlockSpec(shape, lambda *_: zeros, pipeline_mode=pl.Buffered(1))


def _layer_norm(y, g, b):
    mu = jnp.mean(y, axis=-1, keepdims=True)
    yc = y - mu
    var = jnp.mean(yc * yc, axis=-1, keepdims=True)
    return yc * lax.rsqrt(var + LN_EPS) * g + b


def _sigmoid(x):
    return 1.0 / (1.0 + jnp.exp(-x))


def _gelu_tanh(x):
    c = 0.7978845608028654
    return 0.5 * x * (1.0 + jnp.tanh(c * (x + 0.044715 * (x * x * x))))


def _ffn_ln_kernel(x_ref, wg_ref, wu_ref, wd_ref, g_ref, b_ref, o_ref, *, chunks, sub, alpha):
    for r0 in range(0, x_ref.shape[0], sub):
        x = x_ref[r0:r0 + sub, :]
        xb = x.astype(BF16)
        acc = None
        for c0, c1 in chunks:
            gate = jnp.dot(xb, wg_ref[:, c0:c1], preferred_element_type=F32)
            up = jnp.dot(xb, wu_ref[:, c0:c1], preferred_element_type=F32)
            h = (gate * _sigmoid(gate) * up).astype(BF16)
            part = jnp.dot(h, wd_ref[c0:c1, :], preferred_element_type=F32)
            acc = part if acc is None else acc + part
        y = alpha * x + 0.5 * acc
        o_ref[r0:r0 + sub, :] = _layer_norm(y, g_ref[...], b_ref[...])


def _ffn_chunks(d_ff, width):
    edges = list(range(0, d_ff, width)) + [d_ff]
    return tuple(zip(edges[:-1], edges[1:]))


def _ffn_ln(x, wg, wu, wd, g, b, *, alpha, tm=1024, sub=256, chunk=1024):
    t, d = x.shape
    d_ff = wg.shape[1]
    kern = functools.partial(_ffn_ln_kernel, chunks=_ffn_chunks(d_ff, chunk), sub=sub, alpha=alpha)
    return pl.pallas_call(
        kern,
        grid=(t // tm,),
        in_specs=[
            pl.BlockSpec((tm, d), lambda i: (i, 0)),
            _resident((d, d_ff)), _resident((d, d_ff)), _resident((d_ff, d)),
            _resident((1, d)), _resident((1, d)),
        ],
        out_specs=pl.BlockSpec((tm, d), lambda i: (i, 0)),
        out_shape=jax.ShapeDtypeStruct((t, d), F32),
        compiler_params=_params("parallel"),
        name="ffn_ln",
    )(x, wg.astype(BF16), wu.astype(BF16), wd.astype(BF16), g.reshape(1, d), b.reshape(1, d))


def _rope_spread():
    half = ROT_DIM // 2
    rows = lax.broadcasted_iota(jnp.int32, (ROPE_ROWS, 2 * LANES), 0)
    cols = lax.broadcasted_iota(jnp.int32, (ROPE_ROWS, 2 * LANES), 1)
    in_head = cols % HEAD_DIM
    freq = in_head % half
    is_cos = cols < LANES
    rot = in_head < ROT_DIM
    cos_part = is_cos & rot & (rows == freq)
    one_part = is_cos & jnp.logical_not(rot) & (rows == 2 * half)
    sin_part = jnp.logical_not(is_cos) & rot & (rows == half + freq)
    sign = jnp.where(in_head < half, 1.0, -1.0)
    return jnp.where(cos_part | one_part, 1.0, jnp.where(sin_part, sign, 0.0)).astype(BF16)


def _in_proj_kernel(x_ref, pos_ref, invf_ref, w_ref,
                    qkv0_ref, qkv1_ref, qkv2_ref, ga_ref, gs_ref, stage_ref, *, splits, sub):
    half = ROT_DIM // 2
    spread = _rope_spread()
    first = lax.broadcasted_iota(jnp.int32, (sub, LANES), 1) % HEAD_DIM < half
    s0, s1, s2, s3, s4, s5 = splits
    tn = (((0,), (0,)), ((), ()))

    for r0 in range(0, x_ref.shape[0], sub):
        xb = x_ref[r0:r0 + sub, :].astype(BF16)

        ang = invf_ref[...] * pos_ref[:, r0:r0 + sub].astype(F32)
        tab = jnp.concatenate([jnp.cos(ang), jnp.sin(ang), jnp.ones((half, sub), F32),
                               jnp.zeros((ROPE_ROWS - 3 * half, sub), F32)], axis=0)
        tab_hi = tab.astype(BF16)
        tab_lo = (tab - tab_hi.astype(F32)).astype(BF16)
        cs = (lax.dot_general(tab_hi, spread, tn, preferred_element_type=F32)
              + lax.dot_general(tab_lo, spread, tn, preferred_element_type=F32))
        cos = cs[:, :LANES]
        sin = cs[:, LANES:]

        def rotate(z):
            cols = []
            for c in range(z.shape[1] // LANES):
                zc = z[:, c * LANES:(c + 1) * LANES]
                zs = zc * sin
                up = pltpu.roll(jnp.where(first, zs, 0.0), half, axis=1)
                dn = pltpu.roll(jnp.where(first, 0.0, zs), LANES - half, axis=1)
                cols.append(zc * cos + up + dn)
            return jnp.concatenate(cols, axis=1)

        def proj(lo, hi):
            return jnp.dot(xb, w_ref[:, lo:hi], preferred_element_type=F32)

        q = rotate(proj(0, s0)) * (HEAD_DIM ** -0.5 * LOG2_E)
        k = rotate(proj(s0, s1))
        v = proj(s1, s2)
        for gi, out_ref in enumerate((qkv0_ref, qkv1_ref, qkv2_ref)):
            dil = ATTN_PATTERNS[gi][1]
            c0 = gi * GROUP_WIDTH
            qkv = jnp.concatenate([a[:, c0:c0 + GROUP_WIDTH] for a in (q, k, v)], axis=1)
            n = sub // dil
            i0 = r0 // dil
            if dil == 1:
                out_ref[0, i0:i0 + n, :] = qkv.astype(BF16)
            else:
                get = _rows_by_class(qkv, stage_ref.at[r0 // sub, gi - 1], dil)
                for r in range(dil):
                    out_ref[r, i0:i0 + n, :] = get(r).astype(BF16)
        ga_ref[r0:r0 + sub, :] = _sigmoid(proj(s3, s4)).astype(BF16)
        gs_ref[r0:r0 + sub, :] = _sigmoid(proj(s4, s5)).astype(BF16)


def _in_proj(x1, positions, w_in, *, batch, seq, ssm_width, tm=1024, sub=256):
    t, d = x1.shape
    splits = (ATTN_WIDTH, 2 * ATTN_WIDTH, 3 * ATTN_WIDTH, 3 * ATTN_WIDTH + ssm_width,
              3 * ATTN_WIDTH + ssm_width + d, 3 * ATTN_WIDTH + ssm_width + 2 * d)
    assert splits[-1] == w_in.shape[1] and seq % tm == 0
    half = ROT_DIM // 2
    invf = (ROPE_THETA ** (-jnp.arange(half, dtype=F32) * 2.0 / ROT_DIM)).reshape(half, 1)
    tiles = seq // tm
    qkv_w = 3 * GROUP_WIDTH
    qkv_specs = [pl.BlockSpec((None, dil, tm // dil, qkv_w), lambda i: (i // tiles, 0, i % tiles, 0))
                 for _, dil in ATTN_PATTERNS]
    qkv_shapes = [jax.ShapeDtypeStruct((batch, dil, seq // dil, qkv_w), BF16) for _, dil in ATTN_PATTERNS]
    widths = (d, d)
    kern = functools.partial(_in_proj_kernel, splits=splits, sub=sub)
    n_strided = sum(dil > 1 for _, dil in ATTN_PATTERNS)
    stage_rows = max(sub // dil * _scatter_pitch(dil) for _, dil in ATTN_PATTERNS)
    return pl.pallas_call(
        kern,
        grid=(t // tm,),
        in_specs=[
            pl.BlockSpec((tm, d), lambda i: (i, 0)),
            pl.BlockSpec((1, tm), lambda i: (0, i)),
            _resident((half, 1)),
            _resident(w_in.shape),
        ],
        out_specs=qkv_specs + [pl.BlockSpec((tm, w), lambda i: (i, 0)) for w in widths],
        out_shape=qkv_shapes + [jax.ShapeDtypeStruct((t, w), BF16) for w in widths],
        scratch_shapes=[pltpu.VMEM((tm // sub, n_strided, qkv_w // LANES, stage_rows, LANES), F32)],
        compiler_params=_params("parallel"),
        name="in_proj",
    )(x1, positions.reshape(1, t), invf, w_in)


def _attn_kernel(q_ref, kc_ref, kp_ref, vc_ref, vp_ref, o_ref, st_ref, *, n_sub):
    first = pl.program_id(2) == 0
    row = lax.broadcasted_iota(jnp.int32, (ATTN_BLK, 2 * ATTN_BLK), 0)
    col = lax.broadcasted_iota(jnp.int32, (ATTN_BLK, 2 * ATTN_BLK), 1)
    band = (col >= row) & (col <= row + ATTN_BLK)
    lane = lax.broadcasted_iota(jnp.int32, (ATTN_BLK, LANES), 1)
    lo_half = lane < HEAD_DIM

    n_pair = GROUP_WIDTH // LANES
    tiles = [(rc, j, hp, hh) for rc in range(q_ref.shape[0]) for j in range(n_sub)
             for hp in range(n_pair) for hh in range(2)]

    def window(ref_prev, ref_cur, j, c0):
        r0 = j * ATTN_BLK
        if j == 0:
            return jnp.concatenate([ref_prev[:, c0:c0 + LANES], ref_cur[0:ATTN_BLK, c0:c0 + LANES]], axis=0)
        return ref_cur[r0 - ATTN_BLK:r0 + ATTN_BLK, c0:c0 + LANES]

    def scores(tile):
        rc, j, hp, hh = tile
        q2 = q_ref[rc, j * ATTN_BLK:(j + 1) * ATTN_BLK, hp * LANES:(hp + 1) * LANES]
        qm = jnp.where(lo_half if hh == 0 else jnp.logical_not(lo_half), q2, jnp.zeros_like(q2))
        return lax.dot_general(qm, window(kp_ref.at[rc], kc_ref.at[rc], j, hp * LANES), (((1,), (1,)), ((), ())),
                               preferred_element_type=F32)

    def softmax(tile, s):
        valid = band & ((col >= ATTN_BLK) | jnp.logical_not(first)) if tile[1] == 0 else band
        s = jnp.where(valid, s, NEG_INF)
        m = jnp.max(s, axis=1, keepdims=True)
        p = jnp.exp2(s - m)
        return p.astype(BF16), m, jnp.sum(p, axis=1, keepdims=True)

    def weighted_values(tile, p, den):
        rc, j, hp, _ = tile
        return jnp.dot(p, window(vp_ref.at[rc], vc_ref.at[rc], j, hp * LANES), preferred_element_type=F32) / den

    s_q, p_q, outs = {}, {}, {}
    for step in range(len(tiles) + 2):
        if step < len(tiles):
            s_q[step] = scores(tiles[step])
        if 0 <= step - 1 < len(tiles):
            p_q[step - 1] = softmax(tiles[step - 1], s_q.pop(step - 1))
        if 0 <= step - 2 < len(tiles):
            t = step - 2
            rc, j, hp, hh = tiles[t]
            p, m, den = p_q.pop(t)
            outs[hh] = weighted_values(tiles[t], p, den)
            h = 2 * hp + hh
            rows = slice(j * ATTN_BLK, (j + 1) * ATTN_BLK)
            if h == 0:
                st_ref[rc, rows, :] = jnp.zeros((ATTN_BLK, LANES), F32)
            st_ref[rc, rows, h:h + 1] = m
            st_ref[rc, rows, HEADS_PER_GROUP + h:HEADS_PER_GROUP + h + 1] = den
            if hh == 1:
                o_ref[rc, rows, hp * LANES:(hp + 1) * LANES] = (
                    jnp.where(lo_half, outs[0], outs[1]).astype(BF16))


def _dilated_attention(qkv):
    batch, dilation, n, _ = qkv.shape
    rows_per_step = 2048
    qb = min(rows_per_step, n)
    n_sub = qb // ATTN_BLK
    rb = min(max(rows_per_step // qb, 1), dilation)

    def cur(part):
        return pl.BlockSpec((None, rb, qb, GROUP_WIDTH), lambda b, r, i: (b, r, i, part))

    def prev(part):
        return pl.BlockSpec((None, rb, ATTN_BLK, GROUP_WIDTH),
                            lambda b, r, i: (b, r, jnp.maximum(i * n_sub - 1, 0), part))

    return pl.pallas_call(
        functools.partial(_attn_kernel, n_sub=n_sub),
        grid=(batch, dilation // rb, n // qb),
        in_specs=[cur(0), cur(1), prev(1), cur(2), prev(2)],
        out_specs=[pl.BlockSpec((None, rb, qb, GROUP_WIDTH), lambda b, r, i: (b, r, i, 0)),
                   pl.BlockSpec((None, rb, qb, LANES), lambda b, r, i: (b, r, i, 0))],
        out_shape=[jax.ShapeDtypeStruct((batch, dilation, n, GROUP_WIDTH), BF16),
                   jax.ShapeDtypeStruct((batch, dilation, n, LANES), F32)],
        compiler_params=_params("parallel", "parallel", "parallel"),
        name=f"attn_d{dilation}",
    )(qkv, qkv, qkv, qkv, qkv)


def _ssm_in_kernel(*refs, n_grp, ch, n_w):
    x_refs, w_refs, ut_ref, stage = refs[:-n_w - 2], refs[-n_w - 2:-2], refs[-2], refs[-1]
    w = jnp.concatenate([w_ref[...] for w_ref in w_refs], axis=1)
    hop = stage.shape[1]
    n_hop = ch * SSM_CHUNK // hop
    for c, x_ref in enumerate(x_refs):
        for a in range(hop):
            stage[c, a] = x_ref[pl.ds(a, n_hop, stride=hop), :]

    def step_rows(s):
        return jnp.concatenate([stage[c, s % hop, pl.ds(s // hop, ch, stride=SSM_CHUNK // hop), :]
                                for c in range(len(x_refs))], axis=1)

    for s in range(0, SSM_CHUNK, 2):
        xs = jnp.concatenate([step_rows(s + i) for i in range(2)], axis=0).astype(BF16)
        ut = lax.dot_general(w, xs, (((0,), (1,)), ((), ())), preferred_element_type=F32)
        for i in range(2):
            ut_ref[:, (s + i) * SSM_GROUP:(s + i + 1) * SSM_GROUP, :] = (
                ut[:, i * ch:(i + 1) * ch].reshape(n_grp, SSM_GROUP, ch).astype(BF16))


def _ssm_in(x1, w_in_b, u_lo, width, *, ch=LANES, wblk=256):
    t, d = x1.shape
    assert u_lo % wblk == 0 and width % wblk == 0
    n_grp = width // SSM_GROUP
    n_chunk_all = t // SSM_CHUNK
    tm = ch * SSM_CHUNK
    n_w = width // wblk
    slabs = [pl.BlockSpec((tm, LANES), lambda i, c=c: (i, c)) for c in range(d // LANES)]
    w_cols = [pl.BlockSpec((d, wblk), lambda i, c=c: (0, u_lo // wblk + c), pipeline_mode=pl.Buffered(1))
              for c in range(n_w)]
    return pl.pallas_call(
        functools.partial(_ssm_in_kernel, n_grp=n_grp, ch=ch, n_w=n_w),
        grid=(n_chunk_all // ch,),
        in_specs=slabs + w_cols,
        out_specs=pl.BlockSpec((n_grp, SSM_ROW, ch), lambda i: (0, 0, i)),
        out_shape=jax.ShapeDtypeStruct((n_grp, SSM_ROW, n_chunk_all), BF16),
        scratch_shapes=[pltpu.VMEM((d // LANES, SSM_GATHER_HOP, tm // SSM_GATHER_HOP, LANES), F32)],
        compiler_params=_params("parallel"),
        name="ssm_in",
    )(*([x1] * (d // LANES)), *([w_in_b] * n_w))


def _hdot(a, b):
    return jnp.dot(a, b, preferred_element_type=F32, precision=lax.Precision.HIGHEST)


def _cmul(ar, ai, br, bi):
    return ar * br - ai * bi, ar * bi + ai * br


def _cpowers(lbr, lbi, n, bits):
    out_r = jnp.ones(n.shape, F32)
    out_i = jnp.zeros(n.shape, F32)
    for b in range(bits):
        sel = (n & (1 << b)) != 0
        out_r, out_i = _cmul(out_r, out_i, jnp.where(sel, lbr, 1.0), jnp.where(sel, lbi, 0.0))
        lbr, lbi = _cmul(lbr, lbi, lbr, lbi)
    return out_r, out_i


def _ssm_prep_kernel(are_l, aim_l, ldt_ref, bre_ref, bim_ref, cre_ref, cim_ref,
                     tt_ref, win_ref, wout_ref, apr_ref, api_ref, *, gb):
    p = are_l.shape[2]
    eye = (lax.broadcasted_iota(jnp.int32, (p, p), 0) == lax.broadcasted_iota(jnp.int32, (p, p), 1)).astype(F32)
    expand = (lax.broadcasted_iota(jnp.int32, (SSM_CHUNK, SSM_ROW), 1) // SSM_GROUP
              == lax.broadcasted_iota(jnp.int32, (SSM_CHUNK, SSM_ROW), 0)).astype(F32)
    expand_t = (lax.broadcasted_iota(jnp.int32, (SSM_ROW, SSM_CHUNK), 0) // SSM_GROUP
                == lax.broadcasted_iota(jnp.int32, (SSM_ROW, SSM_CHUNK), 1)).astype(F32)
    rem = SSM_CHUNK - 1 - lax.broadcasted_iota(jnp.int32, (p, SSM_CHUNK), 1)
    tp1 = lax.broadcasted_iota(jnp.int32, (SSM_CHUNK, p), 0) + 1
    col = lax.broadcasted_iota(jnp.int32, (p, SSM_POWERS), 1)
    lane = lax.broadcasted_iota(jnp.int32, (SSM_GROUP, SSM_ROW), 1)
    chunk_bits = SSM_CHUNK.bit_length()

    def group(g):
        dt = jnp.exp(ldt_ref[g])
        lam_r = jnp.minimum(are_l[g], -1e-4)
        lam_i = aim_l[g]
        mag_l = jnp.exp(lam_r * dt)
        lbr_l = mag_l * jnp.cos(lam_i * dt)
        lbi_l = mag_l * jnp.sin(lam_i * dt)
        inv = 1.0 / (lam_r * lam_r + lam_i * lam_i)
        cfr_l = ((lbr_l - 1.0) * lam_r + lbi_l * lam_i) * inv
        cfi_l = (lbi_l * lam_r - (lbr_l - 1.0) * lam_i) * inv
        stacked = jnp.concatenate([lbr_l, lbi_l, cfr_l, cfi_l, jnp.zeros((SUBLANES - 4, p), F32)], axis=0)
        cols = lax.dot_general(eye, stacked, (((1,), (1,)), ((), ())), preferred_element_type=F32,
                               precision=lax.Precision.HIGHEST)
        yield
        lbr, lbi = cols[:, 0:1], cols[:, 1:2]

        bre = bre_ref[g]
        bim = bim_ref[g]
        bbar_r, bbar_i = _cmul(cols[:, 2:3], cols[:, 3:4], bre, bim)
        pw_r, pw_i = _cpowers(lbr, lbi, rem, chunk_bits - 1)
        win_r, win_i = _cmul(_hdot(pw_r, expand), _hdot(pw_i, expand), bbar_r, bbar_i)
        win_ref[g] = jnp.concatenate([win_r, win_i], axis=0).astype(BF16)
        yield

        a_r, a_i = _cmul(pw_r[:, 0:1], pw_i[:, 0:1], lbr, lbi)
        apr = jnp.zeros((p, SSM_POWERS), F32)
        api = jnp.zeros((p, SSM_POWERS), F32)
        for j in range(SSM_POWERS):
            apr = jnp.where(col == j, a_r, apr)
            api = jnp.where(col == j, a_i, api)
            a_r, a_i = _cmul(a_r, a_i, a_r, a_i)
        apr_ref[g] = apr
        api_ref[g] = api
        yield

        e_r, e_i = _cpowers(lbr_l, lbi_l, tp1, chunk_bits)
        cre = cre_ref[g]
        cim = cim_ref[g]
        wo_r, wo_i = _cmul(_hdot(expand_t, e_r), _hdot(expand_t, e_i), cre, cim)
        wout_ref[g] = jnp.concatenate([wo_r, -wo_i], axis=1).astype(BF16)
        yield

        hrev = _hdot(cre[0:SSM_GROUP, :], win_r) - _hdot(cim[0:SSM_GROUP, :], win_i)
        yield
        blocks = []
        for t in range(SSM_CHUNK):
            hi = (t + 1) * SSM_GROUP
            rolled = hrev if hi == SSM_ROW else pltpu.roll(hrev, hi, axis=1)
            blocks.append(jnp.where(lane < hi, rolled, 0.0))
        tt_ref[g] = jnp.concatenate(blocks, axis=0).astype(BF16)

    active = [group(g) for g in range(gb)]
    while active:
        for gen in list(active):
            try:
                next(gen)
            except StopIteration:
                active.remove(gen)


def _ssm_prep(a_re, a_im, log_dt, b_re, b_im, c_re, c_im, *, gb=8):
    g, p = a_re.shape
    row = SSM_ROW
    blk = lambda shape: pl.BlockSpec((gb,) + shape, lambda i: (i, 0, 0))
    b_t = lambda b: jnp.tile(b, (1, 1, SSM_CHUNK))
    c_t = lambda c: jnp.tile(c, (1, SSM_CHUNK, 1))
    out_shapes = [
        jax.ShapeDtypeStruct((g, row, row), BF16),
        jax.ShapeDtypeStruct((g, 2 * p, row), BF16),
        jax.ShapeDtypeStruct((g, row, 2 * p), BF16),
        jax.ShapeDtypeStruct((g, p, SSM_POWERS), F32), jax.ShapeDtypeStruct((g, p, SSM_POWERS), F32),
    ]
    return pl.pallas_call(
        functools.partial(_ssm_prep_kernel, gb=gb),
        grid=(g // gb,),
        in_specs=[blk((1, p)), blk((1, p)), blk((1, 1)),
                  blk((p, row)), blk((p, row)), blk((row, p)), blk((row, p))],
        out_specs=[blk(s.shape[1:]) for s in out_shapes],
        out_shape=out_shapes,
        compiler_params=_params("parallel"),
        name="ssm_prep",
    )(a_re.reshape(g, 1, p), a_im.reshape(g, 1, p),
      log_dt.reshape(g, 1, 1), b_t(b_re), b_t(b_im), c_t(c_re), c_t(c_im))


def _ssm_kernel(ut_ref, tt_ref, win_ref, wout_ref, apr_ref, api_ref, dsk_ref, y_ref, *, gb, n_chunk):
    nk = ut_ref.shape[-1]
    p = apr_ref.shape[1]
    pos = lax.broadcasted_iota(jnp.int32, (SUBLANES, nk), 1) % n_chunk
    n_steps = (n_chunk - 1).bit_length()
    assert n_steps <= SSM_POWERS

    def shift_whole_tiles(v, sh):
        segs = []
        for b0 in range(0, nk, n_chunk):
            segs += [jnp.zeros((v.shape[0], sh), F32), v[:, b0:b0 + n_chunk - sh]]
        return jnp.concatenate(segs, axis=1)

    def group(g):
        ut = ut_ref[g]
        sl = jnp.dot(win_ref[g], ut, preferred_element_type=F32)
        yield
        apr = apr_ref[g]
        api = api_ref[g]
        prev_r, prev_i = [], []
        for b0 in range(0, p, SUBLANES * SCAN_INTERLEAVE):
            tiles = range(b0, b0 + SUBLANES * SCAN_INTERLEAVE, SUBLANES)
            s = {r0: (sl[r0:r0 + SUBLANES], sl[p + r0:p + r0 + SUBLANES]) for r0 in tiles}
            for j in range(n_steps):
                sh = 1 << j
                keep = pos >= sh
                for r0 in tiles:
                    s_r, s_i = s[r0]
                    a_r = apr[r0:r0 + SUBLANES, j:j + 1]
                    a_i = api[r0:r0 + SUBLANES, j:j + 1]
                    if sh % LANES == 0:
                        p_r, p_i = shift_whole_tiles(s_r, sh), shift_whole_tiles(s_i, sh)
                    else:
                        p_r = jnp.where(keep, pltpu.roll(s_r, sh, axis=1), 0.0)
                        p_i = jnp.where(keep, pltpu.roll(s_i, sh, axis=1), 0.0)
                    s[r0] = (s_r + a_r * p_r - a_i * p_i, s_i + a_r * p_i + a_i * p_r)
                yield
            keep = pos >= 1
            for r0 in tiles:
                prev_r.append(jnp.where(keep, pltpu.roll(s[r0][0], 1, axis=1), 0.0))
                prev_i.append(jnp.where(keep, pltpu.roll(s[r0][1], 1, axis=1), 0.0))
        prev = jnp.concatenate(prev_r + prev_i, axis=0).astype(BF16)
        yield
        y = jnp.dot(tt_ref[g], ut, preferred_element_type=F32)
        y += jnp.dot(wout_ref[g], prev, preferred_element_type=F32)
        yield
        y += dsk_ref[g] * ut.astype(F32)
        y_ref[g] = _gelu_tanh(y).astype(BF16)

    active = [group(g) for g in range(gb)]
    while active:
        for gen in list(active):
            try:
                next(gen)
            except StopIteration:
                active.remove(gen)


def _ssm(ut, prep, dsk, *, n_chunk, gb=4):
    tt, win, wout, apr, api = prep
    g, row, nk = ut.shape
    p = apr.shape[1]
    blk = lambda shape: pl.BlockSpec((gb,) + shape, lambda i: (i, 0, 0))
    kern = functools.partial(_ssm_kernel, gb=gb, n_chunk=n_chunk)
    return pl.pallas_call(
        kern,
        grid=(g // gb,),
        in_specs=[blk((row, nk)), blk((row, row)), blk((2 * p, row)), blk((row, 2 * p)),
                  blk((p, SSM_POWERS)), blk((p, SSM_POWERS)), blk((row, 1))],
        out_specs=blk((row, nk)),
        out_shape=jax.ShapeDtypeStruct((g, row, nk), BF16),
        compiler_params=_params("parallel"),
        name="ssm",
    )(ut, tt, win, wout, apr, api, dsk)


def _ssm_out_kernel(yt_ref, wglu_ref, bglu_ref, wso_ref, o_ref, *, t_per_pass):
    n_grp, _, ch = yt_ref.shape
    width = n_grp * SSM_GROUP
    def gated(t0):
        y = jnp.concatenate(
            [yt_ref[:, t * SSM_GROUP:(t + 1) * SSM_GROUP, :].reshape(width, ch)
             for t in range(t0, t0 + t_per_pass)], axis=1)
        return y, jnp.dot(wglu_ref[...], y, preferred_element_type=F32) + bglu_ref[...]

    def project(t0, y, gate):
        glu = (y.astype(F32) * _sigmoid(gate)).astype(BF16)
        yo = lax.dot_general(glu, wso_ref[...], (((0,), (0,)), ((), ())), preferred_element_type=F32)
        for i in range(t_per_pass):
            o_ref[t0 + i] = yo[i * ch:(i + 1) * ch, :].astype(BF16)

    starts = list(range(0, SSM_CHUNK, t_per_pass))
    pending = gated(starts[0])
    for n, t0 in enumerate(starts):
        nxt = gated(starts[n + 1]) if n + 1 < len(starts) else None
        project(t0, *pending)
        pending = nxt


def _ssm_out(yt, w_glu, b_glu, w_ssm_out, *, ch=LANES, t_per_pass=4):
    n_grp, row, n_chunk_all = yt.shape
    width, d = w_ssm_out.shape
    return pl.pallas_call(
        functools.partial(_ssm_out_kernel, t_per_pass=t_per_pass),
        grid=(n_chunk_all // ch,),
        in_specs=[pl.BlockSpec((n_grp, row, ch), lambda i: (0, 0, i)),
                  _resident((width, width)), _resident((width, 1)), _resident((width, d))],
        out_specs=pl.BlockSpec((None, SSM_CHUNK, ch, d), lambda i: (i, 0, 0, 0)),
        out_shape=jax.ShapeDtypeStruct((n_chunk_all // ch, SSM_CHUNK, ch, d), BF16),
        compiler_params=_params("parallel"),
        name="ssm_out",
    )(yt, w_glu.T.astype(BF16), b_glu.reshape(width, 1), w_ssm_out.astype(BF16))


def _mix_ffn_kernel(x_ref, o0, o1, o2, st0, st1, st2, ys_ref, ga_ref, gs_ref,
                    wao_ref, wo_ref, g_ref, b_ref, wg_ref, wu_ref, wd_ref, g3_ref, b3_ref,
                    out_ref, o_stage, st_stage, ys_stage, x2_ref, *, sub, chunks, alpha):
    step = pl.program_id(0)
    slot_w = step % 2
    slot_r = 1 - slot_w

    @pl.when(step == 0)
    def _():
        x2_ref[...] = jnp.zeros(x2_ref.shape, F32)

    lane = lax.broadcasted_iota(jnp.int32, (sub, LANES), 1)
    lo_half = lane < HEAD_DIM

    def token_order(ref, stage, r0):
        dil = ref.shape[0]
        n = sub // dil
        i0 = r0 // dil
        if dil == 1:
            return ref[0, i0:i0 + n, :].astype(F32)
        n_col = ref.shape[2] // LANES
        pitch = _scatter_pitch(dil)
        for r in range(dil):
            blk = ref[r, i0:i0 + n, :].astype(F32)
            for c in range(n_col):
                stage[c, pl.ds(r, n, stride=pitch), :] = blk[:, c * LANES:(c + 1) * LANES]
        cols = [stage[c, 0:n * pitch, :] for c in range(n_col)]
        if pitch != dil:
            cols = [a.reshape(n, pitch, LANES)[:, :dil, :].reshape(sub, LANES) for a in cols]
        return jnp.concatenate(cols, axis=1)

    def head_cols(vals):
        cols = []
        for hp in range(GROUP_WIDTH // LANES):
            cols.append(jnp.where(lo_half, vals[2 * hp], vals[2 * hp + 1]))
        return jnp.concatenate(cols, axis=1)

    def merge_groups(k):
        r0 = k * sub
        sts = [token_order(st, st_stage.at[k, i], r0) for i, st in enumerate((st0, st1, st2))]
        outs = [token_order(o, o_stage.at[k, i], r0) for i, o in enumerate((o0, o1, o2))]
        dens = [pltpu.roll(st, LANES - HEADS_PER_GROUP, axis=1) for st in sts]
        mx = jnp.maximum(jnp.maximum(sts[0], sts[1]), sts[2])
        w = [den * jnp.exp2(st - mx) for den, st in zip(dens, sts)]
        tot = jnp.where(lane < HEADS_PER_GROUP, w[0] + w[1] + w[2], 1.0)
        att = None
        for g in range(N_ATTN_GROUPS):
            wt = w[g] / tot
            term = head_cols([wt[:, h:h + 1] for h in range(HEADS_PER_GROUP)]) * outs[g]
            att = term if att is None else att + term
        return att.astype(BF16)

    def gate(k, att):
        rows = slice(k * sub, (k + 1) * sub)
        y_attn = jnp.dot(att, wao_ref[...], preferred_element_type=F32)
        y_ssm = token_order(ys_ref, ys_stage.at[k], k * sub)
        merged = ga_ref[rows, :].astype(F32) * y_attn + gs_ref[rows, :].astype(F32) * y_ssm
        return merged.astype(BF16)

    def project(k, merged):
        rows = slice(k * sub, (k + 1) * sub)
        mix = jnp.dot(merged, wo_ref[...], preferred_element_type=F32)
        x2_ref[slot_w, rows, :] = _layer_norm(alpha * x_ref[rows, :] + mix, g_ref[...], b_ref[...])

    x2_prev = x2_ref[slot_r]
    ffn_state = {}

    def ffn_gate_up(k, c):
        rows = slice(k * sub, (k + 1) * sub)
        if c == chunks[0]:
            ffn_state[k] = {"xb": x2_prev[rows, :].astype(BF16), "acc": None}
        xb = ffn_state[k]["xb"]
        ffn_state[k]["gu"] = (jnp.dot(xb, wg_ref[:, c[0]:c[1]], preferred_element_type=F32),
                              jnp.dot(xb, wu_ref[:, c[0]:c[1]], preferred_element_type=F32))

    def ffn_down(k, c):
        rows = slice(k * sub, (k + 1) * sub)
        gate_v, up = ffn_state[k].pop("gu")
        h = (gate_v * _sigmoid(gate_v) * up).astype(BF16)
        part = jnp.dot(h, wd_ref[c[0]:c[1], :], preferred_element_type=F32)
        acc = ffn_state[k]["acc"]
        ffn_state[k]["acc"] = part if acc is None else acc + part
        if c == chunks[-1]:
            y = alpha * x2_prev[rows, :] + 0.5 * ffn_state.pop(k)["acc"]
            out_ref[rows, :] = _layer_norm(y, g3_ref[...], b3_ref[...])

    n_sub = x_ref.shape[0] // sub
    ffn_stages = [f for k in range(n_sub) for c in chunks
                  for f in (functools.partial(ffn_gate_up, k, c), functools.partial(ffn_down, k, c))]
    mix_stages = [f for k in range(n_sub) for f in (functools.partial(merge_groups, k),
                                                    functools.partial(gate, k),
                                                    functools.partial(project, k))]
    carry = ()
    for f in ffn_stages:
        f()
        if mix_stages:
            out = mix_stages.pop(0)(*carry)
            carry = () if out is None else (out,)
    assert not mix_stages


def _mix_ffn(x1, os_, sts, y_ssm, ga, gs, w_attn_out, w_o, g2, b2, wg, wu, wd, g3, b3, *,
             seq, alpha, tm=512, sub=256, chunk=1024):
    t, d = x1.shape
    d_ff = wg.shape[1]
    tiles = seq // tm
    n_tiles = t // tm
    cur = lambda i: jnp.minimum(i, n_tiles - 1)
    tile = lambda w: pl.BlockSpec((tm, w), lambda i: (cur(i), 0))

    def by_class(a):
        _, dil, n, w = a.shape
        return pl.BlockSpec((None, dil, tm // dil, w), lambda i: (cur(i) // tiles, 0, cur(i) % tiles, 0))

    n_grp = len(os_)
    stage_rows = max(sub // a.shape[1] * _scatter_pitch(a.shape[1]) for a in os_)
    per_big = y_ssm.shape[2] * SSM_CHUNK // tm
    ys_spec = pl.BlockSpec((None, SSM_CHUNK, tm // SSM_CHUNK, d),
                           lambda i: (cur(i) // per_big, 0, cur(i) % per_big, 0))
    ys_rows = sub // SSM_CHUNK * _scatter_pitch(SSM_CHUNK)
    vec = lambda v: v.reshape(1, d)
    return pl.pallas_call(
        functools.partial(_mix_ffn_kernel, sub=sub, chunks=_ffn_chunks(d_ff, chunk), alpha=alpha),
        grid=(n_tiles + 1,),
        in_specs=[tile(d)] + [by_class(a) for a in os_] + [by_class(a) for a in sts] + [ys_spec, tile(d), tile(d),
                  _resident(w_attn_out.shape), _resident(w_o.shape), _resident((1, d)), _resident((1, d)),
                  _resident((d, d_ff)), _resident((d, d_ff)), _resident((d_ff, d)),
                  _resident((1, d)), _resident((1, d))],
        out_specs=pl.BlockSpec((tm, d), lambda i: (jnp.maximum(i - 1, 0), 0)),
        out_shape=jax.ShapeDtypeStruct((t, d), F32),
        scratch_shapes=[pltpu.VMEM((tm // sub, n_grp, GROUP_WIDTH // LANES, stage_rows, LANES), F32),
                        pltpu.VMEM((tm // sub, n_grp, 1, stage_rows, LANES), F32),
                        pltpu.VMEM((tm // sub, d // LANES, ys_rows, LANES), F32),
                        pltpu.VMEM((2, tm, d), F32)],
        compiler_params=_params("arbitrary"),
        name="mix_ffn",
    )(x1, *os_, *sts, y_ssm, ga, gs, w_attn_out.astype(BF16), w_o.astype(BF16), vec(g2), vec(b2),
      wg.astype(BF16), wu.astype(BF16), wd.astype(BF16), vec(g3), vec(b3))


def _layer(x, positions, w_in, w_attn_out, a_re, a_im, log_dt, b_re, b_im, c_re, c_im, d_skip,
           w_glu, b_glu, w_ssm_out, w_o, ffn1, ffn2, ln1, ln2, ln3, *, alpha):
    batch, seq, d = x.shape
    t = batch * seq
    n_grp = a_re.shape[0]
    ssm_width = n_grp * SSM_GROUP
    u_lo = 3 * ATTN_WIDTH

    x1 = _ffn_ln(x.reshape(t, d), *ffn1, *ln1, alpha=alpha)
    w_in_b = w_in.astype(BF16)
    *qkvs, ga, gs = _in_proj(x1, positions, w_in_b, batch=batch, seq=seq, ssm_width=ssm_width)

    os_, sts = [], []
    for qkv, (window, dilation) in zip(qkvs, ATTN_PATTERNS):
        assert window == ATTN_BLK * dilation
        o, st = _dilated_attention(qkv)
        os_.append(o)
        sts.append(st)

    ut = _ssm_in(x1, w_in_b, u_lo, ssm_width)
    prep = _ssm_prep(a_re, a_im, log_dt, b_re, b_im, c_re, c_im)
    dsk = jnp.tile(d_skip.reshape(n_grp, 1, SSM_GROUP), (1, SSM_CHUNK, 1)).reshape(n_grp, SSM_ROW, 1)
    yt = _ssm(ut, prep, dsk, n_chunk=seq // SSM_CHUNK)
    y_ssm = _ssm_out(yt, w_glu, b_glu, w_ssm_out)

    x3 = _mix_ffn(x1, os_, sts, y_ssm, ga, gs, w_attn_out, w_o, *ln2, *ffn2, *ln3, seq=seq, alpha=alpha)
    return x3.reshape(batch, seq, d)


def kernel(x, positions, w_in, w_attn_out, a_re, a_im, log_dt, b_re, b_im, c_re, c_im, d_skip, w_glu, b_glu, w_ssm_out, w_o, ffn1_wg, ffn1_wu, ffn1_wd, ffn2_wg, ffn2_wu, ffn2_wd, ln1_g, ln1_b, ln2_g, ln2_b, ln3_g, ln3_b):
    depth = w_in.shape[0]
    alpha = (2.0 * depth) ** 0.25
    for i in range(depth):
        x = _layer(x, positions, w_in[i], w_attn_out[i], a_re[i], a_im[i], log_dt[i], b_re[i], b_im[i],
                   c_re[i], c_im[i], d_skip[i], w_glu[i], b_glu[i], w_ssm_out[i], w_o[i],
                   (ffn1_wg[i], ffn1_wu[i], ffn1_wd[i]), (ffn2_wg[i], ffn2_wu[i], ffn2_wd[i]),
                   (ln1_g[i], ln1_b[i]), (ln2_g[i], ln2_b[i]), (ln3_g[i], ln3_b[i]), alpha=alpha)
    return x
```

```python
import functools

import jax
import jax.numpy as jnp
from jax import lax
from jax.experimental import pallas as pl
from jax.experimental.pallas import tpu as pltpu

F32 = jnp.float32
BF16 = jnp.bfloat16

HEAD_DIM = 64
HEADS_PER_GROUP = 4
GROUP_WIDTH = HEADS_PER_GROUP * HEAD_DIM
ATTN_PATTERNS = ((128, 1), (512, 4), (2048, 16))
N_ATTN_GROUPS = len(ATTN_PATTERNS)
ATTN_WIDTH = N_ATTN_GROUPS * GROUP_WIDTH
ROT_DIM = HEAD_DIM // 4
ROPE_THETA = 500000.0
ROPE_ROWS = 32
ATTN_BLK = 128
NEG_INF = -1e30
LOG2_E = 1.4426950408889634
SSM_GROUP = 16
SSM_CHUNK = 16
SSM_ROW = SSM_CHUNK * SSM_GROUP
SSM_POWERS = 16
SCAN_INTERLEAVE = 8
CHUNK_PITCH = 24
LN_EPS = 1e-5

LANES = 128
SUBLANES = 8
VMEM_LIMIT_BYTES = 56 * 1024 * 1024


def _scatter_pitch(stride):
    return CHUNK_PITCH if stride == SSM_CHUNK else stride


def _rows_by_class(val, stage, dil):
    rows, w = val.shape
    n = rows // dil
    pitch = _scatter_pitch(dil)
    n_col = w // LANES
    for c in range(n_col):
        col = val[:, c * LANES:(c + 1) * LANES]
        if pitch != dil:
            col = jnp.concatenate([col.reshape(n, dil, LANES), jnp.zeros((n, pitch - dil, LANES), F32)], axis=1)
            col = col.reshape(n * pitch, LANES)
        stage[c, 0:n * pitch, :] = col

    def get(r):
        return jnp.concatenate([stage[c, pl.ds(r, n, stride=pitch), :] for c in range(n_col)], axis=1)

    return get


def _params(*semantics):
    return pltpu.CompilerParams(dimension_semantics=semantics, vmem_limit_bytes=VMEM_LIMIT_BYTES)


def _resident(shape):
    zeros = (0,) * len(shape)
    return pl.BlockSpec(shape, lambda *_: zeros, pipeline_mode=pl.Buffered(1))


def _layer_norm(y, g, b):
    mu = jnp.mean(y, axis=-1, keepdims=True)
    yc = y - mu
    var = jnp.mean(yc * yc, axis=-1, keepdims=True)
    return yc * lax.rsqrt(var + LN_EPS) * g + b


def _sigmoid(x):
    return 1.0 / (1.0 + jnp.exp(-x))


def _gelu_tanh(x):
    c = 0.7978845608028654
    return 0.5 * x * (1.0 + jnp.tanh(c * (x + 0.044715 * (x * x * x))))


def _ffn_ln_kernel(x_ref, wg_ref, wu_ref, wd_ref, g_ref, b_ref, *rest, chunks, sub, alpha, n_cast):
    o_ref = rest[n_cast]
    for src, dst in zip(rest[:n_cast], rest[n_cast + 1:]):
        dst[...] = src[...].astype(BF16)

    for r0 in range(0, x_ref.shape[0], sub):
        x = x_ref[r0:r0 + sub, :]
        xb = x.astype(BF16)
        acc = None
        for c0, c1 in chunks:
            gate = jnp.dot(xb, wg_ref[:, c0:c1], preferred_element_type=F32)
            up = jnp.dot(xb, wu_ref[:, c0:c1], preferred_element_type=F32)
            h = (gate * _sigmoid(gate) * up).astype(BF16)
            part = jnp.dot(h, wd_ref[c0:c1, :], preferred_element_type=F32)
            acc = part if acc is None else acc + part
        y = alpha * x + 0.5 * acc
        o_ref[r0:r0 + sub, :] = _layer_norm(y, g_ref[...], b_ref[...])


def _ffn_chunks(d_ff, width):
    edges = list(range(0, d_ff, width)) + [d_ff]
    return tuple(zip(edges[:-1], edges[1:]))


def _ffn_ln(x, wg, wu, wd, g, b, *, alpha, cast=(), tm=1024, sub=256, chunk=1024):
    t, d = x.shape
    d_ff = wg.shape[1]
    steps = t // tm
    kern = functools.partial(_ffn_ln_kernel, chunks=_ffn_chunks(d_ff, chunk), sub=sub, alpha=alpha,
                             n_cast=len(cast))
    row_blocks = [pl.BlockSpec((w.shape[0] // steps, w.shape[1]), lambda i: (i, 0)) for w in cast]
    assert all(w.shape[0] % (16 * steps) == 0 for w in cast)
    out = pl.pallas_call(
        kern,
        grid=(steps,),
        in_specs=[
            pl.BlockSpec((tm, d), lambda i: (i, 0)),
            _resident((d, d_ff)), _resident((d, d_ff)), _resident((d_ff, d)),
            _resident((1, d)), _resident((1, d)),
        ] + row_blocks,
        out_specs=[pl.BlockSpec((tm, d), lambda i: (i, 0))] + row_blocks,
        out_shape=[jax.ShapeDtypeStruct((t, d), F32)] + [jax.ShapeDtypeStruct(w.shape, BF16) for w in cast],
        compiler_params=_params("parallel"),
        name="ffn_ln",
    )(x, wg.astype(BF16), wu.astype(BF16), wd.astype(BF16), g.reshape(1, d), b.reshape(1, d), *cast)
    return out[0], out[1:]


def _rope_spread():
    half = ROT_DIM // 2
    rows = lax.broadcasted_iota(jnp.int32, (ROPE_ROWS, 2 * LANES), 0)
    cols = lax.broadcasted_iota(jnp.int32, (ROPE_ROWS, 2 * LANES), 1)
    in_head = cols % HEAD_DIM
    freq = in_head % half
    is_cos = cols < LANES
    rot = in_head < ROT_DIM
    cos_part = is_cos & rot & (rows == freq)
    one_part = is_cos & jnp.logical_not(rot) & (rows == 2 * half)
    sin_part = jnp.logical_not(is_cos) & rot & (rows == half + freq)
    sign = jnp.where(in_head < half, 1.0, -1.0)
    return jnp.where(cos_part | one_part, 1.0, jnp.where(sin_part, sign, 0.0)).astype(BF16)


def _in_proj_kernel(x_ref, pos_ref, invf_ref, w_ref,
                    qkv0_ref, qkv1_ref, qkv2_ref, ga_ref, gs_ref, stage_ref, *, splits, sub):
    half = ROT_DIM // 2
    spread = _rope_spread()
    first = lax.broadcasted_iota(jnp.int32, (sub, LANES), 1) % HEAD_DIM < half
    s0, s1, s2, s3, s4, s5 = splits
    tn = (((0,), (0,)), ((), ()))

    for r0 in range(0, x_ref.shape[0], sub):
        xb = x_ref[r0:r0 + sub, :].astype(BF16)

        ang = invf_ref[...] * pos_ref[:, r0:r0 + sub].astype(F32)
        tab = jnp.concatenate([jnp.cos(ang), jnp.sin(ang), jnp.ones((half, sub), F32),
                               jnp.zeros((ROPE_ROWS - 3 * half, sub), F32)], axis=0)
        tab_hi = tab.astype(BF16)
        tab_lo = (tab - tab_hi.astype(F32)).astype(BF16)
        cs = (lax.dot_general(tab_hi, spread, tn, preferred_element_type=F32)
              + lax.dot_general(tab_lo, spread, tn, preferred_element_type=F32))
        cos = cs[:, :LANES]
        sin = cs[:, LANES:]

        def rotate(z):
            cols = []
            for c in range(z.shape[1] // LANES):
                zc = z[:, c * LANES:(c + 1) * LANES]
                zs = zc * sin
                up = pltpu.roll(jnp.where(first, zs, 0.0), half, axis=1)
                dn = pltpu.roll(jnp.where(first, 0.0, zs), LANES - half, axis=1)
                cols.append(zc * cos + up + dn)
            return jnp.concatenate(cols, axis=1)

        def proj(lo, hi):
            return jnp.dot(xb, w_ref[:, lo:hi], preferred_element_type=F32)

        q = rotate(proj(0, s0)) * (HEAD_DIM ** -0.5 * LOG2_E)
        k = rotate(proj(s0, s1))
        v = proj(s1, s2)
        for gi, out_ref in enumerate((qkv0_ref, qkv1_ref, qkv2_ref)):
            dil = ATTN_PATTERNS[gi][1]
            c0 = gi * GROUP_WIDTH
            qkv = jnp.concatenate([a[:, c0:c0 + GROUP_WIDTH] for a in (q, k, v)], axis=1)
            n = sub // dil
            i0 = r0 // dil
            if dil == 1:
                out_ref[0, i0:i0 + n, :] = qkv.astype(BF16)
            else:
                get = _rows_by_class(qkv, stage_ref.at[r0 // sub, gi - 1], dil)
                for r in range(dil):
                    out_ref[r, i0:i0 + n, :] = get(r).astype(BF16)
        ga_ref[r0:r0 + sub, :] = _sigmoid(proj(s3, s4)).astype(BF16)
        gs_ref[r0:r0 + sub, :] = _sigmoid(proj(s4, s5)).astype(BF16)


def _in_proj(x1, positions, w_in, *, batch, seq, ssm_width, tm=1024, sub=256):
    t, d = x1.shape
    splits = (ATTN_WIDTH, 2 * ATTN_WIDTH, 3 * ATTN_WIDTH, 3 * ATTN_WIDTH + ssm_width,
              3 * ATTN_WIDTH + ssm_width + d, 3 * ATTN_WIDTH + ssm_width + 2 * d)
    assert splits[-1] == w_in.shape[1] and seq % tm == 0
    half = ROT_DIM // 2
    invf = (ROPE_THETA ** (-jnp.arange(half, dtype=F32) * 2.0 / ROT_DIM)).reshape(half, 1)
    tiles = seq // tm
    qkv_w = 3 * GROUP_WIDTH
    qkv_specs = [pl.BlockSpec((None, dil, tm // dil, qkv_w), lambda i: (i // tiles, 0, i % tiles, 0))
                 for _, dil in ATTN_PATTERNS]
    qkv_shapes = [jax.ShapeDtypeStruct((batch, dil, seq // dil, qkv_w), BF16) for _, dil in ATTN_PATTERNS]
    widths = (d, d)
    kern = functools.partial(_in_proj_kernel, splits=splits, sub=sub)
    n_strided = sum(dil > 1 for _, dil in ATTN_PATTERNS)
    stage_rows = max(sub // dil * _scatter_pitch(dil) for _, dil in ATTN_PATTERNS)
    return pl.pallas_call(
        kern,
        grid=(t // tm,),
        in_specs=[
            pl.BlockSpec((tm, d), lambda i: (i, 0)),
            pl.BlockSpec((1, tm), lambda i: (0, i)),
            _resident((half, 1)),
            _resident(w_in.shape),
        ],
        out_specs=qkv_specs + [pl.BlockSpec((tm, w), lambda i: (i, 0)) for w in widths],
        out_shape=qkv_shapes + [jax.ShapeDtypeStruct((t, w), BF16) for w in widths],
        scratch_shapes=[pltpu.VMEM((tm // sub, n_strided, qkv_w // LANES, stage_rows, LANES), F32)],
        compiler_params=_params("parallel"),
        name="in_proj",
    )(x1, positions.reshape(1, t), invf, w_in)


def _attn_kernel(q_ref, kc_ref, kp_ref, vc_ref, vp_ref, o_ref, st_ref, *, n_sub):
    first = pl.program_id(2) == 0
    row = lax.broadcasted_iota(jnp.int32, (ATTN_BLK, 2 * ATTN_BLK), 0)
    col = lax.broadcasted_iota(jnp.int32, (ATTN_BLK, 2 * ATTN_BLK), 1)
    band = (col >= row) & (col <= row + ATTN_BLK)
    lane = lax.broadcasted_iota(jnp.int32, (ATTN_BLK, LANES), 1)
    lo_half = lane < HEAD_DIM

    n_pair = GROUP_WIDTH // LANES
    tiles = [(rc, j, hp, hh) for rc in range(q_ref.shape[0]) for j in range(n_sub)
             for hp in range(n_pair) for hh in range(2)]

    def window(ref_prev, ref_cur, j, c0):
        r0 = j * ATTN_BLK
        if j == 0:
            return jnp.concatenate([ref_prev[:, c0:c0 + LANES], ref_cur[0:ATTN_BLK, c0:c0 + LANES]], axis=0)
        return ref_cur[r0 - ATTN_BLK:r0 + ATTN_BLK, c0:c0 + LANES]

    def scores(tile):
        rc, j, hp, hh = tile
        q2 = q_ref[rc, j * ATTN_BLK:(j + 1) * ATTN_BLK, hp * LANES:(hp + 1) * LANES]
        qm = jnp.where(lo_half if hh == 0 else jnp.logical_not(lo_half), q2, jnp.zeros_like(q2))
        return lax.dot_general(qm, window(kp_ref.at[rc], kc_ref.at[rc], j, hp * LANES), (((1,), (1,)), ((), ())),
                               preferred_element_type=F32)

    def softmax(tile, s):
        valid = band & ((col >= ATTN_BLK) | jnp.logical_not(first)) if tile[1] == 0 else band
        s = jnp.where(valid, s, NEG_INF)
        m = jnp.max(s, axis=1, keepdims=True)
        p = jnp.exp2(s - m)
        return p.astype(BF16), m, jnp.sum(p, axis=1, keepdims=True)

    def weighted_values(tile, p, den):
        rc, j, hp, _ = tile
        return jnp.dot(p, window(vp_ref.at[rc], vc_ref.at[rc], j, hp * LANES), preferred_element_type=F32) / den

    s_q, p_q, outs = {}, {}, {}
    for step in range(len(tiles) + 2):
        if step < len(tiles):
            s_q[step] = scores(tiles[step])
        if 0 <= step - 1 < len(tiles):
            p_q[step - 1] = softmax(tiles[step - 1], s_q.pop(step - 1))
        if 0 <= step - 2 < len(tiles):
            t = step - 2
            rc, j, hp, hh = tiles[t]
            p, m, den = p_q.pop(t)
            outs[hh] = weighted_values(tiles[t], p, den)
            h = 2 * hp + hh
            rows = slice(j * ATTN_BLK, (j + 1) * ATTN_BLK)
            if h == 0:
                st_ref[rc, rows, :] = jnp.zeros((ATTN_BLK, LANES), F32)
            st_ref[rc, rows, h:h + 1] = m
            st_ref[rc, rows, HEADS_PER_GROUP + h:HEADS_PER_GROUP + h + 1] = den
            if hh == 1:
                o_ref[rc, rows, hp * LANES:(hp + 1) * LANES] = (
                    jnp.where(lo_half, outs[0], outs[1]).astype(BF16))


def _dilated_attention(qkv):
    batch, dilation, n, _ = qkv.shape
    rows_per_step = 2048
    qb = min(rows_per_step, n)
    n_sub = qb // ATTN_BLK
    rb = min(max(rows_per_step // qb, 1), dilation)

    def cur(part):
        return pl.BlockSpec((None, rb, qb, GROUP_WIDTH), lambda b, r, i: (b, r, i, part))

    def prev(part):
        return pl.BlockSpec((None, rb, ATTN_BLK, GROUP_WIDTH),
                            lambda b, r, i: (b, r, jnp.maximum(i * n_sub - 1, 0), part))

    return pl.pallas_call(
        functools.partial(_attn_kernel, n_sub=n_sub),
        grid=(batch, dilation // rb, n // qb),
        in_specs=[cur(0), cur(1), prev(1), cur(2), prev(2)],
        out_specs=[pl.BlockSpec((None, rb, qb, GROUP_WIDTH), lambda b, r, i: (b, r, i, 0)),
                   pl.BlockSpec((None, rb, qb, LANES), lambda b, r, i: (b, r, i, 0))],
        out_shape=[jax.ShapeDtypeStruct((batch, dilation, n, GROUP_WIDTH), BF16),
                   jax.ShapeDtypeStruct((batch, dilation, n, LANES), F32)],
        compiler_params=_params("parallel", "parallel", "parallel"),
        name=f"attn_d{dilation}",
    )(qkv, qkv, qkv, qkv, qkv)


def _ssm_in_kernel(*refs, n_grp, ch, n_w):
    x_refs, w_refs, ut_ref = refs[:-n_w - 1], refs[-n_w - 1:-1], refs[-1]
    w = jnp.concatenate([w_ref[...] for w_ref in w_refs], axis=1)
    for s in range(0, SSM_CHUNK, 2):
        xs = jnp.concatenate(
            [jnp.concatenate([x_ref[pl.ds(s + i, ch, stride=SSM_CHUNK), :] for x_ref in x_refs], axis=1)
             for i in range(2)], axis=0).astype(BF16)
        ut = lax.dot_general(w, xs, (((0,), (1,)), ((), ())), preferred_element_type=F32)
        for i in range(2):
            ut_ref[:, (s + i) * SSM_GROUP:(s + i + 1) * SSM_GROUP, :] = (
                ut[:, i * ch:(i + 1) * ch].reshape(n_grp, SSM_GROUP, ch).astype(BF16))


def _ssm_in(x1, w_in_b, u_lo, width, *, ch=LANES, wblk=256):
    t, d = x1.shape
    assert u_lo % wblk == 0 and width % wblk == 0
    n_grp = width // SSM_GROUP
    n_chunk_all = t // SSM_CHUNK
    tm = ch * SSM_CHUNK
    n_w = width // wblk
    slabs = [pl.BlockSpec((tm, LANES), lambda i, c=c: (i, c)) for c in range(d // LANES)]
    w_cols = [pl.BlockSpec((d, wblk), lambda i, c=c: (0, u_lo // wblk + c), pipeline_mode=pl.Buffered(1))
              for c in range(n_w)]
    return pl.pallas_call(
        functools.partial(_ssm_in_kernel, n_grp=n_grp, ch=ch, n_w=n_w),
        grid=(n_chunk_all // ch,),
        in_specs=slabs + w_cols,
        out_specs=pl.BlockSpec((n_grp, SSM_ROW, ch), lambda i: (0, 0, i)),
        out_shape=jax.ShapeDtypeStruct((n_grp, SSM_ROW, n_chunk_all), BF16),
        compiler_params=_params("parallel"),
        name="ssm_in",
    )(*([x1] * (d // LANES)), *([w_in_b] * n_w))


def _hdot(a, b):
    return jnp.dot(a, b, preferred_element_type=F32, precision=lax.Precision.HIGHEST)


def _cmul(ar, ai, br, bi):
    return ar * br - ai * bi, ar * bi + ai * br


def _cpowers(lbr, lbi, n, bits):
    out_r = jnp.ones(n.shape, F32)
    out_i = jnp.zeros(n.shape, F32)
    for b in range(bits):
        sel = (n & (1 << b)) != 0
        out_r, out_i = _cmul(out_r, out_i, jnp.where(sel, lbr, 1.0), jnp.where(sel, lbi, 0.0))
        lbr, lbi = _cmul(lbr, lbi, lbr, lbi)
    return out_r, out_i


def _ssm_prep_kernel(are_l, aim_l, ldt_ref, bre_ref, bim_ref, cre_ref, cim_ref,
                     tt_ref, win_ref, wout_ref, apr_ref, api_ref, *, gb):
    p = are_l.shape[2]
    eye = (lax.broadcasted_iota(jnp.int32, (p, p), 0) == lax.broadcasted_iota(jnp.int32, (p, p), 1)).astype(F32)
    expand = (lax.broadcasted_iota(jnp.int32, (SSM_CHUNK, SSM_ROW), 1) // SSM_GROUP
              == lax.broadcasted_iota(jnp.int32, (SSM_CHUNK, SSM_ROW), 0)).astype(F32)
    expand_t = (lax.broadcasted_iota(jnp.int32, (SSM_ROW, SSM_CHUNK), 0) // SSM_GROUP
                == lax.broadcasted_iota(jnp.int32, (SSM_ROW, SSM_CHUNK), 1)).astype(F32)
    rem = SSM_CHUNK - 1 - lax.broadcasted_iota(jnp.int32, (p, SSM_CHUNK), 1)
    tp1 = lax.broadcasted_iota(jnp.int32, (SSM_CHUNK, p), 0) + 1
    col = lax.broadcasted_iota(jnp.int32, (p, SSM_POWERS), 1)
    lane = lax.broadcasted_iota(jnp.int32, (SSM_GROUP, SSM_ROW), 1)
    chunk_bits = SSM_CHUNK.bit_length()

    def group(g):
        dt = jnp.exp(ldt_ref[g])
        lam_r = jnp.minimum(are_l[g], -1e-4)
        lam_i = aim_l[g]
        mag_l = jnp.exp(lam_r * dt)
        lbr_l = mag_l * jnp.cos(lam_i * dt)
        lbi_l = mag_l * jnp.sin(lam_i * dt)
        inv = 1.0 / (lam_r * lam_r + lam_i * lam_i)
        cfr_l = ((lbr_l - 1.0) * lam_r + lbi_l * lam_i) * inv
        cfi_l = (lbi_l * lam_r - (lbr_l - 1.0) * lam_i) * inv
        stacked = jnp.concatenate([lbr_l, lbi_l, cfr_l, cfi_l, jnp.zeros((SUBLANES - 4, p), F32)], axis=0)
        cols = lax.dot_general(eye, stacked, (((1,), (1,)), ((), ())), preferred_element_type=F32,
                               precision=lax.Precision.HIGHEST)
        yield
        lbr, lbi = cols[:, 0:1], cols[:, 1:2]

        bre = bre_ref[g]
        bim = bim_ref[g]
        bbar_r, bbar_i = _cmul(cols[:, 2:3], cols[:, 3:4], bre, bim)
        pw_r, pw_i = _cpowers(lbr, lbi, rem, chunk_bits - 1)
        win_r, win_i = _cmul(_hdot(pw_r, expand), _hdot(pw_i, expand), bbar_r, bbar_i)
        win_ref[g] = jnp.concatenate([win_r, win_i], axis=0).astype(BF16)
        yield

        a_r, a_i = _cmul(pw_r[:, 0:1], pw_i[:, 0:1], lbr, lbi)
        apr = jnp.zeros((p, SSM_POWERS), F32)
        api = jnp.zeros((p, SSM_POWERS), F32)
        for j in range(SSM_POWERS):
            apr = jnp.where(col == j, a_r, apr)
            api = jnp.where(col == j, a_i, api)
            a_r, a_i = _cmul(a_r, a_i, a_r, a_i)
        apr_ref[g] = apr
        api_ref[g] = api
        yield

        e_r, e_i = _cpowers(lbr_l, lbi_l, tp1, chunk_bits)
        cre = cre_ref[g]
        cim = cim_ref[g]
        wo_r, wo_i = _cmul(_hdot(expand_t, e_r), _hdot(expand_t, e_i), cre, cim)
        wout_ref[g] = jnp.concatenate([wo_r, -wo_i], axis=1).astype(BF16)
        yield

        hrev = _hdot(cre[0:SSM_GROUP, :], win_r) - _hdot(cim[0:SSM_GROUP, :], win_i)
        yield
        blocks = []
        for t in range(SSM_CHUNK):
            hi = (t + 1) * SSM_GROUP
            rolled = hrev if hi == SSM_ROW else pltpu.roll(hrev, hi, axis=1)
            blocks.append(jnp.where(lane < hi, rolled, 0.0))
        tt_ref[g] = jnp.concatenate(blocks, axis=0).astype(BF16)

    active = [group(g) for g in range(gb)]
    while active:
        for gen in list(active):
            try:
                next(gen)
            except StopIteration:
                active.remove(gen)


def _ssm_prep(a_re, a_im, log_dt, b_re, b_im, c_re, c_im, *, gb=8):
    g, p = a_re.shape
    row = SSM_ROW
    blk = lambda shape: pl.BlockSpec((gb,) + shape, lambda i: (i, 0, 0))
    b_t = lambda b: jnp.tile(b, (1, 1, SSM_CHUNK))
    c_t = lambda c: jnp.tile(c, (1, SSM_CHUNK, 1))
    out_shapes = [
        jax.ShapeDtypeStruct((g, row, row), BF16),
        jax.ShapeDtypeStruct((g, 2 * p, row), BF16),
        jax.ShapeDtypeStruct((g, row, 2 * p), BF16),
        jax.ShapeDtypeStruct((g, p, SSM_POWERS), F32), jax.ShapeDtypeStruct((g, p, SSM_POWERS), F32),
    ]
    return pl.pallas_call(
        functools.partial(_ssm_prep_kernel, gb=gb),
        grid=(g // gb,),
        in_specs=[blk((1, p)), blk((1, p)), blk((1, 1)),
                  blk((p, row)), blk((p, row)), blk((row, p)), blk((row, p))],
        out_specs=[blk(s.shape[1:]) for s in out_shapes],
        out_shape=out_shapes,
        compiler_params=_params("parallel"),
        name="ssm_prep",
    )(a_re.reshape(g, 1, p), a_im.reshape(g, 1, p),
      log_dt.reshape(g, 1, 1), b_t(b_re), b_t(b_im), c_t(c_re), c_t(c_im))


def _ssm_kernel(ut_ref, tt_ref, win_ref, wout_ref, apr_ref, api_ref, dsk_ref, y_ref, *, gb, n_chunk):
    nk = ut_ref.shape[-1]
    p = apr_ref.shape[1]
    pos = lax.broadcasted_iota(jnp.int32, (SUBLANES, nk), 1) % n_chunk
    n_steps = (n_chunk - 1).bit_length()
    assert n_steps <= SSM_POWERS

    def shift_whole_tiles(v, sh):
        segs = []
        for b0 in range(0, nk, n_chunk):
            segs += [jnp.zeros((v.shape[0], sh), F32), v[:, b0:b0 + n_chunk - sh]]
        return jnp.concatenate(segs, axis=1)

    def group(g):
        ut = ut_ref[g]
        sl = jnp.dot(win_ref[g], ut, preferred_element_type=F32)
        yield
        apr = apr_ref[g]
        api = api_ref[g]
        prev_r, prev_i = [], []
        for b0 in range(0, p, SUBLANES * SCAN_INTERLEAVE):
            tiles = range(b0, b0 + SUBLANES * SCAN_INTERLEAVE, SUBLANES)
            s = {r0: (sl[r0:r0 + SUBLANES], sl[p + r0:p + r0 + SUBLANES]) for r0 in tiles}
            for j in range(n_steps):
                sh = 1 << j
                keep = pos >= sh
                for r0 in tiles:
                    s_r, s_i = s[r0]
                    a_r = apr[r0:r0 + SUBLANES, j:j + 1]
                    a_i = api[r0:r0 + SUBLANES, j:j + 1]
                    if sh % LANES == 0:
                        p_r, p_i = shift_whole_tiles(s_r, sh), shift_whole_tiles(s_i, sh)
                    else:
                        p_r = jnp.where(keep, pltpu.roll(s_r, sh, axis=1), 0.0)
                        p_i = jnp.where(keep, pltpu.roll(s_i, sh, axis=1), 0.0)
                    s[r0] = (s_r + a_r * p_r - a_i * p_i, s_i + a_r * p_i + a_i * p_r)
                yield
            keep = pos >= 1
            for r0 in tiles:
                prev_r.append(jnp.where(keep, pltpu.roll(s[r0][0], 1, axis=1), 0.0))
                prev_i.append(jnp.where(keep, pltpu.roll(s[r0][1], 1, axis=1), 0.0))
        prev = jnp.concatenate(prev_r + prev_i, axis=0).astype(BF16)
        yield
        y = jnp.dot(tt_ref[g], ut, preferred_element_type=F32)
        y += jnp.dot(wout_ref[g], prev, preferred_element_type=F32)
        yield
        y += dsk_ref[g] * ut.astype(F32)
        y_ref[g] = _gelu_tanh(y).astype(BF16)

    active = [group(g) for g in range(gb)]
    while active:
        for gen in list(active):
            try:
                next(gen)
            except StopIteration:
                active.remove(gen)


def _ssm(ut, prep, dsk, *, n_chunk, gb=4):
    tt, win, wout, apr, api = prep
    g, row, nk = ut.shape
    p = apr.shape[1]
    blk = lambda shape: pl.BlockSpec((gb,) + shape, lambda i: (i, 0, 0))
    kern = functools.partial(_ssm_kernel, gb=gb, n_chunk=n_chunk)
    return pl.pallas_call(
        kern,
        grid=(g // gb,),
        in_specs=[blk((row, nk)), blk((row, row)), blk((2 * p, row)), blk((row, 2 * p)),
                  blk((p, SSM_POWERS)), blk((p, SSM_POWERS)), blk((row, 1))],
        out_specs=blk((row, nk)),
        out_shape=jax.ShapeDtypeStruct((g, row, nk), BF16),
        compiler_params=_params("parallel"),
        name="ssm",
    )(ut, tt, win, wout, apr, api, dsk)


def _ssm_out_kernel(yt_ref, wglu_ref, bglu_ref, wso_ref, o_ref, *, t_per_pass):
    n_grp, _, ch = yt_ref.shape
    width = n_grp * SSM_GROUP
    def gated(t0):
        y = jnp.concatenate(
            [yt_ref[:, t * SSM_GROUP:(t + 1) * SSM_GROUP, :].reshape(width, ch)
             for t in range(t0, t0 + t_per_pass)], axis=1)
        return y, jnp.dot(wglu_ref[...], y, preferred_element_type=F32) + bglu_ref[...]

    def project(t0, y, gate):
        glu = (y.astype(F32) * _sigmoid(gate)).astype(BF16)
        yo = lax.dot_general(glu, wso_ref[...], (((0,), (0,)), ((), ())), preferred_element_type=F32)
        for i in range(t_per_pass):
            o_ref[t0 + i] = yo[i * ch:(i + 1) * ch, :].astype(BF16)

    starts = list(range(0, SSM_CHUNK, t_per_pass))
    pending = gated(starts[0])
    for n, t0 in enumerate(starts):
        nxt = gated(starts[n + 1]) if n + 1 < len(starts) else None
        project(t0, *pending)
        pending = nxt


def _ssm_out(yt, w_glu, b_glu, w_ssm_out, *, ch=LANES, t_per_pass=4):
    n_grp, row, n_chunk_all = yt.shape
    width, d = w_ssm_out.shape
    return pl.pallas_call(
        functools.partial(_ssm_out_kernel, t_per_pass=t_per_pass),
        grid=(n_chunk_all // ch,),
        in_specs=[pl.BlockSpec((n_grp, row, ch), lambda i: (0, 0, i)),
                  _resident((width, width)), _resident((width, 1)), _resident((width, d))],
        out_specs=pl.BlockSpec((None, SSM_CHUNK, ch, d), lambda i: (i, 0, 0, 0)),
        out_shape=jax.ShapeDtypeStruct((n_chunk_all // ch, SSM_CHUNK, ch, d), BF16),
        compiler_params=_params("parallel"),
        name="ssm_out",
    )(yt, w_glu.T.astype(BF16), b_glu.reshape(width, 1), w_ssm_out.astype(BF16))


def _mix_ffn_kernel(x_ref, o0, o1, o2, st0, st1, st2, ys_ref, ga_ref, gs_ref,
                    wao_ref, wo_ref, g_ref, b_ref, wg_ref, wu_ref, wd_ref, g3_ref, b3_ref,
                    out_ref, o_stage, st_stage, ys_stage, x2_ref, *, sub, chunks, alpha):
    step = pl.program_id(0)
    slot_w = step % 2
    slot_r = 1 - slot_w

    @pl.when(step == 0)
    def _():
        x2_ref[...] = jnp.zeros(x2_ref.shape, F32)

    lane = lax.broadcasted_iota(jnp.int32, (sub, LANES), 1)
    lo_half = lane < HEAD_DIM

    def token_order(ref, stage, r0):
        dil = ref.shape[0]
        n = sub // dil
        i0 = r0 // dil
        if dil == 1:
            return ref[0, i0:i0 + n, :].astype(F32)
        n_col = ref.shape[2] // LANES
        pitch = _scatter_pitch(dil)
        for r in range(dil):
            blk = ref[r, i0:i0 + n, :].astype(F32)
            for c in range(n_col):
                stage[c, pl.ds(r, n, stride=pitch), :] = blk[:, c * LANES:(c + 1) * LANES]
        cols = [stage[c, 0:n * pitch, :] for c in range(n_col)]
        if pitch != dil:
            cols = [a.reshape(n, pitch, LANES)[:, :dil, :].reshape(sub, LANES) for a in cols]
        return jnp.concatenate(cols, axis=1)

    def head_cols(vals):
        cols = []
        for hp in range(GROUP_WIDTH // LANES):
            cols.append(jnp.where(lo_half, vals[2 * hp], vals[2 * hp + 1]))
        return jnp.concatenate(cols, axis=1)

    def merge_groups(k):
        r0 = k * sub
        sts = [token_order(st, st_stage.at[k, i], r0) for i, st in enumerate((st0, st1, st2))]
        outs = [token_order(o, o_stage.at[k, i], r0) for i, o in enumerate((o0, o1, o2))]
        dens = [pltpu.roll(st, LANES - HEADS_PER_GROUP, axis=1) for st in sts]
        mx = jnp.maximum(jnp.maximum(sts[0], sts[1]), sts[2])
        w = [den * jnp.exp2(st - mx) for den, st in zip(dens, sts)]
        tot = jnp.where(lane < HEADS_PER_GROUP, w[0] + w[1] + w[2], 1.0)
        att = None
        for g in range(N_ATTN_GROUPS):
            wt = w[g] / tot
            term = head_cols([wt[:, h:h + 1] for h in range(HEADS_PER_GROUP)]) * outs[g]
            att = term if att is None else att + term
        return att.astype(BF16)

    def gate(k, att):
        rows = slice(k * sub, (k + 1) * sub)
        y_attn = jnp.dot(att, wao_ref[...], preferred_element_type=F32)
        y_ssm = token_order(ys_ref, ys_stage.at[k], k * sub)
        merged = ga_ref[rows, :].astype(F32) * y_attn + gs_ref[rows, :].astype(F32) * y_ssm
        return merged.astype(BF16)

    def project(k, merged):
        rows = slice(k * sub, (k + 1) * sub)
        mix = jnp.dot(merged, wo_ref[...], preferred_element_type=F32)
        x2_ref[slot_w, rows, :] = _layer_norm(alpha * x_ref[rows, :] + mix, g_ref[...], b_ref[...])

    x2_prev = x2_ref[slot_r]
    ffn_state = {}

    def ffn_gate_up(k, c):
        rows = slice(k * sub, (k + 1) * sub)
        if c == chunks[0]:
            ffn_state[k] = {"xb": x2_prev[rows, :].astype(BF16), "acc": None}
        xb = ffn_state[k]["xb"]
        ffn_state[k]["gu"] = (jnp.dot(xb, wg_ref[:, c[0]:c[1]], preferred_element_type=F32),
                              jnp.dot(xb, wu_ref[:, c[0]:c[1]], preferred_element_type=F32))

    def ffn_down(k, c):
        rows = slice(k * sub, (k + 1) * sub)
        gate_v, up = ffn_state[k].pop("gu")
        h = (gate_v * _sigmoid(gate_v) * up).astype(BF16)
        part = jnp.dot(h, wd_ref[c[0]:c[1], :], preferred_element_type=F32)
        acc = ffn_state[k]["acc"]
        ffn_state[k]["acc"] = part if acc is None else acc + part
        if c == chunks[-1]:
            y = alpha * x2_prev[rows, :] + 0.5 * ffn_state.pop(k)["acc"]
            out_ref[rows, :] = _layer_norm(y, g3_ref[...], b3_ref[...])

    n_sub = x_ref.shape[0] // sub
    ffn_stages = [f for k in range(n_sub) for c in chunks
                  for f in (functools.partial(ffn_gate_up, k, c), functools.partial(ffn_down, k, c))]
    mix_stages = [f for k in range(n_sub) for f in (functools.partial(merge_groups, k),
                                                    functools.partial(gate, k),
                                                    functools.partial(project, k))]
    carry = ()
    for f in ffn_stages:
        f()
        if mix_stages:
            out = mix_stages.pop(0)(*carry)
            carry = () if out is None else (out,)
    assert not mix_stages


def _mix_ffn(x1, os_, sts, y_ssm, ga, gs, w_attn_out, w_o, g2, b2, wg, wu, wd, g3, b3, *,
             seq, alpha, tm=512, sub=256, chunk=1024):
    t, d = x1.shape
    d_ff = wg.shape[1]
    tiles = seq // tm
    n_tiles = t // tm
    cur = lambda i: jnp.minimum(i, n_tiles - 1)
    tile = lambda w: pl.BlockSpec((tm, w), lambda i: (cur(i), 0))

    def by_class(a):
        _, dil, n, w = a.shape
        return pl.BlockSpec((None, dil, tm // dil, w), lambda i: (cur(i) // tiles, 0, cur(i) % tiles, 0))

    n_grp = len(os_)
    stage_rows = max(sub // a.shape[1] * _scatter_pitch(a.shape[1]) for a in os_)
    per_big = y_ssm.shape[2] * SSM_CHUNK // tm
    ys_spec = pl.BlockSpec((None, SSM_CHUNK, tm // SSM_CHUNK, d),
                           lambda i: (cur(i) // per_big, 0, cur(i) % per_big, 0))
    ys_rows = sub // SSM_CHUNK * _scatter_pitch(SSM_CHUNK)
    vec = lambda v: v.reshape(1, d)
    return pl.pallas_call(
        functools.partial(_mix_ffn_kernel, sub=sub, chunks=_ffn_chunks(d_ff, chunk), alpha=alpha),
        grid=(n_tiles + 1,),
        in_specs=[tile(d)] + [by_class(a) for a in os_] + [by_class(a) for a in sts] + [ys_spec, tile(d), tile(d),
                  _resident(w_attn_out.shape), _resident(w_o.shape), _resident((1, d)), _resident((1, d)),
                  _resident((d, d_ff)), _resident((d, d_ff)), _resident((d_ff, d)),
                  _resident((1, d)), _resident((1, d))],
        out_specs=pl.BlockSpec((tm, d), lambda i: (jnp.maximum(i - 1, 0), 0)),
        out_shape=jax.ShapeDtypeStruct((t, d), F32),
        scratch_shapes=[pltpu.VMEM((tm // sub, n_grp, GROUP_WIDTH // LANES, stage_rows, LANES), F32),
                        pltpu.VMEM((tm // sub, n_grp, 1, stage_rows, LANES), F32),
                        pltpu.VMEM((tm // sub, d // LANES, ys_rows, LANES), F32),
                        pltpu.VMEM((2, tm, d), F32)],
        compiler_params=_params("arbitrary"),
        name="mix_ffn",
    )(x1, *os_, *sts, y_ssm, ga, gs, w_attn_out.astype(BF16), w_o.astype(BF16), vec(g2), vec(b2),
      wg.astype(BF16), wu.astype(BF16), wd.astype(BF16), vec(g3), vec(b3))


def _layer(x, positions, w_in, w_attn_out, a_re, a_im, log_dt, b_re, b_im, c_re, c_im, d_skip,
           w_glu, b_glu, w_ssm_out, w_o, ffn1, ffn2, ln1, ln2, ln3, *, alpha):
    batch, seq, d = x.shape
    t = batch * seq
    n_grp = a_re.shape[0]
    ssm_width = n_grp * SSM_GROUP
    u_lo = 3 * ATTN_WIDTH

    later = (w_in, w_attn_out, w_o, w_ssm_out, *ffn2)
    x1, (w_in_b, w_attn_out, w_o, w_ssm_out, *ffn2) = _ffn_ln(x.reshape(t, d), *ffn1, *ln1, alpha=alpha, cast=later)
    *qkvs, ga, gs = _in_proj(x1, positions, w_in_b, batch=batch, seq=seq, ssm_width=ssm_width)

    os_, sts = [], []
    for qkv, (window, dilation) in zip(qkvs, ATTN_PATTERNS):
        assert window == ATTN_BLK * dilation
        o, st = _dilated_attention(qkv)
        os_.append(o)
        sts.append(st)

    ut = _ssm_in(x1, w_in_b, u_lo, ssm_width)
    prep = _ssm_prep(a_re, a_im, log_dt, b_re, b_im, c_re, c_im)
    dsk = jnp.tile(d_skip.reshape(n_grp, 1, SSM_GROUP), (1, SSM_CHUNK, 1)).reshape(n_grp, SSM_ROW, 1)
    yt = _ssm(ut, prep, dsk, n_chunk=seq // SSM_CHUNK)
    y_ssm = _ssm_out(yt, w_glu, b_glu, w_ssm_out)

    x3 = _mix_ffn(x1, os_, sts, y_ssm, ga, gs, w_attn_out, w_o, *ln2, *ffn2, *ln3, seq=seq, alpha=alpha)
    return x3.reshape(batch, seq, d)


def kernel(x, positions, w_in, w_attn_out, a_re, a_im, log_dt, b_re, b_im, c_re, c_im, d_skip, w_glu, b_glu, w_ssm_out, w_o, ffn1_wg, ffn1_wu, ffn1_wd, ffn2_wg, ffn2_wu, ffn2_wd, ln1_g, ln1_b, ln2_g, ln2_b, ln3_g, ln3_b):
    depth = w_in.shape[0]
    alpha = (2.0 * depth) ** 0.25
    for i in range(depth):
        x = _layer(x, positions, w_in[i], w_attn_out[i], a_re[i], a_im[i], log_dt[i], b_re[i], b_im[i],
                   c_re[i], c_im[i], d_skip[i], w_glu[i], b_glu[i], w_ssm_out[i], w_o[i],
                   (ffn1_wg[i], ffn1_wu[i], ffn1_wd[i]), (ffn2_wg[i], ffn2_wu[i], ffn2_wd[i]),
                   (ln1_g[i], ln1_b[i]), (ln2_g[i], ln2_b[i]), (ln3_g[i], ln3_b[i]), alpha=alpha)
    return x
```
